```python
import math
import jax, jax.numpy as jnp
from jax import lax
import numpy as np

D_MODEL = 2048
BATCH = 1
SEQ = 8192
DEPTH = 1

MLA_V_DIM = 128
MLA_HEADS = (D_MODEL // 2) // MLA_V_DIM
MLA_NOPE_DIM = 128
MLA_ROPE_DIM = 64
MLA_QK_DIM = MLA_NOPE_DIM + MLA_ROPE_DIM
Q_LORA_RANK = 512
KV_LORA_RANK = 512
ROPE_THETA = 10000.0
SWA_HEAD_DIM = 64
SWA_HEADS = (D_MODEL // 2) // SWA_HEAD_DIM
SWA_KV_HEADS = 2
SWA_GROUP = SWA_HEADS // SWA_KV_HEADS
WINDOW = 128
NUM_BUCKETS = 32
MAX_DISTANCE = 128
Q_BLOCK = 128
MIX_WIDTH = MLA_HEADS * MLA_V_DIM + SWA_HEADS * SWA_HEAD_DIM
D_FF = ((-(-8 * D_MODEL // 3)) + 255) // 256 * 256
IN_SIZES = (Q_LORA_RANK, KV_LORA_RANK, MLA_ROPE_DIM,
            SWA_HEADS * SWA_HEAD_DIM, SWA_KV_HEADS * SWA_HEAD_DIM, SWA_KV_HEADS * SWA_HEAD_DIM)
IN_WIDTH = sum(IN_SIZES)
RMS_EPS = 1e-6
NEG_INF = -1e30

kernel_name = "hymba_mla_swa_sink_t5_sandwich"


def rms_norm(x, g):
    xf = x.astype(jnp.float32)
    y = xf * lax.rsqrt(jnp.mean(xf * xf, axis=-1, keepdims=True) + RMS_EPS)
    return (y * g.astype(jnp.float32)).astype(x.dtype)


def rope(t, positions):
    half = t.shape[-1] // 2
    inv_freq = ROPE_THETA ** (-jnp.arange(half, dtype=jnp.float32) / half)
    ang = positions.astype(jnp.float32)[..., None] * inv_freq
    cos = jnp.cos(ang)[:, :, None, :]
    sin = jnp.sin(ang)[:, :, None, :]
    tf = t.astype(jnp.float32)
    t1, t2 = tf[..., :half], tf[..., half:]
    return jnp.concatenate([t1 * cos - t2 * sin, t2 * cos + t1 * sin], axis=-1).astype(t.dtype)


def t5_causal_bucket(dist):
    max_exact = NUM_BUCKETS // 2
    dist = np.maximum(dist, 0)
    large = max_exact + (np.log(np.maximum(dist, 1) / max_exact)
                         / np.log(MAX_DISTANCE / max_exact)
                         * (NUM_BUCKETS - max_exact)).astype(np.int32)
    large = np.minimum(large, NUM_BUCKETS - 1)
    return np.where(dist < max_exact, dist, large).astype(np.int32)


def mla_attention(q, k, v):
    B, S, H, _ = q.shape
    nblk = S // Q_BLOCK
    scale = 1.0 / math.sqrt(MLA_QK_DIM)
    qb = q.reshape(B, nblk, Q_BLOCK, H, MLA_QK_DIM).transpose(1, 0, 2, 3, 4)
    starts = jnp.arange(nblk, dtype=jnp.int32) * Q_BLOCK
    kpos = jnp.arange(S, dtype=jnp.int32)

    def one_block(args):
        q_blk, start = args
        s = jnp.einsum('bqhd,bkhd->bhqk', q_blk, k).astype(jnp.float32) * scale
        qpos = start + jnp.arange(Q_BLOCK, dtype=jnp.int32)
        s = jnp.where(kpos[None, :] <= qpos[:, None], s, NEG_INF)
        p = jax.nn.softmax(s, axis=-1)
        return jnp.einsum('bhqk,bkhd->bqhd', p.astype(v.dtype), v)

    out = lax.map(one_block, (qb, starts))
    return out.transpose(1, 0, 2, 3, 4).reshape(B, S, H * MLA_V_DIM)


def swa_attention(q, k, v, sinks, rel_bias):
    B, S = q.shape[:2]
    nblk = S // Q_BLOCK
    scale = 1.0 / math.sqrt(SWA_HEAD_DIM)
    qb = q.reshape(B, nblk, Q_BLOCK, SWA_KV_HEADS, SWA_GROUP, SWA_HEAD_DIM)

    def band(t):
        tb = t.reshape(B, nblk, Q_BLOCK, SWA_KV_HEADS, SWA_HEAD_DIM)
        prev = jnp.pad(tb, ((0, 0), (1, 0), (0, 0), (0, 0), (0, 0)))[:, :-1]
        return jnp.concatenate([prev, tb], axis=2)

    kb, vb = band(k), band(v)
    s = jnp.einsum('bnqhgd,bnkhd->bnhgqk', qb, kb).astype(jnp.float32) * scale
    i = np.arange(Q_BLOCK)[:, None]
    j = np.arange(2 * Q_BLOCK)[None, :]
    dist = i + Q_BLOCK - j
    in_window = (dist >= 0) & (dist < WINDOW)
    blk = np.arange(nblk)[:, None, None]
    valid = in_window[None] & (blk * Q_BLOCK + j[None] - Q_BLOCK >= 0)
    bias = rel_bias.astype(jnp.float32)[t5_causal_bucket(dist)]
    bias = bias.reshape(Q_BLOCK, 2 * Q_BLOCK, SWA_KV_HEADS, SWA_GROUP).transpose(2, 3, 0, 1)
    s = jnp.where(valid[None, :, None, None], s + bias[None, None], NEG_INF)
    sink = sinks.astype(jnp.float32).reshape(SWA_KV_HEADS, SWA_GROUP)[None, None, :, :, None, None]
    m = jnp.maximum(jnp.max(s, axis=-1, keepdims=True), sink)
    p = jnp.exp(s - m)
    probs = p / (jnp.sum(p, axis=-1, keepdims=True) + jnp.exp(sink - m))
    out = jnp.einsum('bnhgqk,bnkhd->bnqhgd', probs.astype(v.dtype), vb)
    return out.reshape(B, S, SWA_HEADS * SWA_HEAD_DIM)


def setup_inputs(seed: int = 0) -> dict:
    key = jax.random.key(seed)
    ks = jax.random.split(key, 20)
    f32 = jnp.float32

    def w(k, shape, fan_in):
        return jax.random.normal(k, shape, f32) * fan_in ** -0.5

    def gain(k, shape):
        return 1.0 + 0.05 * jax.random.normal(k, shape, f32)

    x = jax.random.normal(ks[0], (BATCH, SEQ, D_MODEL), f32)
    positions = jnp.broadcast_to(jnp.arange(SEQ, dtype=jnp.int32)[None, :], (BATCH, SEQ))
    return {
        "x": x,
        "positions": positions,
        "g_mix_pre": gain(ks[1], (DEPTH, D_MODEL)),
        "w_in": w(ks[2], (DEPTH, D_MODEL, IN_WIDTH), D_MODEL),
        "g_cq": gain(ks[3], (DEPTH, Q_LORA_RANK)),
        "g_ckv": gain(ks[4], (DEPTH, KV_LORA_RANK)),
        "w_uq": w(ks[5], (DEPTH, Q_LORA_RANK, MLA_HEADS * MLA_QK_DIM), Q_LORA_RANK),
        "w_ukv": w(ks[6], (DEPTH, KV_LORA_RANK, MLA_HEADS * (MLA_NOPE_DIM + MLA_V_DIM)), KV_LORA_RANK),
        "sinks": 0.5 * jax.random.normal(ks[7], (DEPTH, SWA_HEADS), f32),
        "rel_bias": 0.5 * jax.random.normal(ks[8], (NUM_BUCKETS, SWA_HEADS), f32),
        "w_o": w(ks[9], (DEPTH, MIX_WIDTH, D_MODEL), MIX_WIDTH),
        "g_mix_post": gain(ks[10], (DEPTH, D_MODEL)),
        "g_ffn_pre": gain(ks[11], (DEPTH, D_MODEL)),
        "w_gate": w(ks[12], (DEPTH, D_MODEL, D_FF), D_MODEL),
        "w_up": w(ks[13], (DEPTH, D_MODEL, D_FF), D_MODEL),
        "w_down": w(ks[14], (DEPTH, D_FF, D_MODEL), D_FF),
        "g_ffn_post": gain(ks[15], (DEPTH, D_MODEL)),
    }


def reference(x, positions, g_mix_pre, w_in, g_cq, g_ckv, w_uq, w_ukv, sinks, rel_bias,
              w_o, g_mix_post, g_ffn_pre, w_gate, w_up, w_down, g_ffn_post):
    B, S, _ = x.shape
    split_idx = [int(c) for c in np.cumsum(IN_SIZES)[:-1]]
    for layer in range(DEPTH):
        h = rms_norm(x, g_mix_pre[layer])
        proj = h @ w_in[layer]
        c_q, c_kv, k_r, sq, sk, sv = jnp.split(proj, split_idx, axis=-1)

        c_q = rms_norm(c_q, g_cq[layer])
        c_kv = rms_norm(c_kv, g_ckv[layer])
        q = (c_q @ w_uq[layer]).reshape(B, S, MLA_HEADS, MLA_QK_DIM)
        q = jnp.concatenate([q[..., :MLA_NOPE_DIM], rope(q[..., MLA_NOPE_DIM:], positions)], axis=-1)
        kv = (c_kv @ w_ukv[layer]).reshape(B, S, MLA_HEADS, MLA_NOPE_DIM + MLA_V_DIM)
        k_nope, v_a = kv[..., :MLA_NOPE_DIM], kv[..., MLA_NOPE_DIM:]
        k_rope = rope(k_r[:, :, None, :], positions)
        k_a = jnp.concatenate(
            [k_nope, jnp.broadcast_to(k_rope, (B, S, MLA_HEADS, MLA_ROPE_DIM))], axis=-1)
        out_a = mla_attention(q, k_a, v_a)

        out_b = swa_attention(sq.reshape(B, S, SWA_HEADS, SWA_HEAD_DIM),
                              sk.reshape(B, S, SWA_KV_HEADS, SWA_HEAD_DIM),
                              sv.reshape(B, S, SWA_KV_HEADS, SWA_HEAD_DIM),
                              sinks[layer], rel_bias)

        mix = jnp.concatenate([out_a, out_b], axis=-1) @ w_o[layer]
        x = x + rms_norm(mix, g_mix_post[layer])

        h = rms_norm(x, g_ffn_pre[layer])
        ff = (jax.nn.silu(h @ w_gate[layer]) * (h @ w_up[layer])) @ w_down[layer]
        x = x + rms_norm(ff, g_ffn_post[layer])
    return x
```

```python
import functools
import math

import jax
import jax.numpy as jnp
import numpy as np
from jax import lax
from jax.experimental import pallas as pl
from jax.experimental.pallas import tpu as pltpu

F32 = jnp.float32
BF16 = jnp.bfloat16

MLA_HEADS = 8
MLA_NOPE = 128
MLA_ROPE = 64
MLA_QK = MLA_NOPE + MLA_ROPE
MLA_V = 128
Q_RANK = 512
KV_RANK = 512
ROPE_THETA = 10000.0
SWA_HEADS = 16
SWA_KV_HEADS = 2
SWA_GROUP = SWA_HEADS // SWA_KV_HEADS
SWA_DIM = 64
WINDOW = 128
NUM_BUCKETS = 32
MAX_DISTANCE = 128
BLK = 128
RMS_EPS = 1e-6
NEG = -1e30
LOG2E = math.log2(math.e)

LANES = 128
VMEM_LIMIT = 56 * 1024 * 1024


def _cparams(sem):
    return pltpu.CompilerParams(dimension_semantics=sem, vmem_limit_bytes=VMEM_LIMIT)


def _resident(shape):
    nd = len(shape)
    return pl.BlockSpec(shape, lambda *_: (0,) * nd, pipeline_mode=pl.Buffered(1))


def _proj_kernel(x_ref, g_ref, w_ref, gcq_ref, gckv_ref,
                 cq_ref, ckv_ref, sq_ref, sk_ref, sv_ref, kr_ref, *, swa_scale):
    x = x_ref[...]
    d = x.shape[-1]
    r = lax.rsqrt(jnp.sum(x * x, axis=-1, keepdims=True) * (1.0 / d) + RMS_EPS)
    h = (x * g_ref[...]).astype(BF16)
    y = jnp.dot(h, w_ref[...], preferred_element_type=F32) * r

    def latent_norm(c, gain_ref):
        rc = lax.rsqrt(jnp.sum(c * c, axis=-1, keepdims=True) * (1.0 / c.shape[-1]) + RMS_EPS)
        return (c * rc * gain_ref[...]).astype(BF16)

    o = 0
    cq_ref[...] = latent_norm(y[:, o:o + Q_RANK], gcq_ref); o += Q_RANK
    ckv_ref[...] = latent_norm(y[:, o:o + KV_RANK], gckv_ref); o += KV_RANK
    nq = SWA_HEADS * SWA_DIM
    sq_ref[...] = (y[:, o:o + nq] * swa_scale).astype(BF16); o += nq
    nkv = SWA_KV_HEADS * SWA_DIM
    sk_ref[...] = y[:, o:o + nkv].astype(BF16); o += nkv
    sv_ref[...] = y[:, o:o + nkv].astype(BF16); o += nkv
    kr_ref[...] = y[:, o:o + LANES]


def _proj(x2, g, w_in_p, g_cq, g_ckv, *, tm):
    s, d = x2.shape
    n = w_in_p.shape[1]
    nq = SWA_HEADS * SWA_DIM
    nkv = SWA_KV_HEADS * SWA_DIM
    row = lambda w: pl.BlockSpec((tm, w), lambda i: (i, 0))
    return pl.pallas_call(
        functools.partial(_proj_kernel, swa_scale=1.0 / math.sqrt(SWA_DIM)),
        grid=(s // tm,),
        in_specs=[row(d), _resident((1, d)), _resident((d, n)),
                  _resident((1, Q_RANK)), _resident((1, KV_RANK))],
        out_specs=[row(Q_RANK), row(KV_RANK), row(nq), row(nkv), row(nkv), row(LANES)],
        out_shape=[jax.ShapeDtypeStruct((s, Q_RANK), BF16),
                   jax.ShapeDtypeStruct((s, KV_RANK), BF16),
                   jax.ShapeDtypeStruct((s, nq), BF16),
                   jax.ShapeDtypeStruct((s, nkv), BF16),
                   jax.ShapeDtypeStruct((s, nkv), BF16),
                   jax.ShapeDtypeStruct((s, LANES), F32)],
        compiler_params=_cparams(("arbitrary",)),
        name="proj",
    )(x2, g, w_in_p, g_cq, g_ckv)


def _qkv_kernel(cq_ref, ckv_ref, kr_ref, pos_ref, invf_ref, wuq_ref, wukv_ref,
                q_ref, k_ref, v_ref, *, q_scale):
    tm = cq_ref.shape[0]
    half = MLA_ROPE // 2
    ang = pos_ref[...].astype(F32) * invf_ref[...]
    cos = jnp.cos(ang)
    sin = jnp.sin(ang)
    lane = lax.broadcasted_iota(jnp.int32, (tm, LANES), 1)
    first = (lane & half) == 0
    sin_signed = jnp.where(first, -sin, sin)

    def rope(t):
        swapped = jnp.where(first, pltpu.roll(t, LANES - half, 1), pltpu.roll(t, half, 1))
        return t * cos + swapped * sin_signed

    q = jnp.dot(cq_ref[...], wuq_ref[...], preferred_element_type=F32) * q_scale
    kv = jnp.dot(ckv_ref[...], wukv_ref[...], preferred_element_type=F32)
    kr = rope(kr_ref[...])[:, :MLA_ROPE].astype(BF16)

    nope_w = MLA_HEADS * MLA_NOPE
    for h in range(MLA_HEADS):
        q_ref[h, :, 0:MLA_NOPE] = q[:, h * MLA_NOPE:(h + 1) * MLA_NOPE].astype(BF16)
        k_ref[h, :, 0:MLA_NOPE] = kv[:, h * MLA_NOPE:(h + 1) * MLA_NOPE].astype(BF16)
        k_ref[h, :, MLA_NOPE:MLA_QK] = kr
    for hp in range(MLA_HEADS // 2):
        rp = rope(q[:, nope_w + hp * LANES:nope_w + (hp + 1) * LANES])
        q_ref[2 * hp, :, MLA_NOPE:MLA_QK] = rp[:, :MLA_ROPE].astype(BF16)
        q_ref[2 * hp + 1, :, MLA_NOPE:MLA_QK] = rp[:, MLA_ROPE:].astype(BF16)
    v_ref[...] = kv[:, nope_w:].astype(BF16)


def _qkv(cq, ckv, kr, pos_col, invf, wuq_p, wukv_p, *, tm):
    s = cq.shape[0]
    row = lambda w: pl.BlockSpec((tm, w), lambda i: (i, 0))
    head = pl.BlockSpec((MLA_HEADS, tm, MLA_QK), lambda i: (0, i, 0))
    vw = MLA_HEADS * MLA_V
    return pl.pallas_call(
        functools.partial(_qkv_kernel, q_scale=LOG2E / math.sqrt(MLA_QK)),
        grid=(s // tm,),
        in_specs=[row(Q_RANK), row(KV_RANK), row(LANES), row(1), _resident((1, LANES)),
                  _resident(wuq_p.shape), _resident(wukv_p.shape)],
        out_specs=[head, head, row(vw)],
        out_shape=[jax.ShapeDtypeStruct((MLA_HEADS, s, MLA_QK), BF16),
                   jax.ShapeDtypeStruct((MLA_HEADS, s, MLA_QK), BF16),
                   jax.ShapeDtypeStruct((s, vw), BF16)],
        compiler_params=_cparams(("arbitrary",)),
        name="qkv",
    )(cq, ckv, kr, pos_col, invf, wuq_p, wukv_p)


def _mla_kernel(q_ref, k_ref, v_ref, o_ref, m_sc, l_sc, acc_sc, *, tq, tk):
    qi = pl.program_id(1)
    q = q_ref[0]
    m_sc[...] = jnp.full(m_sc.shape, NEG, F32)
    l_sc[...] = jnp.zeros(l_sc.shape, F32)
    acc_sc[...] = jnp.zeros(acc_sc.shape, F32)

    def step(j, masked):
        start = pl.multiple_of(j * tk, tk)
        k = k_ref[0, pl.ds(start, tk), :]
        v = v_ref[pl.ds(start, tk), :]
        s = lax.dot_general(q, k, (((1,), (1,)), ((), ())), preferred_element_type=F32)
        if masked:
            row = lax.broadcasted_iota(jnp.int32, (tq, tk), 0) + qi * tq
            col = lax.broadcasted_iota(jnp.int32, (tq, tk), 1) + start
            s = jnp.where(col <= row, s, NEG)
        m_old = m_sc[...]
        m_new = jnp.maximum(m_old, jnp.max(s, axis=-1, keepdims=True))
        alpha = jnp.exp2(m_old - m_new)
        p = jnp.exp2(s - m_new)
        l_sc[...] = alpha * l_sc[...] + jnp.sum(p, axis=-1, keepdims=True)
        acc_sc[...] = alpha * acc_sc[...] + jnp.dot(p.astype(BF16), v, preferred_element_type=F32)
        m_sc[...] = m_new

    ratio = tq // tk
    lax.fori_loop(0, qi * ratio, lambda j, c: (step(j, False), c)[1], 0)
    for t in range(ratio):
        step(qi * ratio + t, True)
    o_ref[...] = (acc_sc[...] / l_sc[...]).astype(o_ref.dtype)


def _mla(q, k, v, *, tq, tk):
    _, s, _ = q.shape
    return pl.pallas_call(
        functools.partial(_mla_kernel, tq=tq, tk=tk),
        grid=(MLA_HEADS, s // tq),
        in_specs=[pl.BlockSpec((1, tq, MLA_QK), lambda h, i: (h, i, 0)),
                  pl.BlockSpec((1, s, MLA_QK), lambda h, i: (h, 0, 0)),
                  pl.BlockSpec((s, MLA_V), lambda h, i: (0, h))],
        out_specs=pl.BlockSpec((tq, MLA_V), lambda h, i: (i, h)),
        out_shape=jax.ShapeDtypeStruct((s, MLA_HEADS * MLA_V), BF16),
        scratch_shapes=[pltpu.VMEM((tq, 1), F32), pltpu.VMEM((tq, 1), F32),
                        pltpu.VMEM((tq, MLA_V), F32)],
        compiler_params=_cparams(("arbitrary", "arbitrary")),
        name="mla",
    )(q, k, v)


def _t5_bucket_table():
    i = np.arange(BLK)[:, None]
    j = np.arange(2 * BLK)[None, :]
    dist = i + BLK - j
    max_exact = NUM_BUCKETS // 2
    d = np.maximum(dist, 0)
    large = max_exact + (np.log(np.maximum(d, 1) / max_exact)
                         / np.log(MAX_DISTANCE / max_exact)
                         * (NUM_BUCKETS - max_exact)).astype(np.int32)
    large = np.minimum(large, NUM_BUCKETS - 1)
    bucket = np.where(d < max_exact, d, large).astype(np.int32)
    in_window = (dist >= 0) & (dist < WINDOW)
    return np.where(in_window, bucket, -1).astype(np.int32)


def _swa_kernel(relb_ref, sink_ref, bucket_ref, q_ref, kc_ref, kp_ref, vc_ref, vp_ref,
                o_ref, bias_sc):
    i = pl.program_id(0)

    @pl.when(i == 0)
    def _():
        bucket = bucket_ref[...]
        for h in range(SWA_HEADS):
            b = jnp.full(bucket.shape, NEG, F32)
            for t in range(NUM_BUCKETS):
                b = jnp.where(bucket == t, relb_ref[t, h], b)
            bias_sc[h] = b

    col = lax.broadcasted_iota(jnp.int32, (BLK, 2 * BLK), 1)
    band_ok = jnp.logical_or(col >= BLK, i > 0)
    q = q_ref[...]
    kc = kc_ref[...]
    kp = kp_ref[...]
    vc = vc_ref[...]
    vp = vp_ref[...]
    outs = []
    for c in range(SWA_KV_HEADS):
        ks = slice(c * SWA_DIM, (c + 1) * SWA_DIM)
        kband = jnp.concatenate([kp[:, ks], kc[:, ks]], axis=0)
        vband = jnp.concatenate([vp[:, ks], vc[:, ks]], axis=0)
        for g in range(SWA_GROUP):
            h = c * SWA_GROUP + g
            qh = q[:, h * SWA_DIM:(h + 1) * SWA_DIM]
            s = lax.dot_general(qh, kband, (((1,), (1,)), ((), ())), preferred_element_type=F32)
            s = jnp.where(band_ok, s + bias_sc[h], NEG)
            sink = sink_ref[h]
            m = jnp.maximum(jnp.max(s, axis=-1, keepdims=True), sink)
            p = jnp.exp(s - m)
            denom = jnp.sum(p, axis=-1, keepdims=True) + jnp.exp(sink - m)
            o = jnp.dot(p.astype(BF16), vband, preferred_element_type=F32) / denom
            outs.append(o)
    o_ref[...] = jnp.concatenate(outs, axis=-1).astype(o_ref.dtype)


def _swa(sq, sk, sv, sinks, rel_bias):
    s = sq.shape[0]
    nq = SWA_HEADS * SWA_DIM
    nkv = SWA_KV_HEADS * SWA_DIM
    bucket = jnp.asarray(_t5_bucket_table())
    smem = pl.BlockSpec(memory_space=pltpu.SMEM)
    cur = lambda w: pl.BlockSpec((BLK, w), lambda i: (i, 0))
    prev = lambda w: pl.BlockSpec((BLK, w), lambda i: (jnp.maximum(i - 1, 0), 0))
    return pl.pallas_call(
        _swa_kernel,
        grid=(s // BLK,),
        in_specs=[smem, smem, _resident((BLK, 2 * BLK)),
                  cur(nq), cur(nkv), prev(nkv), cur(nkv), prev(nkv)],
        out_specs=cur(nq),
        out_shape=jax.ShapeDtypeStruct((s, nq), BF16),
        scratch_shapes=[pltpu.VMEM((SWA_HEADS, BLK, 2 * BLK), F32)],
        compiler_params=_cparams(("arbitrary",)),
        name="swa",
    )(rel_bias, sinks, bucket, sq, sk, sk, sv, sv)


def _rms(y, gain):
    r = lax.rsqrt(jnp.sum(y * y, axis=-1, keepdims=True) * (1.0 / y.shape[-1]) + RMS_EPS)
    return y * r * gain


def _oproj_kernel(a_ref, b_ref, wa_ref, wb_ref, x_ref, gpost_ref, gpre_ref, x1_ref, h_ref):
    mix = (jnp.dot(a_ref[...], wa_ref[...], preferred_element_type=F32)
           + jnp.dot(b_ref[...], wb_ref[...], preferred_element_type=F32))
    x1 = x_ref[...] + _rms(mix, gpost_ref[...])
    x1_ref[...] = x1
    h_ref[...] = _rms(x1, gpre_ref[...]).astype(BF16)


def _oproj(out_a, out_b, wo_a, wo_b, x2, g_post, g_pre, *, tm):
    s, d = x2.shape
    row = lambda w: pl.BlockSpec((tm, w), lambda i: (i, 0))
    return pl.pallas_call(
        _oproj_kernel,
        grid=(s // tm,),
        in_specs=[row(out_a.shape[1]), row(out_b.shape[1]), _resident(wo_a.shape),
                  _resident(wo_b.shape), row(d), _resident((1, d)), _resident((1, d))],
        out_specs=[row(d), row(d)],
        out_shape=[jax.ShapeDtypeStruct((s, d), F32), jax.ShapeDtypeStruct((s, d), BF16)],
        compiler_params=_cparams(("arbitrary",)),
        name="oproj",
    )(out_a, out_b, wo_a, wo_b, x2, g_post, g_pre)


def _ffn_kernel(h_ref, wg_ref, wu_ref, wd_ref, x1_ref, gpost_ref, o_ref, acc_sc):
    j = pl.program_id(1)
    h = h_ref[...]
    gate = jnp.dot(h, wg_ref[...], preferred_element_type=F32)
    up = jnp.dot(h, wu_ref[...], preferred_element_type=F32)
    act = (gate * jax.nn.sigmoid(gate) * up).astype(BF16)
    part = jnp.dot(act, wd_ref[...], preferred_element_type=F32)

    @pl.when(j == 0)
    def _():
        acc_sc[...] = part

    @pl.when(j > 0)
    def _():
        acc_sc[...] += part

    @pl.when(j == pl.num_programs(1) - 1)
    def _():
        o_ref[...] = x1_ref[...] + _rms(acc_sc[...], gpost_ref[...])


def _ffn(h, wg, wu, wd, x1, g_post, *, tm, tf):
    s, d = x1.shape
    dff = wg.shape[1]
    return pl.pallas_call(
        _ffn_kernel,
        grid=(s // tm, dff // tf),
        in_specs=[pl.BlockSpec((tm, d), lambda i, j: (i, 0)),
                  pl.BlockSpec((d, tf), lambda i, j: (0, j)),
                  pl.BlockSpec((d, tf), lambda i, j: (0, j)),
                  pl.BlockSpec((tf, d), lambda i, j: (j, 0)),
                  pl.BlockSpec((tm, d), lambda i, j: (i, 0)),
                  _resident((1, d))],
        out_specs=pl.BlockSpec((tm, d), lambda i, j: (i, 0)),
        out_shape=jax.ShapeDtypeStruct((s, d), F32),
        scratch_shapes=[pltpu.VMEM((tm, d), F32)],
        compiler_params=_cparams(("arbitrary", "arbitrary")),
        name="ffn",
    )(h, wg, wu, wd, x1, g_post)


def _permute_w_in(w_in):
    sizes = (Q_RANK, KV_RANK, MLA_ROPE, SWA_HEADS * SWA_DIM,
             SWA_KV_HEADS * SWA_DIM, SWA_KV_HEADS * SWA_DIM)
    offs = np.concatenate([[0], np.cumsum(sizes)])
    part = lambda k: w_in[:, offs[k]:offs[k + 1]]
    pad = jnp.zeros((w_in.shape[0], LANES - MLA_ROPE), w_in.dtype)
    return jnp.concatenate([part(0), part(1), part(3), part(4), part(5), part(2), pad], axis=1)


def _permute_w_uq(w_uq):
    w = w_uq.reshape(w_uq.shape[0], MLA_HEADS, MLA_QK)
    nope = w[:, :, :MLA_NOPE].reshape(w_uq.shape[0], MLA_HEADS * MLA_NOPE)
    rope = w[:, :, MLA_NOPE:].reshape(w_uq.shape[0], MLA_HEADS * MLA_ROPE)
    return jnp.concatenate([nope, rope], axis=1)


def _permute_w_ukv(w_ukv):
    w = w_ukv.reshape(w_ukv.shape[0], MLA_HEADS, MLA_NOPE + MLA_V)
    kn = w[:, :, :MLA_NOPE].reshape(w_ukv.shape[0], MLA_HEADS * MLA_NOPE)
    vv = w[:, :, MLA_NOPE:].reshape(w_ukv.shape[0], MLA_HEADS * MLA_V)
    return jnp.concatenate([kn, vv], axis=1)


def _layer(x2, pos_col, p):
    half = MLA_ROPE // 2
    inv_freq = ROPE_THETA ** (-jnp.arange(half, dtype=F32) / half)
    invf = jnp.tile(inv_freq, LANES // half)[None, :]

    row = lambda a: a[None, :]
    cq, ckv, sq, sk, sv, kr = _proj(
        x2, row(p["g_mix_pre"]), _permute_w_in(p["w_in"]).astype(BF16),
        row(p["g_cq"]), row(p["g_ckv"]), tm=256)
    q, k, v = _qkv(cq, ckv, kr, pos_col, invf,
                   _permute_w_uq(p["w_uq"]).astype(BF16),
                   _permute_w_ukv(p["w_ukv"]).astype(BF16), tm=256)
    out_a = _mla(q, k, v, tq=512, tk=512)
    out_b = _swa(sq, sk, sv, p["sinks"], p["rel_bias"])
    na = MLA_HEADS * MLA_V
    w_o = p["w_o"].astype(BF16)
    x1, h = _oproj(out_a, out_b, w_o[:na], w_o[na:], x2,
                   row(p["g_mix_post"]), row(p["g_ffn_pre"]), tm=256)
    return _ffn(h, p["w_gate"].astype(BF16), p["w_up"].astype(BF16),
                p["w_down"].astype(BF16), x1, row(p["g_ffn_post"]), tm=512, tf=512)


def kernel(x, positions, g_mix_pre, w_in, g_cq, g_ckv, w_uq, w_ukv, sinks, rel_bias,
           w_o, g_mix_post, g_ffn_pre, w_gate, w_up, w_down, g_ffn_post):
    b, s, d = x.shape
    assert b == 1, "the row-major (S, D) pipeline assumes a single sequence"
    x2 = x.reshape(s, d)
    pos_col = positions.reshape(s, 1)
    for layer in range(w_in.shape[0]):
        p = dict(g_mix_pre=g_mix_pre[layer], w_in=w_in[layer], g_cq=g_cq[layer],
                 g_ckv=g_ckv[layer], w_uq=w_uq[layer], w_ukv=w_ukv[layer],
                 sinks=sinks[layer], rel_bias=rel_bias, w_o=w_o[layer],
                 g_mix_post=g_mix_post[layer], g_ffn_pre=g_ffn_pre[layer],
                 w_gate=w_gate[layer], w_up=w_up[layer], w_down=w_down[layer],
                 g_ffn_post=g_ffn_post[layer])
        x2 = _layer(x2, pos_col, p)
    return x2.reshape(b, s, d)
```

```python
import functools
import math

import jax
import jax.numpy as jnp
import numpy as np
from jax import lax
from jax.experimental import pallas as pl
from jax.experimental.pallas import tpu as pltpu

F32 = jnp.float32
BF16 = jnp.bfloat16

MLA_HEADS = 8
MLA_NOPE = 128
MLA_ROPE = 64
MLA_QK = MLA_NOPE + MLA_ROPE
MLA_V = 128
Q_RANK = 512
KV_RANK = 512
ROPE_THETA = 10000.0
SWA_HEADS = 16
SWA_KV_HEADS = 2
SWA_GROUP = SWA_HEADS // SWA_KV_HEADS
SWA_DIM = 64
WINDOW = 128
NUM_BUCKETS = 32
MAX_DISTANCE = 128
BLK = 128
RMS_EPS = 1e-6
NEG = -1e30
LOG2E = math.log2(math.e)

LANES = 128
VMEM_LIMIT = 56 * 1024 * 1024


def _cparams(sem):
    return pltpu.CompilerParams(dimension_semantics=sem, vmem_limit_bytes=VMEM_LIMIT)


def _resident(shape):
    nd = len(shape)
    return pl.BlockSpec(shape, lambda *_: (0,) * nd, pipeline_mode=pl.Buffered(1))


def _proj_kernel(x_ref, g_ref, w_ref, gcq_ref, gckv_ref,
                 cq_ref, ckv_ref, sq_ref, sk_ref, sv_ref, krt_ref, *, swa_scale):
    x = x_ref[...]
    d = x.shape[-1]
    r = lax.rsqrt(jnp.sum(x * x, axis=-1, keepdims=True) * (1.0 / d) + RMS_EPS)
    h = (x * g_ref[...]).astype(BF16)
    y = jnp.dot(h, w_ref[...], preferred_element_type=F32) * r

    def latent_norm(c, gain_ref):
        rc = lax.rsqrt(jnp.sum(c * c, axis=-1, keepdims=True) * (1.0 / c.shape[-1]) + RMS_EPS)
        return (c * rc * gain_ref[...]).astype(BF16)

    o = 0
    cq_ref[...] = latent_norm(y[:, o:o + Q_RANK], gcq_ref); o += Q_RANK
    ckv_ref[...] = latent_norm(y[:, o:o + KV_RANK], gckv_ref); o += KV_RANK
    nq = SWA_HEADS * SWA_DIM
    sq_ref[...] = (y[:, o:o + nq] * swa_scale).astype(BF16); o += nq
    nkv = SWA_KV_HEADS * SWA_DIM
    sk_ref[...] = y[:, o:o + nkv].astype(BF16); o += nkv
    sv_ref[...] = y[:, o:o + nkv].astype(BF16); o += nkv
    krt_ref[...] = y[:, o:o + LANES].T


def _proj(x2, g, w_in_p, g_cq, g_ckv, *, tm):
    s, d = x2.shape
    n = w_in_p.shape[1]
    nq = SWA_HEADS * SWA_DIM
    nkv = SWA_KV_HEADS * SWA_DIM
    row = lambda w: pl.BlockSpec((tm, w), lambda i: (i, 0))
    return pl.pallas_call(
        functools.partial(_proj_kernel, swa_scale=1.0 / math.sqrt(SWA_DIM)),
        grid=(s // tm,),
        in_specs=[row(d), _resident((1, d)), _resident((d, n)),
                  _resident((1, Q_RANK)), _resident((1, KV_RANK))],
        out_specs=[row(Q_RANK), row(KV_RANK), row(nq), row(nkv), row(nkv),
                   pl.BlockSpec((LANES, tm), lambda i: (0, i))],
        out_shape=[jax.ShapeDtypeStruct((s, Q_RANK), BF16),
                   jax.ShapeDtypeStruct((s, KV_RANK), BF16),
                   jax.ShapeDtypeStruct((s, nq), BF16),
                   jax.ShapeDtypeStruct((s, nkv), BF16),
                   jax.ShapeDtypeStruct((s, nkv), BF16),
                   jax.ShapeDtypeStruct((LANES, s), F32)],
        compiler_params=_cparams(("arbitrary",)),
        name="proj",
    )(x2, g, w_in_p, g_cq, g_ckv)


_NT = (((1,), (1,)), ((), ()))


def _qkv_kernel(cq_ref, ckv_ref, krt_ref, pos_ref, invf_ref, wuqt_ref, wuk_ref, wuvt_ref,
                qt_ref, k_ref, vt_ref, *, q_scale):
    tm = cq_ref.shape[0]
    half = MLA_ROPE // 2
    ang = invf_ref[...] * pos_ref[...].astype(F32)
    cos = jnp.cos(ang)
    sin = jnp.sin(ang)

    def rope_t(t):
        t1, t2 = t[:half], t[half:]
        return jnp.concatenate([t1 * cos - t2 * sin, t2 * cos + t1 * sin], axis=0)

    cq = cq_ref[...]
    ckv = ckv_ref[...]
    qt = lax.dot_general(wuqt_ref[...], cq, _NT, preferred_element_type=F32) * q_scale
    for h in range(MLA_HEADS):
        base = h * MLA_QK
        qt_ref[h, 0:MLA_NOPE, :] = qt[base:base + MLA_NOPE].astype(BF16)
        qt_ref[h, MLA_NOPE:MLA_QK, :] = rope_t(qt[base + MLA_NOPE:base + MLA_QK]).astype(BF16)

    krt = krt_ref[...]
    kr = jnp.concatenate([rope_t(krt[:MLA_ROPE]), krt[MLA_ROPE:]], axis=0).T[:, :MLA_ROPE].astype(BF16)
    kn = jnp.dot(ckv, wuk_ref[...], preferred_element_type=F32)
    vt = lax.dot_general(wuvt_ref[...], ckv, _NT, preferred_element_type=F32)
    for h in range(MLA_HEADS):
        k_ref[h, :, 0:MLA_NOPE] = kn[:, h * MLA_NOPE:(h + 1) * MLA_NOPE].astype(BF16)
        k_ref[h, :, MLA_NOPE:MLA_QK] = kr
        for c in range(tm // LANES):
            vt_ref[h, c] = vt[h * MLA_V:(h + 1) * MLA_V, c * LANES:(c + 1) * LANES].astype(BF16)


def _qkv(cq, ckv, krt, pos_row, invf, wuqt, wuk, wuvt, *, tm):
    s = cq.shape[0]
    row = lambda w: pl.BlockSpec((tm, w), lambda i: (i, 0))
    col = lambda r: pl.BlockSpec((r, tm), lambda i: (0, i))
    nb = tm // LANES
    return pl.pallas_call(
        functools.partial(_qkv_kernel, q_scale=LOG2E / math.sqrt(MLA_QK)),
        grid=(s // tm,),
        in_specs=[row(Q_RANK), row(KV_RANK), col(LANES), col(1), _resident(invf.shape),
                  _resident(wuqt.shape), _resident(wuk.shape), _resident(wuvt.shape)],
        out_specs=[pl.BlockSpec((MLA_HEADS, MLA_QK, tm), lambda i: (0, 0, i)),
                   pl.BlockSpec((MLA_HEADS, tm, MLA_QK), lambda i: (0, i, 0)),
                   pl.BlockSpec((MLA_HEADS, nb, MLA_V, LANES), lambda i: (0, i, 0, 0))],
        out_shape=[jax.ShapeDtypeStruct((MLA_HEADS, MLA_QK, s), BF16),
                   jax.ShapeDtypeStruct((MLA_HEADS, s, MLA_QK), BF16),
                   jax.ShapeDtypeStruct((MLA_HEADS, s // LANES, MLA_V, LANES), BF16)],
        compiler_params=_cparams(("arbitrary",)),
        name="qkv",
    )(cq, ckv, krt, pos_row, invf, wuqt, wuk, wuvt)


def _mla_kernel(qt_ref, k_ref, vt_ref, o_ref, m_sc, l_sc, acc_sc, s_a, s_b, *, tq, tk):
    assert tq == 2 * tk
    qi = pl.program_id(1)
    qt = qt_ref[0]
    nsub = tk // LANES
    m_sc[...] = jnp.full(m_sc.shape, NEG, F32)
    l_sc[...] = jnp.zeros(l_sc.shape, F32)
    acc_sc[...] = jnp.zeros(acc_sc.shape, F32)

    def scores(j, s_ref):
        start = pl.multiple_of(j * tk, tk)
        s_ref[...] = jnp.dot(k_ref[0, pl.ds(start, tk), :], qt, preferred_element_type=F32)

    def update(j, s_ref, masked=False):
        s = s_ref[...]
        if masked:
            key = lax.broadcasted_iota(jnp.int32, (tk, tq), 0) + j * tk
            qry = lax.broadcasted_iota(jnp.int32, (tk, tq), 1) + qi * tq
            s = jnp.where(key <= qry, s, NEG)
        m_old = m_sc[...]
        m_new = jnp.maximum(m_old, jnp.max(s, axis=0, keepdims=True))
        alpha = jnp.exp2(m_old - m_new)
        p = jnp.exp2(s - m_new)
        l_sc[...] = alpha * l_sc[...] + jnp.sum(p, axis=0, keepdims=True)
        vt = jnp.concatenate([vt_ref[0, j * nsub + c] for c in range(nsub)], axis=1)
        acc_sc[...] = alpha * acc_sc[...] + jnp.dot(vt, p.astype(BF16), preferred_element_type=F32)
        m_sc[...] = m_new

    def pair(i, carry):
        scores(2 * i + 1, s_b)
        update(2 * i, s_a)
        scores(2 * i + 2, s_a)
        update(2 * i + 1, s_b)
        return carry

    scores(0, s_a)
    lax.fori_loop(0, qi, pair, 0)
    scores(2 * qi + 1, s_b)
    update(2 * qi, s_a, masked=True)
    update(2 * qi + 1, s_b, masked=True)
    o_ref[...] = (acc_sc[...] / l_sc[...]).T.astype(o_ref.dtype)


def _mla(qt, k, vt, *, tq, tk):
    _, s, _ = k.shape
    return pl.pallas_call(
        functools.partial(_mla_kernel, tq=tq, tk=tk),
        grid=(MLA_HEADS, s // tq),
        in_specs=[pl.BlockSpec((1, MLA_QK, tq), lambda h, i: (h, 0, i)),
                  pl.BlockSpec((1, s, MLA_QK), lambda h, i: (h, 0, 0)),
                  pl.BlockSpec((1, s // LANES, MLA_V, LANES), lambda h, i: (h, 0, 0, 0))],
        out_specs=pl.BlockSpec((tq, MLA_V), lambda h, i: (i, h)),
        out_shape=jax.ShapeDtypeStruct((s, MLA_HEADS * MLA_V), BF16),
        scratch_shapes=[pltpu.VMEM((1, tq), F32), pltpu.VMEM((1, tq), F32),
                        pltpu.VMEM((MLA_V, tq), F32),
                        pltpu.VMEM((tk, tq), F32), pltpu.VMEM((tk, tq), F32)],
        compiler_params=_cparams(("arbitrary", "arbitrary")),
        name="mla",
    )(qt, k, vt)


def _t5_bucket_table():
    i = np.arange(BLK)[:, None]
    j = np.arange(2 * BLK)[None, :]
    dist = i + BLK - j
    max_exact = NUM_BUCKETS // 2
    d = np.maximum(dist, 0)
    large = max_exact + (np.log(np.maximum(d, 1) / max_exact)
                         / np.log(MAX_DISTANCE / max_exact)
                         * (NUM_BUCKETS - max_exact)).astype(np.int32)
    large = np.minimum(large, NUM_BUCKETS - 1)
    bucket = np.where(d < max_exact, d, large).astype(np.int32)
    in_window = (dist >= 0) & (dist < WINDOW)
    return np.where(in_window, bucket, -1).astype(np.int32)


def _swa_kernel(relb_ref, sink_ref, bucket_ref, q_ref, kc_ref, kp_ref, vc_ref, vp_ref,
                o_ref, bias_sc):
    i = pl.program_id(0)

    @pl.when(i == 0)
    def _():
        bucket = bucket_ref[...]
        for h in range(SWA_HEADS):
            b = jnp.full(bucket.shape, NEG, F32)
            for t in range(NUM_BUCKETS):
                b = jnp.where(bucket == t, relb_ref[t, h], b)
            bias_sc[h] = b

    col = lax.broadcasted_iota(jnp.int32, (BLK, 2 * BLK), 1)
    band_ok = jnp.logical_or(col >= BLK, i > 0)
    q = q_ref[...]
    kc = kc_ref[...]
    kp = kp_ref[...]
    vc = vc_ref[...]
    vp = vp_ref[...]
    outs = []
    for c in range(SWA_KV_HEADS):
        ks = slice(c * SWA_DIM, (c + 1) * SWA_DIM)
        kband = jnp.concatenate([kp[:, ks], kc[:, ks]], axis=0)
        vband = jnp.concatenate([vp[:, ks], vc[:, ks]], axis=0)
        for g in range(SWA_GROUP):
            h = c * SWA_GROUP + g
            qh = q[:, h * SWA_DIM:(h + 1) * SWA_DIM]
            s = lax.dot_general(qh, kband, _NT, preferred_element_type=F32)
            s = jnp.where(band_ok, s + bias_sc[h], NEG)
            sink = sink_ref[h]
            m = jnp.maximum(jnp.max(s, axis=-1, keepdims=True), sink)
            p = jnp.exp(s - m)
            denom = jnp.sum(p, axis=-1, keepdims=True) + jnp.exp(sink - m)
            o = jnp.dot(p.astype(BF16), vband, preferred_element_type=F32) / denom
            outs.append(o)
    o_ref[...] = jnp.concatenate(outs, axis=-1).astype(o_ref.dtype)


def _swa(sq, sk, sv, sinks, rel_bias):
    s = sq.shape[0]
    nq = SWA_HEADS * SWA_DIM
    nkv = SWA_KV_HEADS * SWA_DIM
    bucket = jnp.asarray(_t5_bucket_table())
    smem = pl.BlockSpec(memory_space=pltpu.SMEM)
    cur = lambda w: pl.BlockSpec((BLK, w), lambda i: (i, 0))
    prev = lambda w: pl.BlockSpec((BLK, w), lambda i: (jnp.maximum(i - 1, 0), 0))
    return pl.pallas_call(
        _swa_kernel,
        grid=(s // BLK,),
        in_specs=[smem, smem, _resident((BLK, 2 * BLK)),
                  cur(nq), cur(nkv), prev(nkv), cur(nkv), prev(nkv)],
        out_specs=cur(nq),
        out_shape=jax.ShapeDtypeStruct((s, nq), BF16),
        scratch_shapes=[pltpu.VMEM((SWA_HEADS, BLK, 2 * BLK), F32)],
        compiler_params=_cparams(("arbitrary",)),
        name="swa",
    )(rel_bias, sinks, bucket, sq, sk, sk, sv, sv)


def _rms(y, gain):
    r = lax.rsqrt(jnp.sum(y * y, axis=-1, keepdims=True) * (1.0 / y.shape[-1]) + RMS_EPS)
    return y * r * gain


def _oproj_kernel(a_ref, b_ref, wa_ref, wb_ref, x_ref, gpost_ref, gpre_ref, x1_ref, h_ref):
    mix = (jnp.dot(a_ref[...], wa_ref[...], preferred_element_type=F32)
           + jnp.dot(b_ref[...], wb_ref[...], preferred_element_type=F32))
    x1 = x_ref[...] + _rms(mix, gpost_ref[...])
    x1_ref[...] = x1
    h_ref[...] = _rms(x1, gpre_ref[...]).astype(BF16)


def _oproj(out_a, out_b, wo_a, wo_b, x2, g_post, g_pre, *, tm):
    s, d = x2.shape
    row = lambda w: pl.BlockSpec((tm, w), lambda i: (i, 0))
    return pl.pallas_call(
        _oproj_kernel,
        grid=(s // tm,),
        in_specs=[row(out_a.shape[1]), row(out_b.shape[1]), _resident(wo_a.shape),
                  _resident(wo_b.shape), row(d), _resident((1, d)), _resident((1, d))],
        out_specs=[row(d), row(d)],
        out_shape=[jax.ShapeDtypeStruct((s, d), F32), jax.ShapeDtypeStruct((s, d), BF16)],
        compiler_params=_cparams(("arbitrary",)),
        name="oproj",
    )(out_a, out_b, wo_a, wo_b, x2, g_post, g_pre)


def _ffn_kernel(h_ref, wg_ref, wu_ref, wd_ref, x1_ref, gpost_ref, o_ref, acc_sc):
    j = pl.program_id(1)
    h = h_ref[...]
    gate = jnp.dot(h, wg_ref[...], preferred_element_type=F32)
    up = jnp.dot(h, wu_ref[...], preferred_element_type=F32)
    act = (gate * jax.nn.sigmoid(gate) * up).astype(BF16)
    part = jnp.dot(act, wd_ref[...], preferred_element_type=F32)

    @pl.when(j == 0)
    def _():
        acc_sc[...] = part

    @pl.when(j > 0)
    def _():
        acc_sc[...] += part

    @pl.when(j == pl.num_programs(1) - 1)
    def _():
        o_ref[...] = x1_ref[...] + _rms(acc_sc[...], gpost_ref[...])


def _ffn(h, wg, wu, wd, x1, g_post, *, tm, tf):
    s, d = x1.shape
    dff = wg.shape[1]
    return pl.pallas_call(
        _ffn_kernel,
        grid=(s // tm, dff // tf),
        in_specs=[pl.BlockSpec((tm, d), lambda i, j: (i, 0)),
                  pl.BlockSpec((d, tf), lambda i, j: (0, j)),
                  pl.BlockSpec((d, tf), lambda i, j: (0, j)),
                  pl.BlockSpec((tf, d), lambda i, j: (j, 0)),
                  pl.BlockSpec((tm, d), lambda i, j: (i, 0)),
                  _resident((1, d))],
        out_specs=pl.BlockSpec((tm, d), lambda i, j: (i, 0)),
        out_shape=jax.ShapeDtypeStruct((s, d), F32),
        scratch_shapes=[pltpu.VMEM((tm, d), F32)],
        compiler_params=_cparams(("arbitrary", "arbitrary")),
        name="ffn",
    )(h, wg, wu, wd, x1, g_post)


def _permute_w_in(w_in):
    sizes = (Q_RANK, KV_RANK, MLA_ROPE, SWA_HEADS * SWA_DIM,
             SWA_KV_HEADS * SWA_DIM, SWA_KV_HEADS * SWA_DIM)
    offs = np.concatenate([[0], np.cumsum(sizes)])
    part = lambda k: w_in[:, offs[k]:offs[k + 1]]
    pad = jnp.zeros((w_in.shape[0], LANES - MLA_ROPE), w_in.dtype)
    return jnp.concatenate([part(0), part(1), part(3), part(4), part(5), part(2), pad], axis=1)


def _layer(x2, pos_row, p, *, tm_qkv=256):
    half = MLA_ROPE // 2
    inv_freq = ROPE_THETA ** (-jnp.arange(half, dtype=F32) / half)
    invf = jnp.broadcast_to(inv_freq[:, None], (half, tm_qkv))

    row = lambda a: a[None, :]
    cq, ckv, sq, sk, sv, krt = _proj(
        x2, row(p["g_mix_pre"]), _permute_w_in(p["w_in"]).astype(BF16),
        row(p["g_cq"]), row(p["g_ckv"]), tm=256)
    w_ukv = p["w_ukv"].reshape(KV_RANK, MLA_HEADS, MLA_NOPE + MLA_V)
    wuk = w_ukv[:, :, :MLA_NOPE].reshape(KV_RANK, MLA_HEADS * MLA_NOPE)
    wuvt = w_ukv[:, :, MLA_NOPE:].reshape(KV_RANK, MLA_HEADS * MLA_V).T
    qt, k, vt = _qkv(cq, ckv, krt, pos_row, invf, p["w_uq"].T.astype(BF16),
                     wuk.astype(BF16), wuvt.astype(BF16), tm=tm_qkv)
    out_a = _mla(qt, k, vt, tq=1024, tk=512)
    out_b = _swa(sq, sk, sv, p["sinks"], p["rel_bias"])
    na = MLA_HEADS * MLA_V
    w_o = p["w_o"].astype(BF16)
    x1, h = _oproj(out_a, out_b, w_o[:na], w_o[na:], x2,
                   row(p["g_mix_post"]), row(p["g_ffn_pre"]), tm=256)
    return _ffn(h, p["w_gate"].astype(BF16), p["w_up"].astype(BF16),
                p["w_down"].astype(BF16), x1, row(p["g_ffn_post"]), tm=512, tf=512)


def kernel(x, positions, g_mix_pre, w_in, g_cq, g_ckv, w_uq, w_ukv, sinks, rel_bias,
           w_o, g_mix_post, g_ffn_pre, w_gate, w_up, w_down, g_ffn_post):
    b, s, d = x.shape
    assert b == 1, "the row-major (S, D) pipeline assumes a single sequence"
    x2 = x.reshape(s, d)
    pos_row = positions.reshape(1, s)
    for layer in range(w_in.shape[0]):
        p = dict(g_mix_pre=g_mix_pre[layer], w_in=w_in[layer], g_cq=g_cq[layer],
                 g_ckv=g_ckv[layer], w_uq=w_uq[layer], w_ukv=w_ukv[layer],
                 sinks=sinks[layer], rel_bias=rel_bias, w_o=w_o[layer],
                 g_mix_post=g_mix_post[layer], g_ffn_pre=g_ffn_pre[layer],
                 w_gate=w_gate[layer], w_up=w_up[layer], w_down=w_down[layer],
                 g_ffn_post=g_ffn_post[layer])
        x2 = _layer(x2, pos_row, p)
    return x2.reshape(b, s, d)
```

```python
import functools
import math

import jax
import jax.numpy as jnp
import numpy as np
from jax import lax
from jax.experimental import pallas as pl
from jax.experimental.pallas import tpu as pltpu

F32 = jnp.float32
BF16 = jnp.bfloat16

MLA_HEADS = 8
MLA_NOPE = 128
MLA_ROPE = 64
MLA_QK = MLA_NOPE + MLA_ROPE
MLA_V = 128
Q_RANK = 512
KV_RANK = 512
ROPE_THETA = 10000.0
SWA_HEADS = 16
SWA_KV_HEADS = 2
SWA_GROUP = SWA_HEADS // SWA_KV_HEADS
SWA_DIM = 64
WINDOW = 128
NUM_BUCKETS = 32
MAX_DISTANCE = 128
BLK = 128
RMS_EPS = 1e-6
NEG = -1e30
LOG2E = math.log2(math.e)

LANES = 128
VMEM_LIMIT = 56 * 1024 * 1024


def _cparams(sem):
    return pltpu.CompilerParams(dimension_semantics=sem, vmem_limit_bytes=VMEM_LIMIT)


def _resident(shape):
    nd = len(shape)
    return pl.BlockSpec(shape, lambda *_: (0,) * nd, pipeline_mode=pl.Buffered(1))


def _proj_kernel(x_ref, g_ref, w_ref, gcq_ref, gckv_ref,
                 cq_ref, ckv_ref, sq_ref, sk_ref, sv_ref, krt_ref, *, swa_scale):
    x = x_ref[...]
    d = x.shape[-1]
    r = lax.rsqrt(jnp.sum(x * x, axis=-1, keepdims=True) * (1.0 / d) + RMS_EPS)
    h = (x * g_ref[...]).astype(BF16)
    y = jnp.dot(h, w_ref[...], preferred_element_type=F32) * r

    def latent_norm(c, gain_ref):
        rc = lax.rsqrt(jnp.sum(c * c, axis=-1, keepdims=True) * (1.0 / c.shape[-1]) + RMS_EPS)
        return (c * rc * gain_ref[...]).astype(BF16)

    o = 0
    cq_ref[...] = latent_norm(y[:, o:o + Q_RANK], gcq_ref); o += Q_RANK
    ckv_ref[...] = latent_norm(y[:, o:o + KV_RANK], gckv_ref); o += KV_RANK
    nq = SWA_HEADS * SWA_DIM
    sq_ref[...] = (y[:, o:o + nq] * swa_scale).astype(BF16); o += nq
    nkv = SWA_KV_HEADS * SWA_DIM
    sk_ref[...] = y[:, o:o + nkv].astype(BF16); o += nkv
    sv_ref[...] = y[:, o:o + nkv].astype(BF16); o += nkv
    krt_ref[...] = y[:, o:o + LANES].T


def _proj(x2, g, w_in_p, g_cq, g_ckv, *, tm):
    s, d = x2.shape
    n = w_in_p.shape[1]
    nq = SWA_HEADS * SWA_DIM
    nkv = SWA_KV_HEADS * SWA_DIM
    row = lambda w: pl.BlockSpec((tm, w), lambda i: (i, 0))
    return pl.pallas_call(
        functools.partial(_proj_kernel, swa_scale=1.0 / math.sqrt(SWA_DIM)),
        grid=(s // tm,),
        in_specs=[row(d), _resident((1, d)), _resident((d, n)),
                  _resident((1, Q_RANK)), _resident((1, KV_RANK))],
        out_specs=[row(Q_RANK), row(KV_RANK), row(nq), row(nkv), row(nkv),
                   pl.BlockSpec((LANES, tm), lambda i: (0, i))],
        out_shape=[jax.ShapeDtypeStruct((s, Q_RANK), BF16),
                   jax.ShapeDtypeStruct((s, KV_RANK), BF16),
                   jax.ShapeDtypeStruct((s, nq), BF16),
                   jax.ShapeDtypeStruct((s, nkv), BF16),
                   jax.ShapeDtypeStruct((s, nkv), BF16),
                   jax.ShapeDtypeStruct((LANES, s), F32)],
        compiler_params=_cparams(("arbitrary",)),
        name="proj",
    )(x2, g, w_in_p, g_cq, g_ckv)


_NT = (((1,), (1,)), ((), ()))


def _qkv_kernel(cq_ref, ckv_ref, krt_ref, pos_ref, invf_ref, wuqt_ref, wuk_ref, wuvt_ref,
                qt_ref, k_ref, vt_ref, *, q_scale):
    tm = cq_ref.shape[0]
    half = MLA_ROPE // 2
    ang = invf_ref[...] * pos_ref[...].astype(F32)
    cos = jnp.cos(ang)
    sin = jnp.sin(ang)

    def rope_t(t):
        t1, t2 = t[:half], t[half:]
        return jnp.concatenate([t1 * cos - t2 * sin, t2 * cos + t1 * sin], axis=0)

    cq = cq_ref[...]
    ckv = ckv_ref[...]
    qt = lax.dot_general(wuqt_ref[...], cq, _NT, preferred_element_type=F32) * q_scale
    for h in range(MLA_HEADS):
        base = h * MLA_QK
        qt_ref[h, 0:MLA_NOPE, :] = qt[base:base + MLA_NOPE].astype(BF16)
        qt_ref[h, MLA_NOPE:MLA_QK, :] = rope_t(qt[base + MLA_NOPE:base + MLA_QK]).astype(BF16)

    krt = krt_ref[...]
    kr = jnp.concatenate([rope_t(krt[:MLA_ROPE]), krt[MLA_ROPE:]], axis=0).T[:, :MLA_ROPE].astype(BF16)
    kn = jnp.dot(ckv, wuk_ref[...], preferred_element_type=F32)
    vt = lax.dot_general(wuvt_ref[...], ckv, _NT, preferred_element_type=F32)
    for h in range(MLA_HEADS):
        k_ref[h, :, 0:MLA_NOPE] = kn[:, h * MLA_NOPE:(h + 1) * MLA_NOPE].astype(BF16)
        k_ref[h, :, MLA_NOPE:MLA_QK] = kr
        for c in range(tm // LANES):
            vt_ref[h, c] = vt[h * MLA_V:(h + 1) * MLA_V, c * LANES:(c + 1) * LANES].astype(BF16)


def _qkv(cq, ckv, krt, pos_row, invf, wuqt, wuk, wuvt, *, tm):
    s = cq.shape[0]
    row = lambda w: pl.BlockSpec((tm, w), lambda i: (i, 0))
    col = lambda r: pl.BlockSpec((r, tm), lambda i: (0, i))
    nb = tm // LANES
    return pl.pallas_call(
        functools.partial(_qkv_kernel, q_scale=LOG2E / math.sqrt(MLA_QK)),
        grid=(s // tm,),
        in_specs=[row(Q_RANK), row(KV_RANK), col(LANES), col(1), _resident(invf.shape),
                  _resident(wuqt.shape), _resident(wuk.shape), _resident(wuvt.shape)],
        out_specs=[pl.BlockSpec((MLA_HEADS, MLA_QK, tm), lambda i: (0, 0, i)),
                   pl.BlockSpec((MLA_HEADS, tm, MLA_QK), lambda i: (0, i, 0)),
                   pl.BlockSpec((MLA_HEADS, nb, MLA_V, LANES), lambda i: (0, i, 0, 0))],
        out_shape=[jax.ShapeDtypeStruct((MLA_HEADS, MLA_QK, s), BF16),
                   jax.ShapeDtypeStruct((MLA_HEADS, s, MLA_QK), BF16),
                   jax.ShapeDtypeStruct((MLA_HEADS, s // LANES, MLA_V, LANES), BF16)],
        compiler_params=_cparams(("arbitrary",)),
        name="qkv",
    )(cq, ckv, krt, pos_row, invf, wuqt, wuk, wuvt)


def _mla_kernel(qt_ref, k_ref, vt_ref, o_ref, m_sc, l_sc, acc_sc, s_sc, *, tq, tk, hps):
    assert tq == 2 * tk
    qi = pl.program_id(1)
    nsub = tk // LANES
    heads = range(hps)
    left, right, full = slice(0, tk), slice(tk, tq), slice(0, tq)
    m_sc[...] = jnp.full(m_sc.shape, NEG, F32)
    l_sc[...] = jnp.zeros(l_sc.shape, F32)
    acc_sc[...] = jnp.zeros(acc_sc.shape, F32)

    def scores(hh, j, slot, cols=full):
        start = pl.multiple_of(j * tk, tk)
        s_sc[hh, slot, :, cols] = jnp.dot(k_ref[hh, pl.ds(start, tk), :], qt_ref[hh, :, cols],
                                          preferred_element_type=F32)

    def update(hh, j, slot, cols=full, tri_cols=None):
        s = s_sc[hh, slot, :, cols]
        if tri_cols is not None:
            tri = (lax.broadcasted_iota(jnp.int32, (tk, tk), 0)
                   <= lax.broadcasted_iota(jnp.int32, (tk, tk), 1))
            parts = [jnp.where(tri, s[:, :tk], NEG)] + ([s[:, tk:]] if s.shape[1] > tk else [])
            s = jnp.concatenate(parts, axis=1)
        m_old = m_sc[hh, :, cols]
        m_new = jnp.maximum(m_old, jnp.max(s, axis=0, keepdims=True))
        alpha = jnp.exp2(m_old - m_new)
        p = jnp.exp2(s - m_new)
        l_sc[hh, :, cols] = alpha * l_sc[hh, :, cols] + jnp.sum(p, axis=0, keepdims=True)
        vt = jnp.concatenate([vt_ref[hh, j * nsub + c] for c in range(nsub)], axis=1)
        acc_sc[hh, :, cols] = (alpha * acc_sc[hh, :, cols]
                               + jnp.dot(vt, p.astype(BF16), preferred_element_type=F32))
        m_sc[hh, :, cols] = m_new

    def pair(i, carry):
        for hh in heads:
            scores(hh, 2 * i + 1, 1)
        for hh in heads:
            update(hh, 2 * i, 0)
        for hh in heads:
            scores(hh, 2 * i + 2, 0)
        for hh in heads:
            update(hh, 2 * i + 1, 1)
        return carry

    for hh in heads:
        scores(hh, 0, 0)
    lax.fori_loop(0, qi, pair, 0)
    for hh in heads:
        scores(hh, 2 * qi + 1, 1, right)
    for hh in heads:
        update(hh, 2 * qi, 0, full, tri_cols=left)
    for hh in heads:
        update(hh, 2 * qi + 1, 1, right, tri_cols=right)
    for hh in heads:
        o_ref[:, hh * MLA_V:(hh + 1) * MLA_V] = (acc_sc[hh] / l_sc[hh]).T.astype(o_ref.dtype)


def _mla(qt, k, vt, *, tq, tk, hps):
    _, s, _ = k.shape
    return pl.pallas_call(
        functools.partial(_mla_kernel, tq=tq, tk=tk, hps=hps),
        grid=(MLA_HEADS // hps, s // tq),
        in_specs=[pl.BlockSpec((hps, MLA_QK, tq), lambda h, i: (h, 0, i)),
                  pl.BlockSpec((hps, s, MLA_QK), lambda h, i: (h, 0, 0)),
                  pl.BlockSpec((hps, s // LANES, MLA_V, LANES), lambda h, i: (h, 0, 0, 0))],
        out_specs=pl.BlockSpec((tq, hps * MLA_V), lambda h, i: (i, h)),
        out_shape=jax.ShapeDtypeStruct((s, MLA_HEADS * MLA_V), BF16),
        scratch_shapes=[pltpu.VMEM((hps, 1, tq), F32), pltpu.VMEM((hps, 1, tq), F32),
                        pltpu.VMEM((hps, MLA_V, tq), F32),
                        pltpu.VMEM((hps, 2, tk, tq), F32)],
        compiler_params=_cparams(("arbitrary", "arbitrary")),
        name="mla",
    )(qt, k, vt)


def _t5_bucket_table():
    i = np.arange(BLK)[:, None]
    j = np.arange(2 * BLK)[None, :]
    dist = i + BLK - j
    max_exact = NUM_BUCKETS // 2
    d = np.maximum(dist, 0)
    large = max_exact + (np.log(np.maximum(d, 1) / max_exact)
                         / np.log(MAX_DISTANCE / max_exact)
                         * (NUM_BUCKETS - max_exact)).astype(np.int32)
    large = np.minimum(large, NUM_BUCKETS - 1)
    bucket = np.where(d < max_exact, d, large).astype(np.int32)
    in_window = (dist >= 0) & (dist < WINDOW)
    return np.where(in_window, bucket, -1).astype(np.int32)


def _swa_kernel(relb_ref, sink_ref, bucket_ref, q_ref, kc_ref, kp_ref, vc_ref, vp_ref,
                o_ref, bias_sc):
    i = pl.program_id(0)

    @pl.when(i == 0)
    def _():
        bucket = bucket_ref[...]
        for h in range(SWA_HEADS):
            b = jnp.full(bucket.shape, NEG, F32)
            for t in range(NUM_BUCKETS):
                b = jnp.where(bucket == t, relb_ref[t, h], b)
            bias_sc[h] = b

    col = lax.broadcasted_iota(jnp.int32, (BLK, 2 * BLK), 1)
    band_ok = jnp.logical_or(col >= BLK, i > 0)
    q = q_ref[...]
    kc = kc_ref[...]
    kp = kp_ref[...]
    vc = vc_ref[...]
    vp = vp_ref[...]
    outs = []
    for c in range(SWA_KV_HEADS):
        ks = slice(c * SWA_DIM, (c + 1) * SWA_DIM)
        kband = jnp.concatenate([kp[:, ks], kc[:, ks]], axis=0)
        vband = jnp.concatenate([vp[:, ks], vc[:, ks]], axis=0)
        for g in range(SWA_GROUP):
            h = c * SWA_GROUP + g
            qh = q[:, h * SWA_DIM:(h + 1) * SWA_DIM]
            s = lax.dot_general(qh, kband, _NT, preferred_element_type=F32)
            s = jnp.where(band_ok, s + bias_sc[h], NEG)
            sink = sink_ref[h]
            m = jnp.maximum(jnp.max(s, axis=-1, keepdims=True), sink)
            p = jnp.exp(s - m)
            denom = jnp.sum(p, axis=-1, keepdims=True) + jnp.exp(sink - m)
            o = jnp.dot(p.astype(BF16), vband, preferred_element_type=F32) / denom
            outs.append(o)
    o_ref[...] = jnp.concatenate(outs, axis=-1).astype(o_ref.dtype)


def _swa(sq, sk, sv, sinks, rel_bias):
    s = sq.shape[0]
    nq = SWA_HEADS * SWA_DIM
    nkv = SWA_KV_HEADS * SWA_DIM
    bucket = jnp.asarray(_t5_bucket_table())
    smem = pl.BlockSpec(memory_space=pltpu.SMEM)
    cur = lambda w: pl.BlockSpec((BLK, w), lambda i: (i, 0))
    prev = lambda w: pl.BlockSpec((BLK, w), lambda i: (jnp.maximum(i - 1, 0), 0))
    return pl.pallas_call(
        _swa_kernel,
        grid=(s // BLK,),
        in_specs=[smem, smem, _resident((BLK, 2 * BLK)),
                  cur(nq), cur(nkv), prev(nkv), cur(nkv), prev(nkv)],
        out_specs=cur(nq),
        out_shape=jax.ShapeDtypeStruct((s, nq), BF16),
        scratch_shapes=[pltpu.VMEM((SWA_HEADS, BLK, 2 * BLK), F32)],
        compiler_params=_cparams(("arbitrary",)),
        name="swa",
    )(rel_bias, sinks, bucket, sq, sk, sk, sv, sv)


def _rms(y, gain):
    r = lax.rsqrt(jnp.sum(y * y, axis=-1, keepdims=True) * (1.0 / y.shape[-1]) + RMS_EPS)
    return y * r * gain


def _oproj_kernel(a_ref, b_ref, wa_ref, wb_ref, x_ref, gpost_ref, gpre_ref, x1_ref, h_ref):
    mix = (jnp.dot(a_ref[...], wa_ref[...], preferred_element_type=F32)
           + jnp.dot(b_ref[...], wb_ref[...], preferred_element_type=F32))
    x1 = x_ref[...] + _rms(mix, gpost_ref[...])
    x1_ref[...] = x1
    h_ref[...] = _rms(x1, gpre_ref[...]).astype(BF16)


def _oproj(out_a, out_b, wo_a, wo_b, x2, g_post, g_pre, *, tm):
    s, d = x2.shape
    row = lambda w: pl.BlockSpec((tm, w), lambda i: (i, 0))
    return pl.pallas_call(
        _oproj_kernel,
        grid=(s // tm,),
        in_specs=[row(out_a.shape[1]), row(out_b.shape[1]), _resident(wo_a.shape),
                  _resident(wo_b.shape), row(d), _resident((1, d)), _resident((1, d))],
        out_specs=[row(d), row(d)],
        out_shape=[jax.ShapeDtypeStruct((s, d), F32), jax.ShapeDtypeStruct((s, d), BF16)],
        compiler_params=_cparams(("arbitrary",)),
        name="oproj",
    )(out_a, out_b, wo_a, wo_b, x2, g_post, g_pre)


def _ffn_kernel(h_ref, wg_ref, wu_ref, wd_ref, x1_ref, gpost_ref, o_ref, acc_sc):
    j = pl.program_id(1)
    h = h_ref[...]
    gate = jnp.dot(h, wg_ref[...], preferred_element_type=F32)
    up = jnp.dot(h, wu_ref[...], preferred_element_type=F32)
    act = (gate * jax.nn.sigmoid(gate) * up).astype(BF16)
    part = jnp.dot(act, wd_ref[...], preferred_element_type=F32)

    @pl.when(j == 0)
    def _():
        acc_sc[...] = part

    @pl.when(j > 0)
    def _():
        acc_sc[...] += part

    @pl.when(j == pl.num_programs(1) - 1)
    def _():
        o_ref[...] = x1_ref[...] + _rms(acc_sc[...], gpost_ref[...])


def _ffn(h, wg, wu, wd, x1, g_post, *, tm, tf):
    s, d = x1.shape
    dff = wg.shape[1]
    return pl.pallas_call(
        _ffn_kernel,
        grid=(s // tm, dff // tf),
        in_specs=[pl.BlockSpec((tm, d), lambda i, j: (i, 0)),
                  pl.BlockSpec((d, tf), lambda i, j: (0, j)),
                  pl.BlockSpec((d, tf), lambda i, j: (0, j)),
                  pl.BlockSpec((tf, d), lambda i, j: (j, 0)),
                  pl.BlockSpec((tm, d), lambda i, j: (i, 0)),
                  _resident((1, d))],
        out_specs=pl.BlockSpec((tm, d), lambda i, j: (i, 0)),
        out_shape=jax.ShapeDtypeStruct((s, d), F32),
        scratch_shapes=[pltpu.VMEM((tm, d), F32)],
        compiler_params=_cparams(("arbitrary", "arbitrary")),
        name="ffn",
    )(h, wg, wu, wd, x1, g_post)


def _permute_w_in(w_in):
    sizes = (Q_RANK, KV_RANK, MLA_ROPE, SWA_HEADS * SWA_DIM,
             SWA_KV_HEADS * SWA_DIM, SWA_KV_HEADS * SWA_DIM)
    offs = np.concatenate([[0], np.cumsum(sizes)])
    part = lambda k: w_in[:, offs[k]:offs[k + 1]]
    pad = jnp.zeros((w_in.shape[0], LANES - MLA_ROPE), w_in.dtype)
    return jnp.concatenate([part(0), part(1), part(3), part(4), part(5), part(2), pad], axis=1)


def _layer(x2, pos_row, p, *, tm_qkv=256):
    half = MLA_ROPE // 2
    inv_freq = ROPE_THETA ** (-jnp.arange(half, dtype=F32) / half)
    invf = jnp.broadcast_to(inv_freq[:, None], (half, tm_qkv))

    row = lambda a: a[None, :]
    cq, ckv, sq, sk, sv, krt = _proj(
        x2, row(p["g_mix_pre"]), _permute_w_in(p["w_in"]).astype(BF16),
        row(p["g_cq"]), row(p["g_ckv"]), tm=256)
    w_ukv = p["w_ukv"].reshape(KV_RANK, MLA_HEADS, MLA_NOPE + MLA_V)
    wuk = w_ukv[:, :, :MLA_NOPE].reshape(KV_RANK, MLA_HEADS * MLA_NOPE)
    wuvt = w_ukv[:, :, MLA_NOPE:].reshape(KV_RANK, MLA_HEADS * MLA_V).T
    qt, k, vt = _qkv(cq, ckv, krt, pos_row, invf, p["w_uq"].T.astype(BF16),
                     wuk.astype(BF16), wuvt.astype(BF16), tm=tm_qkv)
    out_a = _mla(qt, k, vt, tq=1024, tk=512, hps=1)
    out_b = _swa(sq, sk, sv, p["sinks"], p["rel_bias"])
    na = MLA_HEADS * MLA_V
    w_o = p["w_o"].astype(BF16)
    x1, h = _oproj(out_a, out_b, w_o[:na], w_o[na:], x2,
                   row(p["g_mix_post"]), row(p["g_ffn_pre"]), tm=256)
    return _ffn(h, p["w_gate"].astype(BF16), p["w_up"].astype(BF16),
                p["w_down"].astype(BF16), x1, row(p["g_ffn_post"]), tm=512, tf=512)


def kernel(x, positions, g_mix_pre, w_in, g_cq, g_ckv, w_uq, w_ukv, sinks, rel_bias,
           w_o, g_mix_post, g_ffn_pre, w_gate, w_up, w_down, g_ffn_post):
    b, s, d = x.shape
    assert b == 1, "the row-major (S, D) pipeline assumes a single sequence"
    x2 = x.reshape(s, d)
    pos_row = positions.reshape(1, s)
    for layer in range(w_in.shape[0]):
        p = dict(g_mix_pre=g_mix_pre[layer], w_in=w_in[layer], g_cq=g_cq[layer],
                 g_ckv=g_ckv[layer], w_uq=w_uq[layer], w_ukv=w_ukv[layer],
                 sinks=sinks[layer], rel_bias=rel_bias, w_o=w_o[layer],
                 g_mix_post=g_mix_post[layer], g_ffn_pre=g_ffn_pre[layer],
                 w_gate=w_gate[layer], w_up=w_up[layer], w_down=w_down[layer],
                 g_ffn_post=g_ffn_post[layer])
        x2 = _layer(x2, pos_row, p)
    return x2.reshape(b, s, d)
```

```python
import functools
import math

import jax
import jax.numpy as jnp
import numpy as np
from jax import lax
from jax.experimental import pallas as pl
from jax.experimental.pallas import tpu as pltpu

F32 = jnp.float32
BF16 = jnp.bfloat16

MLA_HEADS = 8
MLA_NOPE = 128
MLA_ROPE = 64
MLA_QK = MLA_NOPE + MLA_ROPE
MLA_V = 128
Q_RANK = 512
KV_RANK = 512
ROPE_THETA = 10000.0
SWA_HEADS = 16
SWA_KV_HEADS = 2
SWA_GROUP = SWA_HEADS // SWA_KV_HEADS
SWA_DIM = 64
WINDOW = 128
NUM_BUCKETS = 32
MAX_DISTANCE = 128
BLK = 128
RMS_EPS = 1e-6
NEG = -1e30
LOG2E = math.log2(math.e)

LANES = 128
VMEM_LIMIT = 56 * 1024 * 1024


def _cparams(sem):
    return pltpu.CompilerParams(dimension_semantics=sem, vmem_limit_bytes=VMEM_LIMIT)


def _resident(shape):
    nd = len(shape)
    return pl.BlockSpec(shape, lambda *_: (0,) * nd, pipeline_mode=pl.Buffered(1))


def _proj_kernel(x_ref, g_ref, w_ref, gcq_ref, gckv_ref,
                 cq_ref, ckv_ref, sq_ref, sk_ref, sv_ref, krt_ref, *, swa_scale):
    x = x_ref[...]
    d = x.shape[-1]
    r = lax.rsqrt(jnp.sum(x * x, axis=-1, keepdims=True) * (1.0 / d) + RMS_EPS)
    h = (x * g_ref[...]).astype(BF16)
    y = jnp.dot(h, w_ref[...], preferred_element_type=F32) * r

    def latent_norm(c, gain_ref):
        rc = lax.rsqrt(jnp.sum(c * c, axis=-1, keepdims=True) * (1.0 / c.shape[-1]) + RMS_EPS)
        return (c * rc * gain_ref[...]).astype(BF16)

    o = 0
    cq_ref[...] = latent_norm(y[:, o:o + Q_RANK], gcq_ref); o += Q_RANK
    ckv_ref[...] = latent_norm(y[:, o:o + KV_RANK], gckv_ref); o += KV_RANK
    nq = SWA_HEADS * SWA_DIM
    sq_ref[...] = (y[:, o:o + nq] * swa_scale).astype(BF16); o += nq
    nkv = SWA_KV_HEADS * SWA_DIM
    sk_ref[...] = y[:, o:o + nkv].astype(BF16); o += nkv
    sv_ref[...] = y[:, o:o + nkv].astype(BF16); o += nkv
    krt_ref[...] = y[:, o:o + LANES].T


def _proj(x2, g, w_in_p, g_cq, g_ckv, *, tm):
    s, d = x2.shape
    n = w_in_p.shape[1]
    nq = SWA_HEADS * SWA_DIM
    nkv = SWA_KV_HEADS * SWA_DIM
    row = lambda w: pl.BlockSpec((tm, w), lambda i: (i, 0))
    return pl.pallas_call(
        functools.partial(_proj_kernel, swa_scale=1.0 / math.sqrt(SWA_DIM)),
        grid=(s // tm,),
        in_specs=[row(d), _resident((1, d)), _resident((d, n)),
                  _resident((1, Q_RANK)), _resident((1, KV_RANK))],
        out_specs=[row(Q_RANK), row(KV_RANK), row(nq), row(nkv), row(nkv),
                   pl.BlockSpec((LANES, tm), lambda i: (0, i))],
        out_shape=[jax.ShapeDtypeStruct((s, Q_RANK), BF16),
                   jax.ShapeDtypeStruct((s, KV_RANK), BF16),
                   jax.ShapeDtypeStruct((s, nq), BF16),
                   jax.ShapeDtypeStruct((s, nkv), BF16),
                   jax.ShapeDtypeStruct((s, nkv), BF16),
                   jax.ShapeDtypeStruct((LANES, s), F32)],
        compiler_params=_cparams(("arbitrary",)),
        name="proj",
    )(x2, g, w_in_p, g_cq, g_ckv)


_NT = (((1,), (1,)), ((), ()))


def _qkv_kernel(cq_ref, ckv_ref, krt_ref, pos_ref, invf_ref, wuqt_ref, wuk_ref, wuvt_ref,
                qt_ref, k_ref, vt_ref, *, q_scale):
    tm = cq_ref.shape[0]
    half = MLA_ROPE // 2
    ang = invf_ref[...] * pos_ref[...].astype(F32)
    cos = jnp.cos(ang)
    sin = jnp.sin(ang)

    def rope_t(t):
        t1, t2 = t[:half], t[half:]
        return jnp.concatenate([t1 * cos - t2 * sin, t2 * cos + t1 * sin], axis=0)

    cq = cq_ref[...]
    ckv = ckv_ref[...]
    qt = lax.dot_general(wuqt_ref[...], cq, _NT, preferred_element_type=F32) * q_scale
    for h in range(MLA_HEADS):
        base = h * MLA_QK
        qt_ref[h, 0:MLA_NOPE, :] = qt[base:base + MLA_NOPE].astype(BF16)
        qt_ref[h, MLA_NOPE:MLA_QK, :] = rope_t(qt[base + MLA_NOPE:base + MLA_QK]).astype(BF16)

    krt = krt_ref[...]
    kr = jnp.concatenate([rope_t(krt[:MLA_ROPE]), krt[MLA_ROPE:]], axis=0).T[:, :MLA_ROPE].astype(BF16)
    kn = jnp.dot(ckv, wuk_ref[...], preferred_element_type=F32)
    vt = lax.dot_general(wuvt_ref[...], ckv, _NT, preferred_element_type=F32)
    for h in range(MLA_HEADS):
        k_ref[h, :, 0:MLA_NOPE] = kn[:, h * MLA_NOPE:(h + 1) * MLA_NOPE].astype(BF16)
        k_ref[h, :, MLA_NOPE:MLA_QK] = kr
        for c in range(tm // LANES):
            vt_ref[h, c] = vt[h * MLA_V:(h + 1) * MLA_V, c * LANES:(c + 1) * LANES].astype(BF16)


def _qkv(cq, ckv, krt, pos_row, invf, wuqt, wuk, wuvt, *, tm):
    s = cq.shape[0]
    row = lambda w: pl.BlockSpec((tm, w), lambda i: (i, 0))
    col = lambda r: pl.BlockSpec((r, tm), lambda i: (0, i))
    nb = tm // LANES
    return pl.pallas_call(
        functools.partial(_qkv_kernel, q_scale=LOG2E / math.sqrt(MLA_QK)),
        grid=(s // tm,),
        in_specs=[row(Q_RANK), row(KV_RANK), col(LANES), col(1), _resident(invf.shape),
                  _resident(wuqt.shape), _resident(wuk.shape), _resident(wuvt.shape)],
        out_specs=[pl.BlockSpec((MLA_HEADS, MLA_QK, tm), lambda i: (0, 0, i)),
                   pl.BlockSpec((MLA_HEADS, tm, MLA_QK), lambda i: (0, i, 0)),
                   pl.BlockSpec((MLA_HEADS, nb, MLA_V, LANES), lambda i: (0, i, 0, 0))],
        out_shape=[jax.ShapeDtypeStruct((MLA_HEADS, MLA_QK, s), BF16),
                   jax.ShapeDtypeStruct((MLA_HEADS, s, MLA_QK), BF16),
                   jax.ShapeDtypeStruct((MLA_HEADS, s // LANES, MLA_V, LANES), BF16)],
        compiler_params=_cparams(("arbitrary",)),
        name="qkv",
    )(cq, ckv, krt, pos_row, invf, wuqt, wuk, wuvt)


def _mla_kernel(qt_ref, k_ref, vt_ref, o_ref, m_sc, l_sc, acc_sc, s_sc, *, tq, tk, hps):
    assert tq == 2 * tk
    qi = pl.program_id(1)
    nsub = tk // LANES
    heads = range(hps)
    left, right, full = slice(0, tk), slice(tk, tq), slice(0, tq)
    m_sc[...] = jnp.full(m_sc.shape, NEG, F32)
    l_sc[...] = jnp.zeros(l_sc.shape, F32)
    acc_sc[...] = jnp.zeros(acc_sc.shape, F32)

    def scores(hh, j, slot, cols=full):
        start = pl.multiple_of(j * tk, tk)
        s_sc[hh, slot, :, cols] = jnp.dot(k_ref[hh, pl.ds(start, tk), :], qt_ref[hh, :, cols],
                                          preferred_element_type=F32)

    def update(hh, j, slot, cols=full, tri_cols=None):
        s = s_sc[hh, slot, :, cols]
        if tri_cols is not None:
            tri = (lax.broadcasted_iota(jnp.int32, (tk, tk), 0)
                   <= lax.broadcasted_iota(jnp.int32, (tk, tk), 1))
            parts = [jnp.where(tri, s[:, :tk], NEG)] + ([s[:, tk:]] if s.shape[1] > tk else [])
            s = jnp.concatenate(parts, axis=1)
        m_old = m_sc[hh, :, cols]
        m_new = jnp.maximum(m_old, jnp.max(s, axis=0, keepdims=True))
        alpha = jnp.exp2(m_old - m_new)
        p = jnp.exp2(s - m_new)
        l_sc[hh, :, cols] = alpha * l_sc[hh, :, cols] + jnp.sum(p, axis=0, keepdims=True)
        vt = jnp.concatenate([vt_ref[hh, j * nsub + c] for c in range(nsub)], axis=1)
        acc_sc[hh, :, cols] = (alpha * acc_sc[hh, :, cols]
                               + jnp.dot(vt, p.astype(BF16), preferred_element_type=F32))
        m_sc[hh, :, cols] = m_new

    def pair(i, carry):
        for hh in heads:
            scores(hh, 2 * i + 1, 1)
        for hh in heads:
            update(hh, 2 * i, 0)
        for hh in heads:
            scores(hh, 2 * i + 2, 0)
        for hh in heads:
            update(hh, 2 * i + 1, 1)
        return carry

    for hh in heads:
        scores(hh, 0, 0)
    lax.fori_loop(0, qi, pair, 0)
    for hh in heads:
        scores(hh, 2 * qi + 1, 1, right)
    for hh in heads:
        update(hh, 2 * qi, 0, full, tri_cols=left)
    for hh in heads:
        update(hh, 2 * qi + 1, 1, right, tri_cols=right)
    for hh in heads:
        o_ref[:, hh * MLA_V:(hh + 1) * MLA_V] = (acc_sc[hh] / l_sc[hh]).T.astype(o_ref.dtype)


def _mla(qt, k, vt, *, tq, tk, hps):
    _, s, _ = k.shape
    return pl.pallas_call(
        functools.partial(_mla_kernel, tq=tq, tk=tk, hps=hps),
        grid=(MLA_HEADS // hps, s // tq),
        in_specs=[pl.BlockSpec((hps, MLA_QK, tq), lambda h, i: (h, 0, i)),
                  pl.BlockSpec((hps, s, MLA_QK), lambda h, i: (h, 0, 0)),
                  pl.BlockSpec((hps, s // LANES, MLA_V, LANES), lambda h, i: (h, 0, 0, 0))],
        out_specs=pl.BlockSpec((tq, hps * MLA_V), lambda h, i: (i, h)),
        out_shape=jax.ShapeDtypeStruct((s, MLA_HEADS * MLA_V), BF16),
        scratch_shapes=[pltpu.VMEM((hps, 1, tq), F32), pltpu.VMEM((hps, 1, tq), F32),
                        pltpu.VMEM((hps, MLA_V, tq), F32),
                        pltpu.VMEM((hps, 2, tk, tq), F32)],
        compiler_params=_cparams(("arbitrary", "arbitrary")),
        name="mla",
    )(qt, k, vt)


def _t5_bucket_table():
    i = np.arange(BLK)[:, None]
    j = np.arange(2 * BLK)[None, :]
    dist = i + BLK - j
    max_exact = NUM_BUCKETS // 2
    d = np.maximum(dist, 0)
    large = max_exact + (np.log(np.maximum(d, 1) / max_exact)
                         / np.log(MAX_DISTANCE / max_exact)
                         * (NUM_BUCKETS - max_exact)).astype(np.int32)
    large = np.minimum(large, NUM_BUCKETS - 1)
    bucket = np.where(d < max_exact, d, large).astype(np.int32)
    in_window = (dist >= 0) & (dist < WINDOW)
    return np.where(in_window, bucket, -1).astype(np.int32)


def _swa_kernel(relb_ref, sink_ref, bucket_ref, q_ref, kc_ref, kp_ref, vc_ref, vp_ref,
                o_ref, bias_sc):
    i = pl.program_id(0)

    @pl.when(i == 0)
    def _():
        bucket = bucket_ref[...]
        for h in range(SWA_HEADS):
            b = jnp.full(bucket.shape, NEG, F32)
            for t in range(NUM_BUCKETS):
                b = jnp.where(bucket == t, relb_ref[t, h], b)
            bias_sc[h] = b

    col = lax.broadcasted_iota(jnp.int32, (BLK, 2 * BLK), 1)
    band_ok = jnp.logical_or(col >= BLK, i > 0)
    q = q_ref[...]
    kc = kc_ref[...]
    kp = kp_ref[...]
    vc = vc_ref[...]
    vp = vp_ref[...]
    outs = []
    for c in range(SWA_KV_HEADS):
        ks = slice(c * SWA_DIM, (c + 1) * SWA_DIM)
        kband = jnp.concatenate([kp[:, ks], kc[:, ks]], axis=0)
        vband = jnp.concatenate([vp[:, ks], vc[:, ks]], axis=0)
        for g in range(SWA_GROUP):
            h = c * SWA_GROUP + g
            qh = q[:, h * SWA_DIM:(h + 1) * SWA_DIM]
            s = lax.dot_general(qh, kband, _NT, preferred_element_type=F32)
            s = jnp.where(band_ok, s + bias_sc[h], NEG)
            sink = sink_ref[h]
            m = jnp.maximum(jnp.max(s, axis=-1, keepdims=True), sink)
            p = jnp.exp(s - m)
            denom = jnp.sum(p, axis=-1, keepdims=True) + jnp.exp(sink - m)
            o = jnp.dot(p.astype(BF16), vband, preferred_element_type=F32) / denom
            outs.append(o)
    o_ref[...] = jnp.concatenate(outs, axis=-1).astype(o_ref.dtype)


def _swa(sq, sk, sv, sinks, rel_bias):
    s = sq.shape[0]
    nq = SWA_HEADS * SWA_DIM
    nkv = SWA_KV_HEADS * SWA_DIM
    bucket = jnp.asarray(_t5_bucket_table())
    smem = pl.BlockSpec(memory_space=pltpu.SMEM)
    cur = lambda w: pl.BlockSpec((BLK, w), lambda i: (i, 0))
    prev = lambda w: pl.BlockSpec((BLK, w), lambda i: (jnp.maximum(i - 1, 0), 0))
    return pl.pallas_call(
        _swa_kernel,
        grid=(s // BLK,),
        in_specs=[smem, smem, _resident((BLK, 2 * BLK)),
                  cur(nq), cur(nkv), prev(nkv), cur(nkv), prev(nkv)],
        out_specs=cur(nq),
        out_shape=jax.ShapeDtypeStruct((s, nq), BF16),
        scratch_shapes=[pltpu.VMEM((SWA_HEADS, BLK, 2 * BLK), F32)],
        compiler_params=_cparams(("arbitrary",)),
        name="swa",
    )(rel_bias, sinks, bucket, sq, sk, sk, sv, sv)


def _rms(y, gain):
    r = lax.rsqrt(jnp.sum(y * y, axis=-1, keepdims=True) * (1.0 / y.shape[-1]) + RMS_EPS)
    return y * r * gain


def _oproj_kernel(a_ref, b_ref, wa_ref, wb_ref, x_ref, gpost_ref, gpre_ref, x1_ref, h_ref):
    mix = (jnp.dot(a_ref[...], wa_ref[...], preferred_element_type=F32)
           + jnp.dot(b_ref[...], wb_ref[...], preferred_element_type=F32))
    x1 = x_ref[...] + _rms(mix, gpost_ref[...])
    x1_ref[...] = x1
    h_ref[...] = _rms(x1, gpre_ref[...]).astype(BF16)


def _oproj(out_a, out_b, wo_a, wo_b, x2, g_post, g_pre, *, tm):
    s, d = x2.shape
    row = lambda w: pl.BlockSpec((tm, w), lambda i: (i, 0))
    return pl.pallas_call(
        _oproj_kernel,
        grid=(s // tm,),
        in_specs=[row(out_a.shape[1]), row(out_b.shape[1]), _resident(wo_a.shape),
                  _resident(wo_b.shape), row(d), _resident((1, d)), _resident((1, d))],
        out_specs=[row(d), row(d)],
        out_shape=[jax.ShapeDtypeStruct((s, d), F32), jax.ShapeDtypeStruct((s, d), BF16)],
        compiler_params=_cparams(("arbitrary",)),
        name="oproj",
    )(out_a, out_b, wo_a, wo_b, x2, g_post, g_pre)


def _ffn_kernel(h_ref, wg_ref, wu_ref, wd_ref, x1_ref, gpost_ref, o_ref, acc_sc):
    j = pl.program_id(1)

    @pl.when(j == 0)
    def _():
        acc_sc[...] = jnp.zeros(acc_sc.shape, F32)

    h = h_ref[...]
    tf = wg_ref.shape[1]
    halves = [slice(0, tf // 2), slice(tf // 2, tf)]
    gu = [(jnp.dot(h, wg_ref[:, c], preferred_element_type=F32),
           jnp.dot(h, wu_ref[:, c], preferred_element_type=F32)) for c in halves]
    for c, (gate, up) in zip(halves, gu):
        act = (gate * jax.nn.sigmoid(gate) * up).astype(BF16)
        acc_sc[...] += jnp.dot(act, wd_ref[c, :], preferred_element_type=F32)

    @pl.when(j == pl.num_programs(1) - 1)
    def _():
        o_ref[...] = x1_ref[...] + _rms(acc_sc[...], gpost_ref[...])


def _ffn(h, wg, wu, wd, x1, g_post, *, tm, tf):
    s, d = x1.shape
    dff = wg.shape[1]
    return pl.pallas_call(
        _ffn_kernel,
        grid=(s // tm, dff // tf),
        in_specs=[pl.BlockSpec((tm, d), lambda i, j: (i, 0)),
                  pl.BlockSpec((d, tf), lambda i, j: (0, j)),
                  pl.BlockSpec((d, tf), lambda i, j: (0, j)),
                  pl.BlockSpec((tf, d), lambda i, j: (j, 0)),
                  pl.BlockSpec((tm, d), lambda i, j: (i, 0)),
                  _resident((1, d))],
        out_specs=pl.BlockSpec((tm, d), lambda i, j: (i, 0)),
        out_shape=jax.ShapeDtypeStruct((s, d), F32),
        scratch_shapes=[pltpu.VMEM((tm, d), F32)],
        compiler_params=_cparams(("arbitrary", "arbitrary")),
        name="ffn",
    )(h, wg, wu, wd, x1, g_post)


def _permute_w_in(w_in):
    sizes = (Q_RANK, KV_RANK, MLA_ROPE, SWA_HEADS * SWA_DIM,
             SWA_KV_HEADS * SWA_DIM, SWA_KV_HEADS * SWA_DIM)
    offs = np.concatenate([[0], np.cumsum(sizes)])
    part = lambda k: w_in[:, offs[k]:offs[k + 1]]
    pad = jnp.zeros((w_in.shape[0], LANES - MLA_ROPE), w_in.dtype)
    return jnp.concatenate([part(0), part(1), part(3), part(4), part(5), part(2), pad], axis=1)


def _layer(x2, pos_row, p, *, tm_qkv=256):
    half = MLA_ROPE // 2
    inv_freq = ROPE_THETA ** (-jnp.arange(half, dtype=F32) / half)
    invf = jnp.broadcast_to(inv_freq[:, None], (half, tm_qkv))

    row = lambda a: a[None, :]
    cq, ckv, sq, sk, sv, krt = _proj(
        x2, row(p["g_mix_pre"]), _permute_w_in(p["w_in"]).astype(BF16),
        row(p["g_cq"]), row(p["g_ckv"]), tm=256)
    w_ukv = p["w_ukv"].reshape(KV_RANK, MLA_HEADS, MLA_NOPE + MLA_V)
    wuk = w_ukv[:, :, :MLA_NOPE].reshape(KV_RANK, MLA_HEADS * MLA_NOPE)
    wuvt = w_ukv[:, :, MLA_NOPE:].reshape(KV_RANK, MLA_HEADS * MLA_V).T
    qt, k, vt = _qkv(cq, ckv, krt, pos_row, invf, p["w_uq"].T.astype(BF16),
                     wuk.astype(BF16), wuvt.astype(BF16), tm=tm_qkv)
    out_a = _mla(qt, k, vt, tq=1024, tk=512, hps=1)
    out_b = _swa(sq, sk, sv, p["sinks"], p["rel_bias"])
    na = MLA_HEADS * MLA_V
    w_o = p["w_o"].astype(BF16)
    x1, h = _oproj(out_a, out_b, w_o[:na], w_o[na:], x2,
                   row(p["g_mix_post"]), row(p["g_ffn_pre"]), tm=256)
    return _ffn(h, p["w_gate"].astype(BF16), p["w_up"].astype(BF16),
                p["w_down"].astype(BF16), x1, row(p["g_ffn_post"]), tm=512, tf=512)


def kernel(x, positions, g_mix_pre, w_in, g_cq, g_ckv, w_uq, w_ukv, sinks, rel_bias,
           w_o, g_mix_post, g_ffn_pre, w_gate, w_up, w_down, g_ffn_post):
    b, s, d = x.shape
    assert b == 1, "the row-major (S, D) pipeline assumes a single sequence"
    x2 = x.reshape(s, d)
    pos_row = positions.reshape(1, s)
    for layer in range(w_in.shape[0]):
        p = dict(g_mix_pre=g_mix_pre[layer], w_in=w_in[layer], g_cq=g_cq[layer],
                 g_ckv=g_ckv[layer], w_uq=w_uq[layer], w_ukv=w_ukv[layer],
                 sinks=sinks[layer], rel_bias=rel_bias, w_o=w_o[layer],
                 g_mix_post=g_mix_post[layer], g_ffn_pre=g_ffn_pre[layer],
                 w_gate=w_gate[layer], w_up=w_up[layer], w_down=w_down[layer],
                 g_ffn_post=g_ffn_post[layer])
        x2 = _layer(x2, pos_row, p)
    return x2.reshape(b, s, d)
```

```python
import functools
import math

import jax
import jax.numpy as jnp
import numpy as np
from jax import lax
from jax.experimental import pallas as pl
from jax.experimental.pallas import tpu as pltpu

F32 = jnp.float32
BF16 = jnp.bfloat16

MLA_HEADS = 8
MLA_NOPE = 128
MLA_ROPE = 64
MLA_QK = MLA_NOPE + MLA_ROPE
MLA_V = 128
Q_RANK = 512
KV_RANK = 512
ROPE_THETA = 10000.0
SWA_HEADS = 16
SWA_KV_HEADS = 2
SWA_GROUP = SWA_HEADS // SWA_KV_HEADS
SWA_DIM = 64
WINDOW = 128
NUM_BUCKETS = 32
MAX_DISTANCE = 128
BLK = 128
RMS_EPS = 1e-6
NEG = -1e30
LOG2E = math.log2(math.e)

LANES = 128
VMEM_LIMIT = 56 * 1024 * 1024


def _cparams(sem):
    return pltpu.CompilerParams(dimension_semantics=sem, vmem_limit_bytes=VMEM_LIMIT)


def _resident(shape):
    nd = len(shape)
    return pl.BlockSpec(shape, lambda *_: (0,) * nd, pipeline_mode=pl.Buffered(1))


def _proj_kernel(x_ref, g_ref, wa_ref, wb_ref, gcq_ref, gckv_ref,
                 cq_ref, ckv_ref, krt_ref, sqt_ref, sk_ref, svt_ref, *, swa_scale):
    x = x_ref[...]
    d = x.shape[-1]
    r = lax.rsqrt(jnp.sum(x * x, axis=-1, keepdims=True) * (1.0 / d) + RMS_EPS)
    h = (x * g_ref[...]).astype(BF16)
    ya = jnp.dot(h, wa_ref[...], preferred_element_type=F32) * r
    yb = jnp.dot(h, wb_ref[...], preferred_element_type=F32) * r

    def latent_norm(c, gain_ref):
        rc = lax.rsqrt(jnp.sum(c * c, axis=-1, keepdims=True) * (1.0 / c.shape[-1]) + RMS_EPS)
        return (c * rc * gain_ref[...]).astype(BF16)

    cq_ref[...] = latent_norm(ya[:, :Q_RANK], gcq_ref)
    ckv_ref[...] = latent_norm(ya[:, Q_RANK:Q_RANK + KV_RANK], gckv_ref)
    krt_ref[...] = ya[:, Q_RANK + KV_RANK:].T
    nq = SWA_HEADS * SWA_DIM
    nkv = SWA_KV_HEADS * SWA_DIM
    sqt_ref[...] = (yb[:, :nq] * swa_scale).T.astype(BF16)
    sk_ref[...] = yb[:, nq:nq + nkv].astype(BF16)
    svt_ref[...] = yb[:, nq + nkv:].T.astype(BF16)


def _proj(x2, g, wa, wb, g_cq, g_ckv, *, tm):
    s, d = x2.shape
    nq = SWA_HEADS * SWA_DIM
    nkv = SWA_KV_HEADS * SWA_DIM
    row = lambda w: pl.BlockSpec((tm, w), lambda i: (i, 0))
    col = lambda r: pl.BlockSpec((r, tm), lambda i: (0, i))
    return pl.pallas_call(
        functools.partial(_proj_kernel, swa_scale=LOG2E / math.sqrt(SWA_DIM)),
        grid=(s // tm,),
        in_specs=[row(d), _resident((1, d)), _resident(wa.shape), _resident(wb.shape),
                  _resident((1, Q_RANK)), _resident((1, KV_RANK))],
        out_specs=[row(Q_RANK), row(KV_RANK), col(LANES), col(nq), row(nkv), col(nkv)],
        out_shape=[jax.ShapeDtypeStruct((s, Q_RANK), BF16),
                   jax.ShapeDtypeStruct((s, KV_RANK), BF16),
                   jax.ShapeDtypeStruct((LANES, s), F32),
                   jax.ShapeDtypeStruct((nq, s), BF16),
                   jax.ShapeDtypeStruct((s, nkv), BF16),
                   jax.ShapeDtypeStruct((nkv, s), BF16)],
        compiler_params=_cparams(("arbitrary",)),
        name="proj",
    )(x2, g, wa, wb, g_cq, g_ckv)


_NT = (((1,), (1,)), ((), ()))


def _qkv_kernel(cq_ref, ckv_ref, krt_ref, pos_ref, invf_ref, wuqt_ref, wuk_ref, wuvt_ref,
                qt_ref, k_ref, vt_ref, *, q_scale):
    tm = cq_ref.shape[0]
    half = MLA_ROPE // 2
    ang = invf_ref[...] * pos_ref[...].astype(F32)
    cos = jnp.cos(ang)
    sin = jnp.sin(ang)

    def rope_t(t):
        t1, t2 = t[:half], t[half:]
        return jnp.concatenate([t1 * cos - t2 * sin, t2 * cos + t1 * sin], axis=0)

    cq = cq_ref[...]
    ckv = ckv_ref[...]
    qt = lax.dot_general(wuqt_ref[...], cq, _NT, preferred_element_type=F32) * q_scale
    for h in range(MLA_HEADS):
        base = h * MLA_QK
        qt_ref[h, 0:MLA_NOPE, :] = qt[base:base + MLA_NOPE].astype(BF16)
        qt_ref[h, MLA_NOPE:MLA_QK, :] = rope_t(qt[base + MLA_NOPE:base + MLA_QK]).astype(BF16)

    krt = krt_ref[...]
    kr = jnp.concatenate([rope_t(krt[:MLA_ROPE]), krt[MLA_ROPE:]], axis=0).T[:, :MLA_ROPE].astype(BF16)
    kn = jnp.dot(ckv, wuk_ref[...], preferred_element_type=F32)
    vt = lax.dot_general(wuvt_ref[...], ckv, _NT, preferred_element_type=F32)
    for h in range(MLA_HEADS):
        k_ref[h, :, 0:MLA_NOPE] = kn[:, h * MLA_NOPE:(h + 1) * MLA_NOPE].astype(BF16)
        k_ref[h, :, MLA_NOPE:MLA_QK] = kr
        for c in range(tm // LANES):
            vt_ref[h, c] = vt[h * MLA_V:(h + 1) * MLA_V, c * LANES:(c + 1) * LANES].astype(BF16)


def _qkv(cq, ckv, krt, pos_row, invf, wuqt, wuk, wuvt, *, tm):
    s = cq.shape[0]
    row = lambda w: pl.BlockSpec((tm, w), lambda i: (i, 0))
    col = lambda r: pl.BlockSpec((r, tm), lambda i: (0, i))
    nb = tm // LANES
    return pl.pallas_call(
        functools.partial(_qkv_kernel, q_scale=LOG2E / math.sqrt(MLA_QK)),
        grid=(s // tm,),
        in_specs=[row(Q_RANK), row(KV_RANK), col(LANES), col(1), _resident(invf.shape),
                  _resident(wuqt.shape), _resident(wuk.shape), _resident(wuvt.shape)],
        out_specs=[pl.BlockSpec((MLA_HEADS, MLA_QK, tm), lambda i: (0, 0, i)),
                   pl.BlockSpec((MLA_HEADS, tm, MLA_QK), lambda i: (0, i, 0)),
                   pl.BlockSpec((MLA_HEADS, nb, MLA_V, LANES), lambda i: (0, i, 0, 0))],
        out_shape=[jax.ShapeDtypeStruct((MLA_HEADS, MLA_QK, s), BF16),
                   jax.ShapeDtypeStruct((MLA_HEADS, s, MLA_QK), BF16),
                   jax.ShapeDtypeStruct((MLA_HEADS, s // LANES, MLA_V, LANES), BF16)],
        compiler_params=_cparams(("arbitrary",)),
        name="qkv",
    )(cq, ckv, krt, pos_row, invf, wuqt, wuk, wuvt)


def _mla_kernel(qt_ref, k_ref, vt_ref, o_ref, m_sc, l_sc, acc_sc, s_sc, *, tq, tk, hps):
    assert tq == 2 * tk
    qi = pl.program_id(1)
    nsub = tk // LANES
    heads = range(hps)
    left, right, full = slice(0, tk), slice(tk, tq), slice(0, tq)
    m_sc[...] = jnp.full(m_sc.shape, NEG, F32)
    l_sc[...] = jnp.zeros(l_sc.shape, F32)
    acc_sc[...] = jnp.zeros(acc_sc.shape, F32)

    def scores(hh, j, slot, cols=full):
        start = pl.multiple_of(j * tk, tk)
        s_sc[hh, slot, :, cols] = jnp.dot(k_ref[hh, pl.ds(start, tk), :], qt_ref[hh, :, cols],
                                          preferred_element_type=F32)

    def update(hh, j, slot, cols=full, tri_cols=None):
        s = s_sc[hh, slot, :, cols]
        if tri_cols is not None:
            tri = (lax.broadcasted_iota(jnp.int32, (tk, tk), 0)
                   <= lax.broadcasted_iota(jnp.int32, (tk, tk), 1))
            parts = [jnp.where(tri, s[:, :tk], NEG)] + ([s[:, tk:]] if s.shape[1] > tk else [])
            s = jnp.concatenate(parts, axis=1)
        m_old = m_sc[hh, :, cols]
        m_new = jnp.maximum(m_old, jnp.max(s, axis=0, keepdims=True))
        alpha = jnp.exp2(m_old - m_new)
        p = jnp.exp2(s - m_new)
        l_sc[hh, :, cols] = alpha * l_sc[hh, :, cols] + jnp.sum(p, axis=0, keepdims=True)
        vt = jnp.concatenate([vt_ref[hh, j * nsub + c] for c in range(nsub)], axis=1)
        acc_sc[hh, :, cols] = (alpha * acc_sc[hh, :, cols]
                               + jnp.dot(vt, p.astype(BF16), preferred_element_type=F32))
        m_sc[hh, :, cols] = m_new

    def pair(i, carry):
        for hh in heads:
            scores(hh, 2 * i + 1, 1)
        for hh in heads:
            update(hh, 2 * i, 0)
        for hh in heads:
            scores(hh, 2 * i + 2, 0)
        for hh in heads:
            update(hh, 2 * i + 1, 1)
        return carry

    for hh in heads:
        scores(hh, 0, 0)
    lax.fori_loop(0, qi, pair, 0)
    for hh in heads:
        scores(hh, 2 * qi + 1, 1, right)
    for hh in heads:
        update(hh, 2 * qi, 0, full, tri_cols=left)
    for hh in heads:
        update(hh, 2 * qi + 1, 1, right, tri_cols=right)
    for hh in heads:
        o_ref[:, hh * MLA_V:(hh + 1) * MLA_V] = (acc_sc[hh] / l_sc[hh]).T.astype(o_ref.dtype)


def _mla(qt, k, vt, *, tq, tk, hps):
    _, s, _ = k.shape
    return pl.pallas_call(
        functools.partial(_mla_kernel, tq=tq, tk=tk, hps=hps),
        grid=(MLA_HEADS // hps, s // tq),
        in_specs=[pl.BlockSpec((hps, MLA_QK, tq), lambda h, i: (h, 0, i)),
                  pl.BlockSpec((hps, s, MLA_QK), lambda h, i: (h, 0, 0)),
                  pl.BlockSpec((hps, s // LANES, MLA_V, LANES), lambda h, i: (h, 0, 0, 0))],
        out_specs=pl.BlockSpec((tq, hps * MLA_V), lambda h, i: (i, h)),
        out_shape=jax.ShapeDtypeStruct((s, MLA_HEADS * MLA_V), BF16),
        scratch_shapes=[pltpu.VMEM((hps, 1, tq), F32), pltpu.VMEM((hps, 1, tq), F32),
                        pltpu.VMEM((hps, MLA_V, tq), F32),
                        pltpu.VMEM((hps, 2, tk, tq), F32)],
        compiler_params=_cparams(("arbitrary", "arbitrary")),
        name="mla",
    )(qt, k, vt)


def _t5_bucket_table_t():
    i = np.arange(BLK)[None, :]
    j = np.arange(2 * BLK)[:, None]
    dist = i + BLK - j
    max_exact = NUM_BUCKETS // 2
    d = np.maximum(dist, 0)
    large = max_exact + (np.log(np.maximum(d, 1) / max_exact)
                         / np.log(MAX_DISTANCE / max_exact)
                         * (NUM_BUCKETS - max_exact)).astype(np.int32)
    large = np.minimum(large, NUM_BUCKETS - 1)
    bucket = np.where(d < max_exact, d, large).astype(np.int32)
    in_window = (dist >= 0) & (dist < WINDOW)
    return np.where(in_window, bucket, -1).astype(np.int32)


def _swa_kernel(relb_ref, sink_ref, bucket_ref, qt_ref, kc_ref, kp_ref, vtc_ref, vtp_ref,
                o_ref, bias_sc, sink_sc, *, nb):
    i = pl.program_id(0)
    gw = SWA_GROUP * BLK

    @pl.when(i == 0)
    def _():
        bucket = bucket_ref[...]
        for h in range(SWA_HEADS):
            c, g = divmod(h, SWA_GROUP)
            b = jnp.full(bucket.shape, NEG, F32)
            for t in range(NUM_BUCKETS):
                b = jnp.where(bucket == t, relb_ref[t, h] * LOG2E, b)
            bias_sc[1, c, :, g * BLK:(g + 1) * BLK] = b
            bias_sc[0, c, BLK:, g * BLK:(g + 1) * BLK] = b[BLK:]
            bias_sc[0, c, :BLK, g * BLK:(g + 1) * BLK] = jnp.full((BLK, BLK), NEG, F32)
            sink_sc[c, :, g * BLK:(g + 1) * BLK] = jnp.full((1, BLK), sink_ref[h] * LOG2E, F32)

    zeros = jnp.zeros((SWA_DIM, gw), BF16)
    for t in range(nb):
        tsl = slice(t * BLK, (t + 1) * BLK)
        k_prev = kp_ref[...] if t == 0 else kc_ref[(t - 1) * BLK:t * BLK, :]
        vt_prev = vtp_ref[...] if t == 0 else vtc_ref[:, (t - 1) * BLK:t * BLK]
        kband = jnp.concatenate([k_prev, kc_ref[tsl, :]], axis=0)
        general = 1 if t > 0 else jnp.where(i == 0, 0, 1)
        for c in range(SWA_KV_HEADS):
            qt = jnp.concatenate(
                [qt_ref[(c * SWA_GROUP + g) * SWA_DIM:(c * SWA_GROUP + g + 1) * SWA_DIM, tsl]
                 for g in range(SWA_GROUP)], axis=1)
            qt_ext = jnp.concatenate([qt, zeros] if c == 0 else [zeros, qt], axis=0)
            s = jnp.dot(kband, qt_ext, preferred_element_type=F32) + bias_sc[general, c]
            sink = sink_sc[c]
            m = jnp.maximum(jnp.max(s, axis=0, keepdims=True), sink)
            p = jnp.exp2(s - m)
            denom = jnp.sum(p, axis=0, keepdims=True) + jnp.exp2(sink - m)
            dsl = slice(c * SWA_DIM, (c + 1) * SWA_DIM)
            vt = jnp.concatenate([vt_prev[dsl, :], vtc_ref[dsl, tsl]], axis=1)
            ot = jnp.dot(vt, p.astype(BF16), preferred_element_type=F32) / denom
            for g2 in range(SWA_GROUP // 2):
                two = jnp.concatenate([ot[:, (2 * g2) * BLK:(2 * g2 + 1) * BLK],
                                       ot[:, (2 * g2 + 1) * BLK:(2 * g2 + 2) * BLK]], axis=0)
                col0 = (c * SWA_GROUP + 2 * g2) * SWA_DIM
                o_ref[tsl, col0:col0 + 2 * SWA_DIM] = two.T.astype(o_ref.dtype)


def _swa(sqt, sk, svt, sinks, rel_bias, *, nb):
    nq, s = sqt.shape
    nkv = sk.shape[1]
    tb = nb * BLK
    bucket = jnp.asarray(_t5_bucket_table_t())
    smem = pl.BlockSpec(memory_space=pltpu.SMEM)
    prev_blk = lambda i: jnp.maximum(i * nb - 1, 0)
    return pl.pallas_call(
        functools.partial(_swa_kernel, nb=nb),
        grid=(s // tb,),
        in_specs=[smem, smem, _resident((2 * BLK, BLK)),
                  pl.BlockSpec((nq, tb), lambda i: (0, i)),
                  pl.BlockSpec((tb, nkv), lambda i: (i, 0)),
                  pl.BlockSpec((BLK, nkv), lambda i: (prev_blk(i), 0)),
                  pl.BlockSpec((nkv, tb), lambda i: (0, i)),
                  pl.BlockSpec((nkv, BLK), lambda i: (0, prev_blk(i)))],
        out_specs=pl.BlockSpec((tb, nq), lambda i: (i, 0)),
        out_shape=jax.ShapeDtypeStruct((s, nq), BF16),
        scratch_shapes=[pltpu.VMEM((2, SWA_KV_HEADS, 2 * BLK, SWA_GROUP * BLK), F32),
                        pltpu.VMEM((SWA_KV_HEADS, 1, SWA_GROUP * BLK), F32)],
        compiler_params=_cparams(("arbitrary",)),
        name="swa",
    )(rel_bias, sinks, bucket, sqt, sk, sk, svt, svt)


def _rms(y, gain):
    r = lax.rsqrt(jnp.sum(y * y, axis=-1, keepdims=True) * (1.0 / y.shape[-1]) + RMS_EPS)
    return y * r * gain


def _oproj_kernel(a_ref, b_ref, wo_ref, x_ref, gpost_ref, gpre_ref, x1_ref, h_ref):
    na = a_ref.shape[1]
    mix = (jnp.dot(a_ref[...], wo_ref[:na, :], preferred_element_type=F32)
           + jnp.dot(b_ref[...], wo_ref[na:, :], preferred_element_type=F32))
    x1 = x_ref[...] + _rms(mix, gpost_ref[...])
    x1_ref[...] = x1
    h_ref[...] = _rms(x1, gpre_ref[...]).astype(BF16)


def _oproj(out_a, out_b, w_o, x2, g_post, g_pre, *, tm):
    s, d = x2.shape
    row = lambda w: pl.BlockSpec((tm, w), lambda i: (i, 0))
    return pl.pallas_call(
        _oproj_kernel,
        grid=(s // tm,),
        in_specs=[row(out_a.shape[1]), row(out_b.shape[1]), _resident(w_o.shape),
                  row(d), _resident((1, d)), _resident((1, d))],
        out_specs=[row(d), row(d)],
        out_shape=[jax.ShapeDtypeStruct((s, d), F32), jax.ShapeDtypeStruct((s, d), BF16)],
        compiler_params=_cparams(("arbitrary",)),
        name="oproj",
    )(out_a, out_b, w_o, x2, g_post, g_pre)


def _ffn_kernel(h_ref, wg_ref, wu_ref, wd_ref, x1_ref, gpost_ref, o_ref, acc_sc):
    j = pl.program_id(1)

    @pl.when(j == 0)
    def _():
        acc_sc[...] = jnp.zeros(acc_sc.shape, F32)

    h = h_ref[...]
    tf = wg_ref.shape[1]
    halves = [slice(0, tf // 2), slice(tf // 2, tf)]
    gu = [(jnp.dot(h, wg_ref[:, c], preferred_element_type=F32),
           jnp.dot(h, wu_ref[:, c], preferred_element_type=F32)) for c in halves]
    for c, (gate, up) in zip(halves, gu):
        act = (gate * jax.nn.sigmoid(gate) * up).astype(BF16)
        acc_sc[...] += jnp.dot(act, wd_ref[c, :], preferred_element_type=F32)

    @pl.when(j == pl.num_programs(1) - 1)
    def _():
        o_ref[...] = x1_ref[...] + _rms(acc_sc[...], gpost_ref[...])


def _ffn(h, wg, wu, wd, x1, g_post, *, tm, tf):
    s, d = x1.shape
    dff = wg.shape[1]
    return pl.pallas_call(
        _ffn_kernel,
        grid=(s // tm, dff // tf),
        in_specs=[pl.BlockSpec((tm, d), lambda i, j: (i, 0)),
                  pl.BlockSpec((d, tf), lambda i, j: (0, j)),
                  pl.BlockSpec((d, tf), lambda i, j: (0, j)),
                  pl.BlockSpec((tf, d), lambda i, j: (j, 0)),
                  pl.BlockSpec((tm, d), lambda i, j: (i, 0)),
                  _resident((1, d))],
        out_specs=pl.BlockSpec((tm, d), lambda i, j: (i, 0)),
        out_shape=jax.ShapeDtypeStruct((s, d), F32),
        scratch_shapes=[pltpu.VMEM((tm, d), F32)],
        compiler_params=_cparams(("arbitrary", "arbitrary")),
        name="ffn",
    )(h, wg, wu, wd, x1, g_post)


def _split_w_in(w_in):
    na = Q_RANK + KV_RANK + MLA_ROPE
    wa = jnp.pad(w_in[:, :na], ((0, 0), (0, LANES - MLA_ROPE)))
    return wa.astype(BF16), w_in[:, na:].astype(BF16)


def _layer(x2, pos_row, p, *, tm_qkv=256):
    half = MLA_ROPE // 2
    inv_freq = ROPE_THETA ** (-jnp.arange(half, dtype=F32) / half)
    invf = jnp.broadcast_to(inv_freq[:, None], (half, tm_qkv))

    row = lambda a: a[None, :]
    wa, wb = _split_w_in(p["w_in"])
    cq, ckv, krt, sqt, sk, svt = _proj(x2, row(p["g_mix_pre"]), wa, wb,
                                       row(p["g_cq"]), row(p["g_ckv"]), tm=256)
    w_ukv = p["w_ukv"].reshape(KV_RANK, MLA_HEADS, MLA_NOPE + MLA_V)
    wuk = w_ukv[:, :, :MLA_NOPE].reshape(KV_RANK, MLA_HEADS * MLA_NOPE)
    wuvt = w_ukv[:, :, MLA_NOPE:].reshape(KV_RANK, MLA_HEADS * MLA_V).T
    qt, k, vt = _qkv(cq, ckv, krt, pos_row, invf, p["w_uq"].T.astype(BF16),
                     wuk.astype(BF16), wuvt.astype(BF16), tm=tm_qkv)
    out_a = _mla(qt, k, vt, tq=1024, tk=512, hps=1)
    out_b = _swa(sqt, sk, svt, p["sinks"], p["rel_bias"], nb=4)
    x1, h = _oproj(out_a, out_b, p["w_o"].astype(BF16), x2,
                   row(p["g_mix_post"]), row(p["g_ffn_pre"]), tm=256)
    return _ffn(h, p["w_gate"].astype(BF16), p["w_up"].astype(BF16),
                p["w_down"].astype(BF16), x1, row(p["g_ffn_post"]), tm=512, tf=512)


def kernel(x, positions, g_mix_pre, w_in, g_cq, g_ckv, w_uq, w_ukv, sinks, rel_bias,
           w_o, g_mix_post, g_ffn_pre, w_gate, w_up, w_down, g_ffn_post):
    b, s, d = x.shape
    assert b == 1, "the row-major (S, D) pipeline assumes a single sequence"
    x2 = x.reshape(s, d)
    pos_row = positions.reshape(1, s)
    for layer in range(w_in.shape[0]):
        p = dict(g_mix_pre=g_mix_pre[layer], w_in=w_in[layer], g_cq=g_cq[layer],
                 g_ckv=g_ckv[layer], w_uq=w_uq[layer], w_ukv=w_ukv[layer],
                 sinks=sinks[layer], rel_bias=rel_bias, w_o=w_o[layer],
                 g_mix_post=g_mix_post[layer], g_ffn_pre=g_ffn_pre[layer],
                 w_gate=w_gate[layer], w_up=w_up[layer], w_down=w_down[layer],
                 g_ffn_post=g_ffn_post[layer])
        x2 = _layer(x2, pos_row, p)
    return x2.reshape(b, s, d)
```

```python
import functools
import math

import jax
import jax.numpy as jnp
import numpy as np
from jax import lax
from jax.experimental import pallas as pl
from jax.experimental.pallas import tpu as pltpu

F32 = jnp.float32
BF16 = jnp.bfloat16

MLA_HEADS = 8
MLA_NOPE = 128
MLA_ROPE = 64
MLA_QK = MLA_NOPE + MLA_ROPE
MLA_V = 128
Q_RANK = 512
KV_RANK = 512
ROPE_THETA = 10000.0
SWA_HEADS = 16
SWA_KV_HEADS = 2
SWA_GROUP = SWA_HEADS // SWA_KV_HEADS
SWA_DIM = 64
WINDOW = 128
NUM_BUCKETS = 32
MAX_DISTANCE = 128
BLK = 128
RMS_EPS = 1e-6
NEG = -1e30
LOG2E = math.log2(math.e)

LANES = 128
VMEM_LIMIT = 60 * 1024 * 1024


def _cparams(sem):
    return pltpu.CompilerParams(dimension_semantics=sem, vmem_limit_bytes=VMEM_LIMIT)


def _resident(shape):
    nd = len(shape)
    return pl.BlockSpec(shape, lambda *_: (0,) * nd, pipeline_mode=pl.Buffered(1))


def _proj_kernel(x_ref, g_ref, wa_ref, wb_ref, gcq_ref, gckv_ref,
                 cq_ref, ckv_ref, krt_ref, sqt_ref, sk_ref, svt_ref, *, swa_scale):
    x = x_ref[...]
    d = x.shape[-1]
    r = lax.rsqrt(jnp.sum(x * x, axis=-1, keepdims=True) * (1.0 / d) + RMS_EPS)
    h = (x * g_ref[...]).astype(BF16)
    ya = jnp.dot(h, wa_ref[...], preferred_element_type=F32) * r
    yb = jnp.dot(h, wb_ref[...], preferred_element_type=F32) * r

    def latent_norm(c, gain_ref):
        rc = lax.rsqrt(jnp.sum(c * c, axis=-1, keepdims=True) * (1.0 / c.shape[-1]) + RMS_EPS)
        return (c * rc * gain_ref[...]).astype(BF16)

    cq_ref[...] = latent_norm(ya[:, :Q_RANK], gcq_ref)
    ckv_ref[...] = latent_norm(ya[:, Q_RANK:Q_RANK + KV_RANK], gckv_ref)
    krt_ref[...] = ya[:, Q_RANK + KV_RANK:].T
    nq = SWA_HEADS * SWA_DIM
    nkv = SWA_KV_HEADS * SWA_DIM
    sqt_ref[...] = (yb[:, :nq] * swa_scale).T.astype(BF16)
    sk_ref[...] = yb[:, nq:nq + nkv].astype(BF16)
    svt_ref[...] = yb[:, nq + nkv:].T.astype(BF16)


def _proj(x2, g, wa, wb, g_cq, g_ckv, *, tm):
    s, d = x2.shape
    nq = SWA_HEADS * SWA_DIM
    nkv = SWA_KV_HEADS * SWA_DIM
    row = lambda w: pl.BlockSpec((tm, w), lambda i: (i, 0))
    col = lambda r: pl.BlockSpec((r, tm), lambda i: (0, i))
    return pl.pallas_call(
        functools.partial(_proj_kernel, swa_scale=LOG2E / math.sqrt(SWA_DIM)),
        grid=(s // tm,),
        in_specs=[row(d), _resident((1, d)), _resident(wa.shape), _resident(wb.shape),
                  _resident((1, Q_RANK)), _resident((1, KV_RANK))],
        out_specs=[row(Q_RANK), row(KV_RANK), col(LANES), col(nq), row(nkv), col(nkv)],
        out_shape=[jax.ShapeDtypeStruct((s, Q_RANK), BF16),
                   jax.ShapeDtypeStruct((s, KV_RANK), BF16),
                   jax.ShapeDtypeStruct((LANES, s), F32),
                   jax.ShapeDtypeStruct((nq, s), BF16),
                   jax.ShapeDtypeStruct((s, nkv), BF16),
                   jax.ShapeDtypeStruct((nkv, s), BF16)],
        compiler_params=_cparams(("arbitrary",)),
        name="proj",
    )(x2, g, wa, wb, g_cq, g_ckv)


_NT = (((1,), (1,)), ((), ()))


def _qkv_kernel(cq_ref, ckv_ref, krt_ref, pos_ref, invf_ref, wuqt_ref, wuk_ref, wuvt_ref,
                qt_ref, k_ref, vt_ref, *, q_scale):
    tm = cq_ref.shape[0]
    half = MLA_ROPE // 2
    ang = invf_ref[...] * pos_ref[...].astype(F32)
    cos = jnp.cos(ang)
    sin = jnp.sin(ang)

    def rope_t(t):
        t1, t2 = t[:half], t[half:]
        return jnp.concatenate([t1 * cos - t2 * sin, t2 * cos + t1 * sin], axis=0)

    cq = cq_ref[...]
    ckv = ckv_ref[...]
    qt = lax.dot_general(wuqt_ref[...], cq, _NT, preferred_element_type=F32) * q_scale
    for h in range(MLA_HEADS):
        base = h * MLA_QK
        qt_ref[h, 0:MLA_NOPE, :] = qt[base:base + MLA_NOPE].astype(BF16)
        qt_ref[h, MLA_NOPE:MLA_QK, :] = rope_t(qt[base + MLA_NOPE:base + MLA_QK]).astype(BF16)

    krt = krt_ref[...]
    kr = jnp.concatenate([rope_t(krt[:MLA_ROPE]), krt[MLA_ROPE:]], axis=0).T[:, :MLA_ROPE].astype(BF16)
    kn = jnp.dot(ckv, wuk_ref[...], preferred_element_type=F32)
    vt = lax.dot_general(wuvt_ref[...], ckv, _NT, preferred_element_type=F32)
    for h in range(MLA_HEADS):
        k_ref[h, :, 0:MLA_NOPE] = kn[:, h * MLA_NOPE:(h + 1) * MLA_NOPE].astype(BF16)
        k_ref[h, :, MLA_NOPE:MLA_QK] = kr
        for c in range(tm // LANES):
            vt_ref[h, c] = vt[h * MLA_V:(h + 1) * MLA_V, c * LANES:(c + 1) * LANES].astype(BF16)


def _qkv(cq, ckv, krt, pos_row, invf, wuqt, wuk, wuvt, *, tm):
    s = cq.shape[0]
    row = lambda w: pl.BlockSpec((tm, w), lambda i: (i, 0))
    col = lambda r: pl.BlockSpec((r, tm), lambda i: (0, i))
    nb = tm // LANES
    return pl.pallas_call(
        functools.partial(_qkv_kernel, q_scale=LOG2E / math.sqrt(MLA_QK)),
        grid=(s // tm,),
        in_specs=[row(Q_RANK), row(KV_RANK), col(LANES), col(1), _resident(invf.shape),
                  _resident(wuqt.shape), _resident(wuk.shape), _resident(wuvt.shape)],
        out_specs=[pl.BlockSpec((MLA_HEADS, MLA_QK, tm), lambda i: (0, 0, i)),
                   pl.BlockSpec((MLA_HEADS, tm, MLA_QK), lambda i: (0, i, 0)),
                   pl.BlockSpec((MLA_HEADS, nb, MLA_V, LANES), lambda i: (0, i, 0, 0))],
        out_shape=[jax.ShapeDtypeStruct((MLA_HEADS, MLA_QK, s), BF16),
                   jax.ShapeDtypeStruct((MLA_HEADS, s, MLA_QK), BF16),
                   jax.ShapeDtypeStruct((MLA_HEADS, s // LANES, MLA_V, LANES), BF16)],
        compiler_params=_cparams(("arbitrary",)),
        name="qkv",
    )(cq, ckv, krt, pos_row, invf, wuqt, wuk, wuvt)


def _mla_kernel(qt_ref, k_ref, vt_ref, o_ref, m_sc, l_sc, acc_sc, s_sc, *, tq, tk, hps):
    assert tq == 2 * tk
    qi = pl.program_id(1)
    nsub = tk // LANES
    heads = range(hps)
    left, right, full = slice(0, tk), slice(tk, tq), slice(0, tq)
    m_sc[...] = jnp.full(m_sc.shape, NEG, F32)
    l_sc[...] = jnp.zeros(l_sc.shape, F32)
    acc_sc[...] = jnp.zeros(acc_sc.shape, F32)

    def scores(hh, j, slot, cols=full):
        start = pl.multiple_of(j * tk, tk)
        s_sc[hh, slot, :, cols] = jnp.dot(k_ref[hh, pl.ds(start, tk), :], qt_ref[hh, :, cols],
                                          preferred_element_type=F32)

    def update(hh, j, slot, cols=full, tri_cols=None):
        s = s_sc[hh, slot, :, cols]
        if tri_cols is not None:
            tri = (lax.broadcasted_iota(jnp.int32, (tk, tk), 0)
                   <= lax.broadcasted_iota(jnp.int32, (tk, tk), 1))
            parts = [jnp.where(tri, s[:, :tk], NEG)] + ([s[:, tk:]] if s.shape[1] > tk else [])
            s = jnp.concatenate(parts, axis=1)
        m_old = m_sc[hh, :, cols]
        m_new = jnp.maximum(m_old, jnp.max(s, axis=0, keepdims=True))
        alpha = jnp.exp2(m_old - m_new)
        p = jnp.exp2(s - m_new)
        l_sc[hh, :, cols] = alpha * l_sc[hh, :, cols] + jnp.sum(p, axis=0, keepdims=True)
        vt = jnp.concatenate([vt_ref[hh, j * nsub + c] for c in range(nsub)], axis=1)
        acc_sc[hh, :, cols] = (alpha * acc_sc[hh, :, cols]
                               + jnp.dot(vt, p.astype(BF16), preferred_element_type=F32))
        m_sc[hh, :, cols] = m_new

    def pair(i, carry):
        for hh in heads:
            scores(hh, 2 * i + 1, 1)
        for hh in heads:
            update(hh, 2 * i, 0)
        for hh in heads:
            scores(hh, 2 * i + 2, 0)
        for hh in heads:
            update(hh, 2 * i + 1, 1)
        return carry

    for hh in heads:
        scores(hh, 0, 0)
    lax.fori_loop(0, qi, pair, 0)
    for hh in heads:
        scores(hh, 2 * qi + 1, 1, right)
    for hh in heads:
        update(hh, 2 * qi, 0, full, tri_cols=left)
    for hh in heads:
        update(hh, 2 * qi + 1, 1, right, tri_cols=right)
    for hh in heads:
        o_ref[:, hh * MLA_V:(hh + 1) * MLA_V] = (acc_sc[hh] / l_sc[hh]).T.astype(o_ref.dtype)


def _mla(qt, k, vt, *, tq, tk, hps):
    _, s, _ = k.shape
    return pl.pallas_call(
        functools.partial(_mla_kernel, tq=tq, tk=tk, hps=hps),
        grid=(MLA_HEADS // hps, s // tq),
        in_specs=[pl.BlockSpec((hps, MLA_QK, tq), lambda h, i: (h, 0, i)),
                  pl.BlockSpec((hps, s, MLA_QK), lambda h, i: (h, 0, 0)),
                  pl.BlockSpec((hps, s // LANES, MLA_V, LANES), lambda h, i: (h, 0, 0, 0))],
        out_specs=pl.BlockSpec((tq, hps * MLA_V), lambda h, i: (i, h)),
        out_shape=jax.ShapeDtypeStruct((s, MLA_HEADS * MLA_V), BF16),
        scratch_shapes=[pltpu.VMEM((hps, 1, tq), F32), pltpu.VMEM((hps, 1, tq), F32),
                        pltpu.VMEM((hps, MLA_V, tq), F32),
                        pltpu.VMEM((hps, 2, tk, tq), F32)],
        compiler_params=_cparams(("arbitrary", "arbitrary")),
        name="mla",
    )(qt, k, vt)


def _t5_bucket_table_t():
    i = np.arange(BLK)[None, :]
    j = np.arange(2 * BLK)[:, None]
    dist = i + BLK - j
    max_exact = NUM_BUCKETS // 2
    d = np.maximum(dist, 0)
    large = max_exact + (np.log(np.maximum(d, 1) / max_exact)
                         / np.log(MAX_DISTANCE / max_exact)
                         * (NUM_BUCKETS - max_exact)).astype(np.int32)
    large = np.minimum(large, NUM_BUCKETS - 1)
    bucket = np.where(d < max_exact, d, large).astype(np.int32)
    in_window = (dist >= 0) & (dist < WINDOW)
    return np.where(in_window, bucket, -1).astype(np.int32)


def _swa_kernel(relb_ref, sink_ref, bucket_ref, qt_ref, kc_ref, kp_ref, vtc_ref, vtp_ref,
                o_ref, bias_sc, sink_sc, *, nb):
    i = pl.program_id(0)
    gw = SWA_GROUP * BLK

    @pl.when(i == 0)
    def _():
        bucket = bucket_ref[...]
        for h in range(SWA_HEADS):
            c, g = divmod(h, SWA_GROUP)
            b = jnp.full(bucket.shape, NEG, F32)
            for t in range(NUM_BUCKETS):
                b = jnp.where(bucket == t, relb_ref[t, h] * LOG2E, b)
            bias_sc[1, c, :, g * BLK:(g + 1) * BLK] = b
            bias_sc[0, c, BLK:, g * BLK:(g + 1) * BLK] = b[BLK:]
            bias_sc[0, c, :BLK, g * BLK:(g + 1) * BLK] = jnp.full((BLK, BLK), NEG, F32)
            sink_sc[c, :, g * BLK:(g + 1) * BLK] = jnp.full((1, BLK), sink_ref[h] * LOG2E, F32)

    zeros = jnp.zeros((SWA_DIM, gw), BF16)
    for t in range(nb):
        tsl = slice(t * BLK, (t + 1) * BLK)
        k_prev = kp_ref[...] if t == 0 else kc_ref[(t - 1) * BLK:t * BLK, :]
        vt_prev = vtp_ref[...] if t == 0 else vtc_ref[:, (t - 1) * BLK:t * BLK]
        kband = jnp.concatenate([k_prev, kc_ref[tsl, :]], axis=0)
        general = 1 if t > 0 else jnp.where(i == 0, 0, 1)
        for c in range(SWA_KV_HEADS):
            qt = jnp.concatenate(
                [qt_ref[(c * SWA_GROUP + g) * SWA_DIM:(c * SWA_GROUP + g + 1) * SWA_DIM, tsl]
                 for g in range(SWA_GROUP)], axis=1)
            qt_ext = jnp.concatenate([qt, zeros] if c == 0 else [zeros, qt], axis=0)
            s = jnp.dot(kband, qt_ext, preferred_element_type=F32) + bias_sc[general, c]
            sink = sink_sc[c]
            m = jnp.maximum(jnp.max(s, axis=0, keepdims=True), sink)
            p = jnp.exp2(s - m)
            denom = jnp.sum(p, axis=0, keepdims=True) + jnp.exp2(sink - m)
            dsl = slice(c * SWA_DIM, (c + 1) * SWA_DIM)
            vt = jnp.concatenate([vt_prev[dsl, :], vtc_ref[dsl, tsl]], axis=1)
            ot = jnp.dot(vt, p.astype(BF16), preferred_element_type=F32) / denom
            for g2 in range(SWA_GROUP // 2):
                two = jnp.concatenate([ot[:, (2 * g2) * BLK:(2 * g2 + 1) * BLK],
                                       ot[:, (2 * g2 + 1) * BLK:(2 * g2 + 2) * BLK]], axis=0)
                col0 = (c * SWA_GROUP + 2 * g2) * SWA_DIM
                o_ref[tsl, col0:col0 + 2 * SWA_DIM] = two.T.astype(o_ref.dtype)


def _swa(sqt, sk, svt, sinks, rel_bias, *, nb):
    nq, s = sqt.shape
    nkv = sk.shape[1]
    tb = nb * BLK
    bucket = jnp.asarray(_t5_bucket_table_t())
    smem = pl.BlockSpec(memory_space=pltpu.SMEM)
    prev_blk = lambda i: jnp.maximum(i * nb - 1, 0)
    return pl.pallas_call(
        functools.partial(_swa_kernel, nb=nb),
        grid=(s // tb,),
        in_specs=[smem, smem, _resident((2 * BLK, BLK)),
                  pl.BlockSpec((nq, tb), lambda i: (0, i)),
                  pl.BlockSpec((tb, nkv), lambda i: (i, 0)),
                  pl.BlockSpec((BLK, nkv), lambda i: (prev_blk(i), 0)),
                  pl.BlockSpec((nkv, tb), lambda i: (0, i)),
                  pl.BlockSpec((nkv, BLK), lambda i: (0, prev_blk(i)))],
        out_specs=pl.BlockSpec((tb, nq), lambda i: (i, 0)),
        out_shape=jax.ShapeDtypeStruct((s, nq), BF16),
        scratch_shapes=[pltpu.VMEM((2, SWA_KV_HEADS, 2 * BLK, SWA_GROUP * BLK), F32),
                        pltpu.VMEM((SWA_KV_HEADS, 1, SWA_GROUP * BLK), F32)],
        compiler_params=_cparams(("arbitrary",)),
        name="swa",
    )(rel_bias, sinks, bucket, sqt, sk, sk, svt, svt)


def _rms(y, gain):
    r = lax.rsqrt(jnp.sum(y * y, axis=-1, keepdims=True) * (1.0 / y.shape[-1]) + RMS_EPS)
    return y * r * gain


def _oproj_kernel(a_ref, b_ref, wo_ref, x_ref, gpost_ref, gpre_ref, x1_ref, h_ref):
    na = a_ref.shape[1]
    mix = (jnp.dot(a_ref[...], wo_ref[:na, :], preferred_element_type=F32)
           + jnp.dot(b_ref[...], wo_ref[na:, :], preferred_element_type=F32))
    x1 = x_ref[...] + _rms(mix, gpost_ref[...])
    x1_ref[...] = x1
    h_ref[...] = _rms(x1, gpre_ref[...]).astype(BF16)


def _oproj(out_a, out_b, w_o, x2, g_post, g_pre, *, tm):
    s, d = x2.shape
    row = lambda w: pl.BlockSpec((tm, w), lambda i: (i, 0))
    return pl.pallas_call(
        _oproj_kernel,
        grid=(s // tm,),
        in_specs=[row(out_a.shape[1]), row(out_b.shape[1]), _resident(w_o.shape),
                  row(d), _resident((1, d)), _resident((1, d))],
        out_specs=[row(d), row(d)],
        out_shape=[jax.ShapeDtypeStruct((s, d), F32), jax.ShapeDtypeStruct((s, d), BF16)],
        compiler_params=_cparams(("arbitrary",)),
        name="oproj",
    )(out_a, out_b, w_o, x2, g_post, g_pre)


def _ffn_kernel(h_ref, wg_ref, wu_ref, wd_ref, x1_hbm, gpost_ref, o_ref, x1_buf, x1_sem):
    i = pl.program_id(0)
    j = pl.program_id(1)
    tm = o_ref.shape[0]

    def x1_copy():
        rows = pl.ds(pl.multiple_of(i * tm, tm), tm)
        return pltpu.make_async_copy(x1_hbm.at[rows, :], x1_buf, x1_sem)

    @pl.when(j == 0)
    def _():
        x1_copy().start()
        o_ref[...] = jnp.zeros(o_ref.shape, F32)

    h = h_ref[...]
    tf = wg_ref.shape[1]
    halves = [slice(0, tf // 2), slice(tf // 2, tf)]
    gu = [(jnp.dot(h, wg_ref[:, c], preferred_element_type=F32),
           jnp.dot(h, wu_ref[:, c], preferred_element_type=F32)) for c in halves]
    for c, (gate, up) in zip(halves, gu):
        act = (gate * jax.nn.sigmoid(gate) * up).astype(BF16)
        o_ref[...] += jnp.dot(act, wd_ref[c, :], preferred_element_type=F32)

    @pl.when(j == pl.num_programs(1) - 1)
    def _():
        x1_copy().wait()
        o_ref[...] = x1_buf[...] + _rms(o_ref[...], gpost_ref[...])


def _ffn(h, wg, wu, wd, x1, g_post, *, tm, tf):
    s, d = x1.shape
    dff = wg.shape[1]
    return pl.pallas_call(
        _ffn_kernel,
        grid=(s // tm, dff // tf),
        in_specs=[pl.BlockSpec((tm, d), lambda i, j: (i, 0)),
                  pl.BlockSpec((d, tf), lambda i, j: (0, j)),
                  pl.BlockSpec((d, tf), lambda i, j: (0, j)),
                  pl.BlockSpec((tf, d), lambda i, j: (j, 0)),
                  pl.BlockSpec(memory_space=pl.ANY),
                  _resident((1, d))],
        out_specs=pl.BlockSpec((tm, d), lambda i, j: (i, 0)),
        out_shape=jax.ShapeDtypeStruct((s, d), F32),
        scratch_shapes=[pltpu.VMEM((tm, d), F32), pltpu.SemaphoreType.DMA(())],
        compiler_params=_cparams(("arbitrary", "arbitrary")),
        name="ffn",
    )(h, wg, wu, wd, x1, g_post)


def _split_w_in(w_in):
    na = Q_RANK + KV_RANK + MLA_ROPE
    wa = jnp.pad(w_in[:, :na], ((0, 0), (0, LANES - MLA_ROPE)))
    return wa.astype(BF16), w_in[:, na:].astype(BF16)


def _layer(x2, pos_row, p, *, tm_qkv=256):
    half = MLA_ROPE // 2
    inv_freq = ROPE_THETA ** (-jnp.arange(half, dtype=F32) / half)
    invf = jnp.broadcast_to(inv_freq[:, None], (half, tm_qkv))

    row = lambda a: a[None, :]
    wa, wb = _split_w_in(p["w_in"])
    cq, ckv, krt, sqt, sk, svt = _proj(x2, row(p["g_mix_pre"]), wa, wb,
                                       row(p["g_cq"]), row(p["g_ckv"]), tm=256)
    w_ukv = p["w_ukv"].reshape(KV_RANK, MLA_HEADS, MLA_NOPE + MLA_V)
    wuk = w_ukv[:, :, :MLA_NOPE].reshape(KV_RANK, MLA_HEADS * MLA_NOPE)
    wuvt = w_ukv[:, :, MLA_NOPE:].reshape(KV_RANK, MLA_HEADS * MLA_V).T
    qt, k, vt = _qkv(cq, ckv, krt, pos_row, invf, p["w_uq"].T.astype(BF16),
                     wuk.astype(BF16), wuvt.astype(BF16), tm=tm_qkv)
    out_a = _mla(qt, k, vt, tq=1024, tk=512, hps=1)
    out_b = _swa(sqt, sk, svt, p["sinks"], p["rel_bias"], nb=4)
    x1, h = _oproj(out_a, out_b, p["w_o"].astype(BF16), x2,
                   row(p["g_mix_post"]), row(p["g_ffn_pre"]), tm=256)
    return _ffn(h, p["w_gate"].astype(BF16), p["w_up"].astype(BF16),
                p["w_down"].astype(BF16), x1, row(p["g_ffn_post"]), tm=1024, tf=512)


def kernel(x, positions, g_mix_pre, w_in, g_cq, g_ckv, w_uq, w_ukv, sinks, rel_bias,
           w_o, g_mix_post, g_ffn_pre, w_gate, w_up, w_down, g_ffn_post):
    b, s, d = x.shape
    assert b == 1, "the row-major (S, D) pipeline assumes a single sequence"
    x2 = x.reshape(s, d)
    pos_row = positions.reshape(1, s)
    for layer in range(w_in.shape[0]):
        p = dict(g_mix_pre=g_mix_pre[layer], w_in=w_in[layer], g_cq=g_cq[layer],
                 g_ckv=g_ckv[layer], w_uq=w_uq[layer], w_ukv=w_ukv[layer],
                 sinks=sinks[layer], rel_bias=rel_bias, w_o=w_o[layer],
                 g_mix_post=g_mix_post[layer], g_ffn_pre=g_ffn_pre[layer],
                 w_gate=w_gate[layer], w_up=w_up[layer], w_down=w_down[layer],
                 g_ffn_post=g_ffn_post[layer])
        x2 = _layer(x2, pos_row, p)
    return x2.reshape(b, s, d)
```

```python
import functools
import math

import jax
import jax.numpy as jnp
import numpy as np
from jax import lax
from jax.experimental import pallas as pl
from jax.experimental.pallas import tpu as pltpu

F32 = jnp.float32
BF16 = jnp.bfloat16

MLA_HEADS = 8
MLA_NOPE = 128
MLA_ROPE = 64
MLA_QK = MLA_NOPE + MLA_ROPE
MLA_V = 128
Q_RANK = 512
KV_RANK = 512
ROPE_THETA = 10000.0
SWA_HEADS = 16
SWA_KV_HEADS = 2
SWA_GROUP = SWA_HEADS // SWA_KV_HEADS
SWA_DIM = 64
WINDOW = 128
NUM_BUCKETS = 32
MAX_DISTANCE = 128
BLK = 128
RMS_EPS = 1e-6
NEG = -1e30
LOG2E = math.log2(math.e)

LANES = 128
VMEM_LIMIT = 60 * 1024 * 1024


def _cparams(sem):
    return pltpu.CompilerParams(dimension_semantics=sem, vmem_limit_bytes=VMEM_LIMIT)


def _resident(shape):
    nd = len(shape)
    return pl.BlockSpec(shape, lambda *_: (0,) * nd, pipeline_mode=pl.Buffered(1))


def _proj_kernel(x_ref, g_ref, w_ref, gcq_ref, gckv_ref,
                 cq_ref, ckv_ref, krt_ref, sqt_ref, sk_ref, svt_ref, *, swa_scale, sub):
    nq = SWA_HEADS * SWA_DIM
    nkv = SWA_KV_HEADS * SWA_DIM

    def latent_norm(c, gain_ref):
        rc = lax.rsqrt(jnp.sum(c * c, axis=-1, keepdims=True) * (1.0 / c.shape[-1]) + RMS_EPS)
        return (c * rc * gain_ref[...]).astype(BF16)

    for r0 in range(0, x_ref.shape[0], sub):
        rows = slice(r0, r0 + sub)
        x = x_ref[rows, :]
        r = lax.rsqrt(jnp.sum(x * x, axis=-1, keepdims=True) * (1.0 / x.shape[-1]) + RMS_EPS)
        h = (x * g_ref[...]).astype(BF16)
        y = jnp.dot(h, w_ref[...], preferred_element_type=F32) * r
        cq_ref[rows, :] = latent_norm(y[:, :Q_RANK], gcq_ref)
        ckv_ref[rows, :] = latent_norm(y[:, Q_RANK:Q_RANK + KV_RANK], gckv_ref)
        yt = y[:, Q_RANK + KV_RANK:].T
        o = MLA_ROPE
        krt_ref[:, rows] = yt[:o]
        sqt_ref[:, rows] = (yt[o:o + nq] * swa_scale).astype(BF16)
        sk_ref[rows, :] = yt[o + nq:o + nq + nkv].T.astype(BF16)
        svt_ref[:, rows] = yt[o + nq + nkv:o + nq + 2 * nkv].astype(BF16)


def _proj(x2, g, w_in_p, g_cq, g_ckv, *, tm, sub):
    s, d = x2.shape
    nq = SWA_HEADS * SWA_DIM
    nkv = SWA_KV_HEADS * SWA_DIM
    row = lambda w: pl.BlockSpec((tm, w), lambda i: (i, 0))
    col = lambda r: pl.BlockSpec((r, tm), lambda i: (0, i))
    return pl.pallas_call(
        functools.partial(_proj_kernel, swa_scale=LOG2E / math.sqrt(SWA_DIM), sub=sub),
        grid=(s // tm,),
        in_specs=[row(d), _resident((1, d)), _resident(w_in_p.shape),
                  _resident((1, Q_RANK)), _resident((1, KV_RANK))],
        out_specs=[row(Q_RANK), row(KV_RANK), col(MLA_ROPE), col(nq), row(nkv), col(nkv)],
        out_shape=[jax.ShapeDtypeStruct((s, Q_RANK), BF16),
                   jax.ShapeDtypeStruct((s, KV_RANK), BF16),
                   jax.ShapeDtypeStruct((MLA_ROPE, s), F32),
                   jax.ShapeDtypeStruct((nq, s), BF16),
                   jax.ShapeDtypeStruct((s, nkv), BF16),
                   jax.ShapeDtypeStruct((nkv, s), BF16)],
        compiler_params=_cparams(("arbitrary",)),
        name="proj",
    )(x2, g, w_in_p, g_cq, g_ckv)


_NT = (((1,), (1,)), ((), ()))


def _qkv_kernel(cq_ref, ckv_ref, krt_ref, pos_ref, invf_ref, wuqt_ref, wuk_ref, wuvt_ref,
                qt_ref, k_ref, vt_ref, *, q_scale):
    tm = cq_ref.shape[0]
    half = MLA_ROPE // 2
    ang = invf_ref[...] * pos_ref[...].astype(F32)
    cos = jnp.cos(ang)
    sin = jnp.sin(ang)

    def rope_t(t):
        t1, t2 = t[:half], t[half:]
        return jnp.concatenate([t1 * cos - t2 * sin, t2 * cos + t1 * sin], axis=0)

    cq = cq_ref[...]
    ckv = ckv_ref[...]
    qt = lax.dot_general(wuqt_ref[...], cq, _NT, preferred_element_type=F32) * q_scale
    for h in range(MLA_HEADS):
        base = h * MLA_QK
        qt_ref[h, 0:MLA_NOPE, :] = qt[base:base + MLA_NOPE].astype(BF16)
        qt_ref[h, MLA_NOPE:MLA_QK, :] = rope_t(qt[base + MLA_NOPE:base + MLA_QK]).astype(BF16)

    krt = jnp.concatenate([rope_t(krt_ref[...]), jnp.zeros((LANES - MLA_ROPE, tm), F32)], axis=0)
    kr = krt.T[:, :MLA_ROPE].astype(BF16)
    kn = jnp.dot(ckv, wuk_ref[...], preferred_element_type=F32)
    vt = lax.dot_general(wuvt_ref[...], ckv, _NT, preferred_element_type=F32)
    for h in range(MLA_HEADS):
        k_ref[h, :, 0:MLA_NOPE] = kn[:, h * MLA_NOPE:(h + 1) * MLA_NOPE].astype(BF16)
        k_ref[h, :, MLA_NOPE:MLA_QK] = kr
        for c in range(tm // LANES):
            vt_ref[h, c] = vt[h * MLA_V:(h + 1) * MLA_V, c * LANES:(c + 1) * LANES].astype(BF16)


def _qkv(cq, ckv, krt, pos_row, invf, wuqt, wuk, wuvt, *, tm):
    s = cq.shape[0]
    row = lambda w: pl.BlockSpec((tm, w), lambda i: (i, 0))
    col = lambda r: pl.BlockSpec((r, tm), lambda i: (0, i))
    nb = tm // LANES
    return pl.pallas_call(
        functools.partial(_qkv_kernel, q_scale=LOG2E / math.sqrt(MLA_QK)),
        grid=(s // tm,),
        in_specs=[row(Q_RANK), row(KV_RANK), col(MLA_ROPE), col(1), _resident(invf.shape),
                  _resident(wuqt.shape), _resident(wuk.shape), _resident(wuvt.shape)],
        out_specs=[pl.BlockSpec((MLA_HEADS, MLA_QK, tm), lambda i: (0, 0, i)),
                   pl.BlockSpec((MLA_HEADS, tm, MLA_QK), lambda i: (0, i, 0)),
                   pl.BlockSpec((MLA_HEADS, nb, MLA_V, LANES), lambda i: (0, i, 0, 0))],
        out_shape=[jax.ShapeDtypeStruct((MLA_HEADS, MLA_QK, s), BF16),
                   jax.ShapeDtypeStruct((MLA_HEADS, s, MLA_QK), BF16),
                   jax.ShapeDtypeStruct((MLA_HEADS, s // LANES, MLA_V, LANES), BF16)],
        compiler_params=_cparams(("arbitrary",)),
        name="qkv",
    )(cq, ckv, krt, pos_row, invf, wuqt, wuk, wuvt)


def _mla_kernel(qt_ref, k_ref, vt_ref, o_ref, m_sc, l_sc, acc_sc, s_sc, *, tq, tk):
    group = tq // tk
    assert group * tk == tq and group % 2 == 0
    qi = pl.program_id(1)
    nsub = tk // LANES
    full = slice(0, tq)
    m_sc[...] = jnp.full(m_sc.shape, NEG, F32)
    l_sc[...] = jnp.zeros(l_sc.shape, F32)
    acc_sc[...] = jnp.zeros(acc_sc.shape, F32)

    def scores(j, slot, cols=full):
        start = pl.multiple_of(j * tk, tk)
        s_sc[slot, :, cols] = jnp.dot(k_ref[0, pl.ds(start, tk), :], qt_ref[0, :, cols],
                                      preferred_element_type=F32)

    def update(j, slot, cols=full, tri=False):
        s = s_sc[slot, :, cols]
        if tri:
            mask = (lax.broadcasted_iota(jnp.int32, (tk, tk), 0)
                    <= lax.broadcasted_iota(jnp.int32, (tk, tk), 1))
            parts = [jnp.where(mask, s[:, :tk], NEG)] + ([s[:, tk:]] if s.shape[1] > tk else [])
            s = jnp.concatenate(parts, axis=1)
        m_old = m_sc[:, cols]
        m_new = jnp.maximum(m_old, jnp.max(s, axis=0, keepdims=True))
        alpha = jnp.exp2(m_old - m_new)
        p = jnp.exp2(s - m_new)
        l_sc[:, cols] = alpha * l_sc[:, cols] + jnp.sum(p, axis=0, keepdims=True)
        vt = jnp.concatenate([vt_ref[0, j * nsub + c] for c in range(nsub)], axis=1)
        acc_sc[:, cols] = (alpha * acc_sc[:, cols]
                           + jnp.dot(vt, p.astype(BF16), preferred_element_type=F32))
        m_sc[:, cols] = m_new

    def full_group(i, carry):
        for r in range(group):
            b = group * i + r
            scores(b + 1, (r + 1) % 2)
            update(b, r % 2)
        return carry

    scores(0, 0)
    lax.fori_loop(0, qi, full_group, 0)
    for r in range(group):
        b = group * qi + r
        if r + 1 < group:
            scores(b + 1, (r + 1) % 2, slice((r + 1) * tk, tq))
        update(b, r % 2, slice(r * tk, tq), tri=True)
    o_ref[...] = (acc_sc[...] / l_sc[...]).T.astype(o_ref.dtype)


def _mla(qt, k, vt, *, tq, tk):
    _, s, _ = k.shape
    return pl.pallas_call(
        functools.partial(_mla_kernel, tq=tq, tk=tk),
        grid=(MLA_HEADS, s // tq),
        in_specs=[pl.BlockSpec((1, MLA_QK, tq), lambda h, i: (h, 0, i)),
                  pl.BlockSpec((1, s, MLA_QK), lambda h, i: (h, 0, 0)),
                  pl.BlockSpec((1, s // LANES, MLA_V, LANES), lambda h, i: (h, 0, 0, 0))],
        out_specs=pl.BlockSpec((tq, MLA_V), lambda h, i: (i, h)),
        out_shape=jax.ShapeDtypeStruct((s, MLA_HEADS * MLA_V), BF16),
        scratch_shapes=[pltpu.VMEM((1, tq), F32), pltpu.VMEM((1, tq), F32),
                        pltpu.VMEM((MLA_V, tq), F32),
                        pltpu.VMEM((2, tk, tq), F32)],
        compiler_params=_cparams(("arbitrary", "arbitrary")),
        name="mla",
    )(qt, k, vt)


def _t5_bucket_table_t():
    i = np.arange(BLK)[None, :]
    j = np.arange(2 * BLK)[:, None]
    dist = i + BLK - j
    max_exact = NUM_BUCKETS // 2
    d = np.maximum(dist, 0)
    large = max_exact + (np.log(np.maximum(d, 1) / max_exact)
                         / np.log(MAX_DISTANCE / max_exact)
                         * (NUM_BUCKETS - max_exact)).astype(np.int32)
    large = np.minimum(large, NUM_BUCKETS - 1)
    bucket = np.where(d < max_exact, d, large).astype(np.int32)
    in_window = (dist >= 0) & (dist < WINDOW)
    return np.where(in_window, bucket, -1).astype(np.int32)


def _swa_kernel(relb_ref, sink_ref, bucket_ref, qt_ref, kc_ref, kp_ref, vtc_ref, vtp_ref,
                o_ref, bias_sc, sink_sc, *, nb):
    i = pl.program_id(0)
    gw = SWA_GROUP * BLK

    @pl.when(i == 0)
    def _():
        bucket = bucket_ref[...]
        for h in range(SWA_HEADS):
            c, g = divmod(h, SWA_GROUP)
            b = jnp.full(bucket.shape, NEG, F32)
            for t in range(NUM_BUCKETS):
                b = jnp.where(bucket == t, relb_ref[t, h] * LOG2E, b)
            bias_sc[1, c, :, g * BLK:(g + 1) * BLK] = b
            bias_sc[0, c, BLK:, g * BLK:(g + 1) * BLK] = b[BLK:]
            bias_sc[0, c, :BLK, g * BLK:(g + 1) * BLK] = jnp.full((BLK, BLK), NEG, F32)
            sink_sc[c, :, g * BLK:(g + 1) * BLK] = jnp.full((1, BLK), sink_ref[h] * LOG2E, F32)

    zeros = jnp.zeros((SWA_DIM, gw), BF16)
    for t in range(nb):
        tsl = slice(t * BLK, (t + 1) * BLK)
        k_prev = kp_ref[...] if t == 0 else kc_ref[(t - 1) * BLK:t * BLK, :]
        vt_prev = vtp_ref[...] if t == 0 else vtc_ref[:, (t - 1) * BLK:t * BLK]
        kband = jnp.concatenate([k_prev, kc_ref[tsl, :]], axis=0)
        general = 1 if t > 0 else jnp.where(i == 0, 0, 1)
        for c in range(SWA_KV_HEADS):
            qt = jnp.concatenate(
                [qt_ref[(c * SWA_GROUP + g) * SWA_DIM:(c * SWA_GROUP + g + 1) * SWA_DIM, tsl]
                 for g in range(SWA_GROUP)], axis=1)
            qt_ext = jnp.concatenate([qt, zeros] if c == 0 else [zeros, qt], axis=0)
            s = jnp.dot(kband, qt_ext, preferred_element_type=F32) + bias_sc[general, c]
            sink = sink_sc[c]
            m = jnp.maximum(jnp.max(s, axis=0, keepdims=True), sink)
            p = jnp.exp2(s - m)
            denom = jnp.sum(p, axis=0, keepdims=True) + jnp.exp2(sink - m)
            dsl = slice(c * SWA_DIM, (c + 1) * SWA_DIM)
            vt = jnp.concatenate([vt_prev[dsl, :], vtc_ref[dsl, tsl]], axis=1)
            ot = jnp.dot(vt, p.astype(BF16), preferred_element_type=F32) / denom
            for g2 in range(SWA_GROUP // 2):
                two = jnp.concatenate([ot[:, (2 * g2) * BLK:(2 * g2 + 1) * BLK],
                                       ot[:, (2 * g2 + 1) * BLK:(2 * g2 + 2) * BLK]], axis=0)
                col0 = (c * SWA_GROUP + 2 * g2) * SWA_DIM
                o_ref[tsl, col0:col0 + 2 * SWA_DIM] = two.T.astype(o_ref.dtype)


def _swa(sqt, sk, svt, sinks, rel_bias, *, nb):
    nq, s = sqt.shape
    nkv = sk.shape[1]
    tb = nb * BLK
    bucket = jnp.asarray(_t5_bucket_table_t())
    smem = pl.BlockSpec(memory_space=pltpu.SMEM)
    prev_blk = lambda i: jnp.maximum(i * nb - 1, 0)
    return pl.pallas_call(
        functools.partial(_swa_kernel, nb=nb),
        grid=(s // tb,),
        in_specs=[smem, smem, _resident((2 * BLK, BLK)),
                  pl.BlockSpec((nq, tb), lambda i: (0, i)),
                  pl.BlockSpec((tb, nkv), lambda i: (i, 0)),
                  pl.BlockSpec((BLK, nkv), lambda i: (prev_blk(i), 0)),
                  pl.BlockSpec((nkv, tb), lambda i: (0, i)),
                  pl.BlockSpec((nkv, BLK), lambda i: (0, prev_blk(i)))],
        out_specs=pl.BlockSpec((tb, nq), lambda i: (i, 0)),
        out_shape=jax.ShapeDtypeStruct((s, nq), BF16),
        scratch_shapes=[pltpu.VMEM((2, SWA_KV_HEADS, 2 * BLK, SWA_GROUP * BLK), F32),
                        pltpu.VMEM((SWA_KV_HEADS, 1, SWA_GROUP * BLK), F32)],
        compiler_params=_cparams(("arbitrary",)),
        name="swa",
    )(rel_bias, sinks, bucket, sqt, sk, sk, svt, svt)


def _rms(y, gain):
    r = lax.rsqrt(jnp.sum(y * y, axis=-1, keepdims=True) * (1.0 / y.shape[-1]) + RMS_EPS)
    return y * r * gain


def _oproj_kernel(a_ref, b_ref, wo_ref, x_ref, gpost_ref, gpre_ref, x1_ref, h_ref, *, sub):
    na = a_ref.shape[1]
    for r0 in range(0, x_ref.shape[0], sub):
        rows = slice(r0, r0 + sub)
        mix = (jnp.dot(a_ref[rows, :], wo_ref[:na, :], preferred_element_type=F32)
               + jnp.dot(b_ref[rows, :], wo_ref[na:, :], preferred_element_type=F32))
        x1 = x_ref[rows, :] + _rms(mix, gpost_ref[...])
        x1_ref[rows, :] = x1
        h_ref[rows, :] = _rms(x1, gpre_ref[...]).astype(BF16)


def _oproj(out_a, out_b, w_o, x2, g_post, g_pre, *, tm, sub):
    s, d = x2.shape
    row = lambda w: pl.BlockSpec((tm, w), lambda i: (i, 0))
    return pl.pallas_call(
        functools.partial(_oproj_kernel, sub=sub),
        grid=(s // tm,),
        in_specs=[row(out_a.shape[1]), row(out_b.shape[1]), _resident(w_o.shape),
                  row(d), _resident((1, d)), _resident((1, d))],
        out_specs=[row(d), row(d)],
        out_shape=[jax.ShapeDtypeStruct((s, d), F32), jax.ShapeDtypeStruct((s, d), BF16)],
        compiler_params=_cparams(("arbitrary",)),
        name="oproj",
    )(out_a, out_b, w_o, x2, g_post, g_pre)


def _ffn_kernel(h_ref, wg_ref, wu_ref, wd_ref, x1_hbm, gpost_ref, o_ref, x1_buf, x1_sem):
    i = pl.program_id(0)
    j = pl.program_id(1)
    tm = o_ref.shape[0]

    def x1_copy():
        rows = pl.ds(pl.multiple_of(i * tm, tm), tm)
        return pltpu.make_async_copy(x1_hbm.at[rows, :], x1_buf, x1_sem)

    @pl.when(j == 0)
    def _():
        x1_copy().start()
        o_ref[...] = jnp.zeros(o_ref.shape, F32)

    h = h_ref[...]
    tf = wg_ref.shape[1]
    halves = [slice(0, tf // 2), slice(tf // 2, tf)]
    gu = [(jnp.dot(h, wg_ref[:, c], preferred_element_type=F32),
           jnp.dot(h, wu_ref[:, c], preferred_element_type=F32)) for c in halves]
    for c, (gate, up) in zip(halves, gu):
        act = (gate * jax.nn.sigmoid(gate) * up).astype(BF16)
        o_ref[...] += jnp.dot(act, wd_ref[c, :], preferred_element_type=F32)

    @pl.when(j == pl.num_programs(1) - 1)
    def _():
        x1_copy().wait()
        o_ref[...] = x1_buf[...] + _rms(o_ref[...], gpost_ref[...])


def _ffn(h, wg, wu, wd, x1, g_post, *, tm, tf):
    s, d = x1.shape
    dff = wg.shape[1]
    return pl.pallas_call(
        _ffn_kernel,
        grid=(s // tm, dff // tf),
        in_specs=[pl.BlockSpec((tm, d), lambda i, j: (i, 0)),
                  pl.BlockSpec((d, tf), lambda i, j: (0, j)),
                  pl.BlockSpec((d, tf), lambda i, j: (0, j)),
                  pl.BlockSpec((tf, d), lambda i, j: (j, 0)),
                  pl.BlockSpec(memory_space=pl.ANY),
                  _resident((1, d))],
        out_specs=pl.BlockSpec((tm, d), lambda i, j: (i, 0)),
        out_shape=jax.ShapeDtypeStruct((s, d), F32),
        scratch_shapes=[pltpu.VMEM((tm, d), F32), pltpu.SemaphoreType.DMA(())],
        compiler_params=_cparams(("arbitrary", "arbitrary")),
        name="ffn",
    )(h, wg, wu, wd, x1, g_post)


def _pad_lanes(w):
    return jnp.pad(w, ((0, 0), (0, -w.shape[1] % LANES)))


def _layer(x2, pos_row, p, *, tm_qkv=256):
    half = MLA_ROPE // 2
    inv_freq = ROPE_THETA ** (-jnp.arange(half, dtype=F32) / half)
    invf = jnp.broadcast_to(inv_freq[:, None], (half, tm_qkv))

    row = lambda a: a[None, :]
    cq, ckv, krt, sqt, sk, svt = _proj(x2, row(p["g_mix_pre"]), _pad_lanes(p["w_in"]).astype(BF16),
                                       row(p["g_cq"]), row(p["g_ckv"]), tm=512, sub=256)
    w_ukv = p["w_ukv"].reshape(KV_RANK, MLA_HEADS, MLA_NOPE + MLA_V)
    wuk = w_ukv[:, :, :MLA_NOPE].reshape(KV_RANK, MLA_HEADS * MLA_NOPE)
    wuvt = w_ukv[:, :, MLA_NOPE:].reshape(KV_RANK, MLA_HEADS * MLA_V).T
    qt, k, vt = _qkv(cq, ckv, krt, pos_row, invf, p["w_uq"].T.astype(BF16),
                     wuk.astype(BF16), wuvt.astype(BF16), tm=tm_qkv)
    out_a = _mla(qt, k, vt, tq=1024, tk=512)
    out_b = _swa(sqt, sk, svt, p["sinks"], p["rel_bias"], nb=4)
    x1, h = _oproj(out_a, out_b, p["w_o"].astype(BF16), x2,
                   row(p["g_mix_post"]), row(p["g_ffn_pre"]), tm=512, sub=256)
    return _ffn(h, p["w_gate"].astype(BF16), p["w_up"].astype(BF16),
                p["w_down"].astype(BF16), x1, row(p["g_ffn_post"]), tm=1024, tf=512)


def kernel(x, positions, g_mix_pre, w_in, g_cq, g_ckv, w_uq, w_ukv, sinks, rel_bias,
           w_o, g_mix_post, g_ffn_pre, w_gate, w_up, w_down, g_ffn_post):
    b, s, d = x.shape
    assert b == 1, "the row-major (S, D) pipeline assumes a single sequence"
    x2 = x.reshape(s, d)
    pos_row = positions.reshape(1, s)
    for layer in range(w_in.shape[0]):
        p = dict(g_mix_pre=g_mix_pre[layer], w_in=w_in[layer], g_cq=g_cq[layer],
                 g_ckv=g_ckv[layer], w_uq=w_uq[layer], w_ukv=w_ukv[layer],
                 sinks=sinks[layer], rel_bias=rel_bias, w_o=w_o[layer],
                 g_mix_post=g_mix_post[layer], g_ffn_pre=g_ffn_pre[layer],
                 w_gate=w_gate[layer], w_up=w_up[layer], w_down=w_down[layer],
                 g_ffn_post=g_ffn_post[layer])
        x2 = _layer(x2, pos_row, p)
    return x2.reshape(b, s, d)
```

```python
import functools
import math

import jax
import jax.numpy as jnp
import numpy as np
from jax import lax
from jax.experimental import pallas as pl
from jax.experimental.pallas import tpu as pltpu

F32 = jnp.float32
BF16 = jnp.bfloat16

MLA_HEADS = 8
MLA_NOPE = 128
MLA_ROPE = 64
MLA_QK = MLA_NOPE + MLA_ROPE
MLA_V = 128
Q_RANK = 512
KV_RANK = 512
ROPE_THETA = 10000.0
SWA_HEADS = 16
SWA_KV_HEADS = 2
SWA_GROUP = SWA_HEADS // SWA_KV_HEADS
SWA_DIM = 64
WINDOW = 128
NUM_BUCKETS = 32
MAX_DISTANCE = 128
BLK = 128
RMS_EPS = 1e-6
NEG = -1e30
LOG2E = math.log2(math.e)

LANES = 128
VMEM_LIMIT = 60 * 1024 * 1024


def _cparams(sem):
    return pltpu.CompilerParams(dimension_semantics=sem, vmem_limit_bytes=VMEM_LIMIT)


def _resident(shape):
    nd = len(shape)
    return pl.BlockSpec(shape, lambda *_: (0,) * nd, pipeline_mode=pl.Buffered(1))


def _proj_kernel(x_ref, g_ref, w_ref, gcq_ref, gckv_ref,
                 cq_ref, ckv_ref, krt_ref, sqt_ref, sk_ref, svt_ref, *, swa_scale, sub):
    nq = SWA_HEADS * SWA_DIM
    nkv = SWA_KV_HEADS * SWA_DIM

    def latent_norm(c, gain_ref):
        rc = lax.rsqrt(jnp.sum(c * c, axis=-1, keepdims=True) * (1.0 / c.shape[-1]) + RMS_EPS)
        return (c * rc * gain_ref[...]).astype(BF16)

    for r0 in range(0, x_ref.shape[0], sub):
        rows = slice(r0, r0 + sub)
        x = x_ref[rows, :]
        r = lax.rsqrt(jnp.sum(x * x, axis=-1, keepdims=True) * (1.0 / x.shape[-1]) + RMS_EPS)
        h = (x * g_ref[...]).astype(BF16)
        y = jnp.dot(h, w_ref[...], preferred_element_type=F32) * r
        cq_ref[rows, :] = latent_norm(y[:, :Q_RANK], gcq_ref)
        ckv_ref[rows, :] = latent_norm(y[:, Q_RANK:Q_RANK + KV_RANK], gckv_ref)
        yt = y[:, Q_RANK + KV_RANK:].T
        o = MLA_ROPE
        krt_ref[:, rows] = yt[:o]
        sqt_ref[:, rows] = (yt[o:o + nq] * swa_scale).astype(BF16)
        sk_ref[rows, :] = yt[o + nq:o + nq + nkv].T.astype(BF16)
        svt_ref[:, rows] = yt[o + nq + nkv:o + nq + 2 * nkv].astype(BF16)


def _proj(x2, g, w_in_p, g_cq, g_ckv, *, tm, sub):
    s, d = x2.shape
    nq = SWA_HEADS * SWA_DIM
    nkv = SWA_KV_HEADS * SWA_DIM
    row = lambda w: pl.BlockSpec((tm, w), lambda i: (i, 0))
    col = lambda r: pl.BlockSpec((r, tm), lambda i: (0, i))
    return pl.pallas_call(
        functools.partial(_proj_kernel, swa_scale=LOG2E / math.sqrt(SWA_DIM), sub=sub),
        grid=(s // tm,),
        in_specs=[row(d), _resident((1, d)), _resident(w_in_p.shape),
                  _resident((1, Q_RANK)), _resident((1, KV_RANK))],
        out_specs=[row(Q_RANK), row(KV_RANK), col(MLA_ROPE), col(nq), row(nkv), col(nkv)],
        out_shape=[jax.ShapeDtypeStruct((s, Q_RANK), BF16),
                   jax.ShapeDtypeStruct((s, KV_RANK), BF16),
                   jax.ShapeDtypeStruct((MLA_ROPE, s), F32),
                   jax.ShapeDtypeStruct((nq, s), BF16),
                   jax.ShapeDtypeStruct((s, nkv), BF16),
                   jax.ShapeDtypeStruct((nkv, s), BF16)],
        compiler_params=_cparams(("arbitrary",)),
        name="proj",
    )(x2, g, w_in_p, g_cq, g_ckv)


_NT = (((1,), (1,)), ((), ()))


def _qkv_kernel(cq_ref, ckv_ref, krt_ref, pos_ref, invf_ref, wuqt_ref, wuk_ref, wuvt_ref,
                qt_ref, k_ref, vt_ref, *, q_scale):
    tm = cq_ref.shape[0]
    half = MLA_ROPE // 2
    ang = invf_ref[...] * pos_ref[...].astype(F32)
    cos = jnp.cos(ang)
    sin = jnp.sin(ang)

    def rope_t(t):
        t1, t2 = t[:half], t[half:]
        return jnp.concatenate([t1 * cos - t2 * sin, t2 * cos + t1 * sin], axis=0)

    cq = cq_ref[...]
    ckv = ckv_ref[...]
    qt = lax.dot_general(wuqt_ref[...], cq, _NT, preferred_element_type=F32) * q_scale
    for h in range(MLA_HEADS):
        base = h * MLA_QK
        qt_ref[h, 0:MLA_NOPE, :] = qt[base:base + MLA_NOPE].astype(BF16)
        qt_ref[h, MLA_NOPE:MLA_QK, :] = rope_t(qt[base + MLA_NOPE:base + MLA_QK]).astype(BF16)

    krt = jnp.concatenate([rope_t(krt_ref[...]), jnp.zeros((LANES - MLA_ROPE, tm), F32)], axis=0)
    kr = krt.T[:, :MLA_ROPE].astype(BF16)
    kn = jnp.dot(ckv, wuk_ref[...], preferred_element_type=F32)
    vt = lax.dot_general(wuvt_ref[...], ckv, _NT, preferred_element_type=F32)
    for h in range(MLA_HEADS):
        k_ref[h, :, 0:MLA_NOPE] = kn[:, h * MLA_NOPE:(h + 1) * MLA_NOPE].astype(BF16)
        k_ref[h, :, MLA_NOPE:MLA_QK] = kr
        for c in range(tm // LANES):
            vt_ref[h, c] = vt[h * MLA_V:(h + 1) * MLA_V, c * LANES:(c + 1) * LANES].astype(BF16)


def _qkv(cq, ckv, krt, pos_row, invf, wuqt, wuk, wuvt, *, tm):
    s = cq.shape[0]
    row = lambda w: pl.BlockSpec((tm, w), lambda i: (i, 0))
    col = lambda r: pl.BlockSpec((r, tm), lambda i: (0, i))
    nb = tm // LANES
    return pl.pallas_call(
        functools.partial(_qkv_kernel, q_scale=LOG2E / math.sqrt(MLA_QK)),
        grid=(s // tm,),
        in_specs=[row(Q_RANK), row(KV_RANK), col(MLA_ROPE), col(1), _resident(invf.shape),
                  _resident(wuqt.shape), _resident(wuk.shape), _resident(wuvt.shape)],
        out_specs=[pl.BlockSpec((MLA_HEADS, MLA_QK, tm), lambda i: (0, 0, i)),
                   pl.BlockSpec((MLA_HEADS, tm, MLA_QK), lambda i: (0, i, 0)),
                   pl.BlockSpec((MLA_HEADS, nb, MLA_V, LANES), lambda i: (0, i, 0, 0))],
        out_shape=[jax.ShapeDtypeStruct((MLA_HEADS, MLA_QK, s), BF16),
                   jax.ShapeDtypeStruct((MLA_HEADS, s, MLA_QK), BF16),
                   jax.ShapeDtypeStruct((MLA_HEADS, s // LANES, MLA_V, LANES), BF16)],
        compiler_params=_cparams(("arbitrary",)),
        name="qkv",
    )(cq, ckv, krt, pos_row, invf, wuqt, wuk, wuvt)


def _mla_kernel(qt_ref, k_ref, vt_ref, o_ref, m_sc, l_sc, acc_sc, s_sc, *, tq, tk, qps):
    group = tq // tk
    assert group * tk == tq and group % 2 == 0
    nsub = tk // LANES
    full = slice(0, tq)

    def one_query_block(qi, q0):
        m_sc[...] = jnp.full(m_sc.shape, NEG, F32)
        l_sc[...] = jnp.zeros(l_sc.shape, F32)
        acc_sc[...] = jnp.zeros(acc_sc.shape, F32)

        def scores(j, slot, cols=full):
            start = pl.multiple_of(j * tk, tk)
            qcols = slice(q0 + cols.start, q0 + cols.stop)
            s_sc[slot, :, cols] = jnp.dot(k_ref[0, pl.ds(start, tk), :], qt_ref[0, :, qcols],
                                          preferred_element_type=F32)

        def update(j, slot, cols=full, tri=False):
            s = s_sc[slot, :, cols]
            if tri:
                mask = (lax.broadcasted_iota(jnp.int32, (tk, tk), 0)
                        <= lax.broadcasted_iota(jnp.int32, (tk, tk), 1))
                parts = [jnp.where(mask, s[:, :tk], NEG)] + ([s[:, tk:]] if s.shape[1] > tk else [])
                s = jnp.concatenate(parts, axis=1)
            m_old = m_sc[:, cols]
            m_new = jnp.maximum(m_old, jnp.max(s, axis=0, keepdims=True))
            alpha = jnp.exp2(m_old - m_new)
            p = jnp.exp2(s - m_new)
            l_sc[:, cols] = alpha * l_sc[:, cols] + jnp.sum(p, axis=0, keepdims=True)
            vt = jnp.concatenate([vt_ref[0, j * nsub + c] for c in range(nsub)], axis=1)
            acc_sc[:, cols] = (alpha * acc_sc[:, cols]
                               + jnp.dot(vt, p.astype(BF16), preferred_element_type=F32))
            m_sc[:, cols] = m_new

        def full_group(i, carry):
            for r in range(group):
                b = group * i + r
                scores(b + 1, (r + 1) % 2)
                update(b, r % 2)
            return carry

        scores(0, 0)
        lax.fori_loop(0, qi, full_group, 0)
        for r in range(group):
            b = group * qi + r
            if r + 1 < group:
                scores(b + 1, (r + 1) % 2, slice((r + 1) * tk, tq))
            update(b, r % 2, slice(r * tk, tq), tri=True)
        o_ref[q0:q0 + tq, :] = (acc_sc[...] / l_sc[...]).T.astype(o_ref.dtype)

    for t in range(qps):
        one_query_block(pl.program_id(1) * qps + t, t * tq)


def _mla(qt, k, vt, *, tq, tk, qps):
    _, s, _ = k.shape
    return pl.pallas_call(
        functools.partial(_mla_kernel, tq=tq, tk=tk, qps=qps),
        grid=(MLA_HEADS, s // (tq * qps)),
        in_specs=[pl.BlockSpec((1, MLA_QK, tq * qps), lambda h, i: (h, 0, i)),
                  pl.BlockSpec((1, s, MLA_QK), lambda h, i: (h, 0, 0)),
                  pl.BlockSpec((1, s // LANES, MLA_V, LANES), lambda h, i: (h, 0, 0, 0))],
        out_specs=pl.BlockSpec((tq * qps, MLA_V), lambda h, i: (i, h)),
        out_shape=jax.ShapeDtypeStruct((s, MLA_HEADS * MLA_V), BF16),
        scratch_shapes=[pltpu.VMEM((1, tq), F32), pltpu.VMEM((1, tq), F32),
                        pltpu.VMEM((MLA_V, tq), F32),
                        pltpu.VMEM((2, tk, tq), F32)],
        compiler_params=_cparams(("arbitrary", "arbitrary")),
        name="mla",
    )(qt, k, vt)


def _t5_bucket_table_t():
    i = np.arange(BLK)[None, :]
    j = np.arange(2 * BLK)[:, None]
    dist = i + BLK - j
    max_exact = NUM_BUCKETS // 2
    d = np.maximum(dist, 0)
    large = max_exact + (np.log(np.maximum(d, 1) / max_exact)
                         / np.log(MAX_DISTANCE / max_exact)
                         * (NUM_BUCKETS - max_exact)).astype(np.int32)
    large = np.minimum(large, NUM_BUCKETS - 1)
    bucket = np.where(d < max_exact, d, large).astype(np.int32)
    in_window = (dist >= 0) & (dist < WINDOW)
    return np.where(in_window, bucket, -1).astype(np.int32)


def _swa_kernel(relb_ref, sink_ref, bucket_ref, qt_ref, kc_ref, kp_ref, vtc_ref, vtp_ref,
                o_ref, bias_sc, sink_sc, *, nb):
    i = pl.program_id(0)
    gw = SWA_GROUP * BLK

    @pl.when(i == 0)
    def _():
        bucket = bucket_ref[...]
        for h in range(SWA_HEADS):
            c, g = divmod(h, SWA_GROUP)
            b = jnp.full(bucket.shape, NEG, F32)
            for t in range(NUM_BUCKETS):
                b = jnp.where(bucket == t, relb_ref[t, h] * LOG2E, b)
            bias_sc[1, c, :, g * BLK:(g + 1) * BLK] = b
            bias_sc[0, c, BLK:, g * BLK:(g + 1) * BLK] = b[BLK:]
            bias_sc[0, c, :BLK, g * BLK:(g + 1) * BLK] = jnp.full((BLK, BLK), NEG, F32)
            sink_sc[c, :, g * BLK:(g + 1) * BLK] = jnp.full((1, BLK), sink_ref[h] * LOG2E, F32)

    zeros = jnp.zeros((SWA_DIM, gw), BF16)
    for t in range(nb):
        tsl = slice(t * BLK, (t + 1) * BLK)
        k_prev = kp_ref[...] if t == 0 else kc_ref[(t - 1) * BLK:t * BLK, :]
        vt_prev = vtp_ref[...] if t == 0 else vtc_ref[:, (t - 1) * BLK:t * BLK]
        kband = jnp.concatenate([k_prev, kc_ref[tsl, :]], axis=0)
        general = 1 if t > 0 else jnp.where(i == 0, 0, 1)
        for c in range(SWA_KV_HEADS):
            qt = jnp.concatenate(
                [qt_ref[(c * SWA_GROUP + g) * SWA_DIM:(c * SWA_GROUP + g + 1) * SWA_DIM, tsl]
                 for g in range(SWA_GROUP)], axis=1)
            qt_ext = jnp.concatenate([qt, zeros] if c == 0 else [zeros, qt], axis=0)
            s = jnp.dot(kband, qt_ext, preferred_element_type=F32) + bias_sc[general, c]
            sink = sink_sc[c]
            m = jnp.maximum(jnp.max(s, axis=0, keepdims=True), sink)
            p = jnp.exp2(s - m)
            denom = jnp.sum(p, axis=0, keepdims=True) + jnp.exp2(sink - m)
            dsl = slice(c * SWA_DIM, (c + 1) * SWA_DIM)
            vt = jnp.concatenate([vt_prev[dsl, :], vtc_ref[dsl, tsl]], axis=1)
            ot = jnp.dot(vt, p.astype(BF16), preferred_element_type=F32) / denom
            for g2 in range(SWA_GROUP // 2):
                two = jnp.concatenate([ot[:, (2 * g2) * BLK:(2 * g2 + 1) * BLK],
                                       ot[:, (2 * g2 + 1) * BLK:(2 * g2 + 2) * BLK]], axis=0)
                col0 = (c * SWA_GROUP + 2 * g2) * SWA_DIM
                o_ref[tsl, col0:col0 + 2 * SWA_DIM] = two.T.astype(o_ref.dtype)


def _swa(sqt, sk, svt, sinks, rel_bias, *, nb):
    nq, s = sqt.shape
    nkv = sk.shape[1]
    tb = nb * BLK
    bucket = jnp.asarray(_t5_bucket_table_t())
    smem = pl.BlockSpec(memory_space=pltpu.SMEM)
    prev_blk = lambda i: jnp.maximum(i * nb - 1, 0)
    return pl.pallas_call(
        functools.partial(_swa_kernel, nb=nb),
        grid=(s // tb,),
        in_specs=[smem, smem, _resident((2 * BLK, BLK)),
                  pl.BlockSpec((nq, tb), lambda i: (0, i)),
                  pl.BlockSpec((tb, nkv), lambda i: (i, 0)),
                  pl.BlockSpec((BLK, nkv), lambda i: (prev_blk(i), 0)),
                  pl.BlockSpec((nkv, tb), lambda i: (0, i)),
                  pl.BlockSpec((nkv, BLK), lambda i: (0, prev_blk(i)))],
        out_specs=pl.BlockSpec((tb, nq), lambda i: (i, 0)),
        out_shape=jax.ShapeDtypeStruct((s, nq), BF16),
        scratch_shapes=[pltpu.VMEM((2, SWA_KV_HEADS, 2 * BLK, SWA_GROUP * BLK), F32),
                        pltpu.VMEM((SWA_KV_HEADS, 1, SWA_GROUP * BLK), F32)],
        compiler_params=_cparams(("arbitrary",)),
        name="swa",
    )(rel_bias, sinks, bucket, sqt, sk, sk, svt, svt)


def _rms(y, gain):
    r = lax.rsqrt(jnp.sum(y * y, axis=-1, keepdims=True) * (1.0 / y.shape[-1]) + RMS_EPS)
    return y * r * gain


def _oproj_kernel(a_ref, b_ref, wo_ref, x_ref, gpost_ref, gpre_ref, x1_ref, h_ref, *, sub):
    na = a_ref.shape[1]
    for r0 in range(0, x_ref.shape[0], sub):
        rows = slice(r0, r0 + sub)
        mix = (jnp.dot(a_ref[rows, :], wo_ref[:na, :], preferred_element_type=F32)
               + jnp.dot(b_ref[rows, :], wo_ref[na:, :], preferred_element_type=F32))
        x1 = x_ref[rows, :] + _rms(mix, gpost_ref[...])
        x1_ref[rows, :] = x1
        h_ref[rows, :] = _rms(x1, gpre_ref[...]).astype(BF16)


def _oproj(out_a, out_b, w_o, x2, g_post, g_pre, *, tm, sub):
    s, d = x2.shape
    row = lambda w: pl.BlockSpec((tm, w), lambda i: (i, 0))
    return pl.pallas_call(
        functools.partial(_oproj_kernel, sub=sub),
        grid=(s // tm,),
        in_specs=[row(out_a.shape[1]), row(out_b.shape[1]), _resident(w_o.shape),
                  row(d), _resident((1, d)), _resident((1, d))],
        out_specs=[row(d), row(d)],
        out_shape=[jax.ShapeDtypeStruct((s, d), F32), jax.ShapeDtypeStruct((s, d), BF16)],
        compiler_params=_cparams(("arbitrary",)),
        name="oproj",
    )(out_a, out_b, w_o, x2, g_post, g_pre)


def _ffn_kernel(h_ref, wg_ref, wu_ref, wd_ref, x1_hbm, gpost_ref, o_ref, x1_buf, x1_sem):
    i = pl.program_id(0)
    j = pl.program_id(1)
    tm = o_ref.shape[0]

    def x1_copy():
        rows = pl.ds(pl.multiple_of(i * tm, tm), tm)
        return pltpu.make_async_copy(x1_hbm.at[rows, :], x1_buf, x1_sem)

    @pl.when(j == 0)
    def _():
        x1_copy().start()
        o_ref[...] = jnp.zeros(o_ref.shape, F32)

    h = h_ref[...]
    tf = wg_ref.shape[1]
    halves = [slice(0, tf // 2), slice(tf // 2, tf)]
    gu = [(jnp.dot(h, wg_ref[:, c], preferred_element_type=F32),
           jnp.dot(h, wu_ref[:, c], preferred_element_type=F32)) for c in halves]
    for c, (gate, up) in zip(halves, gu):
        act = (gate * jax.nn.sigmoid(gate) * up).astype(BF16)
        o_ref[...] += jnp.dot(act, wd_ref[c, :], preferred_element_type=F32)

    @pl.when(j == pl.num_programs(1) - 1)
    def _():
        x1_copy().wait()
        o_ref[...] = x1_buf[...] + _rms(o_ref[...], gpost_ref[...])


def _ffn(h, wg, wu, wd, x1, g_post, *, tm, tf):
    s, d = x1.shape
    dff = wg.shape[1]
    return pl.pallas_call(
        _ffn_kernel,
        grid=(s // tm, dff // tf),
        in_specs=[pl.BlockSpec((tm, d), lambda i, j: (i, 0)),
                  pl.BlockSpec((d, tf), lambda i, j: (0, j)),
                  pl.BlockSpec((d, tf), lambda i, j: (0, j)),
                  pl.BlockSpec((tf, d), lambda i, j: (j, 0)),
                  pl.BlockSpec(memory_space=pl.ANY),
                  _resident((1, d))],
        out_specs=pl.BlockSpec((tm, d), lambda i, j: (i, 0)),
        out_shape=jax.ShapeDtypeStruct((s, d), F32),
        scratch_shapes=[pltpu.VMEM((tm, d), F32), pltpu.SemaphoreType.DMA(())],
        compiler_params=_cparams(("arbitrary", "arbitrary")),
        name="ffn",
    )(h, wg, wu, wd, x1, g_post)


def _pad_lanes(w):
    return jnp.pad(w, ((0, 0), (0, -w.shape[1] % LANES)))


def _layer(x2, pos_row, p, *, tm_qkv=512):
    half = MLA_ROPE // 2
    inv_freq = ROPE_THETA ** (-jnp.arange(half, dtype=F32) / half)
    invf = jnp.broadcast_to(inv_freq[:, None], (half, tm_qkv))

    row = lambda a: a[None, :]
    cq, ckv, krt, sqt, sk, svt = _proj(x2, row(p["g_mix_pre"]), _pad_lanes(p["w_in"]).astype(BF16),
                                       row(p["g_cq"]), row(p["g_ckv"]), tm=512, sub=256)
    w_ukv = p["w_ukv"].reshape(KV_RANK, MLA_HEADS, MLA_NOPE + MLA_V)
    wuk = w_ukv[:, :, :MLA_NOPE].reshape(KV_RANK, MLA_HEADS * MLA_NOPE)
    wuvt = w_ukv[:, :, MLA_NOPE:].reshape(KV_RANK, MLA_HEADS * MLA_V).T
    qt, k, vt = _qkv(cq, ckv, krt, pos_row, invf, p["w_uq"].T.astype(BF16),
                     wuk.astype(BF16), wuvt.astype(BF16), tm=tm_qkv)
    out_a = _mla(qt, k, vt, tq=1024, tk=512, qps=4)
    out_b = _swa(sqt, sk, svt, p["sinks"], p["rel_bias"], nb=8)
    x1, h = _oproj(out_a, out_b, p["w_o"].astype(BF16), x2,
                   row(p["g_mix_post"]), row(p["g_ffn_pre"]), tm=512, sub=256)
    return _ffn(h, p["w_gate"].astype(BF16), p["w_up"].astype(BF16),
                p["w_down"].astype(BF16), x1, row(p["g_ffn_post"]), tm=1024, tf=512)


def kernel(x, positions, g_mix_pre, w_in, g_cq, g_ckv, w_uq, w_ukv, sinks, rel_bias,
           w_o, g_mix_post, g_ffn_pre, w_gate, w_up, w_down, g_ffn_post):
    b, s, d = x.shape
    assert b == 1, "the row-major (S, D) pipeline assumes a single sequence"
    x2 = x.reshape(s, d)
    pos_row = positions.reshape(1, s)
    for layer in range(w_in.shape[0]):
        p = dict(g_mix_pre=g_mix_pre[layer], w_in=w_in[layer], g_cq=g_cq[layer],
                 g_ckv=g_ckv[layer], w_uq=w_uq[layer], w_ukv=w_ukv[layer],
                 sinks=sinks[layer], rel_bias=rel_bias, w_o=w_o[layer],
                 g_mix_post=g_mix_post[layer], g_ffn_pre=g_ffn_pre[layer],
                 w_gate=w_gate[layer], w_up=w_up[layer], w_down=w_down[layer],
                 g_ffn_post=g_ffn_post[layer])
        x2 = _layer(x2, pos_row, p)
    return x2.reshape(b, s, d)
```

```python
import functools
import math

import jax
import jax.numpy as jnp
import numpy as np
from jax import lax
from jax.experimental import pallas as pl
from jax.experimental.pallas import tpu as pltpu

F32 = jnp.float32
BF16 = jnp.bfloat16

MLA_HEADS = 8
MLA_NOPE = 128
MLA_ROPE = 64
MLA_QK = MLA_NOPE + MLA_ROPE
MLA_V = 128
Q_RANK = 512
KV_RANK = 512
ROPE_THETA = 10000.0
SWA_HEADS = 16
SWA_KV_HEADS = 2
SWA_GROUP = SWA_HEADS // SWA_KV_HEADS
SWA_DIM = 64
WINDOW = 128
NUM_BUCKETS = 32
MAX_DISTANCE = 128
BLK = 128
RMS_EPS = 1e-6
NEG = -1e30
LOG2E = math.log2(math.e)

LANES = 128
VMEM_LIMIT = 60 * 1024 * 1024


def _cparams(sem):
    return pltpu.CompilerParams(dimension_semantics=sem, vmem_limit_bytes=VMEM_LIMIT)


def _resident(shape):
    nd = len(shape)
    return pl.BlockSpec(shape, lambda *_: (0,) * nd, pipeline_mode=pl.Buffered(1))


def _proj_kernel(x_ref, g_ref, w_ref, gcq_ref, gckv_ref,
                 cq_ref, ckv_ref, krt_ref, sqt_ref, sk_ref, svt_ref, *, swa_scale, sub):
    nq = SWA_HEADS * SWA_DIM
    nkv = SWA_KV_HEADS * SWA_DIM

    def latent_norm(c, gain_ref):
        rc = lax.rsqrt(jnp.sum(c * c, axis=-1, keepdims=True) * (1.0 / c.shape[-1]) + RMS_EPS)
        return (c * rc * gain_ref[...]).astype(BF16)

    for r0 in range(0, x_ref.shape[0], sub):
        rows = slice(r0, r0 + sub)
        x = x_ref[rows, :]
        r = lax.rsqrt(jnp.sum(x * x, axis=-1, keepdims=True) * (1.0 / x.shape[-1]) + RMS_EPS)
        h = (x * g_ref[...]).astype(BF16)
        y = jnp.dot(h, w_ref[...], preferred_element_type=F32) * r
        cq_ref[rows, :] = latent_norm(y[:, :Q_RANK], gcq_ref)
        ckv_ref[rows, :] = latent_norm(y[:, Q_RANK:Q_RANK + KV_RANK], gckv_ref)
        yt = y[:, Q_RANK + KV_RANK:].T
        o = MLA_ROPE
        krt_ref[:, rows] = yt[:o]
        sqt_ref[:, rows] = (yt[o:o + nq] * swa_scale).astype(BF16)
        sk_ref[rows, :] = yt[o + nq:o + nq + nkv].T.astype(BF16)
        svt_ref[:, rows] = yt[o + nq + nkv:o + nq + 2 * nkv].astype(BF16)


def _proj(x2, g, w_in_p, g_cq, g_ckv, *, tm, sub):
    s, d = x2.shape
    nq = SWA_HEADS * SWA_DIM
    nkv = SWA_KV_HEADS * SWA_DIM
    row = lambda w: pl.BlockSpec((tm, w), lambda i: (i, 0))
    col = lambda r: pl.BlockSpec((r, tm), lambda i: (0, i))
    return pl.pallas_call(
        functools.partial(_proj_kernel, swa_scale=LOG2E / math.sqrt(SWA_DIM), sub=sub),
        grid=(s // tm,),
        in_specs=[row(d), _resident((1, d)), _resident(w_in_p.shape),
                  _resident((1, Q_RANK)), _resident((1, KV_RANK))],
        out_specs=[row(Q_RANK), row(KV_RANK), col(MLA_ROPE), col(nq), row(nkv), col(nkv)],
        out_shape=[jax.ShapeDtypeStruct((s, Q_RANK), BF16),
                   jax.ShapeDtypeStruct((s, KV_RANK), BF16),
                   jax.ShapeDtypeStruct((MLA_ROPE, s), F32),
                   jax.ShapeDtypeStruct((nq, s), BF16),
                   jax.ShapeDtypeStruct((s, nkv), BF16),
                   jax.ShapeDtypeStruct((nkv, s), BF16)],
        compiler_params=_cparams(("arbitrary",)),
        name="proj",
    )(x2, g, w_in_p, g_cq, g_ckv)


_NT = (((1,), (1,)), ((), ()))


def _qkv_kernel(cq_ref, ckv_ref, krt_ref, pos_ref, invf_ref, wuqt_ref, wuk_ref, wuvt_ref,
                qt_ref, k_ref, vt_ref, *, q_scale):
    tm = cq_ref.shape[0]
    half = MLA_ROPE // 2
    ang = invf_ref[...] * pos_ref[...].astype(F32)
    cos = jnp.cos(ang)
    sin = jnp.sin(ang)

    def rope_t(t):
        t1, t2 = t[:half], t[half:]
        return jnp.concatenate([t1 * cos - t2 * sin, t2 * cos + t1 * sin], axis=0)

    cq = cq_ref[...]
    ckv = ckv_ref[...]
    qt = lax.dot_general(wuqt_ref[...], cq, _NT, preferred_element_type=F32) * q_scale
    for h in range(MLA_HEADS):
        base = h * MLA_QK
        qt_ref[h, 0:MLA_NOPE, :] = qt[base:base + MLA_NOPE].astype(BF16)
        qt_ref[h, MLA_NOPE:MLA_QK, :] = rope_t(qt[base + MLA_NOPE:base + MLA_QK]).astype(BF16)

    krt = jnp.concatenate([rope_t(krt_ref[...]), jnp.zeros((LANES - MLA_ROPE, tm), F32)], axis=0)
    kr = krt.T[:, :MLA_ROPE].astype(BF16)
    kn = jnp.dot(ckv, wuk_ref[...], preferred_element_type=F32)
    vt = lax.dot_general(wuvt_ref[...], ckv, _NT, preferred_element_type=F32)
    for h in range(MLA_HEADS):
        k_ref[h, :, 0:MLA_NOPE] = kn[:, h * MLA_NOPE:(h + 1) * MLA_NOPE].astype(BF16)
        k_ref[h, :, MLA_NOPE:MLA_QK] = kr
        for c in range(tm // LANES):
            vt_ref[h, c] = vt[h * MLA_V:(h + 1) * MLA_V, c * LANES:(c + 1) * LANES].astype(BF16)


def _qkv(cq, ckv, krt, pos_row, invf, wuqt, wuk, wuvt, *, tm):
    s = cq.shape[0]
    row = lambda w: pl.BlockSpec((tm, w), lambda i: (i, 0))
    col = lambda r: pl.BlockSpec((r, tm), lambda i: (0, i))
    nb = tm // LANES
    return pl.pallas_call(
        functools.partial(_qkv_kernel, q_scale=LOG2E / math.sqrt(MLA_QK)),
        grid=(s // tm,),
        in_specs=[row(Q_RANK), row(KV_RANK), col(MLA_ROPE), col(1), _resident(invf.shape),
                  _resident(wuqt.shape), _resident(wuk.shape), _resident(wuvt.shape)],
        out_specs=[pl.BlockSpec((MLA_HEADS, MLA_QK, tm), lambda i: (0, 0, i)),
                   pl.BlockSpec((MLA_HEADS, tm, MLA_QK), lambda i: (0, i, 0)),
                   pl.BlockSpec((MLA_HEADS, nb, MLA_V, LANES), lambda i: (0, i, 0, 0))],
        out_shape=[jax.ShapeDtypeStruct((MLA_HEADS, MLA_QK, s), BF16),
                   jax.ShapeDtypeStruct((MLA_HEADS, s, MLA_QK), BF16),
                   jax.ShapeDtypeStruct((MLA_HEADS, s // LANES, MLA_V, LANES), BF16)],
        compiler_params=_cparams(("arbitrary",)),
        name="qkv",
    )(cq, ckv, krt, pos_row, invf, wuqt, wuk, wuvt)


def _mla_kernel(*refs, tq, tk, qps, ncast):
    qt_ref, k_ref, vt_ref = refs[:3]
    w_hbm = refs[3:3 + ncast]
    o_ref = refs[3 + ncast]
    wout_hbm = refs[4 + ncast:4 + 2 * ncast]
    m_sc, l_sc, acc_sc, s_sc = refs[4 + 2 * ncast:8 + 2 * ncast]
    stage_in = refs[8 + 2 * ncast:8 + 3 * ncast]
    stage_out = refs[8 + 3 * ncast:8 + 4 * ncast]
    in_sems, out_sems = refs[8 + 4 * ncast:]

    step = pl.program_id(0) * pl.num_programs(1) + pl.program_id(1)
    last_step = pl.num_programs(0) * pl.num_programs(1) - 1

    def slab_in(w):
        rows = stage_in[w].shape[0]
        src = w_hbm[w].at[pl.ds(pl.multiple_of(step * rows, rows), rows), :]
        return pltpu.make_async_copy(src, stage_in[w], in_sems.at[w])

    def slab_out(w, at_step):
        rows = stage_out[w].shape[0]
        dst = wout_hbm[w].at[pl.ds(pl.multiple_of(at_step * rows, rows), rows), :]
        return pltpu.make_async_copy(stage_out[w], dst, out_sems.at[w])

    @pl.when(step > 0)
    def _():
        for w in range(ncast):
            slab_out(w, step - 1).wait()

    for w in range(ncast):
        slab_in(w).start()

    group = tq // tk
    assert group * tk == tq and group % 2 == 0
    nsub = tk // LANES
    full = slice(0, tq)

    def one_query_block(qi, q0):
        m_sc[...] = jnp.full(m_sc.shape, NEG, F32)
        l_sc[...] = jnp.zeros(l_sc.shape, F32)
        acc_sc[...] = jnp.zeros(acc_sc.shape, F32)

        def scores(j, slot, cols=full):
            start = pl.multiple_of(j * tk, tk)
            qcols = slice(q0 + cols.start, q0 + cols.stop)
            s_sc[slot, :, cols] = jnp.dot(k_ref[0, pl.ds(start, tk), :], qt_ref[0, :, qcols],
                                          preferred_element_type=F32)

        def update(j, slot, cols=full, tri=False):
            s = s_sc[slot, :, cols]
            if tri:
                mask = (lax.broadcasted_iota(jnp.int32, (tk, tk), 0)
                        <= lax.broadcasted_iota(jnp.int32, (tk, tk), 1))
                parts = [jnp.where(mask, s[:, :tk], NEG)] + ([s[:, tk:]] if s.shape[1] > tk else [])
                s = jnp.concatenate(parts, axis=1)
            m_old = m_sc[:, cols]
            m_new = jnp.maximum(m_old, jnp.max(s, axis=0, keepdims=True))
            alpha = jnp.exp2(m_old - m_new)
            p = jnp.exp2(s - m_new)
            l_sc[:, cols] = alpha * l_sc[:, cols] + jnp.sum(p, axis=0, keepdims=True)
            vt = jnp.concatenate([vt_ref[0, j * nsub + c] for c in range(nsub)], axis=1)
            acc_sc[:, cols] = (alpha * acc_sc[:, cols]
                               + jnp.dot(vt, p.astype(BF16), preferred_element_type=F32))
            m_sc[:, cols] = m_new

        def full_group(i, carry):
            for r in range(group):
                b = group * i + r
                scores(b + 1, (r + 1) % 2)
                update(b, r % 2)
            return carry

        scores(0, 0)
        lax.fori_loop(0, qi, full_group, 0)
        for r in range(group):
            b = group * qi + r
            if r + 1 < group:
                scores(b + 1, (r + 1) % 2, slice((r + 1) * tk, tq))
            update(b, r % 2, slice(r * tk, tq), tri=True)
        o_ref[q0:q0 + tq, :] = (acc_sc[...] / l_sc[...]).T.astype(o_ref.dtype)

    for t in range(qps):
        one_query_block(pl.program_id(1) * qps + t, t * tq)

    for w in range(ncast):
        slab_in(w).wait()
        stage_out[w][...] = stage_in[w][...].astype(BF16)
        slab_out(w, step).start()

    @pl.when(step == last_step)
    def _():
        for w in range(ncast):
            slab_out(w, step).wait()


def _mla(qt, k, vt, cast_weights, *, tq, tk, qps):
    _, s, _ = k.shape
    grid = (MLA_HEADS, s // (tq * qps))
    nsteps = grid[0] * grid[1]
    ncast = len(cast_weights)
    slabs = [(w.shape[0] // nsteps, w.shape[1]) for w in cast_weights]
    for w, (rows, _) in zip(cast_weights, slabs):
        assert rows * nsteps == w.shape[0] and rows % 16 == 0, (w.shape, nsteps)
    anywhere = pl.BlockSpec(memory_space=pl.ANY)
    outs = pl.pallas_call(
        functools.partial(_mla_kernel, tq=tq, tk=tk, qps=qps, ncast=ncast),
        grid=grid,
        in_specs=[pl.BlockSpec((1, MLA_QK, tq * qps), lambda h, i: (h, 0, i)),
                  pl.BlockSpec((1, s, MLA_QK), lambda h, i: (h, 0, 0)),
                  pl.BlockSpec((1, s // LANES, MLA_V, LANES), lambda h, i: (h, 0, 0, 0))]
                 + [anywhere] * ncast,
        out_specs=[pl.BlockSpec((tq * qps, MLA_V), lambda h, i: (i, h))] + [anywhere] * ncast,
        out_shape=[jax.ShapeDtypeStruct((s, MLA_HEADS * MLA_V), BF16)]
                  + [jax.ShapeDtypeStruct(w.shape, BF16) for w in cast_weights],
        scratch_shapes=[pltpu.VMEM((1, tq), F32), pltpu.VMEM((1, tq), F32),
                        pltpu.VMEM((MLA_V, tq), F32),
                        pltpu.VMEM((2, tk, tq), F32)]
                       + [pltpu.VMEM(sl, F32) for sl in slabs]
                       + [pltpu.VMEM(sl, BF16) for sl in slabs]
                       + [pltpu.SemaphoreType.DMA((ncast,)), pltpu.SemaphoreType.DMA((ncast,))],
        compiler_params=_cparams(("arbitrary", "arbitrary")),
        name="mla",
    )(qt, k, vt, *cast_weights)
    return outs[0], outs[1:]


def _t5_bucket_table_t():
    i = np.arange(BLK)[None, :]
    j = np.arange(2 * BLK)[:, None]
    dist = i + BLK - j
    max_exact = NUM_BUCKETS // 2
    d = np.maximum(dist, 0)
    large = max_exact + (np.log(np.maximum(d, 1) / max_exact)
                         / np.log(MAX_DISTANCE / max_exact)
                         * (NUM_BUCKETS - max_exact)).astype(np.int32)
    large = np.minimum(large, NUM_BUCKETS - 1)
    bucket = np.where(d < max_exact, d, large).astype(np.int32)
    in_window = (dist >= 0) & (dist < WINDOW)
    return np.where(in_window, bucket, -1).astype(np.int32)


def _swa_kernel(relb_ref, sink_ref, bucket_ref, qt_ref, kc_ref, kp_ref, vtc_ref, vtp_ref,
                o_ref, bias_sc, sink_sc, *, nb):
    i = pl.program_id(0)
    gw = SWA_GROUP * BLK

    @pl.when(i == 0)
    def _():
        bucket = bucket_ref[...]
        for h in range(SWA_HEADS):
            c, g = divmod(h, SWA_GROUP)
            b = jnp.full(bucket.shape, NEG, F32)
            for t in range(NUM_BUCKETS):
                b = jnp.where(bucket == t, relb_ref[t, h] * LOG2E, b)
            bias_sc[1, c, :, g * BLK:(g + 1) * BLK] = b
            bias_sc[0, c, BLK:, g * BLK:(g + 1) * BLK] = b[BLK:]
            bias_sc[0, c, :BLK, g * BLK:(g + 1) * BLK] = jnp.full((BLK, BLK), NEG, F32)
            sink_sc[c, :, g * BLK:(g + 1) * BLK] = jnp.full((1, BLK), sink_ref[h] * LOG2E, F32)

    zeros = jnp.zeros((SWA_DIM, gw), BF16)
    for t in range(nb):
        tsl = slice(t * BLK, (t + 1) * BLK)
        k_prev = kp_ref[...] if t == 0 else kc_ref[(t - 1) * BLK:t * BLK, :]
        vt_prev = vtp_ref[...] if t == 0 else vtc_ref[:, (t - 1) * BLK:t * BLK]
        kband = jnp.concatenate([k_prev, kc_ref[tsl, :]], axis=0)
        general = 1 if t > 0 else jnp.where(i == 0, 0, 1)
        for c in range(SWA_KV_HEADS):
            qt = jnp.concatenate(
                [qt_ref[(c * SWA_GROUP + g) * SWA_DIM:(c * SWA_GROUP + g + 1) * SWA_DIM, tsl]
                 for g in range(SWA_GROUP)], axis=1)
            qt_ext = jnp.concatenate([qt, zeros] if c == 0 else [zeros, qt], axis=0)
            s = jnp.dot(kband, qt_ext, preferred_element_type=F32) + bias_sc[general, c]
            sink = sink_sc[c]
            m = jnp.maximum(jnp.max(s, axis=0, keepdims=True), sink)
            p = jnp.exp2(s - m)
            denom = jnp.sum(p, axis=0, keepdims=True) + jnp.exp2(sink - m)
            dsl = slice(c * SWA_DIM, (c + 1) * SWA_DIM)
            vt = jnp.concatenate([vt_prev[dsl, :], vtc_ref[dsl, tsl]], axis=1)
            ot = jnp.dot(vt, p.astype(BF16), preferred_element_type=F32) / denom
            for g2 in range(SWA_GROUP // 2):
                two = jnp.concatenate([ot[:, (2 * g2) * BLK:(2 * g2 + 1) * BLK],
                                       ot[:, (2 * g2 + 1) * BLK:(2 * g2 + 2) * BLK]], axis=0)
                col0 = (c * SWA_GROUP + 2 * g2) * SWA_DIM
                o_ref[tsl, col0:col0 + 2 * SWA_DIM] = two.T.astype(o_ref.dtype)


def _swa(sqt, sk, svt, sinks, rel_bias, *, nb):
    nq, s = sqt.shape
    nkv = sk.shape[1]
    tb = nb * BLK
    bucket = jnp.asarray(_t5_bucket_table_t())
    smem = pl.BlockSpec(memory_space=pltpu.SMEM)
    prev_blk = lambda i: jnp.maximum(i * nb - 1, 0)
    return pl.pallas_call(
        functools.partial(_swa_kernel, nb=nb),
        grid=(s // tb,),
        in_specs=[smem, smem, _resident((2 * BLK, BLK)),
                  pl.BlockSpec((nq, tb), lambda i: (0, i)),
                  pl.BlockSpec((tb, nkv), lambda i: (i, 0)),
                  pl.BlockSpec((BLK, nkv), lambda i: (prev_blk(i), 0)),
                  pl.BlockSpec((nkv, tb), lambda i: (0, i)),
                  pl.BlockSpec((nkv, BLK), lambda i: (0, prev_blk(i)))],
        out_specs=pl.BlockSpec((tb, nq), lambda i: (i, 0)),
        out_shape=jax.ShapeDtypeStruct((s, nq), BF16),
        scratch_shapes=[pltpu.VMEM((2, SWA_KV_HEADS, 2 * BLK, SWA_GROUP * BLK), F32),
                        pltpu.VMEM((SWA_KV_HEADS, 1, SWA_GROUP * BLK), F32)],
        compiler_params=_cparams(("arbitrary",)),
        name="swa",
    )(rel_bias, sinks, bucket, sqt, sk, sk, svt, svt)


def _rms(y, gain):
    r = lax.rsqrt(jnp.sum(y * y, axis=-1, keepdims=True) * (1.0 / y.shape[-1]) + RMS_EPS)
    return y * r * gain


def _oproj_kernel(a_ref, b_ref, wo_ref, x_ref, gpost_ref, gpre_ref, x1_ref, h_ref, *, sub):
    na = a_ref.shape[1]
    for r0 in range(0, x_ref.shape[0], sub):
        rows = slice(r0, r0 + sub)
        mix = (jnp.dot(a_ref[rows, :], wo_ref[:na, :], preferred_element_type=F32)
               + jnp.dot(b_ref[rows, :], wo_ref[na:, :], preferred_element_type=F32))
        x1 = x_ref[rows, :] + _rms(mix, gpost_ref[...])
        x1_ref[rows, :] = x1
        h_ref[rows, :] = _rms(x1, gpre_ref[...]).astype(BF16)


def _oproj(out_a, out_b, w_o, x2, g_post, g_pre, *, tm, sub):
    s, d = x2.shape
    row = lambda w: pl.BlockSpec((tm, w), lambda i: (i, 0))
    return pl.pallas_call(
        functools.partial(_oproj_kernel, sub=sub),
        grid=(s // tm,),
        in_specs=[row(out_a.shape[1]), row(out_b.shape[1]), _resident(w_o.shape),
                  row(d), _resident((1, d)), _resident((1, d))],
        out_specs=[row(d), row(d)],
        out_shape=[jax.ShapeDtypeStruct((s, d), F32), jax.ShapeDtypeStruct((s, d), BF16)],
        compiler_params=_cparams(("arbitrary",)),
        name="oproj",
    )(out_a, out_b, w_o, x2, g_post, g_pre)


def _ffn_kernel(h_ref, wg_ref, wu_ref, wd_ref, x1_hbm, gpost_ref, o_ref, x1_buf, x1_sem):
    i = pl.program_id(0)
    j = pl.program_id(1)
    tm = o_ref.shape[0]

    def x1_copy():
        rows = pl.ds(pl.multiple_of(i * tm, tm), tm)
        return pltpu.make_async_copy(x1_hbm.at[rows, :], x1_buf, x1_sem)

    @pl.when(j == 0)
    def _():
        x1_copy().start()
        o_ref[...] = jnp.zeros(o_ref.shape, F32)

    h = h_ref[...]
    tf = wg_ref.shape[1]
    halves = [slice(0, tf // 2), slice(tf // 2, tf)]
    gu = [(jnp.dot(h, wg_ref[:, c], preferred_element_type=F32),
           jnp.dot(h, wu_ref[:, c], preferred_element_type=F32)) for c in halves]
    for c, (gate, up) in zip(halves, gu):
        act = (gate * jax.nn.sigmoid(gate) * up).astype(BF16)
        o_ref[...] += jnp.dot(act, wd_ref[c, :], preferred_element_type=F32)

    @pl.when(j == pl.num_programs(1) - 1)
    def _():
        x1_copy().wait()
        o_ref[...] = x1_buf[...] + _rms(o_ref[...], gpost_ref[...])


def _ffn(h, wg, wu, wd, x1, g_post, *, tm, tf):
    s, d = x1.shape
    dff = wg.shape[1]
    return pl.pallas_call(
        _ffn_kernel,
        grid=(s // tm, dff // tf),
        in_specs=[pl.BlockSpec((tm, d), lambda i, j: (i, 0)),
                  pl.BlockSpec((d, tf), lambda i, j: (0, j)),
                  pl.BlockSpec((d, tf), lambda i, j: (0, j)),
                  pl.BlockSpec((tf, d), lambda i, j: (j, 0)),
                  pl.BlockSpec(memory_space=pl.ANY),
                  _resident((1, d))],
        out_specs=pl.BlockSpec((tm, d), lambda i, j: (i, 0)),
        out_shape=jax.ShapeDtypeStruct((s, d), F32),
        scratch_shapes=[pltpu.VMEM((tm, d), F32), pltpu.SemaphoreType.DMA(())],
        compiler_params=_cparams(("arbitrary", "arbitrary")),
        name="ffn",
    )(h, wg, wu, wd, x1, g_post)


def _pad_lanes(w):
    return jnp.pad(w, ((0, 0), (0, -w.shape[1] % LANES)))


def _layer(x2, pos_row, p, *, tm_qkv=512):
    half = MLA_ROPE // 2
    inv_freq = ROPE_THETA ** (-jnp.arange(half, dtype=F32) / half)
    invf = jnp.broadcast_to(inv_freq[:, None], (half, tm_qkv))

    row = lambda a: a[None, :]
    cq, ckv, krt, sqt, sk, svt = _proj(x2, row(p["g_mix_pre"]), _pad_lanes(p["w_in"]).astype(BF16),
                                       row(p["g_cq"]), row(p["g_ckv"]), tm=512, sub=256)
    w_ukv = p["w_ukv"].reshape(KV_RANK, MLA_HEADS, MLA_NOPE + MLA_V)
    wuk = w_ukv[:, :, :MLA_NOPE].reshape(KV_RANK, MLA_HEADS * MLA_NOPE)
    wuvt = w_ukv[:, :, MLA_NOPE:].reshape(KV_RANK, MLA_HEADS * MLA_V).T
    qt, k, vt = _qkv(cq, ckv, krt, pos_row, invf, p["w_uq"].T.astype(BF16),
                     wuk.astype(BF16), wuvt.astype(BF16), tm=tm_qkv)
    out_a, (w_o, w_gate, w_up, w_down) = _mla(
        qt, k, vt, [p["w_o"], p["w_gate"], p["w_up"], p["w_down"]], tq=1024, tk=512, qps=4)
    out_b = _swa(sqt, sk, svt, p["sinks"], p["rel_bias"], nb=8)
    x1, h = _oproj(out_a, out_b, w_o, x2,
                   row(p["g_mix_post"]), row(p["g_ffn_pre"]), tm=512, sub=256)
    return _ffn(h, w_gate, w_up, w_down, x1, row(p["g_ffn_post"]), tm=1024, tf=512)


def kernel(x, positions, g_mix_pre, w_in, g_cq, g_ckv, w_uq, w_ukv, sinks, rel_bias,
           w_o, g_mix_post, g_ffn_pre, w_gate, w_up, w_down, g_ffn_post):
    b, s, d = x.shape
    assert b == 1, "the row-major (S, D) pipeline assumes a single sequence"
    x2 = x.reshape(s, d)
    pos_row = positions.reshape(1, s)
    for layer in range(w_in.shape[0]):
        p = dict(g_mix_pre=g_mix_pre[layer], w_in=w_in[layer], g_cq=g_cq[layer],
                 g_ckv=g_ckv[layer], w_uq=w_uq[layer], w_ukv=w_ukv[layer],
                 sinks=sinks[layer], rel_bias=rel_bias, w_o=w_o[layer],
                 g_mix_post=g_mix_post[layer], g_ffn_pre=g_ffn_pre[layer],
                 w_gate=w_gate[layer], w_up=w_up[layer], w_down=w_down[layer],
                 g_ffn_post=g_ffn_post[layer])
        x2 = _layer(x2, pos_row, p)
    return x2.reshape(b, s, d)
```

```python
import functools
import math

import jax
import jax.numpy as jnp
import numpy as np
from jax import lax
from jax.experimental import pallas as pl
from jax.experimental.pallas import tpu as pltpu

F32 = jnp.float32
BF16 = jnp.bfloat16

MLA_HEADS = 8
MLA_NOPE = 128
MLA_ROPE = 64
MLA_QK = MLA_NOPE + MLA_ROPE
MLA_V = 128
Q_RANK = 512
KV_RANK = 512
ROPE_THETA = 10000.0
SWA_HEADS = 16
SWA_KV_HEADS = 2
SWA_GROUP = SWA_HEADS // SWA_KV_HEADS
SWA_DIM = 64
WINDOW = 128
NUM_BUCKETS = 32
MAX_DISTANCE = 128
BLK = 128
RMS_EPS = 1e-6
NEG = -1e30
LOG2E = math.log2(math.e)

LANES = 128
VMEM_LIMIT = 60 * 1024 * 1024


def _cparams(sem):
    return pltpu.CompilerParams(dimension_semantics=sem, vmem_limit_bytes=VMEM_LIMIT)


def _resident(shape):
    nd = len(shape)
    return pl.BlockSpec(shape, lambda *_: (0,) * nd, pipeline_mode=pl.Buffered(1))


def _proj_kernel(x_ref, g_ref, w_ref, gcq_ref, gckv_ref,
                 cq_ref, ckv_ref, krt_ref, sqt_ref, sk_ref, svt_ref, *, swa_scale, sub):
    nq = SWA_HEADS * SWA_DIM
    nkv = SWA_KV_HEADS * SWA_DIM

    def latent_norm(c, gain_ref):
        rc = lax.rsqrt(jnp.sum(c * c, axis=-1, keepdims=True) * (1.0 / c.shape[-1]) + RMS_EPS)
        return (c * rc * gain_ref[...]).astype(BF16)

    for r0 in range(0, x_ref.shape[0], sub):
        rows = slice(r0, r0 + sub)
        x = x_ref[rows, :]
        r = lax.rsqrt(jnp.sum(x * x, axis=-1, keepdims=True) * (1.0 / x.shape[-1]) + RMS_EPS)
        h = (x * g_ref[...]).astype(BF16)
        y = jnp.dot(h, w_ref[...], preferred_element_type=F32) * r
        cq_ref[rows, :] = latent_norm(y[:, :Q_RANK], gcq_ref)
        ckv_ref[rows, :] = latent_norm(y[:, Q_RANK:Q_RANK + KV_RANK], gckv_ref)
        yt = y[:, Q_RANK + KV_RANK:].T
        o = MLA_ROPE
        krt_ref[:, rows] = yt[:o]
        sqt_ref[:, rows] = (yt[o:o + nq] * swa_scale).astype(BF16)
        sk_ref[rows, :] = yt[o + nq:o + nq + nkv].T.astype(BF16)
        svt_ref[:, rows] = yt[o + nq + nkv:o + nq + 2 * nkv].astype(BF16)


def _proj(x2, g, w_in_p, g_cq, g_ckv, *, tm, sub):
    s, d = x2.shape
    nq = SWA_HEADS * SWA_DIM
    nkv = SWA_KV_HEADS * SWA_DIM
    row = lambda w: pl.BlockSpec((tm, w), lambda i: (i, 0))
    col = lambda r: pl.BlockSpec((r, tm), lambda i: (0, i))
    return pl.pallas_call(
        functools.partial(_proj_kernel, swa_scale=LOG2E / math.sqrt(SWA_DIM), sub=sub),
        grid=(s // tm,),
        in_specs=[row(d), _resident((1, d)), _resident(w_in_p.shape),
                  _resident((1, Q_RANK)), _resident((1, KV_RANK))],
        out_specs=[row(Q_RANK), row(KV_RANK), col(MLA_ROPE), col(nq), row(nkv), col(nkv)],
        out_shape=[jax.ShapeDtypeStruct((s, Q_RANK), BF16),
                   jax.ShapeDtypeStruct((s, KV_RANK), BF16),
                   jax.ShapeDtypeStruct((MLA_ROPE, s), F32),
                   jax.ShapeDtypeStruct((nq, s), BF16),
                   jax.ShapeDtypeStruct((s, nkv), BF16),
                   jax.ShapeDtypeStruct((nkv, s), BF16)],
        compiler_params=_cparams(("arbitrary",)),
        name="proj",
    )(x2, g, w_in_p, g_cq, g_ckv)


_NT = (((1,), (1,)), ((), ()))


def _qkv_kernel(cq_ref, ckv_ref, krt_ref, pos_ref, invf_ref, wuqt_ref, wuk_ref, wuvt_ref,
                qt_ref, k_ref, vt_ref, *, q_scale):
    tm = cq_ref.shape[0]
    half = MLA_ROPE // 2
    ang = invf_ref[...] * pos_ref[...].astype(F32)
    cos = jnp.cos(ang)
    sin = jnp.sin(ang)

    def rope_t(t):
        t1, t2 = t[:half], t[half:]
        return jnp.concatenate([t1 * cos - t2 * sin, t2 * cos + t1 * sin], axis=0)

    cq = cq_ref[...]
    ckv = ckv_ref[...]
    qt = lax.dot_general(wuqt_ref[...], cq, _NT, preferred_element_type=F32) * q_scale
    for h in range(MLA_HEADS):
        base = h * MLA_QK
        qt_ref[h, 0:MLA_NOPE, :] = qt[base:base + MLA_NOPE].astype(BF16)
        qt_ref[h, MLA_NOPE:MLA_QK, :] = rope_t(qt[base + MLA_NOPE:base + MLA_QK]).astype(BF16)

    krt = jnp.concatenate([rope_t(krt_ref[...]), jnp.zeros((LANES - MLA_ROPE, tm), F32)], axis=0)
    kr = krt.T[:, :MLA_ROPE].astype(BF16)
    kn = jnp.dot(ckv, wuk_ref[...], preferred_element_type=F32)
    vt = lax.dot_general(wuvt_ref[...], ckv, _NT, preferred_element_type=F32)
    for h in range(MLA_HEADS):
        k_ref[h, :, 0:MLA_NOPE] = kn[:, h * MLA_NOPE:(h + 1) * MLA_NOPE].astype(BF16)
        k_ref[h, :, MLA_NOPE:MLA_QK] = kr
        for c in range(tm // LANES):
            vt_ref[h, c] = vt[h * MLA_V:(h + 1) * MLA_V, c * LANES:(c + 1) * LANES].astype(BF16)


def _qkv(cq, ckv, krt, pos_row, invf, wuqt, wuk, wuvt, *, tm):
    s = cq.shape[0]
    row = lambda w: pl.BlockSpec((tm, w), lambda i: (i, 0))
    col = lambda r: pl.BlockSpec((r, tm), lambda i: (0, i))
    nb = tm // LANES
    return pl.pallas_call(
        functools.partial(_qkv_kernel, q_scale=LOG2E / math.sqrt(MLA_QK)),
        grid=(s // tm,),
        in_specs=[row(Q_RANK), row(KV_RANK), col(MLA_ROPE), col(1), _resident(invf.shape),
                  _resident(wuqt.shape), _resident(wuk.shape), _resident(wuvt.shape)],
        out_specs=[pl.BlockSpec((MLA_HEADS, MLA_QK, tm), lambda i: (0, 0, i)),
                   pl.BlockSpec((MLA_HEADS, tm, MLA_QK), lambda i: (0, i, 0)),
                   pl.BlockSpec((MLA_HEADS, nb, MLA_V, LANES), lambda i: (0, i, 0, 0))],
        out_shape=[jax.ShapeDtypeStruct((MLA_HEADS, MLA_QK, s), BF16),
                   jax.ShapeDtypeStruct((MLA_HEADS, s, MLA_QK), BF16),
                   jax.ShapeDtypeStruct((MLA_HEADS, s // LANES, MLA_V, LANES), BF16)],
        compiler_params=_cparams(("arbitrary",)),
        name="qkv",
    )(cq, ckv, krt, pos_row, invf, wuqt, wuk, wuvt)


def _mla_kernel(*refs, tq, tk, qps, ncast):
    qt_ref, k_ref, vt_ref = refs[:3]
    w_hbm = refs[3:3 + ncast]
    o_ref = refs[3 + ncast]
    wout_hbm = refs[4 + ncast:4 + 2 * ncast]
    m_sc, l_sc, acc_sc, s_sc = refs[4 + 2 * ncast:8 + 2 * ncast]
    stage_in = refs[8 + 2 * ncast:8 + 3 * ncast]
    stage_out = refs[8 + 3 * ncast:8 + 4 * ncast]
    in_sems, out_sems = refs[8 + 4 * ncast:]

    step = pl.program_id(0) * pl.num_programs(1) + pl.program_id(1)
    last_step = pl.num_programs(0) * pl.num_programs(1) - 1

    def slab_in(w):
        rows = stage_in[w].shape[0]
        src = w_hbm[w].at[pl.ds(pl.multiple_of(step * rows, rows), rows), :]
        return pltpu.make_async_copy(src, stage_in[w], in_sems.at[w])

    def slab_out(w, at_step):
        rows = stage_out[w].shape[0]
        dst = wout_hbm[w].at[pl.ds(pl.multiple_of(at_step * rows, rows), rows), :]
        return pltpu.make_async_copy(stage_out[w], dst, out_sems.at[w])

    for w in range(ncast):
        slab_in(w).start()

    group = tq // tk
    assert group * tk == tq and group % 2 == 0
    nsub = tk // LANES
    full = slice(0, tq)

    def one_query_block(qi, q0):
        m_sc[...] = jnp.full(m_sc.shape, NEG, F32)
        l_sc[...] = jnp.zeros(l_sc.shape, F32)
        acc_sc[...] = jnp.zeros(acc_sc.shape, F32)

        def scores(j, slot, cols=full):
            start = pl.multiple_of(j * tk, tk)
            qcols = slice(q0 + cols.start, q0 + cols.stop)
            s_sc[slot, :, cols] = jnp.dot(k_ref[0, pl.ds(start, tk), :], qt_ref[0, :, qcols],
                                          preferred_element_type=F32)

        def update(j, slot, cols=full, tri=False):
            s = s_sc[slot, :, cols]
            if tri:
                mask = (lax.broadcasted_iota(jnp.int32, (tk, tk), 0)
                        <= lax.broadcasted_iota(jnp.int32, (tk, tk), 1))
                parts = [jnp.where(mask, s[:, :tk], NEG)] + ([s[:, tk:]] if s.shape[1] > tk else [])
                s = jnp.concatenate(parts, axis=1)
            m_old = m_sc[:, cols]
            m_new = jnp.maximum(m_old, jnp.max(s, axis=0, keepdims=True))
            alpha = jnp.exp2(m_old - m_new)
            p = jnp.exp2(s - m_new)
            l_sc[:, cols] = alpha * l_sc[:, cols] + jnp.sum(p, axis=0, keepdims=True)
            vt = jnp.concatenate([vt_ref[0, j * nsub + c] for c in range(nsub)], axis=1)
            acc_sc[:, cols] = (alpha * acc_sc[:, cols]
                               + jnp.dot(vt, p.astype(BF16), preferred_element_type=F32))
            m_sc[:, cols] = m_new

        def full_group(i, carry):
            for r in range(group):
                b = group * i + r
                scores(b + 1, (r + 1) % 2)
                update(b, r % 2)
            return carry

        scores(0, 0)
        lax.fori_loop(0, qi, full_group, 0)
        for r in range(group):
            b = group * qi + r
            if r + 1 < group:
                scores(b + 1, (r + 1) % 2, slice((r + 1) * tk, tq))
            update(b, r % 2, slice(r * tk, tq), tri=True)
        o_ref[q0:q0 + tq, :] = (acc_sc[...] / l_sc[...]).T.astype(o_ref.dtype)

    for t in range(qps):
        one_query_block(pl.program_id(1) * qps + t, t * tq)

    @pl.when(step > 0)
    def _():
        for w in range(ncast):
            slab_out(w, step - 1).wait()

    for w in range(ncast):
        slab_in(w).wait()
        stage_out[w][...] = stage_in[w][...].astype(BF16)
        slab_out(w, step).start()

    @pl.when(step == last_step)
    def _():
        for w in range(ncast):
            slab_out(w, step).wait()


def _mla(qt, k, vt, cast_weights, *, tq, tk, qps):
    _, s, _ = k.shape
    grid = (MLA_HEADS, s // (tq * qps))
    nsteps = grid[0] * grid[1]
    ncast = len(cast_weights)
    slabs = [(w.shape[0] // nsteps, w.shape[1]) for w in cast_weights]
    for w, (rows, _) in zip(cast_weights, slabs):
        assert rows * nsteps == w.shape[0] and rows % 16 == 0, (w.shape, nsteps)
    anywhere = pl.BlockSpec(memory_space=pl.ANY)
    outs = pl.pallas_call(
        functools.partial(_mla_kernel, tq=tq, tk=tk, qps=qps, ncast=ncast),
        grid=grid,
        in_specs=[pl.BlockSpec((1, MLA_QK, tq * qps), lambda h, i: (h, 0, i)),
                  pl.BlockSpec((1, s, MLA_QK), lambda h, i: (h, 0, 0)),
                  pl.BlockSpec((1, s // LANES, MLA_V, LANES), lambda h, i: (h, 0, 0, 0))]
                 + [anywhere] * ncast,
        out_specs=[pl.BlockSpec((tq * qps, MLA_V), lambda h, i: (i, h))] + [anywhere] * ncast,
        out_shape=[jax.ShapeDtypeStruct((s, MLA_HEADS * MLA_V), BF16)]
                  + [jax.ShapeDtypeStruct(w.shape, BF16) for w in cast_weights],
        scratch_shapes=[pltpu.VMEM((1, tq), F32), pltpu.VMEM((1, tq), F32),
                        pltpu.VMEM((MLA_V, tq), F32),
                        pltpu.VMEM((2, tk, tq), F32)]
                       + [pltpu.VMEM(sl, F32) for sl in slabs]
                       + [pltpu.VMEM(sl, BF16) for sl in slabs]
                       + [pltpu.SemaphoreType.DMA((ncast,)), pltpu.SemaphoreType.DMA((ncast,))],
        compiler_params=_cparams(("arbitrary", "arbitrary")),
        name="mla",
    )(qt, k, vt, *cast_weights)
    return outs[0], outs[1:]


def _t5_bucket_table_t():
    i = np.arange(BLK)[None, :]
    j = np.arange(2 * BLK)[:, None]
    dist = i + BLK - j
    max_exact = NUM_BUCKETS // 2
    d = np.maximum(dist, 0)
    large = max_exact + (np.log(np.maximum(d, 1) / max_exact)
                         / np.log(MAX_DISTANCE / max_exact)
                         * (NUM_BUCKETS - max_exact)).astype(np.int32)
    large = np.minimum(large, NUM_BUCKETS - 1)
    bucket = np.where(d < max_exact, d, large).astype(np.int32)
    in_window = (dist >= 0) & (dist < WINDOW)
    return np.where(in_window, bucket, -1).astype(np.int32)


def _swa_kernel(relb_ref, sink_ref, bucket_ref, qt_ref, kc_ref, kp_ref, vtc_ref, vtp_ref,
                o_ref, bias_sc, sink_sc, *, nb):
    i = pl.program_id(0)
    gw = SWA_GROUP * BLK

    @pl.when(i == 0)
    def _():
        bucket = bucket_ref[...]
        for h in range(SWA_HEADS):
            c, g = divmod(h, SWA_GROUP)
            b = jnp.full(bucket.shape, NEG, F32)
            for t in range(NUM_BUCKETS):
                b = jnp.where(bucket == t, relb_ref[t, h] * LOG2E, b)
            bias_sc[1, c, :, g * BLK:(g + 1) * BLK] = b
            bias_sc[0, c, BLK:, g * BLK:(g + 1) * BLK] = b[BLK:]
            bias_sc[0, c, :BLK, g * BLK:(g + 1) * BLK] = jnp.full((BLK, BLK), NEG, F32)
            sink_sc[c, :, g * BLK:(g + 1) * BLK] = jnp.full((1, BLK), sink_ref[h] * LOG2E, F32)

    zeros = jnp.zeros((SWA_DIM, gw), BF16)
    for t in range(nb):
        tsl = slice(t * BLK, (t + 1) * BLK)
        k_prev = kp_ref[...] if t == 0 else kc_ref[(t - 1) * BLK:t * BLK, :]
        vt_prev = vtp_ref[...] if t == 0 else vtc_ref[:, (t - 1) * BLK:t * BLK]
        kband = jnp.concatenate([k_prev, kc_ref[tsl, :]], axis=0)
        general = 1 if t > 0 else jnp.where(i == 0, 0, 1)
        for c in range(SWA_KV_HEADS):
            qt = jnp.concatenate(
                [qt_ref[(c * SWA_GROUP + g) * SWA_DIM:(c * SWA_GROUP + g + 1) * SWA_DIM, tsl]
                 for g in range(SWA_GROUP)], axis=1)
            qt_ext = jnp.concatenate([qt, zeros] if c == 0 else [zeros, qt], axis=0)
            s = jnp.dot(kband, qt_ext, preferred_element_type=F32) + bias_sc[general, c]
            sink = sink_sc[c]
            m = jnp.maximum(jnp.max(s, axis=0, keepdims=True), sink)
            p = jnp.exp2(s - m)
            denom = jnp.sum(p, axis=0, keepdims=True) + jnp.exp2(sink - m)
            dsl = slice(c * SWA_DIM, (c + 1) * SWA_DIM)
            vt = jnp.concatenate([vt_prev[dsl, :], vtc_ref[dsl, tsl]], axis=1)
            ot = jnp.dot(vt, p.astype(BF16), preferred_element_type=F32) / denom
            for g2 in range(SWA_GROUP // 2):
                two = jnp.concatenate([ot[:, (2 * g2) * BLK:(2 * g2 + 1) * BLK],
                                       ot[:, (2 * g2 + 1) * BLK:(2 * g2 + 2) * BLK]], axis=0)
                col0 = (c * SWA_GROUP + 2 * g2) * SWA_DIM
                o_ref[tsl, col0:col0 + 2 * SWA_DIM] = two.T.astype(o_ref.dtype)


def _swa(sqt, sk, svt, sinks, rel_bias, *, nb):
    nq, s = sqt.shape
    nkv = sk.shape[1]
    tb = nb * BLK
    bucket = jnp.asarray(_t5_bucket_table_t())
    smem = pl.BlockSpec(memory_space=pltpu.SMEM)
    prev_blk = lambda i: jnp.maximum(i * nb - 1, 0)
    return pl.pallas_call(
        functools.partial(_swa_kernel, nb=nb),
        grid=(s // tb,),
        in_specs=[smem, smem, _resident((2 * BLK, BLK)),
                  pl.BlockSpec((nq, tb), lambda i: (0, i)),
                  pl.BlockSpec((tb, nkv), lambda i: (i, 0)),
                  pl.BlockSpec((BLK, nkv), lambda i: (prev_blk(i), 0)),
                  pl.BlockSpec((nkv, tb), lambda i: (0, i)),
                  pl.BlockSpec((nkv, BLK), lambda i: (0, prev_blk(i)))],
        out_specs=pl.BlockSpec((tb, nq), lambda i: (i, 0)),
        out_shape=jax.ShapeDtypeStruct((s, nq), BF16),
        scratch_shapes=[pltpu.VMEM((2, SWA_KV_HEADS, 2 * BLK, SWA_GROUP * BLK), F32),
                        pltpu.VMEM((SWA_KV_HEADS, 1, SWA_GROUP * BLK), F32)],
        compiler_params=_cparams(("arbitrary",)),
        name="swa",
    )(rel_bias, sinks, bucket, sqt, sk, sk, svt, svt)


def _rms(y, gain):
    r = lax.rsqrt(jnp.sum(y * y, axis=-1, keepdims=True) * (1.0 / y.shape[-1]) + RMS_EPS)
    return y * r * gain


def _oproj_kernel(a_ref, b_ref, wo_ref, x_ref, gpost_ref, gpre_ref, x1_ref, h_ref, *, sub):
    na = a_ref.shape[1]
    for r0 in range(0, x_ref.shape[0], sub):
        rows = slice(r0, r0 + sub)
        mix = (jnp.dot(a_ref[rows, :], wo_ref[:na, :], preferred_element_type=F32)
               + jnp.dot(b_ref[rows, :], wo_ref[na:, :], preferred_element_type=F32))
        x1 = x_ref[rows, :] + _rms(mix, gpost_ref[...])
        x1_ref[rows, :] = x1
        h_ref[rows, :] = _rms(x1, gpre_ref[...]).astype(BF16)


def _oproj(out_a, out_b, w_o, x2, g_post, g_pre, *, tm, sub):
    s, d = x2.shape
    row = lambda w: pl.BlockSpec((tm, w), lambda i: (i, 0))
    return pl.pallas_call(
        functools.partial(_oproj_kernel, sub=sub),
        grid=(s // tm,),
        in_specs=[row(out_a.shape[1]), row(out_b.shape[1]), _resident(w_o.shape),
                  row(d), _resident((1, d)), _resident((1, d))],
        out_specs=[row(d), row(d)],
        out_shape=[jax.ShapeDtypeStruct((s, d), F32), jax.ShapeDtypeStruct((s, d), BF16)],
        compiler_params=_cparams(("arbitrary",)),
        name="oproj",
    )(out_a, out_b, w_o, x2, g_post, g_pre)


def _ffn_kernel(h_ref, wg_ref, wu_ref, wd_ref, x1_hbm, gpost_ref, o_ref, x1_buf, x1_sem):
    i = pl.program_id(0)
    j = pl.program_id(1)
    tm = o_ref.shape[0]

    def x1_copy():
        rows = pl.ds(pl.multiple_of(i * tm, tm), tm)
        return pltpu.make_async_copy(x1_hbm.at[rows, :], x1_buf, x1_sem)

    @pl.when(j == 0)
    def _():
        x1_copy().start()
        o_ref[...] = jnp.zeros(o_ref.shape, F32)

    h = h_ref[...]
    tf = wg_ref.shape[1]
    halves = [slice(0, tf // 2), slice(tf // 2, tf)]
    gu = [(jnp.dot(h, wg_ref[:, c], preferred_element_type=F32),
           jnp.dot(h, wu_ref[:, c], preferred_element_type=F32)) for c in halves]
    for c, (gate, up) in zip(halves, gu):
        act = (gate * jax.nn.sigmoid(gate) * up).astype(BF16)
        o_ref[...] += jnp.dot(act, wd_ref[c, :], preferred_element_type=F32)

    @pl.when(j == pl.num_programs(1) - 1)
    def _():
        x1_copy().wait()
        o_ref[...] = x1_buf[...] + _rms(o_ref[...], gpost_ref[...])


def _ffn(h, wg, wu, wd, x1, g_post, *, tm, tf):
    s, d = x1.shape
    dff = wg.shape[1]
    return pl.pallas_call(
        _ffn_kernel,
        grid=(s // tm, dff // tf),
        in_specs=[pl.BlockSpec((tm, d), lambda i, j: (i, 0)),
                  pl.BlockSpec((d, tf), lambda i, j: (0, j)),
                  pl.BlockSpec((d, tf), lambda i, j: (0, j)),
                  pl.BlockSpec((tf, d), lambda i, j: (j, 0)),
                  pl.BlockSpec(memory_space=pl.ANY),
                  _resident((1, d))],
        out_specs=pl.BlockSpec((tm, d), lambda i, j: (i, 0)),
        out_shape=jax.ShapeDtypeStruct((s, d), F32),
        scratch_shapes=[pltpu.VMEM((tm, d), F32), pltpu.SemaphoreType.DMA(())],
        compiler_params=_cparams(("arbitrary", "arbitrary")),
        name="ffn",
    )(h, wg, wu, wd, x1, g_post)


def _pad_lanes(w):
    return jnp.pad(w, ((0, 0), (0, -w.shape[1] % LANES)))


def _layer(x2, pos_row, p, *, tm_qkv=512):
    half = MLA_ROPE // 2
    inv_freq = ROPE_THETA ** (-jnp.arange(half, dtype=F32) / half)
    invf = jnp.broadcast_to(inv_freq[:, None], (half, tm_qkv))

    row = lambda a: a[None, :]
    cq, ckv, krt, sqt, sk, svt = _proj(x2, row(p["g_mix_pre"]), _pad_lanes(p["w_in"]).astype(BF16),
                                       row(p["g_cq"]), row(p["g_ckv"]), tm=512, sub=256)
    w_ukv = p["w_ukv"].reshape(KV_RANK, MLA_HEADS, MLA_NOPE + MLA_V)
    wuk = w_ukv[:, :, :MLA_NOPE].reshape(KV_RANK, MLA_HEADS * MLA_NOPE)
    wuvt = w_ukv[:, :, MLA_NOPE:].reshape(KV_RANK, MLA_HEADS * MLA_V).T
    qt, k, vt = _qkv(cq, ckv, krt, pos_row, invf, p["w_uq"].T.astype(BF16),
                     wuk.astype(BF16), wuvt.astype(BF16), tm=tm_qkv)
    out_a, (w_o, w_gate, w_up, w_down) = _mla(
        qt, k, vt, [p["w_o"], p["w_gate"], p["w_up"], p["w_down"]], tq=1024, tk=512, qps=4)
    out_b = _swa(sqt, sk, svt, p["sinks"], p["rel_bias"], nb=8)
    x1, h = _oproj(out_a, out_b, w_o, x2,
                   row(p["g_mix_post"]), row(p["g_ffn_pre"]), tm=512, sub=256)
    return _ffn(h, w_gate, w_up, w_down, x1, row(p["g_ffn_post"]), tm=1024, tf=512)


def kernel(x, positions, g_mix_pre, w_in, g_cq, g_ckv, w_uq, w_ukv, sinks, rel_bias,
           w_o, g_mix_post, g_ffn_pre, w_gate, w_up, w_down, g_ffn_post):
    b, s, d = x.shape
    assert b == 1, "the row-major (S, D) pipeline assumes a single sequence"
    x2 = x.reshape(s, d)
    pos_row = positions.reshape(1, s)
    for layer in range(w_in.shape[0]):
        p = dict(g_mix_pre=g_mix_pre[layer], w_in=w_in[layer], g_cq=g_cq[layer],
                 g_ckv=g_ckv[layer], w_uq=w_uq[layer], w_ukv=w_ukv[layer],
                 sinks=sinks[layer], rel_bias=rel_bias, w_o=w_o[layer],
                 g_mix_post=g_mix_post[layer], g_ffn_pre=g_ffn_pre[layer],
                 w_gate=w_gate[layer], w_up=w_up[layer], w_down=w_down[layer],
                 g_ffn_post=g_ffn_post[layer])
        x2 = _layer(x2, pos_row, p)
    return x2.reshape(b, s, d)
```

```python
import functools
import math

import jax
import jax.numpy as jnp
import numpy as np
from jax import lax
from jax.experimental import pallas as pl
from jax.experimental.pallas import tpu as pltpu

F32 = jnp.float32
BF16 = jnp.bfloat16

MLA_HEADS = 8
MLA_NOPE = 128
MLA_ROPE = 64
MLA_QK = MLA_NOPE + MLA_ROPE
MLA_V = 128
Q_RANK = 512
KV_RANK = 512
ROPE_THETA = 10000.0
SWA_HEADS = 16
SWA_KV_HEADS = 2
SWA_GROUP = SWA_HEADS // SWA_KV_HEADS
SWA_DIM = 64
WINDOW = 128
NUM_BUCKETS = 32
MAX_DISTANCE = 128
BLK = 128
RMS_EPS = 1e-6
NEG = -1e30
LOG2E = math.log2(math.e)

LANES = 128
VMEM_LIMIT = 60 * 1024 * 1024


def _cparams(sem):
    return pltpu.CompilerParams(dimension_semantics=sem, vmem_limit_bytes=VMEM_LIMIT)


def _resident(shape):
    nd = len(shape)
    return pl.BlockSpec(shape, lambda *_: (0,) * nd, pipeline_mode=pl.Buffered(1))


def _proj_kernel(x_ref, g_ref, w32_ref, gcq_ref, gckv_ref,
                 cq_ref, ckv_ref, krt_ref, sqt_ref, sk_ref, svt_ref, w_ref, *, swa_scale, sub):
    nq = SWA_HEADS * SWA_DIM
    nkv = SWA_KV_HEADS * SWA_DIM

    @pl.when(pl.program_id(0) == 0)
    def _():
        n = w32_ref.shape[1]
        w_ref[:, :n] = w32_ref[...].astype(BF16)
        w_ref[:, n:] = jnp.zeros((w_ref.shape[0], w_ref.shape[1] - n), BF16)

    def latent_norm(c, gain_ref):
        rc = lax.rsqrt(jnp.sum(c * c, axis=-1, keepdims=True) * (1.0 / c.shape[-1]) + RMS_EPS)
        return (c * rc * gain_ref[...]).astype(BF16)

    for r0 in range(0, x_ref.shape[0], sub):
        rows = slice(r0, r0 + sub)
        x = x_ref[rows, :]
        r = lax.rsqrt(jnp.sum(x * x, axis=-1, keepdims=True) * (1.0 / x.shape[-1]) + RMS_EPS)
        h = (x * g_ref[...]).astype(BF16)
        y = jnp.dot(h, w_ref[...], preferred_element_type=F32) * r
        cq_ref[rows, :] = latent_norm(y[:, :Q_RANK], gcq_ref)
        ckv_ref[rows, :] = latent_norm(y[:, Q_RANK:Q_RANK + KV_RANK], gckv_ref)
        yt = y[:, Q_RANK + KV_RANK:].T
        o = MLA_ROPE
        krt_ref[:, rows] = yt[:o]
        sqt_ref[:, rows] = (yt[o:o + nq] * swa_scale).astype(BF16)
        sk_ref[rows, :] = yt[o + nq:o + nq + nkv].T.astype(BF16)
        svt_ref[:, rows] = yt[o + nq + nkv:o + nq + 2 * nkv].astype(BF16)


def _proj(x2, g, w_in, g_cq, g_ckv, *, tm, sub):
    s, d = x2.shape
    n_pad = w_in.shape[1] + (-w_in.shape[1]) % LANES
    nq = SWA_HEADS * SWA_DIM
    nkv = SWA_KV_HEADS * SWA_DIM
    row = lambda w: pl.BlockSpec((tm, w), lambda i: (i, 0))
    col = lambda r: pl.BlockSpec((r, tm), lambda i: (0, i))
    return pl.pallas_call(
        functools.partial(_proj_kernel, swa_scale=LOG2E / math.sqrt(SWA_DIM), sub=sub),
        grid=(s // tm,),
        in_specs=[row(d), _resident((1, d)), _resident(w_in.shape),
                  _resident((1, Q_RANK)), _resident((1, KV_RANK))],
        out_specs=[row(Q_RANK), row(KV_RANK), col(MLA_ROPE), col(nq), row(nkv), col(nkv)],
        out_shape=[jax.ShapeDtypeStruct((s, Q_RANK), BF16),
                   jax.ShapeDtypeStruct((s, KV_RANK), BF16),
                   jax.ShapeDtypeStruct((MLA_ROPE, s), F32),
                   jax.ShapeDtypeStruct((nq, s), BF16),
                   jax.ShapeDtypeStruct((s, nkv), BF16),
                   jax.ShapeDtypeStruct((nkv, s), BF16)],
        scratch_shapes=[pltpu.VMEM((d, n_pad), BF16)],
        compiler_params=_cparams(("arbitrary",)),
        name="proj",
    )(x2, g, w_in, g_cq, g_ckv)


_NT = (((1,), (1,)), ((), ()))


def _qkv_kernel(cq_ref, ckv_ref, krt_ref, pos_ref, invf_ref, wuq_ref, wukv_ref,
                qt_ref, k_ref, vt_ref, wuqt_ref, wuk_ref, wuvt_ref, *, q_scale):
    tm = cq_ref.shape[0]
    half = MLA_ROPE // 2

    @pl.when(pl.program_id(0) == 0)
    def _():
        wuqt_ref[...] = wuq_ref[...].T.astype(BF16)
        for h in range(MLA_HEADS):
            c0 = h * (MLA_NOPE + MLA_V)
            wuk_ref[:, h * MLA_NOPE:(h + 1) * MLA_NOPE] = wukv_ref[:, c0:c0 + MLA_NOPE].astype(BF16)
            wuvt_ref[h * MLA_V:(h + 1) * MLA_V, :] = (
                wukv_ref[:, c0 + MLA_NOPE:c0 + MLA_NOPE + MLA_V].T.astype(BF16))
    ang = invf_ref[...] * pos_ref[...].astype(F32)
    cos = jnp.cos(ang)
    sin = jnp.sin(ang)

    def rope_t(t):
        t1, t2 = t[:half], t[half:]
        return jnp.concatenate([t1 * cos - t2 * sin, t2 * cos + t1 * sin], axis=0)

    cq = cq_ref[...]
    ckv = ckv_ref[...]
    qt = lax.dot_general(wuqt_ref[...], cq, _NT, preferred_element_type=F32) * q_scale
    for h in range(MLA_HEADS):
        base = h * MLA_QK
        qt_ref[h, 0:MLA_NOPE, :] = qt[base:base + MLA_NOPE].astype(BF16)
        qt_ref[h, MLA_NOPE:MLA_QK, :] = rope_t(qt[base + MLA_NOPE:base + MLA_QK]).astype(BF16)

    krt = jnp.concatenate([rope_t(krt_ref[...]), jnp.zeros((LANES - MLA_ROPE, tm), F32)], axis=0)
    kr = krt.T[:, :MLA_ROPE].astype(BF16)
    kn = jnp.dot(ckv, wuk_ref[...], preferred_element_type=F32)
    vt = lax.dot_general(wuvt_ref[...], ckv, _NT, preferred_element_type=F32)
    for h in range(MLA_HEADS):
        k_ref[h, :, 0:MLA_NOPE] = kn[:, h * MLA_NOPE:(h + 1) * MLA_NOPE].astype(BF16)
        k_ref[h, :, MLA_NOPE:MLA_QK] = kr
        for c in range(tm // LANES):
            vt_ref[h, c] = vt[h * MLA_V:(h + 1) * MLA_V, c * LANES:(c + 1) * LANES].astype(BF16)


def _qkv(cq, ckv, krt, pos_row, invf, w_uq, w_ukv, *, tm):
    s = cq.shape[0]
    rank = w_uq.shape[0]
    row = lambda w: pl.BlockSpec((tm, w), lambda i: (i, 0))
    col = lambda r: pl.BlockSpec((r, tm), lambda i: (0, i))
    nb = tm // LANES
    return pl.pallas_call(
        functools.partial(_qkv_kernel, q_scale=LOG2E / math.sqrt(MLA_QK)),
        grid=(s // tm,),
        in_specs=[row(Q_RANK), row(KV_RANK), col(MLA_ROPE), col(1), _resident(invf.shape),
                  _resident(w_uq.shape), _resident(w_ukv.shape)],
        out_specs=[pl.BlockSpec((MLA_HEADS, MLA_QK, tm), lambda i: (0, 0, i)),
                   pl.BlockSpec((MLA_HEADS, tm, MLA_QK), lambda i: (0, i, 0)),
                   pl.BlockSpec((MLA_HEADS, nb, MLA_V, LANES), lambda i: (0, i, 0, 0))],
        out_shape=[jax.ShapeDtypeStruct((MLA_HEADS, MLA_QK, s), BF16),
                   jax.ShapeDtypeStruct((MLA_HEADS, s, MLA_QK), BF16),
                   jax.ShapeDtypeStruct((MLA_HEADS, s // LANES, MLA_V, LANES), BF16)],
        scratch_shapes=[pltpu.VMEM((MLA_HEADS * MLA_QK, rank), BF16),
                        pltpu.VMEM((rank, MLA_HEADS * MLA_NOPE), BF16),
                        pltpu.VMEM((MLA_HEADS * MLA_V, rank), BF16)],
        compiler_params=_cparams(("arbitrary",)),
        name="qkv",
    )(cq, ckv, krt, pos_row, invf, w_uq, w_ukv)


def _mla_kernel(*refs, tq, tk, qps, ncast):
    qt_ref, k_ref, vt_ref = refs[:3]
    w_hbm = refs[3:3 + ncast]
    o_ref = refs[3 + ncast]
    wout_hbm = refs[4 + ncast:4 + 2 * ncast]
    m_sc, l_sc, acc_sc, s_sc = refs[4 + 2 * ncast:8 + 2 * ncast]
    stage_in = refs[8 + 2 * ncast:8 + 3 * ncast]
    stage_out = refs[8 + 3 * ncast:8 + 4 * ncast]
    in_sems, out_sems = refs[8 + 4 * ncast:]

    step = pl.program_id(0) * pl.num_programs(1) + pl.program_id(1)
    last_step = pl.num_programs(0) * pl.num_programs(1) - 1

    def slab_in(w):
        rows = stage_in[w].shape[0]
        src = w_hbm[w].at[pl.ds(pl.multiple_of(step * rows, rows), rows), :]
        return pltpu.make_async_copy(src, stage_in[w], in_sems.at[w])

    def slab_out(w, at_step):
        rows = stage_out[w].shape[0]
        dst = wout_hbm[w].at[pl.ds(pl.multiple_of(at_step * rows, rows), rows), :]
        return pltpu.make_async_copy(stage_out[w], dst, out_sems.at[w])

    for w in range(ncast):
        slab_in(w).start()

    group = tq // tk
    assert group * tk == tq and group % 2 == 0
    nsub = tk // LANES
    full = slice(0, tq)

    def one_query_block(qi, q0):
        m_sc[...] = jnp.full(m_sc.shape, NEG, F32)
        l_sc[...] = jnp.zeros(l_sc.shape, F32)
        acc_sc[...] = jnp.zeros(acc_sc.shape, F32)

        def scores(j, slot, cols=full):
            start = pl.multiple_of(j * tk, tk)
            qcols = slice(q0 + cols.start, q0 + cols.stop)
            s_sc[slot, :, cols] = jnp.dot(k_ref[0, pl.ds(start, tk), :], qt_ref[0, :, qcols],
                                          preferred_element_type=F32)

        def update(j, slot, cols=full, tri=False):
            s = s_sc[slot, :, cols]
            if tri:
                mask = (lax.broadcasted_iota(jnp.int32, (tk, tk), 0)
                        <= lax.broadcasted_iota(jnp.int32, (tk, tk), 1))
                parts = [jnp.where(mask, s[:, :tk], NEG)] + ([s[:, tk:]] if s.shape[1] > tk else [])
                s = jnp.concatenate(parts, axis=1)
            m_old = m_sc[:, cols]
            m_new = jnp.maximum(m_old, jnp.max(s, axis=0, keepdims=True))
            alpha = jnp.exp2(m_old - m_new)
            p = jnp.exp2(s - m_new)
            l_sc[:, cols] = alpha * l_sc[:, cols] + jnp.sum(p, axis=0, keepdims=True)
            vt = jnp.concatenate([vt_ref[0, j * nsub + c] for c in range(nsub)], axis=1)
            acc_sc[:, cols] = (alpha * acc_sc[:, cols]
                               + jnp.dot(vt, p.astype(BF16), preferred_element_type=F32))
            m_sc[:, cols] = m_new

        def full_group(i, carry):
            for r in range(group):
                b = group * i + r
                scores(b + 1, (r + 1) % 2)
                update(b, r % 2)
            return carry

        scores(0, 0)
        lax.fori_loop(0, qi, full_group, 0)
        for r in range(group):
            b = group * qi + r
            if r + 1 < group:
                scores(b + 1, (r + 1) % 2, slice((r + 1) * tk, tq))
            update(b, r % 2, slice(r * tk, tq), tri=True)
        o_ref[q0:q0 + tq, :] = (acc_sc[...] / l_sc[...]).T.astype(o_ref.dtype)

    for t in range(qps):
        one_query_block(pl.program_id(1) * qps + t, t * tq)

    @pl.when(step > 0)
    def _():
        for w in range(ncast):
            slab_out(w, step - 1).wait()

    for w in range(ncast):
        slab_in(w).wait()
        stage_out[w][...] = stage_in[w][...].astype(BF16)
        slab_out(w, step).start()

    @pl.when(step == last_step)
    def _():
        for w in range(ncast):
            slab_out(w, step).wait()


def _mla(qt, k, vt, cast_weights, *, tq, tk, qps):
    _, s, _ = k.shape
    grid = (MLA_HEADS, s // (tq * qps))
    nsteps = grid[0] * grid[1]
    ncast = len(cast_weights)
    slabs = [(w.shape[0] // nsteps, w.shape[1]) for w in cast_weights]
    for w, (rows, _) in zip(cast_weights, slabs):
        assert rows * nsteps == w.shape[0] and rows % 16 == 0, (w.shape, nsteps)
    anywhere = pl.BlockSpec(memory_space=pl.ANY)
    outs = pl.pallas_call(
        functools.partial(_mla_kernel, tq=tq, tk=tk, qps=qps, ncast=ncast),
        grid=grid,
        in_specs=[pl.BlockSpec((1, MLA_QK, tq * qps), lambda h, i: (h, 0, i)),
                  pl.BlockSpec((1, s, MLA_QK), lambda h, i: (h, 0, 0)),
                  pl.BlockSpec((1, s // LANES, MLA_V, LANES), lambda h, i: (h, 0, 0, 0))]
                 + [anywhere] * ncast,
        out_specs=[pl.BlockSpec((tq * qps, MLA_V), lambda h, i: (i, h))] + [anywhere] * ncast,
        out_shape=[jax.ShapeDtypeStruct((s, MLA_HEADS * MLA_V), BF16)]
                  + [jax.ShapeDtypeStruct(w.shape, BF16) for w in cast_weights],
        scratch_shapes=[pltpu.VMEM((1, tq), F32), pltpu.VMEM((1, tq), F32),
                        pltpu.VMEM((MLA_V, tq), F32),
                        pltpu.VMEM((2, tk, tq), F32)]
                       + [pltpu.VMEM(sl, F32) for sl in slabs]
                       + [pltpu.VMEM(sl, BF16) for sl in slabs]
                       + [pltpu.SemaphoreType.DMA((ncast,)), pltpu.SemaphoreType.DMA((ncast,))],
        compiler_params=_cparams(("arbitrary", "arbitrary")),
        name="mla",
    )(qt, k, vt, *cast_weights)
    return outs[0], outs[1:]


def _t5_bucket_table_t():
    i = np.arange(BLK)[None, :]
    j = np.arange(2 * BLK)[:, None]
    dist = i + BLK - j
    max_exact = NUM_BUCKETS // 2
    d = np.maximum(dist, 0)
    large = max_exact + (np.log(np.maximum(d, 1) / max_exact)
                         / np.log(MAX_DISTANCE / max_exact)
                         * (NUM_BUCKETS - max_exact)).astype(np.int32)
    large = np.minimum(large, NUM_BUCKETS - 1)
    bucket = np.where(d < max_exact, d, large).astype(np.int32)
    in_window = (dist >= 0) & (dist < WINDOW)
    return np.where(in_window, bucket, -1).astype(np.int32)


def _swa_kernel(relb_ref, sink_ref, bucket_ref, qt_ref, kc_ref, kp_ref, vtc_ref, vtp_ref,
                o_ref, bias_sc, sink_sc, *, nb):
    i = pl.program_id(0)
    gw = SWA_GROUP * BLK

    @pl.when(i == 0)
    def _():
        bucket = bucket_ref[...]
        for h in range(SWA_HEADS):
            c, g = divmod(h, SWA_GROUP)
            b = jnp.full(bucket.shape, NEG, F32)
            for t in range(NUM_BUCKETS):
                b = jnp.where(bucket == t, relb_ref[t, h] * LOG2E, b)
            bias_sc[1, c, :, g * BLK:(g + 1) * BLK] = b
            bias_sc[0, c, BLK:, g * BLK:(g + 1) * BLK] = b[BLK:]
            bias_sc[0, c, :BLK, g * BLK:(g + 1) * BLK] = jnp.full((BLK, BLK), NEG, F32)
            sink_sc[c, :, g * BLK:(g + 1) * BLK] = jnp.full((1, BLK), sink_ref[h] * LOG2E, F32)

    zeros = jnp.zeros((SWA_DIM, gw), BF16)
    for t in range(nb):
        tsl = slice(t * BLK, (t + 1) * BLK)
        k_prev = kp_ref[...] if t == 0 else kc_ref[(t - 1) * BLK:t * BLK, :]
        vt_prev = vtp_ref[...] if t == 0 else vtc_ref[:, (t - 1) * BLK:t * BLK]
        kband = jnp.concatenate([k_prev, kc_ref[tsl, :]], axis=0)
        general = 1 if t > 0 else jnp.where(i == 0, 0, 1)
        for c in range(SWA_KV_HEADS):
            qt = jnp.concatenate(
                [qt_ref[(c * SWA_GROUP + g) * SWA_DIM:(c * SWA_GROUP + g + 1) * SWA_DIM, tsl]
                 for g in range(SWA_GROUP)], axis=1)
            qt_ext = jnp.concatenate([qt, zeros] if c == 0 else [zeros, qt], axis=0)
            s = jnp.dot(kband, qt_ext, preferred_element_type=F32) + bias_sc[general, c]
            sink = sink_sc[c]
            m = jnp.maximum(jnp.max(s, axis=0, keepdims=True), sink)
            p = jnp.exp2(s - m)
            denom = jnp.sum(p, axis=0, keepdims=True) + jnp.exp2(sink - m)
            dsl = slice(c * SWA_DIM, (c + 1) * SWA_DIM)
            vt = jnp.concatenate([vt_prev[dsl, :], vtc_ref[dsl, tsl]], axis=1)
            ot = jnp.dot(vt, p.astype(BF16), preferred_element_type=F32) / denom
            for g2 in range(SWA_GROUP // 2):
                two = jnp.concatenate([ot[:, (2 * g2) * BLK:(2 * g2 + 1) * BLK],
                                       ot[:, (2 * g2 + 1) * BLK:(2 * g2 + 2) * BLK]], axis=0)
                col0 = (c * SWA_GROUP + 2 * g2) * SWA_DIM
                o_ref[tsl, col0:col0 + 2 * SWA_DIM] = two.T.astype(o_ref.dtype)


def _swa(sqt, sk, svt, sinks, rel_bias, *, nb):
    nq, s = sqt.shape
    nkv = sk.shape[1]
    tb = nb * BLK
    bucket = jnp.asarray(_t5_bucket_table_t())
    smem = pl.BlockSpec(memory_space=pltpu.SMEM)
    prev_blk = lambda i: jnp.maximum(i * nb - 1, 0)
    return pl.pallas_call(
        functools.partial(_swa_kernel, nb=nb),
        grid=(s // tb,),
        in_specs=[smem, smem, _resident((2 * BLK, BLK)),
                  pl.BlockSpec((nq, tb), lambda i: (0, i)),
                  pl.BlockSpec((tb, nkv), lambda i: (i, 0)),
                  pl.BlockSpec((BLK, nkv), lambda i: (prev_blk(i), 0)),
                  pl.BlockSpec((nkv, tb), lambda i: (0, i)),
                  pl.BlockSpec((nkv, BLK), lambda i: (0, prev_blk(i)))],
        out_specs=pl.BlockSpec((tb, nq), lambda i: (i, 0)),
        out_shape=jax.ShapeDtypeStruct((s, nq), BF16),
        scratch_shapes=[pltpu.VMEM((2, SWA_KV_HEADS, 2 * BLK, SWA_GROUP * BLK), F32),
                        pltpu.VMEM((SWA_KV_HEADS, 1, SWA_GROUP * BLK), F32)],
        compiler_params=_cparams(("arbitrary",)),
        name="swa",
    )(rel_bias, sinks, bucket, sqt, sk, sk, svt, svt)


def _rms(y, gain):
    r = lax.rsqrt(jnp.sum(y * y, axis=-1, keepdims=True) * (1.0 / y.shape[-1]) + RMS_EPS)
    return y * r * gain


def _oproj_kernel(a_ref, b_ref, wo_ref, x_ref, gpost_ref, gpre_ref, x1_ref, h_ref, *, sub):
    na = a_ref.shape[1]
    for r0 in range(0, x_ref.shape[0], sub):
        rows = slice(r0, r0 + sub)
        mix = (jnp.dot(a_ref[rows, :], wo_ref[:na, :], preferred_element_type=F32)
               + jnp.dot(b_ref[rows, :], wo_ref[na:, :], preferred_element_type=F32))
        x1 = x_ref[rows, :] + _rms(mix, gpost_ref[...])
        x1_ref[rows, :] = x1
        h_ref[rows, :] = _rms(x1, gpre_ref[...]).astype(BF16)


def _oproj(out_a, out_b, w_o, x2, g_post, g_pre, *, tm, sub):
    s, d = x2.shape
    row = lambda w: pl.BlockSpec((tm, w), lambda i: (i, 0))
    return pl.pallas_call(
        functools.partial(_oproj_kernel, sub=sub),
        grid=(s // tm,),
        in_specs=[row(out_a.shape[1]), row(out_b.shape[1]), _resident(w_o.shape),
                  row(d), _resident((1, d)), _resident((1, d))],
        out_specs=[row(d), row(d)],
        out_shape=[jax.ShapeDtypeStruct((s, d), F32), jax.ShapeDtypeStruct((s, d), BF16)],
        compiler_params=_cparams(("arbitrary",)),
        name="oproj",
    )(out_a, out_b, w_o, x2, g_post, g_pre)


def _ffn_kernel(h_ref, wg_ref, wu_ref, wd_ref, x1_hbm, gpost_ref, o_ref, x1_buf, x1_sem):
    i = pl.program_id(0)
    j = pl.program_id(1)
    tm = o_ref.shape[0]

    def x1_copy():
        rows = pl.ds(pl.multiple_of(i * tm, tm), tm)
        return pltpu.make_async_copy(x1_hbm.at[rows, :], x1_buf, x1_sem)

    @pl.when(j == 0)
    def _():
        x1_copy().start()
        o_ref[...] = jnp.zeros(o_ref.shape, F32)

    h = h_ref[...]
    tf = wg_ref.shape[1]
    halves = [slice(0, tf // 2), slice(tf // 2, tf)]
    gu = [(jnp.dot(h, wg_ref[:, c], preferred_element_type=F32),
           jnp.dot(h, wu_ref[:, c], preferred_element_type=F32)) for c in halves]
    for c, (gate, up) in zip(halves, gu):
        act = (gate * jax.nn.sigmoid(gate) * up).astype(BF16)
        o_ref[...] += jnp.dot(act, wd_ref[c, :], preferred_element_type=F32)

    @pl.when(j == pl.num_programs(1) - 1)
    def _():
        x1_copy().wait()
        o_ref[...] = x1_buf[...] + _rms(o_ref[...], gpost_ref[...])


def _ffn(h, wg, wu, wd, x1, g_post, *, tm, tf):
    s, d = x1.shape
    dff = wg.shape[1]
    return pl.pallas_call(
        _ffn_kernel,
        grid=(s // tm, dff // tf),
        in_specs=[pl.BlockSpec((tm, d), lambda i, j: (i, 0)),
                  pl.BlockSpec((d, tf), lambda i, j: (0, j)),
                  pl.BlockSpec((d, tf), lambda i, j: (0, j)),
                  pl.BlockSpec((tf, d), lambda i, j: (j, 0)),
                  pl.BlockSpec(memory_space=pl.ANY),
                  _resident((1, d))],
        out_specs=pl.BlockSpec((tm, d), lambda i, j: (i, 0)),
        out_shape=jax.ShapeDtypeStruct((s, d), F32),
        scratch_shapes=[pltpu.VMEM((tm, d), F32), pltpu.SemaphoreType.DMA(())],
        compiler_params=_cparams(("arbitrary", "arbitrary")),
        name="ffn",
    )(h, wg, wu, wd, x1, g_post)


def _layer(x2, pos_row, p, *, tm_qkv=512):
    half = MLA_ROPE // 2
    inv_freq = ROPE_THETA ** (-jnp.arange(half, dtype=F32) / half)
    invf = jnp.broadcast_to(inv_freq[:, None], (half, tm_qkv))

    row = lambda a: a[None, :]
    cq, ckv, krt, sqt, sk, svt = _proj(x2, row(p["g_mix_pre"]), p["w_in"],
                                       row(p["g_cq"]), row(p["g_ckv"]), tm=512, sub=256)
    qt, k, vt = _qkv(cq, ckv, krt, pos_row, invf, p["w_uq"], p["w_ukv"], tm=tm_qkv)
    out_a, (w_o, w_gate, w_up, w_down) = _mla(
        qt, k, vt, [p["w_o"], p["w_gate"], p["w_up"], p["w_down"]], tq=1024, tk=512, qps=4)
    out_b = _swa(sqt, sk, svt, p["sinks"], p["rel_bias"], nb=8)
    x1, h = _oproj(out_a, out_b, w_o, x2,
                   row(p["g_mix_post"]), row(p["g_ffn_pre"]), tm=512, sub=256)
    return _ffn(h, w_gate, w_up, w_down, x1, row(p["g_ffn_post"]), tm=1024, tf=512)


def kernel(x, positions, g_mix_pre, w_in, g_cq, g_ckv, w_uq, w_ukv, sinks, rel_bias,
           w_o, g_mix_post, g_ffn_pre, w_gate, w_up, w_down, g_ffn_post):
    b, s, d = x.shape
    assert b == 1, "the row-major (S, D) pipeline assumes a single sequence"
    x2 = x.reshape(s, d)
    pos_row = positions.reshape(1, s)
    for layer in range(w_in.shape[0]):
        p = dict(g_mix_pre=g_mix_pre[layer], w_in=w_in[layer], g_cq=g_cq[layer],
                 g_ckv=g_ckv[layer], w_uq=w_uq[layer], w_ukv=w_ukv[layer],
                 sinks=sinks[layer], rel_bias=rel_bias, w_o=w_o[layer],
                 g_mix_post=g_mix_post[layer], g_ffn_pre=g_ffn_pre[layer],
                 w_gate=w_gate[layer], w_up=w_up[layer], w_down=w_down[layer],
                 g_ffn_post=g_ffn_post[layer])
        x2 = _layer(x2, pos_row, p)
    return x2.reshape(b, s, d)
```

```python
import functools
import math

import jax
import jax.numpy as jnp
import numpy as np
from jax import lax
from jax.experimental import pallas as pl
from jax.experimental.pallas import tpu as pltpu

F32 = jnp.float32
BF16 = jnp.bfloat16

MLA_HEADS = 8
MLA_NOPE = 128
MLA_ROPE = 64
MLA_QK = MLA_NOPE + MLA_ROPE
MLA_V = 128
Q_RANK = 512
KV_RANK = 512
ROPE_THETA = 10000.0
SWA_HEADS = 16
SWA_KV_HEADS = 2
SWA_GROUP = SWA_HEADS // SWA_KV_HEADS
SWA_DIM = 64
WINDOW = 128
NUM_BUCKETS = 32
MAX_DISTANCE = 128
BLK = 128
RMS_EPS = 1e-6
NEG = -1e30
LOG2E = math.log2(math.e)

LANES = 128
VMEM_LIMIT = 60 * 1024 * 1024


def _cparams(sem):
    return pltpu.CompilerParams(dimension_semantics=sem, vmem_limit_bytes=VMEM_LIMIT)


_NT = (((1,), (1,)), ((), ()))


def _resident(shape):
    nd = len(shape)
    return pl.BlockSpec(shape, lambda *_: (0,) * nd, pipeline_mode=pl.Buffered(1))


def _proj_kernel(x_ref, g_ref, wt32_ref, gcq_ref, gckv_ref,
                 cq_ref, ckv_ref, krt_ref, sqt_ref, sk_ref, svt_ref, wt_ref, *, swa_scale, sub):
    nq = SWA_HEADS * SWA_DIM
    nkv = SWA_KV_HEADS * SWA_DIM
    nl = Q_RANK + KV_RANK

    @pl.when(pl.program_id(0) == 0)
    def _():
        wt_ref[...] = wt32_ref[...].astype(BF16)

    def latent_norm(c, gain_ref):
        rc = lax.rsqrt(jnp.sum(c * c, axis=-1, keepdims=True) * (1.0 / c.shape[-1]) + RMS_EPS)
        return (c * rc * gain_ref[...]).astype(BF16)

    for r0 in range(0, x_ref.shape[0], sub):
        rows = slice(r0, r0 + sub)
        x = x_ref[rows, :]
        r = lax.rsqrt(jnp.sum(x * x, axis=-1, keepdims=True) * (1.0 / x.shape[-1]) + RMS_EPS)
        h = (x * g_ref[...]).astype(BF16)
        yt = lax.dot_general(wt_ref[...], h, _NT, preferred_element_type=F32)
        c = yt[:nl].T * r
        cq_ref[rows, :] = latent_norm(c[:, :Q_RANK], gcq_ref)
        ckv_ref[rows, :] = latent_norm(c[:, Q_RANK:], gckv_ref)
        r_lanes = jnp.broadcast_to(r, (sub, LANES)).T[0:1, :]
        tail = yt[nl:] * r_lanes
        o = MLA_ROPE
        krt_ref[:, rows] = tail[:o]
        sqt_ref[:, rows] = (tail[o:o + nq] * swa_scale).astype(BF16)
        sk_ref[rows, :] = tail[o + nq:o + nq + nkv].T.astype(BF16)
        svt_ref[:, rows] = tail[o + nq + nkv:o + nq + 2 * nkv].astype(BF16)


def _proj(x2, g, w_in_t, g_cq, g_ckv, *, tm, sub):
    s, d = x2.shape
    nq = SWA_HEADS * SWA_DIM
    nkv = SWA_KV_HEADS * SWA_DIM
    row = lambda w: pl.BlockSpec((tm, w), lambda i: (i, 0))
    col = lambda r: pl.BlockSpec((r, tm), lambda i: (0, i))
    return pl.pallas_call(
        functools.partial(_proj_kernel, swa_scale=LOG2E / math.sqrt(SWA_DIM), sub=sub),
        grid=(s // tm,),
        in_specs=[row(d), _resident((1, d)), _resident(w_in_t.shape),
                  _resident((1, Q_RANK)), _resident((1, KV_RANK))],
        out_specs=[row(Q_RANK), row(KV_RANK), col(MLA_ROPE), col(nq), row(nkv), col(nkv)],
        out_shape=[jax.ShapeDtypeStruct((s, Q_RANK), BF16),
                   jax.ShapeDtypeStruct((s, KV_RANK), BF16),
                   jax.ShapeDtypeStruct((MLA_ROPE, s), F32),
                   jax.ShapeDtypeStruct((nq, s), BF16),
                   jax.ShapeDtypeStruct((s, nkv), BF16),
                   jax.ShapeDtypeStruct((nkv, s), BF16)],
        scratch_shapes=[pltpu.VMEM(w_in_t.shape, BF16)],
        compiler_params=_cparams(("arbitrary",)),
        name="proj",
    )(x2, g, w_in_t, g_cq, g_ckv)


def _qkv_kernel(cq_ref, ckv_ref, krt_ref, pos_ref, invf_ref, wuq_ref, wukv_ref,
                qt_ref, k_ref, vt_ref, wuqt_ref, wuk_ref, wuvt_ref, *, q_scale):
    tm = cq_ref.shape[0]
    half = MLA_ROPE // 2

    @pl.when(pl.program_id(0) == 0)
    def _():
        wuqt_ref[...] = wuq_ref[...].T.astype(BF16)
        for h in range(MLA_HEADS):
            c0 = h * (MLA_NOPE + MLA_V)
            wuk_ref[:, h * MLA_NOPE:(h + 1) * MLA_NOPE] = wukv_ref[:, c0:c0 + MLA_NOPE].astype(BF16)
            wuvt_ref[h * MLA_V:(h + 1) * MLA_V, :] = (
                wukv_ref[:, c0 + MLA_NOPE:c0 + MLA_NOPE + MLA_V].T.astype(BF16))
    ang = invf_ref[...] * pos_ref[...].astype(F32)
    cos = jnp.cos(ang)
    sin = jnp.sin(ang)

    def rope_t(t):
        t1, t2 = t[:half], t[half:]
        return jnp.concatenate([t1 * cos - t2 * sin, t2 * cos + t1 * sin], axis=0)

    cq = cq_ref[...]
    ckv = ckv_ref[...]
    qt = lax.dot_general(wuqt_ref[...], cq, _NT, preferred_element_type=F32) * q_scale
    for h in range(MLA_HEADS):
        base = h * MLA_QK
        qt_ref[h, 0:MLA_NOPE, :] = qt[base:base + MLA_NOPE].astype(BF16)
        qt_ref[h, MLA_NOPE:MLA_QK, :] = rope_t(qt[base + MLA_NOPE:base + MLA_QK]).astype(BF16)

    krt = jnp.concatenate([rope_t(krt_ref[...]), jnp.zeros((LANES - MLA_ROPE, tm), F32)], axis=0)
    kr = krt.T[:, :MLA_ROPE].astype(BF16)
    kn = jnp.dot(ckv, wuk_ref[...], preferred_element_type=F32)
    vt = lax.dot_general(wuvt_ref[...], ckv, _NT, preferred_element_type=F32)
    for h in range(MLA_HEADS):
        k_ref[h, :, 0:MLA_NOPE] = kn[:, h * MLA_NOPE:(h + 1) * MLA_NOPE].astype(BF16)
        k_ref[h, :, MLA_NOPE:MLA_QK] = kr
        for c in range(tm // LANES):
            vt_ref[h, c] = vt[h * MLA_V:(h + 1) * MLA_V, c * LANES:(c + 1) * LANES].astype(BF16)


def _qkv(cq, ckv, krt, pos_row, invf, w_uq, w_ukv, *, tm):
    s = cq.shape[0]
    rank = w_uq.shape[0]
    row = lambda w: pl.BlockSpec((tm, w), lambda i: (i, 0))
    col = lambda r: pl.BlockSpec((r, tm), lambda i: (0, i))
    nb = tm // LANES
    return pl.pallas_call(
        functools.partial(_qkv_kernel, q_scale=LOG2E / math.sqrt(MLA_QK)),
        grid=(s // tm,),
        in_specs=[row(Q_RANK), row(KV_RANK), col(MLA_ROPE), col(1), _resident(invf.shape),
                  _resident(w_uq.shape), _resident(w_ukv.shape)],
        out_specs=[pl.BlockSpec((MLA_HEADS, MLA_QK, tm), lambda i: (0, 0, i)),
                   pl.BlockSpec((MLA_HEADS, tm, MLA_QK), lambda i: (0, i, 0)),
                   pl.BlockSpec((MLA_HEADS, nb, MLA_V, LANES), lambda i: (0, i, 0, 0))],
        out_shape=[jax.ShapeDtypeStruct((MLA_HEADS, MLA_QK, s), BF16),
                   jax.ShapeDtypeStruct((MLA_HEADS, s, MLA_QK), BF16),
                   jax.ShapeDtypeStruct((MLA_HEADS, s // LANES, MLA_V, LANES), BF16)],
        scratch_shapes=[pltpu.VMEM((MLA_HEADS * MLA_QK, rank), BF16),
                        pltpu.VMEM((rank, MLA_HEADS * MLA_NOPE), BF16),
                        pltpu.VMEM((MLA_HEADS * MLA_V, rank), BF16)],
        compiler_params=_cparams(("arbitrary",)),
        name="qkv",
    )(cq, ckv, krt, pos_row, invf, w_uq, w_ukv)


def _mla_kernel(*refs, tq, tk, qps, ncast):
    qt_ref, k_ref, vt_ref = refs[:3]
    w_hbm = refs[3:3 + ncast]
    o_ref = refs[3 + ncast]
    wout_hbm = refs[4 + ncast:4 + 2 * ncast]
    m_sc, l_sc, acc_sc, s_sc = refs[4 + 2 * ncast:8 + 2 * ncast]
    stage_in = refs[8 + 2 * ncast:8 + 3 * ncast]
    stage_out = refs[8 + 3 * ncast:8 + 4 * ncast]
    in_sems, out_sems = refs[8 + 4 * ncast:]

    step = pl.program_id(0) * pl.num_programs(1) + pl.program_id(1)
    last_step = pl.num_programs(0) * pl.num_programs(1) - 1

    def slab_in(w):
        rows = stage_in[w].shape[0]
        src = w_hbm[w].at[pl.ds(pl.multiple_of(step * rows, rows), rows), :]
        return pltpu.make_async_copy(src, stage_in[w], in_sems.at[w])

    def slab_out(w, at_step):
        rows = stage_out[w].shape[0]
        dst = wout_hbm[w].at[pl.ds(pl.multiple_of(at_step * rows, rows), rows), :]
        return pltpu.make_async_copy(stage_out[w], dst, out_sems.at[w])

    for w in range(ncast):
        slab_in(w).start()

    group = tq // tk
    assert group * tk == tq and group % 2 == 0
    nsub = tk // LANES
    full = slice(0, tq)

    def one_query_block(qi, q0):
        m_sc[...] = jnp.full(m_sc.shape, NEG, F32)
        l_sc[...] = jnp.zeros(l_sc.shape, F32)
        acc_sc[...] = jnp.zeros(acc_sc.shape, F32)

        def scores(j, slot, cols=full):
            start = pl.multiple_of(j * tk, tk)
            qcols = slice(q0 + cols.start, q0 + cols.stop)
            s_sc[slot, :, cols] = jnp.dot(k_ref[0, pl.ds(start, tk), :], qt_ref[0, :, qcols],
                                          preferred_element_type=F32)

        def update(j, slot, cols=full, tri=False):
            s = s_sc[slot, :, cols]
            if tri:
                mask = (lax.broadcasted_iota(jnp.int32, (tk, tk), 0)
                        <= lax.broadcasted_iota(jnp.int32, (tk, tk), 1))
                parts = [jnp.where(mask, s[:, :tk], NEG)] + ([s[:, tk:]] if s.shape[1] > tk else [])
                s = jnp.concatenate(parts, axis=1)
            m_old = m_sc[:, cols]
            m_new = jnp.maximum(m_old, jnp.max(s, axis=0, keepdims=True))
            alpha = jnp.exp2(m_old - m_new)
            p = jnp.exp2(s - m_new)
            l_sc[:, cols] = alpha * l_sc[:, cols] + jnp.sum(p, axis=0, keepdims=True)
            vt = jnp.concatenate([vt_ref[0, j * nsub + c] for c in range(nsub)], axis=1)
            acc_sc[:, cols] = (alpha * acc_sc[:, cols]
                               + jnp.dot(vt, p.astype(BF16), preferred_element_type=F32))
            m_sc[:, cols] = m_new

        def full_group(i, carry):
            for r in range(group):
                b = group * i + r
                scores(b + 1, (r + 1) % 2)
                update(b, r % 2)
            return carry

        scores(0, 0)
        lax.fori_loop(0, qi, full_group, 0)
        for r in range(group):
            b = group * qi + r
            if r + 1 < group:
                scores(b + 1, (r + 1) % 2, slice((r + 1) * tk, tq))
            update(b, r % 2, slice(r * tk, tq), tri=True)
        o_ref[q0:q0 + tq, :] = (acc_sc[...] / l_sc[...]).T.astype(o_ref.dtype)

    for t in range(qps):
        one_query_block(pl.program_id(1) * qps + t, t * tq)

    @pl.when(step > 0)
    def _():
        for w in range(ncast):
            slab_out(w, step - 1).wait()

    for w in range(ncast):
        slab_in(w).wait()
        stage_out[w][...] = stage_in[w][...].astype(BF16)
        slab_out(w, step).start()

    @pl.when(step == last_step)
    def _():
        for w in range(ncast):
            slab_out(w, step).wait()


def _mla(qt, k, vt, cast_weights, *, tq, tk, qps):
    _, s, _ = k.shape
    grid = (MLA_HEADS, s // (tq * qps))
    nsteps = grid[0] * grid[1]
    ncast = len(cast_weights)
    slabs = [(w.shape[0] // nsteps, w.shape[1]) for w in cast_weights]
    for w, (rows, _) in zip(cast_weights, slabs):
        assert rows * nsteps == w.shape[0] and rows % 16 == 0, (w.shape, nsteps)
    anywhere = pl.BlockSpec(memory_space=pl.ANY)
    outs = pl.pallas_call(
        functools.partial(_mla_kernel, tq=tq, tk=tk, qps=qps, ncast=ncast),
        grid=grid,
        in_specs=[pl.BlockSpec((1, MLA_QK, tq * qps), lambda h, i: (h, 0, i)),
                  pl.BlockSpec((1, s, MLA_QK), lambda h, i: (h, 0, 0)),
                  pl.BlockSpec((1, s // LANES, MLA_V, LANES), lambda h, i: (h, 0, 0, 0))]
                 + [anywhere] * ncast,
        out_specs=[pl.BlockSpec((tq * qps, MLA_V), lambda h, i: (i, h))] + [anywhere] * ncast,
        out_shape=[jax.ShapeDtypeStruct((s, MLA_HEADS * MLA_V), BF16)]
                  + [jax.ShapeDtypeStruct(w.shape, BF16) for w in cast_weights],
        scratch_shapes=[pltpu.VMEM((1, tq), F32), pltpu.VMEM((1, tq), F32),
                        pltpu.VMEM((MLA_V, tq), F32),
                        pltpu.VMEM((2, tk, tq), F32)]
                       + [pltpu.VMEM(sl, F32) for sl in slabs]
                       + [pltpu.VMEM(sl, BF16) for sl in slabs]
                       + [pltpu.SemaphoreType.DMA((ncast,)), pltpu.SemaphoreType.DMA((ncast,))],
        compiler_params=_cparams(("arbitrary", "arbitrary")),
        name="mla",
    )(qt, k, vt, *cast_weights)
    return outs[0], outs[1:]


def _t5_bucket_table_t():
    i = np.arange(BLK)[None, :]
    j = np.arange(2 * BLK)[:, None]
    dist = i + BLK - j
    max_exact = NUM_BUCKETS // 2
    d = np.maximum(dist, 0)
    large = max_exact + (np.log(np.maximum(d, 1) / max_exact)
                         / np.log(MAX_DISTANCE / max_exact)
                         * (NUM_BUCKETS - max_exact)).astype(np.int32)
    large = np.minimum(large, NUM_BUCKETS - 1)
    bucket = np.where(d < max_exact, d, large).astype(np.int32)
    in_window = (dist >= 0) & (dist < WINDOW)
    return np.where(in_window, bucket, -1).astype(np.int32)


def _swa_kernel(relb_ref, sink_ref, bucket_ref, qt_ref, kc_ref, kp_ref, vtc_ref, vtp_ref,
                o_ref, bias_sc, sink_sc, *, nb):
    i = pl.program_id(0)
    gw = SWA_GROUP * BLK

    @pl.when(i == 0)
    def _():
        bucket = bucket_ref[...]
        for h in range(SWA_HEADS):
            c, g = divmod(h, SWA_GROUP)
            b = jnp.full(bucket.shape, NEG, F32)
            for t in range(NUM_BUCKETS):
                b = jnp.where(bucket == t, relb_ref[t, h] * LOG2E, b)
            bias_sc[1, c, :, g * BLK:(g + 1) * BLK] = b
            bias_sc[0, c, BLK:, g * BLK:(g + 1) * BLK] = b[BLK:]
            bias_sc[0, c, :BLK, g * BLK:(g + 1) * BLK] = jnp.full((BLK, BLK), NEG, F32)
            sink_sc[c, :, g * BLK:(g + 1) * BLK] = jnp.full((1, BLK), sink_ref[h] * LOG2E, F32)

    zeros = jnp.zeros((SWA_DIM, gw), BF16)
    for t in range(nb):
        tsl = slice(t * BLK, (t + 1) * BLK)
        k_prev = kp_ref[...] if t == 0 else kc_ref[(t - 1) * BLK:t * BLK, :]
        vt_prev = vtp_ref[...] if t == 0 else vtc_ref[:, (t - 1) * BLK:t * BLK]
        kband = jnp.concatenate([k_prev, kc_ref[tsl, :]], axis=0)
        general = 1 if t > 0 else jnp.where(i == 0, 0, 1)
        for c in range(SWA_KV_HEADS):
            qt = jnp.concatenate(
                [qt_ref[(c * SWA_GROUP + g) * SWA_DIM:(c * SWA_GROUP + g + 1) * SWA_DIM, tsl]
                 for g in range(SWA_GROUP)], axis=1)
            qt_ext = jnp.concatenate([qt, zeros] if c == 0 else [zeros, qt], axis=0)
            s = jnp.dot(kband, qt_ext, preferred_element_type=F32) + bias_sc[general, c]
            sink = sink_sc[c]
            m = jnp.maximum(jnp.max(s, axis=0, keepdims=True), sink)
            p = jnp.exp2(s - m)
            denom = jnp.sum(p, axis=0, keepdims=True) + jnp.exp2(sink - m)
            dsl = slice(c * SWA_DIM, (c + 1) * SWA_DIM)
            vt = jnp.concatenate([vt_prev[dsl, :], vtc_ref[dsl, tsl]], axis=1)
            ot = jnp.dot(vt, p.astype(BF16), preferred_element_type=F32) / denom
            for g2 in range(SWA_GROUP // 2):
                two = jnp.concatenate([ot[:, (2 * g2) * BLK:(2 * g2 + 1) * BLK],
                                       ot[:, (2 * g2 + 1) * BLK:(2 * g2 + 2) * BLK]], axis=0)
                col0 = (c * SWA_GROUP + 2 * g2) * SWA_DIM
                o_ref[tsl, col0:col0 + 2 * SWA_DIM] = two.T.astype(o_ref.dtype)


def _swa(sqt, sk, svt, sinks, rel_bias, *, nb):
    nq, s = sqt.shape
    nkv = sk.shape[1]
    tb = nb * BLK
    bucket = jnp.asarray(_t5_bucket_table_t())
    smem = pl.BlockSpec(memory_space=pltpu.SMEM)
    prev_blk = lambda i: jnp.maximum(i * nb - 1, 0)
    return pl.pallas_call(
        functools.partial(_swa_kernel, nb=nb),
        grid=(s // tb,),
        in_specs=[smem, smem, _resident((2 * BLK, BLK)),
                  pl.BlockSpec((nq, tb), lambda i: (0, i)),
                  pl.BlockSpec((tb, nkv), lambda i: (i, 0)),
                  pl.BlockSpec((BLK, nkv), lambda i: (prev_blk(i), 0)),
                  pl.BlockSpec((nkv, tb), lambda i: (0, i)),
                  pl.BlockSpec((nkv, BLK), lambda i: (0, prev_blk(i)))],
        out_specs=pl.BlockSpec((tb, nq), lambda i: (i, 0)),
        out_shape=jax.ShapeDtypeStruct((s, nq), BF16),
        scratch_shapes=[pltpu.VMEM((2, SWA_KV_HEADS, 2 * BLK, SWA_GROUP * BLK), F32),
                        pltpu.VMEM((SWA_KV_HEADS, 1, SWA_GROUP * BLK), F32)],
        compiler_params=_cparams(("arbitrary",)),
        name="swa",
    )(rel_bias, sinks, bucket, sqt, sk, sk, svt, svt)


def _rms(y, gain):
    r = lax.rsqrt(jnp.sum(y * y, axis=-1, keepdims=True) * (1.0 / y.shape[-1]) + RMS_EPS)
    return y * r * gain


def _oproj_kernel(a_ref, b_ref, wo_ref, x_ref, gpost_ref, gpre_ref, x1_ref, h_ref, *, sub):
    na = a_ref.shape[1]
    for r0 in range(0, x_ref.shape[0], sub):
        rows = slice(r0, r0 + sub)
        mix = (jnp.dot(a_ref[rows, :], wo_ref[:na, :], preferred_element_type=F32)
               + jnp.dot(b_ref[rows, :], wo_ref[na:, :], preferred_element_type=F32))
        x1 = x_ref[rows, :] + _rms(mix, gpost_ref[...])
        x1_ref[rows, :] = x1
        h_ref[rows, :] = _rms(x1, gpre_ref[...]).astype(BF16)


def _oproj(out_a, out_b, w_o, x2, g_post, g_pre, *, tm, sub):
    s, d = x2.shape
    row = lambda w: pl.BlockSpec((tm, w), lambda i: (i, 0))
    return pl.pallas_call(
        functools.partial(_oproj_kernel, sub=sub),
        grid=(s // tm,),
        in_specs=[row(out_a.shape[1]), row(out_b.shape[1]), _resident(w_o.shape),
                  row(d), _resident((1, d)), _resident((1, d))],
        out_specs=[row(d), row(d)],
        out_shape=[jax.ShapeDtypeStruct((s, d), F32), jax.ShapeDtypeStruct((s, d), BF16)],
        compiler_params=_cparams(("arbitrary",)),
        name="oproj",
    )(out_a, out_b, w_o, x2, g_post, g_pre)


def _ffn_kernel(h_ref, wg_ref, wu_ref, wd_ref, x1_hbm, gpost_ref, o_ref, x1_buf, x1_sem):
    i = pl.program_id(0)
    j = pl.program_id(1)
    tm = o_ref.shape[0]

    def x1_copy():
        rows = pl.ds(pl.multiple_of(i * tm, tm), tm)
        return pltpu.make_async_copy(x1_hbm.at[rows, :], x1_buf, x1_sem)

    @pl.when(j == 0)
    def _():
        x1_copy().start()
        o_ref[...] = jnp.zeros(o_ref.shape, F32)

    h = h_ref[...]
    tf = wg_ref.shape[1]
    halves = [slice(0, tf // 2), slice(tf // 2, tf)]
    gu = [(jnp.dot(h, wg_ref[:, c], preferred_element_type=F32),
           jnp.dot(h, wu_ref[:, c], preferred_element_type=F32)) for c in halves]
    for c, (gate, up) in zip(halves, gu):
        act = (gate * jax.nn.sigmoid(gate) * up).astype(BF16)
        o_ref[...] += jnp.dot(act, wd_ref[c, :], preferred_element_type=F32)

    @pl.when(j == pl.num_programs(1) - 1)
    def _():
        x1_copy().wait()
        o_ref[...] = x1_buf[...] + _rms(o_ref[...], gpost_ref[...])


def _ffn(h, wg, wu, wd, x1, g_post, *, tm, tf):
    s, d = x1.shape
    dff = wg.shape[1]
    return pl.pallas_call(
        _ffn_kernel,
        grid=(s // tm, dff // tf),
        in_specs=[pl.BlockSpec((tm, d), lambda i, j: (i, 0)),
                  pl.BlockSpec((d, tf), lambda i, j: (0, j)),
                  pl.BlockSpec((d, tf), lambda i, j: (0, j)),
                  pl.BlockSpec((tf, d), lambda i, j: (j, 0)),
                  pl.BlockSpec(memory_space=pl.ANY),
                  _resident((1, d))],
        out_specs=pl.BlockSpec((tm, d), lambda i, j: (i, 0)),
        out_shape=jax.ShapeDtypeStruct((s, d), F32),
        scratch_shapes=[pltpu.VMEM((tm, d), F32), pltpu.SemaphoreType.DMA(())],
        compiler_params=_cparams(("arbitrary", "arbitrary")),
        name="ffn",
    )(h, wg, wu, wd, x1, g_post)


def _layer(x2, pos_row, p, *, tm_qkv=512):
    half = MLA_ROPE // 2
    inv_freq = ROPE_THETA ** (-jnp.arange(half, dtype=F32) / half)
    invf = jnp.broadcast_to(inv_freq[:, None], (half, tm_qkv))

    row = lambda a: a[None, :]
    cq, ckv, krt, sqt, sk, svt = _proj(x2, row(p["g_mix_pre"]), p["w_in"].T,
                                       row(p["g_cq"]), row(p["g_ckv"]), tm=512, sub=256)
    qt, k, vt = _qkv(cq, ckv, krt, pos_row, invf, p["w_uq"], p["w_ukv"], tm=tm_qkv)
    out_a, (w_o, w_gate, w_up, w_down) = _mla(
        qt, k, vt, [p["w_o"], p["w_gate"], p["w_up"], p["w_down"]], tq=1024, tk=512, qps=4)
    out_b = _swa(sqt, sk, svt, p["sinks"], p["rel_bias"], nb=8)
    x1, h = _oproj(out_a, out_b, w_o, x2,
                   row(p["g_mix_post"]), row(p["g_ffn_pre"]), tm=512, sub=256)
    return _ffn(h, w_gate, w_up, w_down, x1, row(p["g_ffn_post"]), tm=1024, tf=512)


def kernel(x, positions, g_mix_pre, w_in, g_cq, g_ckv, w_uq, w_ukv, sinks, rel_bias,
           w_o, g_mix_post, g_ffn_pre, w_gate, w_up, w_down, g_ffn_post):
    b, s, d = x.shape
    assert b == 1, "the row-major (S, D) pipeline assumes a single sequence"
    x2 = x.reshape(s, d)
    pos_row = positions.reshape(1, s)
    for layer in range(w_in.shape[0]):
        p = dict(g_mix_pre=g_mix_pre[layer], w_in=w_in[layer], g_cq=g_cq[layer],
                 g_ckv=g_ckv[layer], w_uq=w_uq[layer], w_ukv=w_ukv[layer],
                 sinks=sinks[layer], rel_bias=rel_bias, w_o=w_o[layer],
                 g_mix_post=g_mix_post[layer], g_ffn_pre=g_ffn_pre[layer],
                 w_gate=w_gate[layer], w_up=w_up[layer], w_down=w_down[layer],
                 g_ffn_post=g_ffn_post[layer])
        x2 = _layer(x2, pos_row, p)
    return x2.reshape(b, s, d)
```

```python
import functools
import math

import jax
import jax.numpy as jnp
import numpy as np
from jax import lax
from jax.experimental import pallas as pl
from jax.experimental.pallas import tpu as pltpu

F32 = jnp.float32
BF16 = jnp.bfloat16

MLA_HEADS = 8
MLA_NOPE = 128
MLA_ROPE = 64
MLA_QK = MLA_NOPE + MLA_ROPE
MLA_V = 128
Q_RANK = 512
KV_RANK = 512
ROPE_THETA = 10000.0
SWA_HEADS = 16
SWA_KV_HEADS = 2
SWA_GROUP = SWA_HEADS // SWA_KV_HEADS
SWA_DIM = 64
WINDOW = 128
NUM_BUCKETS = 32
MAX_DISTANCE = 128
BLK = 128
RMS_EPS = 1e-6
NEG = -1e30
LOG2E = math.log2(math.e)

LANES = 128
VMEM_LIMIT = 60 * 1024 * 1024


def _cparams(sem):
    return pltpu.CompilerParams(dimension_semantics=sem, vmem_limit_bytes=VMEM_LIMIT)


_NT = (((1,), (1,)), ((), ()))


def _resident(shape):
    nd = len(shape)
    return pl.BlockSpec(shape, lambda *_: (0,) * nd, pipeline_mode=pl.Buffered(1))


def _proj_kernel(x_ref, g_ref, wt32_ref, gcq_ref, gckv_ref,
                 cq_ref, ckv_ref, krt_ref, sqt_ref, sk_ref, svt_ref, wt_ref, *, swa_scale, sub):
    nq = SWA_HEADS * SWA_DIM
    nkv = SWA_KV_HEADS * SWA_DIM
    nl = Q_RANK + KV_RANK

    @pl.when(pl.program_id(0) == 0)
    def _():
        wt_ref[...] = wt32_ref[...].astype(BF16)

    def latent_norm(c, gain_ref):
        rc = lax.rsqrt(jnp.sum(c * c, axis=-1, keepdims=True) * (1.0 / c.shape[-1]) + RMS_EPS)
        return (c * rc * gain_ref[...]).astype(BF16)

    for r0 in range(0, x_ref.shape[0], sub):
        rows = slice(r0, r0 + sub)
        x = x_ref[rows, :]
        r = lax.rsqrt(jnp.sum(x * x, axis=-1, keepdims=True) * (1.0 / x.shape[-1]) + RMS_EPS)
        h = (x * g_ref[...]).astype(BF16)
        yt = lax.dot_general(wt_ref[...], h, _NT, preferred_element_type=F32)
        c = yt[:nl].T * r
        cq_ref[rows, :] = latent_norm(c[:, :Q_RANK], gcq_ref)
        ckv_ref[rows, :] = latent_norm(c[:, Q_RANK:], gckv_ref)
        r_lanes = jnp.broadcast_to(r, (sub, LANES)).T[0:1, :]
        tail = yt[nl:] * r_lanes
        o = MLA_ROPE
        krt_ref[:, rows] = tail[:o]
        sqt_ref[:, rows] = (tail[o:o + nq] * swa_scale).astype(BF16)
        sk_ref[rows, :] = tail[o + nq:o + nq + nkv].T.astype(BF16)
        svt_ref[:, rows] = tail[o + nq + nkv:o + nq + 2 * nkv].astype(BF16)


def _proj(x2, g, w_in_t, g_cq, g_ckv, *, tm, sub):
    s, d = x2.shape
    nq = SWA_HEADS * SWA_DIM
    nkv = SWA_KV_HEADS * SWA_DIM
    row = lambda w: pl.BlockSpec((tm, w), lambda i: (i, 0))
    col = lambda r: pl.BlockSpec((r, tm), lambda i: (0, i))
    return pl.pallas_call(
        functools.partial(_proj_kernel, swa_scale=LOG2E / math.sqrt(SWA_DIM), sub=sub),
        grid=(s // tm,),
        in_specs=[row(d), _resident((1, d)), _resident(w_in_t.shape),
                  _resident((1, Q_RANK)), _resident((1, KV_RANK))],
        out_specs=[row(Q_RANK), row(KV_RANK), col(MLA_ROPE), col(nq), row(nkv), col(nkv)],
        out_shape=[jax.ShapeDtypeStruct((s, Q_RANK), BF16),
                   jax.ShapeDtypeStruct((s, KV_RANK), BF16),
                   jax.ShapeDtypeStruct((MLA_ROPE, s), F32),
                   jax.ShapeDtypeStruct((nq, s), BF16),
                   jax.ShapeDtypeStruct((s, nkv), BF16),
                   jax.ShapeDtypeStruct((nkv, s), BF16)],
        scratch_shapes=[pltpu.VMEM(w_in_t.shape, BF16)],
        compiler_params=_cparams(("arbitrary",)),
        name="proj",
    )(x2, g, w_in_t, g_cq, g_ckv)


def _qkv_kernel(cq_ref, ckv_ref, krt_ref, pos_ref, invf_ref, wuq_ref, wukv_ref,
                qt_ref, k_ref, vt_ref, wuqt_ref, wuk_ref, wuvt_ref, *, q_scale):
    tm = cq_ref.shape[0]
    half = MLA_ROPE // 2

    @pl.when(pl.program_id(0) == 0)
    def _():
        wuqt_ref[...] = wuq_ref[...].T.astype(BF16)
        for h in range(MLA_HEADS):
            c0 = h * (MLA_NOPE + MLA_V)
            wuk_ref[:, h * MLA_NOPE:(h + 1) * MLA_NOPE] = wukv_ref[:, c0:c0 + MLA_NOPE].astype(BF16)
            wuvt_ref[h * MLA_V:(h + 1) * MLA_V, :] = (
                wukv_ref[:, c0 + MLA_NOPE:c0 + MLA_NOPE + MLA_V].T.astype(BF16))
    ang = invf_ref[...] * pos_ref[...].astype(F32)
    cos = jnp.cos(ang)
    sin = jnp.sin(ang)

    def rope_t(t):
        t1, t2 = t[:half], t[half:]
        return jnp.concatenate([t1 * cos - t2 * sin, t2 * cos + t1 * sin], axis=0)

    cq = cq_ref[...]
    ckv = ckv_ref[...]
    qt = lax.dot_general(wuqt_ref[...], cq, _NT, preferred_element_type=F32) * q_scale
    for h in range(MLA_HEADS):
        base = h * MLA_QK
        qt_ref[h, 0:MLA_NOPE, :] = qt[base:base + MLA_NOPE].astype(BF16)
        qt_ref[h, MLA_NOPE:MLA_QK, :] = rope_t(qt[base + MLA_NOPE:base + MLA_QK]).astype(BF16)

    krt = jnp.concatenate([rope_t(krt_ref[...]), jnp.zeros((LANES - MLA_ROPE, tm), F32)], axis=0)
    kr = krt.T[:, :MLA_ROPE].astype(BF16)
    kn = jnp.dot(ckv, wuk_ref[...], preferred_element_type=F32)
    vt = lax.dot_general(wuvt_ref[...], ckv, _NT, preferred_element_type=F32)
    for h in range(MLA_HEADS):
        k_ref[h, :, 0:MLA_NOPE] = kn[:, h * MLA_NOPE:(h + 1) * MLA_NOPE].astype(BF16)
        k_ref[h, :, MLA_NOPE:MLA_QK] = kr
        for c in range(tm // LANES):
            vt_ref[h, c] = vt[h * MLA_V:(h + 1) * MLA_V, c * LANES:(c + 1) * LANES].astype(BF16)


def _qkv(cq, ckv, krt, pos_row, invf, w_uq, w_ukv, *, tm):
    s = cq.shape[0]
    rank = w_uq.shape[0]
    row = lambda w: pl.BlockSpec((tm, w), lambda i: (i, 0))
    col = lambda r: pl.BlockSpec((r, tm), lambda i: (0, i))
    nb = tm // LANES
    return pl.pallas_call(
        functools.partial(_qkv_kernel, q_scale=LOG2E / math.sqrt(MLA_QK)),
        grid=(s // tm,),
        in_specs=[row(Q_RANK), row(KV_RANK), col(MLA_ROPE), col(1), _resident(invf.shape),
                  _resident(w_uq.shape), _resident(w_ukv.shape)],
        out_specs=[pl.BlockSpec((MLA_HEADS, MLA_QK, tm), lambda i: (0, 0, i)),
                   pl.BlockSpec((MLA_HEADS, tm, MLA_QK), lambda i: (0, i, 0)),
                   pl.BlockSpec((MLA_HEADS, nb, MLA_V, LANES), lambda i: (0, i, 0, 0))],
        out_shape=[jax.ShapeDtypeStruct((MLA_HEADS, MLA_QK, s), BF16),
                   jax.ShapeDtypeStruct((MLA_HEADS, s, MLA_QK), BF16),
                   jax.ShapeDtypeStruct((MLA_HEADS, s // LANES, MLA_V, LANES), BF16)],
        scratch_shapes=[pltpu.VMEM((MLA_HEADS * MLA_QK, rank), BF16),
                        pltpu.VMEM((rank, MLA_HEADS * MLA_NOPE), BF16),
                        pltpu.VMEM((MLA_HEADS * MLA_V, rank), BF16)],
        compiler_params=_cparams(("arbitrary",)),
        name="qkv",
    )(cq, ckv, krt, pos_row, invf, w_uq, w_ukv)


def _mla_kernel(*refs, tq, tk, qps, ncast):
    qt_ref, k_ref, vt_ref = refs[:3]
    w_hbm = refs[3:3 + ncast]
    o_ref = refs[3 + ncast]
    wout_hbm = refs[4 + ncast:4 + 2 * ncast]
    m_sc, l_sc, acc_sc, s_sc, smax_sc = refs[4 + 2 * ncast:9 + 2 * ncast]
    stage_in = refs[9 + 2 * ncast:9 + 3 * ncast]
    stage_out = refs[9 + 3 * ncast:9 + 4 * ncast]
    in_sems, out_sems = refs[9 + 4 * ncast:]

    step = pl.program_id(0) * pl.num_programs(1) + pl.program_id(1)
    last_step = pl.num_programs(0) * pl.num_programs(1) - 1

    def slab_in(w):
        rows = stage_in[w].shape[0]
        src = w_hbm[w].at[pl.ds(pl.multiple_of(step * rows, rows), rows), :]
        return pltpu.make_async_copy(src, stage_in[w], in_sems.at[w])

    def slab_out(w, at_step):
        rows = stage_out[w].shape[0]
        dst = wout_hbm[w].at[pl.ds(pl.multiple_of(at_step * rows, rows), rows), :]
        return pltpu.make_async_copy(stage_out[w], dst, out_sems.at[w])

    for w in range(ncast):
        slab_in(w).start()

    group = tq // tk
    assert group * tk == tq and group % 2 == 0
    nsub = tk // LANES
    full = slice(0, tq)

    def one_query_block(qi, q0):
        m_sc[...] = jnp.full(m_sc.shape, NEG, F32)
        l_sc[...] = jnp.zeros(l_sc.shape, F32)
        acc_sc[...] = jnp.zeros(acc_sc.shape, F32)

        def scores(j, slot, cols=full):
            start = pl.multiple_of(j * tk, tk)
            qcols = slice(q0 + cols.start, q0 + cols.stop)
            s = jnp.dot(k_ref[0, pl.ds(start, tk), :], qt_ref[0, :, qcols],
                        preferred_element_type=F32)
            s_sc[slot, :, cols] = s
            smax_sc[slot, :, cols] = jnp.max(s, axis=0, keepdims=True)

        def update(j, slot, cols=full, tri=False):
            s = s_sc[slot, :, cols]
            if tri:
                mask = (lax.broadcasted_iota(jnp.int32, (tk, tk), 0)
                        <= lax.broadcasted_iota(jnp.int32, (tk, tk), 1))
                parts = [jnp.where(mask, s[:, :tk], NEG)] + ([s[:, tk:]] if s.shape[1] > tk else [])
                s = jnp.concatenate(parts, axis=1)
            s_max = jnp.max(s, axis=0, keepdims=True) if tri else smax_sc[slot, :, cols]
            m_old = m_sc[:, cols]
            m_new = jnp.maximum(m_old, s_max)
            alpha = jnp.exp2(m_old - m_new)
            p = jnp.exp2(s - m_new)
            l_sc[:, cols] = alpha * l_sc[:, cols] + jnp.sum(p, axis=0, keepdims=True)
            vt = jnp.concatenate([vt_ref[0, j * nsub + c] for c in range(nsub)], axis=1)
            acc_sc[:, cols] = (alpha * acc_sc[:, cols]
                               + jnp.dot(vt, p.astype(BF16), preferred_element_type=F32))
            m_sc[:, cols] = m_new

        def full_group(i, carry):
            for r in range(group):
                b = group * i + r
                scores(b + 1, (r + 1) % 2)
                update(b, r % 2)
            return carry

        scores(0, 0)
        lax.fori_loop(0, qi, full_group, 0)
        for r in range(group):
            b = group * qi + r
            if r + 1 < group:
                scores(b + 1, (r + 1) % 2, slice((r + 1) * tk, tq))
            update(b, r % 2, slice(r * tk, tq), tri=True)
        o_ref[q0:q0 + tq, :] = (acc_sc[...] / l_sc[...]).T.astype(o_ref.dtype)

    for t in range(qps):
        one_query_block(pl.program_id(1) * qps + t, t * tq)

    @pl.when(step > 0)
    def _():
        for w in range(ncast):
            slab_out(w, step - 1).wait()

    for w in range(ncast):
        slab_in(w).wait()
        stage_out[w][...] = stage_in[w][...].astype(BF16)
        slab_out(w, step).start()

    @pl.when(step == last_step)
    def _():
        for w in range(ncast):
            slab_out(w, step).wait()


def _mla(qt, k, vt, cast_weights, *, tq, tk, qps):
    _, s, _ = k.shape
    grid = (MLA_HEADS, s // (tq * qps))
    nsteps = grid[0] * grid[1]
    ncast = len(cast_weights)
    slabs = [(w.shape[0] // nsteps, w.shape[1]) for w in cast_weights]
    for w, (rows, _) in zip(cast_weights, slabs):
        assert rows * nsteps == w.shape[0] and rows % 16 == 0, (w.shape, nsteps)
    anywhere = pl.BlockSpec(memory_space=pl.ANY)
    outs = pl.pallas_call(
        functools.partial(_mla_kernel, tq=tq, tk=tk, qps=qps, ncast=ncast),
        grid=grid,
        in_specs=[pl.BlockSpec((1, MLA_QK, tq * qps), lambda h, i: (h, 0, i)),
                  pl.BlockSpec((1, s, MLA_QK), lambda h, i: (h, 0, 0)),
                  pl.BlockSpec((1, s // LANES, MLA_V, LANES), lambda h, i: (h, 0, 0, 0))]
                 + [anywhere] * ncast,
        out_specs=[pl.BlockSpec((tq * qps, MLA_V), lambda h, i: (i, h))] + [anywhere] * ncast,
        out_shape=[jax.ShapeDtypeStruct((s, MLA_HEADS * MLA_V), BF16)]
                  + [jax.ShapeDtypeStruct(w.shape, BF16) for w in cast_weights],
        scratch_shapes=[pltpu.VMEM((1, tq), F32), pltpu.VMEM((1, tq), F32),
                        pltpu.VMEM((MLA_V, tq), F32),
                        pltpu.VMEM((2, tk, tq), F32), pltpu.VMEM((2, 1, tq), F32)]
                       + [pltpu.VMEM(sl, F32) for sl in slabs]
                       + [pltpu.VMEM(sl, BF16) for sl in slabs]
                       + [pltpu.SemaphoreType.DMA((ncast,)), pltpu.SemaphoreType.DMA((ncast,))],
        compiler_params=_cparams(("arbitrary", "arbitrary")),
        name="mla",
    )(qt, k, vt, *cast_weights)
    return outs[0], outs[1:]


def _t5_bucket_table_t():
    i = np.arange(BLK)[None, :]
    j = np.arange(2 * BLK)[:, None]
    dist = i + BLK - j
    max_exact = NUM_BUCKETS // 2
    d = np.maximum(dist, 0)
    large = max_exact + (np.log(np.maximum(d, 1) / max_exact)
                         / np.log(MAX_DISTANCE / max_exact)
                         * (NUM_BUCKETS - max_exact)).astype(np.int32)
    large = np.minimum(large, NUM_BUCKETS - 1)
    bucket = np.where(d < max_exact, d, large).astype(np.int32)
    in_window = (dist >= 0) & (dist < WINDOW)
    return np.where(in_window, bucket, -1).astype(np.int32)


def _swa_kernel(relb_ref, sink_ref, bucket_ref, qt_ref, kc_ref, kp_ref, vtc_ref, vtp_ref,
                o_ref, bias_sc, sink_sc, *, nb):
    i = pl.program_id(0)
    gw = SWA_GROUP * BLK

    @pl.when(i == 0)
    def _():
        bucket = bucket_ref[...]
        for h in range(SWA_HEADS):
            c, g = divmod(h, SWA_GROUP)
            b = jnp.full(bucket.shape, NEG, F32)
            for t in range(NUM_BUCKETS):
                b = jnp.where(bucket == t, relb_ref[t, h] * LOG2E, b)
            bias_sc[1, c, :, g * BLK:(g + 1) * BLK] = b
            bias_sc[0, c, BLK:, g * BLK:(g + 1) * BLK] = b[BLK:]
            bias_sc[0, c, :BLK, g * BLK:(g + 1) * BLK] = jnp.full((BLK, BLK), NEG, F32)
            sink_sc[c, :, g * BLK:(g + 1) * BLK] = jnp.full((1, BLK), sink_ref[h] * LOG2E, F32)

    zeros = jnp.zeros((SWA_DIM, gw), BF16)
    for t in range(nb):
        tsl = slice(t * BLK, (t + 1) * BLK)
        k_prev = kp_ref[...] if t == 0 else kc_ref[(t - 1) * BLK:t * BLK, :]
        vt_prev = vtp_ref[...] if t == 0 else vtc_ref[:, (t - 1) * BLK:t * BLK]
        kband = jnp.concatenate([k_prev, kc_ref[tsl, :]], axis=0)
        general = 1 if t > 0 else jnp.where(i == 0, 0, 1)
        for c in range(SWA_KV_HEADS):
            qt = jnp.concatenate(
                [qt_ref[(c * SWA_GROUP + g) * SWA_DIM:(c * SWA_GROUP + g + 1) * SWA_DIM, tsl]
                 for g in range(SWA_GROUP)], axis=1)
            qt_ext = jnp.concatenate([qt, zeros] if c == 0 else [zeros, qt], axis=0)
            s = jnp.dot(kband, qt_ext, preferred_element_type=F32) + bias_sc[general, c]
            sink = sink_sc[c]
            m = jnp.maximum(jnp.max(s, axis=0, keepdims=True), sink)
            p = jnp.exp2(s - m)
            denom = jnp.sum(p, axis=0, keepdims=True) + jnp.exp2(sink - m)
            dsl = slice(c * SWA_DIM, (c + 1) * SWA_DIM)
            vt = jnp.concatenate([vt_prev[dsl, :], vtc_ref[dsl, tsl]], axis=1)
            ot = jnp.dot(vt, p.astype(BF16), preferred_element_type=F32) / denom
            for g2 in range(SWA_GROUP // 2):
                two = jnp.concatenate([ot[:, (2 * g2) * BLK:(2 * g2 + 1) * BLK],
                                       ot[:, (2 * g2 + 1) * BLK:(2 * g2 + 2) * BLK]], axis=0)
                col0 = (c * SWA_GROUP + 2 * g2) * SWA_DIM
                o_ref[tsl, col0:col0 + 2 * SWA_DIM] = two.T.astype(o_ref.dtype)


def _swa(sqt, sk, svt, sinks, rel_bias, *, nb):
    nq, s = sqt.shape
    nkv = sk.shape[1]
    tb = nb * BLK
    bucket = jnp.asarray(_t5_bucket_table_t())
    smem = pl.BlockSpec(memory_space=pltpu.SMEM)
    prev_blk = lambda i: jnp.maximum(i * nb - 1, 0)
    return pl.pallas_call(
        functools.partial(_swa_kernel, nb=nb),
        grid=(s // tb,),
        in_specs=[smem, smem, _resident((2 * BLK, BLK)),
                  pl.BlockSpec((nq, tb), lambda i: (0, i)),
                  pl.BlockSpec((tb, nkv), lambda i: (i, 0)),
                  pl.BlockSpec((BLK, nkv), lambda i: (prev_blk(i), 0)),
                  pl.BlockSpec((nkv, tb), lambda i: (0, i)),
                  pl.BlockSpec((nkv, BLK), lambda i: (0, prev_blk(i)))],
        out_specs=pl.BlockSpec((tb, nq), lambda i: (i, 0)),
        out_shape=jax.ShapeDtypeStruct((s, nq), BF16),
        scratch_shapes=[pltpu.VMEM((2, SWA_KV_HEADS, 2 * BLK, SWA_GROUP * BLK), F32),
                        pltpu.VMEM((SWA_KV_HEADS, 1, SWA_GROUP * BLK), F32)],
        compiler_params=_cparams(("arbitrary",)),
        name="swa",
    )(rel_bias, sinks, bucket, sqt, sk, sk, svt, svt)


def _rms(y, gain):
    r = lax.rsqrt(jnp.sum(y * y, axis=-1, keepdims=True) * (1.0 / y.shape[-1]) + RMS_EPS)
    return y * r * gain


def _oproj_kernel(a_ref, b_ref, wo_ref, x_ref, gpost_ref, gpre_ref, x1_ref, h_ref, *, sub):
    na = a_ref.shape[1]
    for r0 in range(0, x_ref.shape[0], sub):
        rows = slice(r0, r0 + sub)
        mix = (jnp.dot(a_ref[rows, :], wo_ref[:na, :], preferred_element_type=F32)
               + jnp.dot(b_ref[rows, :], wo_ref[na:, :], preferred_element_type=F32))
        x1 = x_ref[rows, :] + _rms(mix, gpost_ref[...])
        x1_ref[rows, :] = x1
        h_ref[rows, :] = _rms(x1, gpre_ref[...]).astype(BF16)


def _oproj(out_a, out_b, w_o, x2, g_post, g_pre, *, tm, sub):
    s, d = x2.shape
    row = lambda w: pl.BlockSpec((tm, w), lambda i: (i, 0))
    return pl.pallas_call(
        functools.partial(_oproj_kernel, sub=sub),
        grid=(s // tm,),
        in_specs=[row(out_a.shape[1]), row(out_b.shape[1]), _resident(w_o.shape),
                  row(d), _resident((1, d)), _resident((1, d))],
        out_specs=[row(d), row(d)],
        out_shape=[jax.ShapeDtypeStruct((s, d), F32), jax.ShapeDtypeStruct((s, d), BF16)],
        compiler_params=_cparams(("arbitrary",)),
        name="oproj",
    )(out_a, out_b, w_o, x2, g_post, g_pre)


def _ffn_kernel(h_ref, wg_ref, wu_ref, wd_ref, x1_hbm, gpost_ref, o_ref, x1_buf, x1_sem):
    i = pl.program_id(0)
    j = pl.program_id(1)
    tm = o_ref.shape[0]

    def x1_copy():
        rows = pl.ds(pl.multiple_of(i * tm, tm), tm)
        return pltpu.make_async_copy(x1_hbm.at[rows, :], x1_buf, x1_sem)

    @pl.when(j == 0)
    def _():
        x1_copy().start()
        o_ref[...] = jnp.zeros(o_ref.shape, F32)

    h = h_ref[...]
    tf = wg_ref.shape[1]
    halves = [slice(0, tf // 2), slice(tf // 2, tf)]
    gu = [(jnp.dot(h, wg_ref[:, c], preferred_element_type=F32),
           jnp.dot(h, wu_ref[:, c], preferred_element_type=F32)) for c in halves]
    for c, (gate, up) in zip(halves, gu):
        act = (gate * jax.nn.sigmoid(gate) * up).astype(BF16)
        o_ref[...] += jnp.dot(act, wd_ref[c, :], preferred_element_type=F32)

    @pl.when(j == pl.num_programs(1) - 1)
    def _():
        x1_copy().wait()
        o_ref[...] = x1_buf[...] + _rms(o_ref[...], gpost_ref[...])


def _ffn(h, wg, wu, wd, x1, g_post, *, tm, tf):
    s, d = x1.shape
    dff = wg.shape[1]
    return pl.pallas_call(
        _ffn_kernel,
        grid=(s // tm, dff // tf),
        in_specs=[pl.BlockSpec((tm, d), lambda i, j: (i, 0)),
                  pl.BlockSpec((d, tf), lambda i, j: (0, j)),
                  pl.BlockSpec((d, tf), lambda i, j: (0, j)),
                  pl.BlockSpec((tf, d), lambda i, j: (j, 0)),
                  pl.BlockSpec(memory_space=pl.ANY),
                  _resident((1, d))],
        out_specs=pl.BlockSpec((tm, d), lambda i, j: (i, 0)),
        out_shape=jax.ShapeDtypeStruct((s, d), F32),
        scratch_shapes=[pltpu.VMEM((tm, d), F32), pltpu.SemaphoreType.DMA(())],
        compiler_params=_cparams(("arbitrary", "arbitrary")),
        name="ffn",
    )(h, wg, wu, wd, x1, g_post)


def _layer(x2, pos_row, p, *, tm_qkv=512):
    half = MLA_ROPE // 2
    inv_freq = ROPE_THETA ** (-jnp.arange(half, dtype=F32) / half)
    invf = jnp.broadcast_to(inv_freq[:, None], (half, tm_qkv))

    row = lambda a: a[None, :]
    cq, ckv, krt, sqt, sk, svt = _proj(x2, row(p["g_mix_pre"]), p["w_in"].T,
                                       row(p["g_cq"]), row(p["g_ckv"]), tm=512, sub=256)
    qt, k, vt = _qkv(cq, ckv, krt, pos_row, invf, p["w_uq"], p["w_ukv"], tm=tm_qkv)
    out_a, (w_o, w_gate, w_up, w_down) = _mla(
        qt, k, vt, [p["w_o"], p["w_gate"], p["w_up"], p["w_down"]], tq=1024, tk=512, qps=4)
    out_b = _swa(sqt, sk, svt, p["sinks"], p["rel_bias"], nb=8)
    x1, h = _oproj(out_a, out_b, w_o, x2,
                   row(p["g_mix_post"]), row(p["g_ffn_pre"]), tm=512, sub=256)
    return _ffn(h, w_gate, w_up, w_down, x1, row(p["g_ffn_post"]), tm=1024, tf=512)


def kernel(x, positions, g_mix_pre, w_in, g_cq, g_ckv, w_uq, w_ukv, sinks, rel_bias,
           w_o, g_mix_post, g_ffn_pre, w_gate, w_up, w_down, g_ffn_post):
    b, s, d = x.shape
    assert b == 1, "the row-major (S, D) pipeline assumes a single sequence"
    x2 = x.reshape(s, d)
    pos_row = positions.reshape(1, s)
    for layer in range(w_in.shape[0]):
        p = dict(g_mix_pre=g_mix_pre[layer], w_in=w_in[layer], g_cq=g_cq[layer],
                 g_ckv=g_ckv[layer], w_uq=w_uq[layer], w_ukv=w_ukv[layer],
                 sinks=sinks[layer], rel_bias=rel_bias, w_o=w_o[layer],
                 g_mix_post=g_mix_post[layer], g_ffn_pre=g_ffn_pre[layer],
                 w_gate=w_gate[layer], w_up=w_up[layer], w_down=w_down[layer],
                 g_ffn_post=g_ffn_post[layer])
        x2 = _layer(x2, pos_row, p)
    return x2.reshape(b, s, d)
```

```python
import functools
import math

import jax
import jax.numpy as jnp
import numpy as np
from jax import lax
from jax.experimental import pallas as pl
from jax.experimental.pallas import tpu as pltpu

F32 = jnp.float32
BF16 = jnp.bfloat16

MLA_HEADS = 8
MLA_NOPE = 128
MLA_ROPE = 64
MLA_QK = MLA_NOPE + MLA_ROPE
MLA_V = 128
Q_RANK = 512
KV_RANK = 512
ROPE_THETA = 10000.0
SWA_HEADS = 16
SWA_KV_HEADS = 2
SWA_GROUP = SWA_HEADS // SWA_KV_HEADS
SWA_DIM = 64
WINDOW = 128
NUM_BUCKETS = 32
MAX_DISTANCE = 128
BLK = 128
RMS_EPS = 1e-6
NEG = -1e30
LOG2E = math.log2(math.e)

LANES = 128
ONES_ROWS = 16
VMEM_LIMIT = 60 * 1024 * 1024


def _cparams(sem):
    return pltpu.CompilerParams(dimension_semantics=sem, vmem_limit_bytes=VMEM_LIMIT)


_NT = (((1,), (1,)), ((), ()))


def _resident(shape):
    nd = len(shape)
    return pl.BlockSpec(shape, lambda *_: (0,) * nd, pipeline_mode=pl.Buffered(1))


def _proj_kernel(x_ref, g_ref, wt32_ref, gcq_ref, gckv_ref,
                 cq_ref, ckv_ref, krt_ref, sqt_ref, sk_ref, svt_ref, wt_ref, *, swa_scale, sub):
    nq = SWA_HEADS * SWA_DIM
    nkv = SWA_KV_HEADS * SWA_DIM
    nl = Q_RANK + KV_RANK

    @pl.when(pl.program_id(0) == 0)
    def _():
        wt_ref[...] = wt32_ref[...].astype(BF16)

    def latent_norm(c, gain_ref):
        rc = lax.rsqrt(jnp.sum(c * c, axis=-1, keepdims=True) * (1.0 / c.shape[-1]) + RMS_EPS)
        return (c * rc * gain_ref[...]).astype(BF16)

    for r0 in range(0, x_ref.shape[0], sub):
        rows = slice(r0, r0 + sub)
        x = x_ref[rows, :]
        r = lax.rsqrt(jnp.sum(x * x, axis=-1, keepdims=True) * (1.0 / x.shape[-1]) + RMS_EPS)
        h = (x * g_ref[...]).astype(BF16)
        yt = lax.dot_general(wt_ref[...], h, _NT, preferred_element_type=F32)
        c = yt[:nl].T * r
        cq_ref[rows, :] = latent_norm(c[:, :Q_RANK], gcq_ref)
        ckv_ref[rows, :] = latent_norm(c[:, Q_RANK:], gckv_ref)
        r_lanes = jnp.broadcast_to(r, (sub, LANES)).T[0:1, :]
        tail = yt[nl:] * r_lanes
        o = MLA_ROPE
        krt_ref[:, rows] = tail[:o]
        sqt_ref[:, rows] = (tail[o:o + nq] * swa_scale).astype(BF16)
        sk_ref[rows, :] = tail[o + nq:o + nq + nkv].T.astype(BF16)
        svt_ref[:, rows] = tail[o + nq + nkv:o + nq + 2 * nkv].astype(BF16)


def _proj(x2, g, w_in_t, g_cq, g_ckv, *, tm, sub):
    s, d = x2.shape
    nq = SWA_HEADS * SWA_DIM
    nkv = SWA_KV_HEADS * SWA_DIM
    row = lambda w: pl.BlockSpec((tm, w), lambda i: (i, 0))
    col = lambda r: pl.BlockSpec((r, tm), lambda i: (0, i))
    return pl.pallas_call(
        functools.partial(_proj_kernel, swa_scale=LOG2E / math.sqrt(SWA_DIM), sub=sub),
        grid=(s // tm,),
        in_specs=[row(d), _resident((1, d)), _resident(w_in_t.shape),
                  _resident((1, Q_RANK)), _resident((1, KV_RANK))],
        out_specs=[row(Q_RANK), row(KV_RANK), col(MLA_ROPE), col(nq), row(nkv), col(nkv)],
        out_shape=[jax.ShapeDtypeStruct((s, Q_RANK), BF16),
                   jax.ShapeDtypeStruct((s, KV_RANK), BF16),
                   jax.ShapeDtypeStruct((MLA_ROPE, s), F32),
                   jax.ShapeDtypeStruct((nq, s), BF16),
                   jax.ShapeDtypeStruct((s, nkv), BF16),
                   jax.ShapeDtypeStruct((nkv, s), BF16)],
        scratch_shapes=[pltpu.VMEM(w_in_t.shape, BF16)],
        compiler_params=_cparams(("arbitrary",)),
        name="proj",
    )(x2, g, w_in_t, g_cq, g_ckv)


def _qkv_kernel(cq_ref, ckv_ref, krt_ref, pos_ref, invf_ref, wuq_ref, wukv_ref,
                qt_ref, k_ref, vt_ref, wuqt_ref, wuk_ref, wuvt_ref, *, q_scale):
    tm = cq_ref.shape[0]
    half = MLA_ROPE // 2

    @pl.when(pl.program_id(0) == 0)
    def _():
        wuqt_ref[...] = wuq_ref[...].T.astype(BF16)
        for h in range(MLA_HEADS):
            c0 = h * (MLA_NOPE + MLA_V)
            wuk_ref[:, h * MLA_NOPE:(h + 1) * MLA_NOPE] = wukv_ref[:, c0:c0 + MLA_NOPE].astype(BF16)
            wuvt_ref[h * MLA_V:(h + 1) * MLA_V, :] = (
                wukv_ref[:, c0 + MLA_NOPE:c0 + MLA_NOPE + MLA_V].T.astype(BF16))
    ang = invf_ref[...] * pos_ref[...].astype(F32)
    cos = jnp.cos(ang)
    sin = jnp.sin(ang)

    def rope_t(t):
        t1, t2 = t[:half], t[half:]
        return jnp.concatenate([t1 * cos - t2 * sin, t2 * cos + t1 * sin], axis=0)

    cq = cq_ref[...]
    ckv = ckv_ref[...]
    qt = lax.dot_general(wuqt_ref[...], cq, _NT, preferred_element_type=F32) * q_scale
    for h in range(MLA_HEADS):
        base = h * MLA_QK
        qt_ref[h, 0:MLA_NOPE, :] = qt[base:base + MLA_NOPE].astype(BF16)
        qt_ref[h, MLA_NOPE:MLA_QK, :] = rope_t(qt[base + MLA_NOPE:base + MLA_QK]).astype(BF16)

    krt = jnp.concatenate([rope_t(krt_ref[...]), jnp.zeros((LANES - MLA_ROPE, tm), F32)], axis=0)
    kr = krt.T[:, :MLA_ROPE].astype(BF16)
    kn = jnp.dot(ckv, wuk_ref[...], preferred_element_type=F32)
    vt = lax.dot_general(wuvt_ref[...], ckv, _NT, preferred_element_type=F32)
    for h in range(MLA_HEADS):
        k_ref[h, :, 0:MLA_NOPE] = kn[:, h * MLA_NOPE:(h + 1) * MLA_NOPE].astype(BF16)
        k_ref[h, :, MLA_NOPE:MLA_QK] = kr
        for c in range(tm // LANES):
            vt_ref[h, c] = vt[h * MLA_V:(h + 1) * MLA_V, c * LANES:(c + 1) * LANES].astype(BF16)


def _qkv(cq, ckv, krt, pos_row, invf, w_uq, w_ukv, *, tm):
    s = cq.shape[0]
    rank = w_uq.shape[0]
    row = lambda w: pl.BlockSpec((tm, w), lambda i: (i, 0))
    col = lambda r: pl.BlockSpec((r, tm), lambda i: (0, i))
    nb = tm // LANES
    return pl.pallas_call(
        functools.partial(_qkv_kernel, q_scale=LOG2E / math.sqrt(MLA_QK)),
        grid=(s // tm,),
        in_specs=[row(Q_RANK), row(KV_RANK), col(MLA_ROPE), col(1), _resident(invf.shape),
                  _resident(w_uq.shape), _resident(w_ukv.shape)],
        out_specs=[pl.BlockSpec((MLA_HEADS, MLA_QK, tm), lambda i: (0, 0, i)),
                   pl.BlockSpec((MLA_HEADS, tm, MLA_QK), lambda i: (0, i, 0)),
                   pl.BlockSpec((MLA_HEADS, nb, MLA_V, LANES), lambda i: (0, i, 0, 0))],
        out_shape=[jax.ShapeDtypeStruct((MLA_HEADS, MLA_QK, s), BF16),
                   jax.ShapeDtypeStruct((MLA_HEADS, s, MLA_QK), BF16),
                   jax.ShapeDtypeStruct((MLA_HEADS, s // LANES, MLA_V, LANES), BF16)],
        scratch_shapes=[pltpu.VMEM((MLA_HEADS * MLA_QK, rank), BF16),
                        pltpu.VMEM((rank, MLA_HEADS * MLA_NOPE), BF16),
                        pltpu.VMEM((MLA_HEADS * MLA_V, rank), BF16)],
        compiler_params=_cparams(("arbitrary",)),
        name="qkv",
    )(cq, ckv, krt, pos_row, invf, w_uq, w_ukv)


def _mla_kernel(*refs, tq, tk, qps, ncast):
    qt_ref, k_ref, vt_ref = refs[:3]
    w_hbm = refs[3:3 + ncast]
    o_ref = refs[3 + ncast]
    wout_hbm = refs[4 + ncast:4 + 2 * ncast]
    m_sc, acc_sc, s_sc, smax_sc = refs[4 + 2 * ncast:8 + 2 * ncast]
    stage_in = refs[8 + 2 * ncast:8 + 3 * ncast]
    stage_out = refs[8 + 3 * ncast:8 + 4 * ncast]
    in_sems, out_sems = refs[8 + 4 * ncast:]

    step = pl.program_id(0) * pl.num_programs(1) + pl.program_id(1)
    last_step = pl.num_programs(0) * pl.num_programs(1) - 1

    def slab_in(w):
        rows = stage_in[w].shape[0]
        src = w_hbm[w].at[pl.ds(pl.multiple_of(step * rows, rows), rows), :]
        return pltpu.make_async_copy(src, stage_in[w], in_sems.at[w])

    def slab_out(w, at_step):
        rows = stage_out[w].shape[0]
        dst = wout_hbm[w].at[pl.ds(pl.multiple_of(at_step * rows, rows), rows), :]
        return pltpu.make_async_copy(stage_out[w], dst, out_sems.at[w])

    for w in range(ncast):
        slab_in(w).start()

    group = tq // tk
    assert group * tk == tq and group % 2 == 0
    nsub = tk // LANES
    full = slice(0, tq)

    def one_query_block(qi, q0):
        m_sc[...] = jnp.full(m_sc.shape, NEG, F32)
        acc_sc[...] = jnp.zeros(acc_sc.shape, F32)

        def scores(j, slot, cols=full):
            start = pl.multiple_of(j * tk, tk)
            qcols = slice(q0 + cols.start, q0 + cols.stop)
            s = jnp.dot(k_ref[0, pl.ds(start, tk), :], qt_ref[0, :, qcols],
                        preferred_element_type=F32)
            s_sc[slot, :, cols] = s
            smax_sc[slot, :, cols] = jnp.max(s, axis=0, keepdims=True)

        def update(j, slot, cols=full, tri=False):
            s = s_sc[slot, :, cols]
            if tri:
                mask = (lax.broadcasted_iota(jnp.int32, (tk, tk), 0)
                        <= lax.broadcasted_iota(jnp.int32, (tk, tk), 1))
                parts = [jnp.where(mask, s[:, :tk], NEG)] + ([s[:, tk:]] if s.shape[1] > tk else [])
                s = jnp.concatenate(parts, axis=1)
            s_max = jnp.max(s, axis=0, keepdims=True) if tri else smax_sc[slot, :, cols]
            m_old = m_sc[:, cols]
            m_new = jnp.maximum(m_old, s_max)
            alpha = jnp.exp2(m_old - m_new)
            p = jnp.exp2(s - m_new)
            vt = jnp.concatenate([vt_ref[0, j * nsub + c] for c in range(nsub)], axis=1)
            vt = jnp.concatenate([vt, jnp.ones((ONES_ROWS, tk), BF16)], axis=0)
            acc_sc[:, cols] = (alpha * acc_sc[:, cols]
                               + jnp.dot(vt, p.astype(BF16), preferred_element_type=F32))
            m_sc[:, cols] = m_new

        def full_group(i, carry):
            for r in range(group):
                b = group * i + r
                scores(b + 1, (r + 1) % 2)
                update(b, r % 2)
            return carry

        scores(0, 0)
        lax.fori_loop(0, qi, full_group, 0)
        for r in range(group):
            b = group * qi + r
            if r + 1 < group:
                scores(b + 1, (r + 1) % 2, slice((r + 1) * tk, tq))
            update(b, r % 2, slice(r * tk, tq), tri=True)
        out_t = acc_sc[:MLA_V, :] / acc_sc[MLA_V:MLA_V + 1, :]
        o_ref[q0:q0 + tq, :] = out_t.T.astype(o_ref.dtype)

    for t in range(qps):
        one_query_block(pl.program_id(1) * qps + t, t * tq)

    @pl.when(step > 0)
    def _():
        for w in range(ncast):
            slab_out(w, step - 1).wait()

    for w in range(ncast):
        slab_in(w).wait()
        stage_out[w][...] = stage_in[w][...].astype(BF16)
        slab_out(w, step).start()

    @pl.when(step == last_step)
    def _():
        for w in range(ncast):
            slab_out(w, step).wait()


def _mla(qt, k, vt, cast_weights, *, tq, tk, qps):
    _, s, _ = k.shape
    grid = (MLA_HEADS, s // (tq * qps))
    nsteps = grid[0] * grid[1]
    ncast = len(cast_weights)
    slabs = [(w.shape[0] // nsteps, w.shape[1]) for w in cast_weights]
    for w, (rows, _) in zip(cast_weights, slabs):
        assert rows * nsteps == w.shape[0] and rows % 16 == 0, (w.shape, nsteps)
    anywhere = pl.BlockSpec(memory_space=pl.ANY)
    outs = pl.pallas_call(
        functools.partial(_mla_kernel, tq=tq, tk=tk, qps=qps, ncast=ncast),
        grid=grid,
        in_specs=[pl.BlockSpec((1, MLA_QK, tq * qps), lambda h, i: (h, 0, i)),
                  pl.BlockSpec((1, s, MLA_QK), lambda h, i: (h, 0, 0)),
                  pl.BlockSpec((1, s // LANES, MLA_V, LANES), lambda h, i: (h, 0, 0, 0))]
                 + [anywhere] * ncast,
        out_specs=[pl.BlockSpec((tq * qps, MLA_V), lambda h, i: (i, h))] + [anywhere] * ncast,
        out_shape=[jax.ShapeDtypeStruct((s, MLA_HEADS * MLA_V), BF16)]
                  + [jax.ShapeDtypeStruct(w.shape, BF16) for w in cast_weights],
        scratch_shapes=[pltpu.VMEM((1, tq), F32),
                        pltpu.VMEM((MLA_V + ONES_ROWS, tq), F32),
                        pltpu.VMEM((2, tk, tq), F32), pltpu.VMEM((2, 1, tq), F32)]
                       + [pltpu.VMEM(sl, F32) for sl in slabs]
                       + [pltpu.VMEM(sl, BF16) for sl in slabs]
                       + [pltpu.SemaphoreType.DMA((ncast,)), pltpu.SemaphoreType.DMA((ncast,))],
        compiler_params=_cparams(("arbitrary", "arbitrary")),
        name="mla",
    )(qt, k, vt, *cast_weights)
    return outs[0], outs[1:]


def _t5_bucket_table_t():
    i = np.arange(BLK)[None, :]
    j = np.arange(2 * BLK)[:, None]
    dist = i + BLK - j
    max_exact = NUM_BUCKETS // 2
    d = np.maximum(dist, 0)
    large = max_exact + (np.log(np.maximum(d, 1) / max_exact)
                         / np.log(MAX_DISTANCE / max_exact)
                         * (NUM_BUCKETS - max_exact)).astype(np.int32)
    large = np.minimum(large, NUM_BUCKETS - 1)
    bucket = np.where(d < max_exact, d, large).astype(np.int32)
    in_window = (dist >= 0) & (dist < WINDOW)
    return np.where(in_window, bucket, -1).astype(np.int32)


def _swa_kernel(relb_ref, sink_ref, bucket_ref, qt_ref, kc_ref, kp_ref, vtc_ref, vtp_ref,
                o_ref, bias_sc, sink_sc, *, nb):
    i = pl.program_id(0)
    gw = SWA_GROUP * BLK

    @pl.when(i == 0)
    def _():
        bucket = bucket_ref[...]
        for h in range(SWA_HEADS):
            c, g = divmod(h, SWA_GROUP)
            b = jnp.full(bucket.shape, NEG, F32)
            for t in range(NUM_BUCKETS):
                b = jnp.where(bucket == t, relb_ref[t, h] * LOG2E, b)
            bias_sc[1, c, :, g * BLK:(g + 1) * BLK] = b
            bias_sc[0, c, BLK:, g * BLK:(g + 1) * BLK] = b[BLK:]
            bias_sc[0, c, :BLK, g * BLK:(g + 1) * BLK] = jnp.full((BLK, BLK), NEG, F32)
            sink_sc[c, :, g * BLK:(g + 1) * BLK] = jnp.full((1, BLK), sink_ref[h] * LOG2E, F32)

    zeros = jnp.zeros((SWA_DIM, gw), BF16)
    for t in range(nb):
        tsl = slice(t * BLK, (t + 1) * BLK)
        k_prev = kp_ref[...] if t == 0 else kc_ref[(t - 1) * BLK:t * BLK, :]
        vt_prev = vtp_ref[...] if t == 0 else vtc_ref[:, (t - 1) * BLK:t * BLK]
        kband = jnp.concatenate([k_prev, kc_ref[tsl, :]], axis=0)
        general = 1 if t > 0 else jnp.where(i == 0, 0, 1)
        for c in range(SWA_KV_HEADS):
            qt = jnp.concatenate(
                [qt_ref[(c * SWA_GROUP + g) * SWA_DIM:(c * SWA_GROUP + g + 1) * SWA_DIM, tsl]
                 for g in range(SWA_GROUP)], axis=1)
            qt_ext = jnp.concatenate([qt, zeros] if c == 0 else [zeros, qt], axis=0)
            s = jnp.dot(kband, qt_ext, preferred_element_type=F32) + bias_sc[general, c]
            sink = sink_sc[c]
            m = jnp.maximum(jnp.max(s, axis=0, keepdims=True), sink)
            p = jnp.exp2(s - m)
            denom = jnp.sum(p, axis=0, keepdims=True) + jnp.exp2(sink - m)
            dsl = slice(c * SWA_DIM, (c + 1) * SWA_DIM)
            vt = jnp.concatenate([vt_prev[dsl, :], vtc_ref[dsl, tsl]], axis=1)
            ot = jnp.dot(vt, p.astype(BF16), preferred_element_type=F32) / denom
            for g2 in range(SWA_GROUP // 2):
                two = jnp.concatenate([ot[:, (2 * g2) * BLK:(2 * g2 + 1) * BLK],
                                       ot[:, (2 * g2 + 1) * BLK:(2 * g2 + 2) * BLK]], axis=0)
                col0 = (c * SWA_GROUP + 2 * g2) * SWA_DIM
                o_ref[tsl, col0:col0 + 2 * SWA_DIM] = two.T.astype(o_ref.dtype)


def _swa(sqt, sk, svt, sinks, rel_bias, *, nb):
    nq, s = sqt.shape
    nkv = sk.shape[1]
    tb = nb * BLK
    bucket = jnp.asarray(_t5_bucket_table_t())
    smem = pl.BlockSpec(memory_space=pltpu.SMEM)
    prev_blk = lambda i: jnp.maximum(i * nb - 1, 0)
    return pl.pallas_call(
        functools.partial(_swa_kernel, nb=nb),
        grid=(s // tb,),
        in_specs=[smem, smem, _resident((2 * BLK, BLK)),
                  pl.BlockSpec((nq, tb), lambda i: (0, i)),
                  pl.BlockSpec((tb, nkv), lambda i: (i, 0)),
                  pl.BlockSpec((BLK, nkv), lambda i: (prev_blk(i), 0)),
                  pl.BlockSpec((nkv, tb), lambda i: (0, i)),
                  pl.BlockSpec((nkv, BLK), lambda i: (0, prev_blk(i)))],
        out_specs=pl.BlockSpec((tb, nq), lambda i: (i, 0)),
        out_shape=jax.ShapeDtypeStruct((s, nq), BF16),
        scratch_shapes=[pltpu.VMEM((2, SWA_KV_HEADS, 2 * BLK, SWA_GROUP * BLK), F32),
                        pltpu.VMEM((SWA_KV_HEADS, 1, SWA_GROUP * BLK), F32)],
        compiler_params=_cparams(("arbitrary",)),
        name="swa",
    )(rel_bias, sinks, bucket, sqt, sk, sk, svt, svt)


def _rms(y, gain):
    r = lax.rsqrt(jnp.sum(y * y, axis=-1, keepdims=True) * (1.0 / y.shape[-1]) + RMS_EPS)
    return y * r * gain


def _oproj_kernel(a_ref, b_ref, wo_ref, x_ref, gpost_ref, gpre_ref, x1_ref, h_ref, *, sub):
    na = a_ref.shape[1]
    for r0 in range(0, x_ref.shape[0], sub):
        rows = slice(r0, r0 + sub)
        mix = (jnp.dot(a_ref[rows, :], wo_ref[:na, :], preferred_element_type=F32)
               + jnp.dot(b_ref[rows, :], wo_ref[na:, :], preferred_element_type=F32))
        x1 = x_ref[rows, :] + _rms(mix, gpost_ref[...])
        x1_ref[rows, :] = x1
        h_ref[rows, :] = _rms(x1, gpre_ref[...]).astype(BF16)


def _oproj(out_a, out_b, w_o, x2, g_post, g_pre, *, tm, sub):
    s, d = x2.shape
    row = lambda w: pl.BlockSpec((tm, w), lambda i: (i, 0))
    return pl.pallas_call(
        functools.partial(_oproj_kernel, sub=sub),
        grid=(s // tm,),
        in_specs=[row(out_a.shape[1]), row(out_b.shape[1]), _resident(w_o.shape),
                  row(d), _resident((1, d)), _resident((1, d))],
        out_specs=[row(d), row(d)],
        out_shape=[jax.ShapeDtypeStruct((s, d), F32), jax.ShapeDtypeStruct((s, d), BF16)],
        compiler_params=_cparams(("arbitrary",)),
        name="oproj",
    )(out_a, out_b, w_o, x2, g_post, g_pre)


def _ffn_kernel(h_ref, wg_ref, wu_ref, wd_ref, x1_hbm, gpost_ref, o_ref, x1_buf, x1_sem):
    i = pl.program_id(0)
    j = pl.program_id(1)
    tm = o_ref.shape[0]

    def x1_copy():
        rows = pl.ds(pl.multiple_of(i * tm, tm), tm)
        return pltpu.make_async_copy(x1_hbm.at[rows, :], x1_buf, x1_sem)

    @pl.when(j == 0)
    def _():
        x1_copy().start()
        o_ref[...] = jnp.zeros(o_ref.shape, F32)

    h = h_ref[...]
    tf = wg_ref.shape[1]
    halves = [slice(0, tf // 2), slice(tf // 2, tf)]
    gu = [(jnp.dot(h, wg_ref[:, c], preferred_element_type=F32),
           jnp.dot(h, wu_ref[:, c], preferred_element_type=F32)) for c in halves]
    for c, (gate, up) in zip(halves, gu):
        act = (gate * jax.nn.sigmoid(gate) * up).astype(BF16)
        o_ref[...] += jnp.dot(act, wd_ref[c, :], preferred_element_type=F32)

    @pl.when(j == pl.num_programs(1) - 1)
    def _():
        x1_copy().wait()
        o_ref[...] = x1_buf[...] + _rms(o_ref[...], gpost_ref[...])


def _ffn(h, wg, wu, wd, x1, g_post, *, tm, tf):
    s, d = x1.shape
    dff = wg.shape[1]
    return pl.pallas_call(
        _ffn_kernel,
        grid=(s // tm, dff // tf),
        in_specs=[pl.BlockSpec((tm, d), lambda i, j: (i, 0)),
                  pl.BlockSpec((d, tf), lambda i, j: (0, j)),
                  pl.BlockSpec((d, tf), lambda i, j: (0, j)),
                  pl.BlockSpec((tf, d), lambda i, j: (j, 0)),
                  pl.BlockSpec(memory_space=pl.ANY),
                  _resident((1, d))],
        out_specs=pl.BlockSpec((tm, d), lambda i, j: (i, 0)),
        out_shape=jax.ShapeDtypeStruct((s, d), F32),
        scratch_shapes=[pltpu.VMEM((tm, d), F32), pltpu.SemaphoreType.DMA(())],
        compiler_params=_cparams(("arbitrary", "arbitrary")),
        name="ffn",
    )(h, wg, wu, wd, x1, g_post)


def _layer(x2, pos_row, p, *, tm_qkv=512):
    half = MLA_ROPE // 2
    inv_freq = ROPE_THETA ** (-jnp.arange(half, dtype=F32) / half)
    invf = jnp.broadcast_to(inv_freq[:, None], (half, tm_qkv))

    row = lambda a: a[None, :]
    cq, ckv, krt, sqt, sk, svt = _proj(x2, row(p["g_mix_pre"]), p["w_in"].T,
                                       row(p["g_cq"]), row(p["g_ckv"]), tm=512, sub=256)
    qt, k, vt = _qkv(cq, ckv, krt, pos_row, invf, p["w_uq"], p["w_ukv"], tm=tm_qkv)
    out_a, (w_o, w_gate, w_up, w_down) = _mla(
        qt, k, vt, [p["w_o"], p["w_gate"], p["w_up"], p["w_down"]], tq=1024, tk=512, qps=4)
    out_b = _swa(sqt, sk, svt, p["sinks"], p["rel_bias"], nb=8)
    x1, h = _oproj(out_a, out_b, w_o, x2,
                   row(p["g_mix_post"]), row(p["g_ffn_pre"]), tm=512, sub=256)
    return _ffn(h, w_gate, w_up, w_down, x1, row(p["g_ffn_post"]), tm=1024, tf=512)


def kernel(x, positions, g_mix_pre, w_in, g_cq, g_ckv, w_uq, w_ukv, sinks, rel_bias,
           w_o, g_mix_post, g_ffn_pre, w_gate, w_up, w_down, g_ffn_post):
    b, s, d = x.shape
    assert b == 1, "the row-major (S, D) pipeline assumes a single sequence"
    x2 = x.reshape(s, d)
    pos_row = positions.reshape(1, s)
    for layer in range(w_in.shape[0]):
        p = dict(g_mix_pre=g_mix_pre[layer], w_in=w_in[layer], g_cq=g_cq[layer],
                 g_ckv=g_ckv[layer], w_uq=w_uq[layer], w_ukv=w_ukv[layer],
                 sinks=sinks[layer], rel_bias=rel_bias, w_o=w_o[layer],
                 g_mix_post=g_mix_post[layer], g_ffn_pre=g_ffn_pre[layer],
                 w_gate=w_gate[layer], w_up=w_up[layer], w_down=w_down[layer],
                 g_ffn_post=g_ffn_post[layer])
        x2 = _layer(x2, pos_row, p)
    return x2.reshape(b, s, d)
```

```python
import functools
import math

import jax
import jax.numpy as jnp
import numpy as np
from jax import lax
from jax.experimental import pallas as pl
from jax.experimental.pallas import tpu as pltpu

F32 = jnp.float32
BF16 = jnp.bfloat16

MLA_HEADS = 8
MLA_NOPE = 128
MLA_ROPE = 64
MLA_QK = MLA_NOPE + MLA_ROPE
MLA_V = 128
Q_RANK = 512
KV_RANK = 512
ROPE_THETA = 10000.0
SWA_HEADS = 16
SWA_KV_HEADS = 2
SWA_GROUP = SWA_HEADS // SWA_KV_HEADS
SWA_DIM = 64
WINDOW = 128
NUM_BUCKETS = 32
MAX_DISTANCE = 128
BLK = 128
RMS_EPS = 1e-6
NEG = -1e30
LOG2E = math.log2(math.e)

LANES = 128
ONES_ROWS = 16
VMEM_LIMIT = 60 * 1024 * 1024


def _cparams(sem):
    return pltpu.CompilerParams(dimension_semantics=sem, vmem_limit_bytes=VMEM_LIMIT)


_NT = (((1,), (1,)), ((), ()))


def _resident(shape):
    nd = len(shape)
    return pl.BlockSpec(shape, lambda *_: (0,) * nd, pipeline_mode=pl.Buffered(1))


def _proj_kernel(x_ref, g_ref, wt32_ref, gcq_ref, gckv_ref,
                 cq_ref, ckv_ref, krt_ref, sqt_ref, sk_ref, svt_ref, wt_ref, *, swa_scale, sub):
    nq = SWA_HEADS * SWA_DIM
    nkv = SWA_KV_HEADS * SWA_DIM
    nl = Q_RANK + KV_RANK

    @pl.when(pl.program_id(0) == 0)
    def _():
        wt_ref[...] = wt32_ref[...].astype(BF16)

    def latent_norm(c, gain_ref):
        rc = lax.rsqrt(jnp.sum(c * c, axis=-1, keepdims=True) * (1.0 / c.shape[-1]) + RMS_EPS)
        return (c * rc * gain_ref[...]).astype(BF16)

    for r0 in range(0, x_ref.shape[0], sub):
        rows = slice(r0, r0 + sub)
        x = x_ref[rows, :]
        r = lax.rsqrt(jnp.sum(x * x, axis=-1, keepdims=True) * (1.0 / x.shape[-1]) + RMS_EPS)
        h = (x * g_ref[...]).astype(BF16)
        yt = lax.dot_general(wt_ref[...], h, _NT, preferred_element_type=F32)
        c = yt[:nl].T * r
        cq_ref[rows, :] = latent_norm(c[:, :Q_RANK], gcq_ref)
        ckv_ref[rows, :] = latent_norm(c[:, Q_RANK:], gckv_ref)
        r_lanes = jnp.broadcast_to(r, (sub, LANES)).T[0:1, :]
        tail = yt[nl:] * r_lanes
        o = MLA_ROPE
        krt_ref[:, rows] = tail[:o]
        sqt_ref[:, rows] = (tail[o:o + nq] * swa_scale).astype(BF16)
        sk_ref[rows, :] = tail[o + nq:o + nq + nkv].T.astype(BF16)
        svt_ref[:, rows] = tail[o + nq + nkv:o + nq + 2 * nkv].astype(BF16)


def _proj(x2, g, w_in_t, g_cq, g_ckv, *, tm, sub):
    s, d = x2.shape
    nq = SWA_HEADS * SWA_DIM
    nkv = SWA_KV_HEADS * SWA_DIM
    row = lambda w: pl.BlockSpec((tm, w), lambda i: (i, 0))
    col = lambda r: pl.BlockSpec((r, tm), lambda i: (0, i))
    return pl.pallas_call(
        functools.partial(_proj_kernel, swa_scale=LOG2E / math.sqrt(SWA_DIM), sub=sub),
        grid=(s // tm,),
        in_specs=[row(d), _resident((1, d)), _resident(w_in_t.shape),
                  _resident((1, Q_RANK)), _resident((1, KV_RANK))],
        out_specs=[row(Q_RANK), row(KV_RANK), col(MLA_ROPE), col(nq), row(nkv), col(nkv)],
        out_shape=[jax.ShapeDtypeStruct((s, Q_RANK), BF16),
                   jax.ShapeDtypeStruct((s, KV_RANK), BF16),
                   jax.ShapeDtypeStruct((MLA_ROPE, s), F32),
                   jax.ShapeDtypeStruct((nq, s), BF16),
                   jax.ShapeDtypeStruct((s, nkv), BF16),
                   jax.ShapeDtypeStruct((nkv, s), BF16)],
        scratch_shapes=[pltpu.VMEM(w_in_t.shape, BF16)],
        compiler_params=_cparams(("arbitrary",)),
        name="proj",
    )(x2, g, w_in_t, g_cq, g_ckv)


def _qkv_kernel(cq_ref, ckv_ref, krt_ref, pos_ref, invf_ref, wuq_ref, wukv_ref,
                qt_ref, k_ref, vt_ref, wuqt_ref, wuk_ref, wuvt_ref, *, q_scale):
    tm = cq_ref.shape[0]
    half = MLA_ROPE // 2

    @pl.when(pl.program_id(0) == 0)
    def _():
        wuqt_ref[...] = wuq_ref[...].T.astype(BF16)
        for h in range(MLA_HEADS):
            c0 = h * (MLA_NOPE + MLA_V)
            wuk_ref[:, h * MLA_NOPE:(h + 1) * MLA_NOPE] = wukv_ref[:, c0:c0 + MLA_NOPE].astype(BF16)
            wuvt_ref[h * MLA_V:(h + 1) * MLA_V, :] = (
                wukv_ref[:, c0 + MLA_NOPE:c0 + MLA_NOPE + MLA_V].T.astype(BF16))
    ang = invf_ref[...] * pos_ref[...].astype(F32)
    cos = jnp.cos(ang)
    sin = jnp.sin(ang)

    def rope_t(t):
        t1, t2 = t[:half], t[half:]
        return jnp.concatenate([t1 * cos - t2 * sin, t2 * cos + t1 * sin], axis=0)

    cq = cq_ref[...]
    ckv = ckv_ref[...]
    qt = lax.dot_general(wuqt_ref[...], cq, _NT, preferred_element_type=F32) * q_scale
    for h in range(MLA_HEADS):
        base = h * MLA_QK
        qt_ref[h, 0:MLA_NOPE, :] = qt[base:base + MLA_NOPE].astype(BF16)
        qt_ref[h, MLA_NOPE:MLA_QK, :] = rope_t(qt[base + MLA_NOPE:base + MLA_QK]).astype(BF16)

    krt = jnp.concatenate([rope_t(krt_ref[...]), jnp.zeros((LANES - MLA_ROPE, tm), F32)], axis=0)
    kr = krt.T[:, :MLA_ROPE].astype(BF16)
    kn = jnp.dot(ckv, wuk_ref[...], preferred_element_type=F32)
    vt = lax.dot_general(wuvt_ref[...], ckv, _NT, preferred_element_type=F32)
    for h in range(MLA_HEADS):
        k_ref[h, :, 0:MLA_NOPE] = kn[:, h * MLA_NOPE:(h + 1) * MLA_NOPE].astype(BF16)
        k_ref[h, :, MLA_NOPE:MLA_QK] = kr
        for c in range(tm // LANES):
            vt_ref[h, c] = vt[h * MLA_V:(h + 1) * MLA_V, c * LANES:(c + 1) * LANES].astype(BF16)


def _qkv(cq, ckv, krt, pos_row, invf, w_uq, w_ukv, *, tm):
    s = cq.shape[0]
    rank = w_uq.shape[0]
    row = lambda w: pl.BlockSpec((tm, w), lambda i: (i, 0))
    col = lambda r: pl.BlockSpec((r, tm), lambda i: (0, i))
    nb = tm // LANES
    return pl.pallas_call(
        functools.partial(_qkv_kernel, q_scale=LOG2E / math.sqrt(MLA_QK)),
        grid=(s // tm,),
        in_specs=[row(Q_RANK), row(KV_RANK), col(MLA_ROPE), col(1), _resident(invf.shape),
                  _resident(w_uq.shape), _resident(w_ukv.shape)],
        out_specs=[pl.BlockSpec((MLA_HEADS, MLA_QK, tm), lambda i: (0, 0, i)),
                   pl.BlockSpec((MLA_HEADS, tm, MLA_QK), lambda i: (0, i, 0)),
                   pl.BlockSpec((MLA_HEADS, nb, MLA_V, LANES), lambda i: (0, i, 0, 0))],
        out_shape=[jax.ShapeDtypeStruct((MLA_HEADS, MLA_QK, s), BF16),
                   jax.ShapeDtypeStruct((MLA_HEADS, s, MLA_QK), BF16),
                   jax.ShapeDtypeStruct((MLA_HEADS, s // LANES, MLA_V, LANES), BF16)],
        scratch_shapes=[pltpu.VMEM((MLA_HEADS * MLA_QK, rank), BF16),
                        pltpu.VMEM((rank, MLA_HEADS * MLA_NOPE), BF16),
                        pltpu.VMEM((MLA_HEADS * MLA_V, rank), BF16)],
        compiler_params=_cparams(("arbitrary",)),
        name="qkv",
    )(cq, ckv, krt, pos_row, invf, w_uq, w_ukv)


def _mla_kernel(*refs, tq, tk, qps, ncast):
    qt_ref, k_ref, vt_ref = refs[:3]
    w_hbm = refs[3:3 + ncast]
    o_ref = refs[3 + ncast]
    wout_hbm = refs[4 + ncast:4 + 2 * ncast]
    m_sc, acc_sc, s_sc, smax_sc = refs[4 + 2 * ncast:8 + 2 * ncast]
    stage_in = refs[8 + 2 * ncast:8 + 3 * ncast]
    stage_out = refs[8 + 3 * ncast:8 + 4 * ncast]
    in_sems, out_sems = refs[8 + 4 * ncast:]

    step = pl.program_id(0) * pl.num_programs(1) + pl.program_id(1)
    last_step = pl.num_programs(0) * pl.num_programs(1) - 1

    def slab_in(w):
        rows = stage_in[w].shape[0]
        src = w_hbm[w].at[pl.ds(pl.multiple_of(step * rows, rows), rows), :]
        return pltpu.make_async_copy(src, stage_in[w], in_sems.at[w])

    def slab_out(w, at_step):
        rows = stage_out[w].shape[0]
        dst = wout_hbm[w].at[pl.ds(pl.multiple_of(at_step * rows, rows), rows), :]
        return pltpu.make_async_copy(stage_out[w], dst, out_sems.at[w])

    for w in range(ncast):
        slab_in(w).start()

    group = tq // tk
    assert group * tk == tq and group % 2 == 0
    nsub = tk // LANES
    full = slice(0, tq)

    def one_query_block(qi, q0):
        m_sc[...] = jnp.full(m_sc.shape, NEG, F32)
        acc_sc[...] = jnp.zeros(acc_sc.shape, F32)

        def scores(j, slot, cols=full):
            start = pl.multiple_of(j * tk, tk)
            qcols = slice(q0 + cols.start, q0 + cols.stop)
            s = jnp.dot(k_ref[0, pl.ds(start, tk), :], qt_ref[0, :, qcols],
                        preferred_element_type=F32)
            s_sc[slot, :, cols] = s
            smax_sc[slot, :, cols] = jnp.max(s, axis=0, keepdims=True)

        def update(j, slot, cols=full, tri=False):
            s = s_sc[slot, :, cols]
            if tri:
                mask = (lax.broadcasted_iota(jnp.int32, (tk, tk), 0)
                        <= lax.broadcasted_iota(jnp.int32, (tk, tk), 1))
                parts = [jnp.where(mask, s[:, :tk], NEG)] + ([s[:, tk:]] if s.shape[1] > tk else [])
                s = jnp.concatenate(parts, axis=1)
            s_max = jnp.max(s, axis=0, keepdims=True) if tri else smax_sc[slot, :, cols]
            m_old = m_sc[:, cols]
            m_new = jnp.maximum(m_old, s_max)
            alpha = jnp.exp2(m_old - m_new)
            p = jnp.exp2(s - m_new)
            vt = jnp.concatenate([vt_ref[0, j * nsub + c] for c in range(nsub)], axis=1)
            vt = jnp.concatenate([vt, jnp.ones((ONES_ROWS, tk), BF16)], axis=0)
            acc_sc[:, cols] = (alpha * acc_sc[:, cols]
                               + jnp.dot(vt, p.astype(BF16), preferred_element_type=F32))
            m_sc[:, cols] = m_new

        def full_group(i, carry):
            for r in range(group):
                b = group * i + r
                scores(b + 1, (r + 1) % 2)
                update(b, r % 2)
            return carry

        scores(0, 0)
        lax.fori_loop(0, qi, full_group, 0)
        for r in range(group):
            b = group * qi + r
            if r + 1 < group:
                scores(b + 1, (r + 1) % 2, slice((r + 1) * tk, tq))
            update(b, r % 2, slice(r * tk, tq), tri=True)
        out_t = acc_sc[:MLA_V, :] / acc_sc[MLA_V:MLA_V + 1, :]
        o_ref[q0:q0 + tq, :] = out_t.T.astype(o_ref.dtype)

    for t in range(qps):
        one_query_block(pl.program_id(1) * qps + t, t * tq)

    @pl.when(step > 0)
    def _():
        for w in range(ncast):
            slab_out(w, step - 1).wait()

    for w in range(ncast):
        slab_in(w).wait()
        stage_out[w][...] = stage_in[w][...].astype(BF16)
        slab_out(w, step).start()

    @pl.when(step == last_step)
    def _():
        for w in range(ncast):
            slab_out(w, step).wait()


def _mla(qt, k, vt, cast_weights, *, tq, tk, qps):
    _, s, _ = k.shape
    grid = (MLA_HEADS, s // (tq * qps))
    nsteps = grid[0] * grid[1]
    ncast = len(cast_weights)
    slabs = [(w.shape[0] // nsteps, w.shape[1]) for w in cast_weights]
    for w, (rows, _) in zip(cast_weights, slabs):
        assert rows * nsteps == w.shape[0] and rows % 16 == 0, (w.shape, nsteps)
    anywhere = pl.BlockSpec(memory_space=pl.ANY)
    outs = pl.pallas_call(
        functools.partial(_mla_kernel, tq=tq, tk=tk, qps=qps, ncast=ncast),
        grid=grid,
        in_specs=[pl.BlockSpec((1, MLA_QK, tq * qps), lambda h, i: (h, 0, i)),
                  pl.BlockSpec((1, s, MLA_QK), lambda h, i: (h, 0, 0)),
                  pl.BlockSpec((1, s // LANES, MLA_V, LANES), lambda h, i: (h, 0, 0, 0))]
                 + [anywhere] * ncast,
        out_specs=[pl.BlockSpec((tq * qps, MLA_V), lambda h, i: (i, h))] + [anywhere] * ncast,
        out_shape=[jax.ShapeDtypeStruct((s, MLA_HEADS * MLA_V), BF16)]
                  + [jax.ShapeDtypeStruct(w.shape, BF16) for w in cast_weights],
        scratch_shapes=[pltpu.VMEM((1, tq), F32),
                        pltpu.VMEM((MLA_V + ONES_ROWS, tq), F32),
                        pltpu.VMEM((2, tk, tq), F32), pltpu.VMEM((2, 1, tq), F32)]
                       + [pltpu.VMEM(sl, F32) for sl in slabs]
                       + [pltpu.VMEM(sl, BF16) for sl in slabs]
                       + [pltpu.SemaphoreType.DMA((ncast,)), pltpu.SemaphoreType.DMA((ncast,))],
        compiler_params=_cparams(("arbitrary", "arbitrary")),
        name="mla",
    )(qt, k, vt, *cast_weights)
    return outs[0], outs[1:]


def _t5_bucket_table_t():
    i = np.arange(BLK)[None, :]
    j = np.arange(2 * BLK)[:, None]
    dist = i + BLK - j
    max_exact = NUM_BUCKETS // 2
    d = np.maximum(dist, 0)
    large = max_exact + (np.log(np.maximum(d, 1) / max_exact)
                         / np.log(MAX_DISTANCE / max_exact)
                         * (NUM_BUCKETS - max_exact)).astype(np.int32)
    large = np.minimum(large, NUM_BUCKETS - 1)
    bucket = np.where(d < max_exact, d, large).astype(np.int32)
    in_window = (dist >= 0) & (dist < WINDOW)
    return np.where(in_window, bucket, -1).astype(np.int32)


def _swa_kernel(relb_ref, sink_ref, bucket_ref, qt_ref, kc_ref, kp_ref, vtc_ref, vtp_ref,
                o_ref, bias_sc, sink_sc, *, nb):
    i = pl.program_id(0)
    gw = SWA_GROUP * BLK

    @pl.when(i == 0)
    def _():
        bucket = bucket_ref[...]
        for h in range(SWA_HEADS):
            c, g = divmod(h, SWA_GROUP)
            b = jnp.full(bucket.shape, NEG, F32)
            for t in range(NUM_BUCKETS):
                b = jnp.where(bucket == t, relb_ref[t, h] * LOG2E, b)
            bias_sc[1, c, :, g * BLK:(g + 1) * BLK] = b
            bias_sc[0, c, BLK:, g * BLK:(g + 1) * BLK] = b[BLK:]
            bias_sc[0, c, :BLK, g * BLK:(g + 1) * BLK] = jnp.full((BLK, BLK), NEG, F32)
            sink_sc[c, :, g * BLK:(g + 1) * BLK] = jnp.full((1, BLK), sink_ref[h] * LOG2E, F32)

    zeros = jnp.zeros((SWA_DIM, gw), BF16)
    for t in range(nb):
        tsl = slice(t * BLK, (t + 1) * BLK)
        k_prev = kp_ref[...] if t == 0 else kc_ref[(t - 1) * BLK:t * BLK, :]
        vt_prev = vtp_ref[...] if t == 0 else vtc_ref[:, (t - 1) * BLK:t * BLK]
        kband = jnp.concatenate([k_prev, kc_ref[tsl, :]], axis=0)
        general = 1 if t > 0 else jnp.where(i == 0, 0, 1)
        for c in range(SWA_KV_HEADS):
            qt = jnp.concatenate(
                [qt_ref[(c * SWA_GROUP + g) * SWA_DIM:(c * SWA_GROUP + g + 1) * SWA_DIM, tsl]
                 for g in range(SWA_GROUP)], axis=1)
            qt_ext = jnp.concatenate([qt, zeros] if c == 0 else [zeros, qt], axis=0)
            s = jnp.dot(kband, qt_ext, preferred_element_type=F32) + bias_sc[general, c]
            sink = sink_sc[c]
            m = jnp.maximum(jnp.max(s, axis=0, keepdims=True), sink)
            p = jnp.exp2(s - m)
            dsl = slice(c * SWA_DIM, (c + 1) * SWA_DIM)
            vt = jnp.concatenate([vt_prev[dsl, :], vtc_ref[dsl, tsl]], axis=1)
            vt = jnp.concatenate([vt, jnp.ones((ONES_ROWS, 2 * BLK), BF16)], axis=0)
            pv = jnp.dot(vt, p.astype(BF16), preferred_element_type=F32)
            denom = pv[SWA_DIM:SWA_DIM + 1] + jnp.exp2(sink - m)
            ot = pv[:SWA_DIM] / denom
            for g2 in range(SWA_GROUP // 2):
                two = jnp.concatenate([ot[:, (2 * g2) * BLK:(2 * g2 + 1) * BLK],
                                       ot[:, (2 * g2 + 1) * BLK:(2 * g2 + 2) * BLK]], axis=0)
                col0 = (c * SWA_GROUP + 2 * g2) * SWA_DIM
                o_ref[tsl, col0:col0 + 2 * SWA_DIM] = two.T.astype(o_ref.dtype)


def _swa(sqt, sk, svt, sinks, rel_bias, *, nb):
    nq, s = sqt.shape
    nkv = sk.shape[1]
    tb = nb * BLK
    bucket = jnp.asarray(_t5_bucket_table_t())
    smem = pl.BlockSpec(memory_space=pltpu.SMEM)
    prev_blk = lambda i: jnp.maximum(i * nb - 1, 0)
    return pl.pallas_call(
        functools.partial(_swa_kernel, nb=nb),
        grid=(s // tb,),
        in_specs=[smem, smem, _resident((2 * BLK, BLK)),
                  pl.BlockSpec((nq, tb), lambda i: (0, i)),
                  pl.BlockSpec((tb, nkv), lambda i: (i, 0)),
                  pl.BlockSpec((BLK, nkv), lambda i: (prev_blk(i), 0)),
                  pl.BlockSpec((nkv, tb), lambda i: (0, i)),
                  pl.BlockSpec((nkv, BLK), lambda i: (0, prev_blk(i)))],
        out_specs=pl.BlockSpec((tb, nq), lambda i: (i, 0)),
        out_shape=jax.ShapeDtypeStruct((s, nq), BF16),
        scratch_shapes=[pltpu.VMEM((2, SWA_KV_HEADS, 2 * BLK, SWA_GROUP * BLK), F32),
                        pltpu.VMEM((SWA_KV_HEADS, 1, SWA_GROUP * BLK), F32)],
        compiler_params=_cparams(("arbitrary",)),
        name="swa",
    )(rel_bias, sinks, bucket, sqt, sk, sk, svt, svt)


def _rms(y, gain):
    r = lax.rsqrt(jnp.sum(y * y, axis=-1, keepdims=True) * (1.0 / y.shape[-1]) + RMS_EPS)
    return y * r * gain


def _oproj_kernel(a_ref, b_ref, wo_ref, x_ref, gpost_ref, gpre_ref, x1_ref, h_ref, *, sub):
    na = a_ref.shape[1]
    for r0 in range(0, x_ref.shape[0], sub):
        rows = slice(r0, r0 + sub)
        mix = (jnp.dot(a_ref[rows, :], wo_ref[:na, :], preferred_element_type=F32)
               + jnp.dot(b_ref[rows, :], wo_ref[na:, :], preferred_element_type=F32))
        x1 = x_ref[rows, :] + _rms(mix, gpost_ref[...])
        x1_ref[rows, :] = x1
        h_ref[rows, :] = _rms(x1, gpre_ref[...]).astype(BF16)


def _oproj(out_a, out_b, w_o, x2, g_post, g_pre, *, tm, sub):
    s, d = x2.shape
    row = lambda w: pl.BlockSpec((tm, w), lambda i: (i, 0))
    return pl.pallas_call(
        functools.partial(_oproj_kernel, sub=sub),
        grid=(s // tm,),
        in_specs=[row(out_a.shape[1]), row(out_b.shape[1]), _resident(w_o.shape),
                  row(d), _resident((1, d)), _resident((1, d))],
        out_specs=[row(d), row(d)],
        out_shape=[jax.ShapeDtypeStruct((s, d), F32), jax.ShapeDtypeStruct((s, d), BF16)],
        compiler_params=_cparams(("arbitrary",)),
        name="oproj",
    )(out_a, out_b, w_o, x2, g_post, g_pre)


def _ffn_kernel(h_ref, wg_ref, wu_ref, wd_ref, x1_hbm, gpost_ref, o_ref, x1_buf, x1_sem):
    i = pl.program_id(0)
    j = pl.program_id(1)
    tm = o_ref.shape[0]

    def x1_copy():
        rows = pl.ds(pl.multiple_of(i * tm, tm), tm)
        return pltpu.make_async_copy(x1_hbm.at[rows, :], x1_buf, x1_sem)

    @pl.when(j == 0)
    def _():
        x1_copy().start()
        o_ref[...] = jnp.zeros(o_ref.shape, F32)

    h = h_ref[...]
    tf = wg_ref.shape[1]
    halves = [slice(0, tf // 2), slice(tf // 2, tf)]
    gu = [(jnp.dot(h, wg_ref[:, c], preferred_element_type=F32),
           jnp.dot(h, wu_ref[:, c], preferred_element_type=F32)) for c in halves]
    for c, (gate, up) in zip(halves, gu):
        act = (gate * jax.nn.sigmoid(gate) * up).astype(BF16)
        o_ref[...] += jnp.dot(act, wd_ref[c, :], preferred_element_type=F32)

    @pl.when(j == pl.num_programs(1) - 1)
    def _():
        x1_copy().wait()
        o_ref[...] = x1_buf[...] + _rms(o_ref[...], gpost_ref[...])


def _ffn(h, wg, wu, wd, x1, g_post, *, tm, tf):
    s, d = x1.shape
    dff = wg.shape[1]
    return pl.pallas_call(
        _ffn_kernel,
        grid=(s // tm, dff // tf),
        in_specs=[pl.BlockSpec((tm, d), lambda i, j: (i, 0)),
                  pl.BlockSpec((d, tf), lambda i, j: (0, j)),
                  pl.BlockSpec((d, tf), lambda i, j: (0, j)),
                  pl.BlockSpec((tf, d), lambda i, j: (j, 0)),
                  pl.BlockSpec(memory_space=pl.ANY),
                  _resident((1, d))],
        out_specs=pl.BlockSpec((tm, d), lambda i, j: (i, 0)),
        out_shape=jax.ShapeDtypeStruct((s, d), F32),
        scratch_shapes=[pltpu.VMEM((tm, d), F32), pltpu.SemaphoreType.DMA(())],
        compiler_params=_cparams(("arbitrary", "arbitrary")),
        name="ffn",
    )(h, wg, wu, wd, x1, g_post)


def _layer(x2, pos_row, p, *, tm_qkv=1024):
    half = MLA_ROPE // 2
    inv_freq = ROPE_THETA ** (-jnp.arange(half, dtype=F32) / half)
    invf = jnp.broadcast_to(inv_freq[:, None], (half, tm_qkv))

    row = lambda a: a[None, :]
    cq, ckv, krt, sqt, sk, svt = _proj(x2, row(p["g_mix_pre"]), p["w_in"].T,
                                       row(p["g_cq"]), row(p["g_ckv"]), tm=512, sub=256)
    qt, k, vt = _qkv(cq, ckv, krt, pos_row, invf, p["w_uq"], p["w_ukv"], tm=tm_qkv)
    out_a, (w_o, w_gate, w_up, w_down) = _mla(
        qt, k, vt, [p["w_o"], p["w_gate"], p["w_up"], p["w_down"]], tq=1024, tk=512, qps=4)
    out_b = _swa(sqt, sk, svt, p["sinks"], p["rel_bias"], nb=8)
    x1, h = _oproj(out_a, out_b, w_o, x2,
                   row(p["g_mix_post"]), row(p["g_ffn_pre"]), tm=512, sub=256)
    return _ffn(h, w_gate, w_up, w_down, x1, row(p["g_ffn_post"]), tm=1024, tf=512)


def kernel(x, positions, g_mix_pre, w_in, g_cq, g_ckv, w_uq, w_ukv, sinks, rel_bias,
           w_o, g_mix_post, g_ffn_pre, w_gate, w_up, w_down, g_ffn_post):
    b, s, d = x.shape
    assert b == 1, "the row-major (S, D) pipeline assumes a single sequence"
    x2 = x.reshape(s, d)
    pos_row = positions.reshape(1, s)
    for layer in range(w_in.shape[0]):
        p = dict(g_mix_pre=g_mix_pre[layer], w_in=w_in[layer], g_cq=g_cq[layer],
                 g_ckv=g_ckv[layer], w_uq=w_uq[layer], w_ukv=w_ukv[layer],
                 sinks=sinks[layer], rel_bias=rel_bias, w_o=w_o[layer],
                 g_mix_post=g_mix_post[layer], g_ffn_pre=g_ffn_pre[layer],
                 w_gate=w_gate[layer], w_up=w_up[layer], w_down=w_down[layer],
                 g_ffn_post=g_ffn_post[layer])
        x2 = _layer(x2, pos_row, p)
    return x2.reshape(b, s, d)
```

```python
import functools
import math
from typing import NamedTuple

import jax
import jax.numpy as jnp
import numpy as np
from jax import lax
from jax.experimental import pallas as pl
from jax.experimental.pallas import tpu as pltpu

F32 = jnp.float32
BF16 = jnp.bfloat16

MLA_HEADS = 8
MLA_NOPE = 128
MLA_ROPE = 64
MLA_QK = MLA_NOPE + MLA_ROPE
MLA_V = 128
Q_RANK = 512
KV_RANK = 512
ROPE_THETA = 10000.0
SWA_HEADS = 16
SWA_KV_HEADS = 2
SWA_GROUP = SWA_HEADS // SWA_KV_HEADS
SWA_DIM = 64
WINDOW = 128
NUM_BUCKETS = 32
MAX_DISTANCE = 128
BLK = 128
RMS_EPS = 1e-6
NEG = -1e30
LOG2E = math.log2(math.e)

LANES = 128
ONES_ROWS = 16
VMEM_LIMIT = 60 * 1024 * 1024


def _cparams(sem):
    return pltpu.CompilerParams(dimension_semantics=sem, vmem_limit_bytes=VMEM_LIMIT)


_NT = (((1,), (1,)), ((), ()))


def _resident(shape):
    nd = len(shape)
    return pl.BlockSpec(shape, lambda *_: (0,) * nd, pipeline_mode=pl.Buffered(1))


def _proj_kernel(x_ref, g_ref, wt32_ref, gcq_ref, gckv_ref,
                 cq_ref, ckv_ref, krt_ref, sqt_ref, sk_ref, svt_ref, wt_ref, *, swa_scale, sub):
    nq = SWA_HEADS * SWA_DIM
    nkv = SWA_KV_HEADS * SWA_DIM
    nl = Q_RANK + KV_RANK

    @pl.when(pl.program_id(0) == 0)
    def _():
        wt_ref[...] = wt32_ref[...].astype(BF16)

    def latent_norm(c, gain_ref):
        rc = lax.rsqrt(jnp.sum(c * c, axis=-1, keepdims=True) * (1.0 / c.shape[-1]) + RMS_EPS)
        return (c * rc * gain_ref[...]).astype(BF16)

    for r0 in range(0, x_ref.shape[0], sub):
        rows = slice(r0, r0 + sub)
        x = x_ref[rows, :]
        r = lax.rsqrt(jnp.sum(x * x, axis=-1, keepdims=True) * (1.0 / x.shape[-1]) + RMS_EPS)
        h = (x * g_ref[...]).astype(BF16)
        yt = lax.dot_general(wt_ref[...], h, _NT, preferred_element_type=F32)
        c = yt[:nl].T * r
        cq_ref[rows, :] = latent_norm(c[:, :Q_RANK], gcq_ref)
        ckv_ref[rows, :] = latent_norm(c[:, Q_RANK:], gckv_ref)
        r_lanes = jnp.broadcast_to(r, (sub, LANES)).T[0:1, :]
        tail = yt[nl:] * r_lanes
        o = MLA_ROPE
        krt_ref[:, rows] = tail[:o]
        sqt_ref[:, rows] = (tail[o:o + nq] * swa_scale).astype(BF16)
        sk_ref[rows, :] = tail[o + nq:o + nq + nkv].T.astype(BF16)
        svt_ref[:, rows] = tail[o + nq + nkv:o + nq + 2 * nkv].astype(BF16)


def _proj(x2, g, w_in_t, g_cq, g_ckv, *, tm, sub):
    s, d = x2.shape
    nq = SWA_HEADS * SWA_DIM
    nkv = SWA_KV_HEADS * SWA_DIM
    row = lambda w: pl.BlockSpec((tm, w), lambda i: (i, 0))
    col = lambda r: pl.BlockSpec((r, tm), lambda i: (0, i))
    return pl.pallas_call(
        functools.partial(_proj_kernel, swa_scale=LOG2E / math.sqrt(SWA_DIM), sub=sub),
        grid=(s // tm,),
        in_specs=[row(d), _resident((1, d)), _resident(w_in_t.shape),
                  _resident((1, Q_RANK)), _resident((1, KV_RANK))],
        out_specs=[row(Q_RANK), row(KV_RANK), col(MLA_ROPE), col(nq), row(nkv), col(nkv)],
        out_shape=[jax.ShapeDtypeStruct((s, Q_RANK), BF16),
                   jax.ShapeDtypeStruct((s, KV_RANK), BF16),
                   jax.ShapeDtypeStruct((MLA_ROPE, s), F32),
                   jax.ShapeDtypeStruct((nq, s), BF16),
                   jax.ShapeDtypeStruct((s, nkv), BF16),
                   jax.ShapeDtypeStruct((nkv, s), BF16)],
        scratch_shapes=[pltpu.VMEM(w_in_t.shape, BF16)],
        compiler_params=_cparams(("arbitrary",)),
        name="proj",
    )(x2, g, w_in_t, g_cq, g_ckv)


def _qkv_kernel(cq_ref, ckv_ref, krt_ref, pos_ref, invf_ref, wuq_ref, wukv_ref,
                qt_ref, k_ref, vt_ref, wuqt_ref, wuk_ref, wuvt_ref, *, q_scale):
    tm = cq_ref.shape[0]
    half = MLA_ROPE // 2

    @pl.when(pl.program_id(0) == 0)
    def _():
        wuqt_ref[...] = wuq_ref[...].T.astype(BF16)
        for h in range(MLA_HEADS):
            c0 = h * (MLA_NOPE + MLA_V)
            wuk_ref[:, h * MLA_NOPE:(h + 1) * MLA_NOPE] = wukv_ref[:, c0:c0 + MLA_NOPE].astype(BF16)
            wuvt_ref[h * MLA_V:(h + 1) * MLA_V, :] = (
                wukv_ref[:, c0 + MLA_NOPE:c0 + MLA_NOPE + MLA_V].T.astype(BF16))
    ang = invf_ref[...] * pos_ref[...].astype(F32)
    cos = jnp.cos(ang)
    sin = jnp.sin(ang)

    def rope_t(t):
        t1, t2 = t[:half], t[half:]
        return jnp.concatenate([t1 * cos - t2 * sin, t2 * cos + t1 * sin], axis=0)

    cq = cq_ref[...]
    ckv = ckv_ref[...]
    qt = lax.dot_general(wuqt_ref[...], cq, _NT, preferred_element_type=F32) * q_scale
    for h in range(MLA_HEADS):
        base = h * MLA_QK
        qt_ref[h, 0:MLA_NOPE, :] = qt[base:base + MLA_NOPE].astype(BF16)
        qt_ref[h, MLA_NOPE:MLA_QK, :] = rope_t(qt[base + MLA_NOPE:base + MLA_QK]).astype(BF16)

    krt = jnp.concatenate([rope_t(krt_ref[...]), jnp.zeros((LANES - MLA_ROPE, tm), F32)], axis=0)
    kr = krt.T[:, :MLA_ROPE].astype(BF16)
    kn = jnp.dot(ckv, wuk_ref[...], preferred_element_type=F32)
    vt = lax.dot_general(wuvt_ref[...], ckv, _NT, preferred_element_type=F32)
    for h in range(MLA_HEADS):
        k_ref[h, :, 0:MLA_NOPE] = kn[:, h * MLA_NOPE:(h + 1) * MLA_NOPE].astype(BF16)
        k_ref[h, :, MLA_NOPE:MLA_QK] = kr
        for c in range(tm // LANES):
            vt_ref[h, c] = vt[h * MLA_V:(h + 1) * MLA_V, c * LANES:(c + 1) * LANES].astype(BF16)


def _qkv(cq, ckv, krt, pos_row, invf, w_uq, w_ukv, *, tm):
    s = cq.shape[0]
    rank = w_uq.shape[0]
    row = lambda w: pl.BlockSpec((tm, w), lambda i: (i, 0))
    col = lambda r: pl.BlockSpec((r, tm), lambda i: (0, i))
    nb = tm // LANES
    return pl.pallas_call(
        functools.partial(_qkv_kernel, q_scale=LOG2E / math.sqrt(MLA_QK)),
        grid=(s // tm,),
        in_specs=[row(Q_RANK), row(KV_RANK), col(MLA_ROPE), col(1), _resident(invf.shape),
                  _resident(w_uq.shape), _resident(w_ukv.shape)],
        out_specs=[pl.BlockSpec((MLA_HEADS, MLA_QK, tm), lambda i: (0, 0, i)),
                   pl.BlockSpec((MLA_HEADS, tm, MLA_QK), lambda i: (0, i, 0)),
                   pl.BlockSpec((MLA_HEADS, nb, MLA_V, LANES), lambda i: (0, i, 0, 0))],
        out_shape=[jax.ShapeDtypeStruct((MLA_HEADS, MLA_QK, s), BF16),
                   jax.ShapeDtypeStruct((MLA_HEADS, s, MLA_QK), BF16),
                   jax.ShapeDtypeStruct((MLA_HEADS, s // LANES, MLA_V, LANES), BF16)],
        scratch_shapes=[pltpu.VMEM((MLA_HEADS * MLA_QK, rank), BF16),
                        pltpu.VMEM((rank, MLA_HEADS * MLA_NOPE), BF16),
                        pltpu.VMEM((MLA_HEADS * MLA_V, rank), BF16)],
        compiler_params=_cparams(("arbitrary",)),
        name="qkv",
    )(cq, ckv, krt, pos_row, invf, w_uq, w_ukv)


def _mla_kernel(*refs, tq, tk, qps, ncast):
    qt_ref, k_ref, vt_ref = refs[:3]
    w_hbm = refs[3:3 + ncast]
    o_ref = refs[3 + ncast]
    wout_hbm = refs[4 + ncast:4 + 2 * ncast]
    m_sc, acc_sc, s_sc, smax_sc = refs[4 + 2 * ncast:8 + 2 * ncast]
    stage_in = refs[8 + 2 * ncast:8 + 3 * ncast]
    stage_out = refs[8 + 3 * ncast:8 + 4 * ncast]
    in_sems, out_sems = refs[8 + 4 * ncast:]

    step = pl.program_id(0) * pl.num_programs(1) + pl.program_id(1)
    last_step = pl.num_programs(0) * pl.num_programs(1) - 1

    def slab_in(w):
        rows = stage_in[w].shape[0]
        src = w_hbm[w].at[pl.ds(pl.multiple_of(step * rows, rows), rows), :]
        return pltpu.make_async_copy(src, stage_in[w], in_sems.at[w])

    def slab_out(w, at_step):
        rows = stage_out[w].shape[0]
        dst = wout_hbm[w].at[pl.ds(pl.multiple_of(at_step * rows, rows), rows), :]
        return pltpu.make_async_copy(stage_out[w], dst, out_sems.at[w])

    for w in range(ncast):
        slab_in(w).start()

    group = tq // tk
    assert group * tk == tq and group % 2 == 0
    nsub = tk // LANES
    full = slice(0, tq)

    def one_query_block(qi, q0):
        m_sc[...] = jnp.full(m_sc.shape, NEG, F32)
        acc_sc[...] = jnp.zeros(acc_sc.shape, F32)

        def scores(j, slot, cols=full):
            start = pl.multiple_of(j * tk, tk)
            qcols = slice(q0 + cols.start, q0 + cols.stop)
            s = jnp.dot(k_ref[0, pl.ds(start, tk), :], qt_ref[0, :, qcols],
                        preferred_element_type=F32)
            s_sc[slot, :, cols] = s
            smax_sc[slot, :, cols] = jnp.max(s, axis=0, keepdims=True)

        def update(j, slot, cols=full, tri=False):
            s = s_sc[slot, :, cols]
            if tri:
                mask = (lax.broadcasted_iota(jnp.int32, (tk, tk), 0)
                        <= lax.broadcasted_iota(jnp.int32, (tk, tk), 1))
                parts = [jnp.where(mask, s[:, :tk], NEG)] + ([s[:, tk:]] if s.shape[1] > tk else [])
                s = jnp.concatenate(parts, axis=1)
            s_max = jnp.max(s, axis=0, keepdims=True) if tri else smax_sc[slot, :, cols]
            m_old = m_sc[:, cols]
            m_new = jnp.maximum(m_old, s_max)
            alpha = jnp.exp2(m_old - m_new)
            p = jnp.exp2(s - m_new)
            vt = jnp.concatenate([vt_ref[0, j * nsub + c] for c in range(nsub)], axis=1)
            vt = jnp.concatenate([vt, jnp.ones((ONES_ROWS, tk), BF16)], axis=0)
            acc_sc[:, cols] = (alpha * acc_sc[:, cols]
                               + jnp.dot(vt, p.astype(BF16), preferred_element_type=F32))
            m_sc[:, cols] = m_new

        def full_group(i, carry):
            for r in range(group):
                b = group * i + r
                scores(b + 1, (r + 1) % 2)
                update(b, r % 2)
            return carry

        scores(0, 0)
        lax.fori_loop(0, qi, full_group, 0)
        for r in range(group):
            b = group * qi + r
            if r + 1 < group:
                scores(b + 1, (r + 1) % 2, slice((r + 1) * tk, tq))
            update(b, r % 2, slice(r * tk, tq), tri=True)
        out_t = acc_sc[:MLA_V, :] / acc_sc[MLA_V:MLA_V + 1, :]
        o_ref[q0:q0 + tq, :] = out_t.T.astype(o_ref.dtype)

    for t in range(qps):
        one_query_block(pl.program_id(1) * qps + t, t * tq)

    @pl.when(step > 0)
    def _():
        for w in range(ncast):
            slab_out(w, step - 1).wait()

    for w in range(ncast):
        slab_in(w).wait()
        stage_out[w][...] = stage_in[w][...].astype(BF16)
        slab_out(w, step).start()

    @pl.when(step == last_step)
    def _():
        for w in range(ncast):
            slab_out(w, step).wait()


def _mla(qt, k, vt, cast_weights, *, tq, tk, qps):
    _, s, _ = k.shape
    grid = (MLA_HEADS, s // (tq * qps))
    nsteps = grid[0] * grid[1]
    ncast = len(cast_weights)
    slabs = [(w.shape[0] // nsteps, w.shape[1]) for w in cast_weights]
    for w, (rows, _) in zip(cast_weights, slabs):
        assert rows * nsteps == w.shape[0] and rows % 16 == 0, (w.shape, nsteps)
    anywhere = pl.BlockSpec(memory_space=pl.ANY)
    outs = pl.pallas_call(
        functools.partial(_mla_kernel, tq=tq, tk=tk, qps=qps, ncast=ncast),
        grid=grid,
        in_specs=[pl.BlockSpec((1, MLA_QK, tq * qps), lambda h, i: (h, 0, i)),
                  pl.BlockSpec((1, s, MLA_QK), lambda h, i: (h, 0, 0)),
                  pl.BlockSpec((1, s // LANES, MLA_V, LANES), lambda h, i: (h, 0, 0, 0))]
                 + [anywhere] * ncast,
        out_specs=[pl.BlockSpec((tq * qps, MLA_V), lambda h, i: (i, h))] + [anywhere] * ncast,
        out_shape=[jax.ShapeDtypeStruct((s, MLA_HEADS * MLA_V), BF16)]
                  + [jax.ShapeDtypeStruct(w.shape, BF16) for w in cast_weights],
        scratch_shapes=[pltpu.VMEM((1, tq), F32),
                        pltpu.VMEM((MLA_V + ONES_ROWS, tq), F32),
                        pltpu.VMEM((2, tk, tq), F32), pltpu.VMEM((2, 1, tq), F32)]
                       + [pltpu.VMEM(sl, F32) for sl in slabs]
                       + [pltpu.VMEM(sl, BF16) for sl in slabs]
                       + [pltpu.SemaphoreType.DMA((ncast,)), pltpu.SemaphoreType.DMA((ncast,))],
        compiler_params=_cparams(("arbitrary", "arbitrary")),
        name="mla",
    )(qt, k, vt, *cast_weights)
    return outs[0], outs[1:]


def _t5_bucket_table_t():
    i = np.arange(BLK)[None, :]
    j = np.arange(2 * BLK)[:, None]
    dist = i + BLK - j
    max_exact = NUM_BUCKETS // 2
    d = np.maximum(dist, 0)
    large = max_exact + (np.log(np.maximum(d, 1) / max_exact)
                         / np.log(MAX_DISTANCE / max_exact)
                         * (NUM_BUCKETS - max_exact)).astype(np.int32)
    large = np.minimum(large, NUM_BUCKETS - 1)
    bucket = np.where(d < max_exact, d, large).astype(np.int32)
    in_window = (dist >= 0) & (dist < WINDOW)
    return np.where(in_window, bucket, -1).astype(np.int32)


def _swa_kernel(relb_ref, sink_ref, bucket_ref, qt_ref, kc_ref, kp_ref, vtc_ref, vtp_ref,
                o_ref, bias_sc, sink_sc, *, nb):
    i = pl.program_id(0)
    gw = SWA_GROUP * BLK

    @pl.when(i == 0)
    def _():
        bucket = bucket_ref[...]
        for h in range(SWA_HEADS):
            c, g = divmod(h, SWA_GROUP)
            b = jnp.full(bucket.shape, NEG, F32)
            for t in range(NUM_BUCKETS):
                b = jnp.where(bucket == t, relb_ref[t, h] * LOG2E, b)
            bias_sc[1, c, :, g * BLK:(g + 1) * BLK] = b
            bias_sc[0, c, BLK:, g * BLK:(g + 1) * BLK] = b[BLK:]
            bias_sc[0, c, :BLK, g * BLK:(g + 1) * BLK] = jnp.full((BLK, BLK), NEG, F32)
            sink_sc[c, :, g * BLK:(g + 1) * BLK] = jnp.full((1, BLK), sink_ref[h] * LOG2E, F32)

    zeros = jnp.zeros((SWA_DIM, gw), BF16)
    for t in range(nb):
        tsl = slice(t * BLK, (t + 1) * BLK)
        k_prev = kp_ref[...] if t == 0 else kc_ref[(t - 1) * BLK:t * BLK, :]
        vt_prev = vtp_ref[...] if t == 0 else vtc_ref[:, (t - 1) * BLK:t * BLK]
        kband = jnp.concatenate([k_prev, kc_ref[tsl, :]], axis=0)
        general = 1 if t > 0 else jnp.where(i == 0, 0, 1)
        for c in range(SWA_KV_HEADS):
            qt = jnp.concatenate(
                [qt_ref[(c * SWA_GROUP + g) * SWA_DIM:(c * SWA_GROUP + g + 1) * SWA_DIM, tsl]
                 for g in range(SWA_GROUP)], axis=1)
            qt_ext = jnp.concatenate([qt, zeros] if c == 0 else [zeros, qt], axis=0)
            s = jnp.dot(kband, qt_ext, preferred_element_type=F32) + bias_sc[general, c]
            sink = sink_sc[c]
            m = jnp.maximum(jnp.max(s, axis=0, keepdims=True), sink)
            p = jnp.exp2(s - m)
            dsl = slice(c * SWA_DIM, (c + 1) * SWA_DIM)
            vt = jnp.concatenate([vt_prev[dsl, :], vtc_ref[dsl, tsl]], axis=1)
            vt = jnp.concatenate([vt, jnp.ones((ONES_ROWS, 2 * BLK), BF16)], axis=0)
            pv = jnp.dot(vt, p.astype(BF16), preferred_element_type=F32)
            denom = pv[SWA_DIM:SWA_DIM + 1] + jnp.exp2(sink - m)
            ot = pv[:SWA_DIM] / denom
            for g2 in range(SWA_GROUP // 2):
                two = jnp.concatenate([ot[:, (2 * g2) * BLK:(2 * g2 + 1) * BLK],
                                       ot[:, (2 * g2 + 1) * BLK:(2 * g2 + 2) * BLK]], axis=0)
                col0 = (c * SWA_GROUP + 2 * g2) * SWA_DIM
                o_ref[tsl, col0:col0 + 2 * SWA_DIM] = two.T.astype(o_ref.dtype)


def _swa(sqt, sk, svt, sinks, rel_bias, *, nb):
    nq, s = sqt.shape
    nkv = sk.shape[1]
    tb = nb * BLK
    bucket = jnp.asarray(_t5_bucket_table_t())
    smem = pl.BlockSpec(memory_space=pltpu.SMEM)
    prev_blk = lambda i: jnp.maximum(i * nb - 1, 0)
    return pl.pallas_call(
        functools.partial(_swa_kernel, nb=nb),
        grid=(s // tb,),
        in_specs=[smem, smem, _resident((2 * BLK, BLK)),
                  pl.BlockSpec((nq, tb), lambda i: (0, i)),
                  pl.BlockSpec((tb, nkv), lambda i: (i, 0)),
                  pl.BlockSpec((BLK, nkv), lambda i: (prev_blk(i), 0)),
                  pl.BlockSpec((nkv, tb), lambda i: (0, i)),
                  pl.BlockSpec((nkv, BLK), lambda i: (0, prev_blk(i)))],
        out_specs=pl.BlockSpec((tb, nq), lambda i: (i, 0)),
        out_shape=jax.ShapeDtypeStruct((s, nq), BF16),
        scratch_shapes=[pltpu.VMEM((2, SWA_KV_HEADS, 2 * BLK, SWA_GROUP * BLK), F32),
                        pltpu.VMEM((SWA_KV_HEADS, 1, SWA_GROUP * BLK), F32)],
        compiler_params=_cparams(("arbitrary",)),
        name="swa",
    )(rel_bias, sinks, bucket, sqt, sk, sk, svt, svt)


def _rms(y, gain):
    r = lax.rsqrt(jnp.sum(y * y, axis=-1, keepdims=True) * (1.0 / y.shape[-1]) + RMS_EPS)
    return y * r * gain


def _oproj_kernel(a_ref, b_ref, wo_ref, x_ref, gpost_ref, gpre_ref, x1_ref, h_ref, *, sub):
    na = a_ref.shape[1]
    for r0 in range(0, x_ref.shape[0], sub):
        rows = slice(r0, r0 + sub)
        mix = (jnp.dot(a_ref[rows, :], wo_ref[:na, :], preferred_element_type=F32)
               + jnp.dot(b_ref[rows, :], wo_ref[na:, :], preferred_element_type=F32))
        x1 = x_ref[rows, :] + _rms(mix, gpost_ref[...])
        x1_ref[rows, :] = x1
        h_ref[rows, :] = _rms(x1, gpre_ref[...]).astype(BF16)


def _oproj(out_a, out_b, w_o, x2, g_post, g_pre, *, tm, sub):
    s, d = x2.shape
    row = lambda w: pl.BlockSpec((tm, w), lambda i: (i, 0))
    return pl.pallas_call(
        functools.partial(_oproj_kernel, sub=sub),
        grid=(s // tm,),
        in_specs=[row(out_a.shape[1]), row(out_b.shape[1]), _resident(w_o.shape),
                  row(d), _resident((1, d)), _resident((1, d))],
        out_specs=[row(d), row(d)],
        out_shape=[jax.ShapeDtypeStruct((s, d), F32), jax.ShapeDtypeStruct((s, d), BF16)],
        compiler_params=_cparams(("arbitrary",)),
        name="oproj",
    )(out_a, out_b, w_o, x2, g_post, g_pre)


def _ffn_kernel(h_ref, wg_ref, wu_ref, wd_ref, x1_hbm, gpost_ref, o_ref, x1_buf, x1_sem):
    i = pl.program_id(0)
    j = pl.program_id(1)
    tm = o_ref.shape[0]

    def x1_copy():
        rows = pl.ds(pl.multiple_of(i * tm, tm), tm)
        return pltpu.make_async_copy(x1_hbm.at[rows, :], x1_buf, x1_sem)

    @pl.when(j == 0)
    def _():
        x1_copy().start()
        o_ref[...] = jnp.zeros(o_ref.shape, F32)

    h = h_ref[...]
    tf = wg_ref.shape[1]
    halves = [slice(0, tf // 2), slice(tf // 2, tf)]
    gu = [(jnp.dot(h, wg_ref[:, c], preferred_element_type=F32),
           jnp.dot(h, wu_ref[:, c], preferred_element_type=F32)) for c in halves]
    for c, (gate, up) in zip(halves, gu):
        act = (gate * jax.nn.sigmoid(gate) * up).astype(BF16)
        o_ref[...] += jnp.dot(act, wd_ref[c, :], preferred_element_type=F32)

    @pl.when(j == pl.num_programs(1) - 1)
    def _():
        x1_copy().wait()
        o_ref[...] = x1_buf[...] + _rms(o_ref[...], gpost_ref[...])


def _ffn(h, wg, wu, wd, x1, g_post, *, tm, tf):
    s, d = x1.shape
    dff = wg.shape[1]
    return pl.pallas_call(
        _ffn_kernel,
        grid=(s // tm, dff // tf),
        in_specs=[pl.BlockSpec((tm, d), lambda i, j: (i, 0)),
                  pl.BlockSpec((d, tf), lambda i, j: (0, j)),
                  pl.BlockSpec((d, tf), lambda i, j: (0, j)),
                  pl.BlockSpec((tf, d), lambda i, j: (j, 0)),
                  pl.BlockSpec(memory_space=pl.ANY),
                  _resident((1, d))],
        out_specs=pl.BlockSpec((tm, d), lambda i, j: (i, 0)),
        out_shape=jax.ShapeDtypeStruct((s, d), F32),
        scratch_shapes=[pltpu.VMEM((tm, d), F32), pltpu.SemaphoreType.DMA(())],
        compiler_params=_cparams(("arbitrary", "arbitrary")),
        name="ffn",
    )(h, wg, wu, wd, x1, g_post)


class _Tiles(NamedTuple):
    proj_rows: int = 512
    proj_sub: int = 256
    qkv_rows: int = 1024
    mla_tq: int = 1024
    mla_tk: int = 512
    mla_qps: int = 8
    swa_blocks: int = 8
    ffn_rows: int = 1024
    ffn_cols: int = 512


def _layer(x2, pos_row, p, t=_Tiles()):
    half = MLA_ROPE // 2
    inv_freq = ROPE_THETA ** (-jnp.arange(half, dtype=F32) / half)
    invf = jnp.broadcast_to(inv_freq[:, None], (half, t.qkv_rows))

    row = lambda a: a[None, :]
    cq, ckv, krt, sqt, sk, svt = _proj(x2, row(p["g_mix_pre"]), p["w_in"].T, row(p["g_cq"]),
                                       row(p["g_ckv"]), tm=t.proj_rows, sub=t.proj_sub)
    qt, k, vt = _qkv(cq, ckv, krt, pos_row, invf, p["w_uq"], p["w_ukv"], tm=t.qkv_rows)
    out_a, (w_o, w_gate, w_up, w_down) = _mla(
        qt, k, vt, [p["w_o"], p["w_gate"], p["w_up"], p["w_down"]],
        tq=t.mla_tq, tk=t.mla_tk, qps=t.mla_qps)
    out_b = _swa(sqt, sk, svt, p["sinks"], p["rel_bias"], nb=t.swa_blocks)
    x1, h = _oproj(out_a, out_b, w_o, x2, row(p["g_mix_post"]), row(p["g_ffn_pre"]),
                   tm=t.proj_rows, sub=t.proj_sub)
    return _ffn(h, w_gate, w_up, w_down, x1, row(p["g_ffn_post"]), tm=t.ffn_rows, tf=t.ffn_cols)


def kernel(x, positions, g_mix_pre, w_in, g_cq, g_ckv, w_uq, w_ukv, sinks, rel_bias,
           w_o, g_mix_post, g_ffn_pre, w_gate, w_up, w_down, g_ffn_post):
    b, s, d = x.shape
    assert b == 1, "the row-major (S, D) pipeline assumes a single sequence"
    n_in = Q_RANK + KV_RANK + MLA_ROPE + (SWA_HEADS + 2 * SWA_KV_HEADS) * SWA_DIM
    d_mix = MLA_HEADS * MLA_V + SWA_HEADS * SWA_DIM
    assert w_in.shape[1:] == (d, n_in) and w_o.shape[1:] == (d_mix, d), (w_in.shape, w_o.shape)
    assert w_uq.shape[1:] == (Q_RANK, MLA_HEADS * MLA_QK), w_uq.shape
    assert w_ukv.shape[1:] == (KV_RANK, MLA_HEADS * (MLA_NOPE + MLA_V)), w_ukv.shape
    assert rel_bias.shape == (NUM_BUCKETS, SWA_HEADS) and sinks.shape[1:] == (SWA_HEADS,)
    x2 = x.reshape(s, d)
    pos_row = positions.reshape(1, s)
    for layer in range(w_in.shape[0]):
        p = dict(g_mix_pre=g_mix_pre[layer], w_in=w_in[layer], g_cq=g_cq[layer],
                 g_ckv=g_ckv[layer], w_uq=w_uq[layer], w_ukv=w_ukv[layer],
                 sinks=sinks[layer], rel_bias=rel_bias, w_o=w_o[layer],
                 g_mix_post=g_mix_post[layer], g_ffn_pre=g_ffn_pre[layer],
                 w_gate=w_gate[layer], w_up=w_up[layer], w_down=w_down[layer],
                 g_ffn_post=g_ffn_post[layer])
        x2 = _layer(x2, pos_row, p)
    return x2.reshape(b, s, d)
```

```python
import functools
import math
from typing import NamedTuple

import jax
import jax.numpy as jnp
import numpy as np
from jax import lax
from jax.experimental import pallas as pl
from jax.experimental.pallas import tpu as pltpu

F32 = jnp.float32
BF16 = jnp.bfloat16

MLA_HEADS = 8
MLA_NOPE = 128
MLA_ROPE = 64
MLA_QK = MLA_NOPE + MLA_ROPE
MLA_V = 128
Q_RANK = 512
KV_RANK = 512
ROPE_THETA = 10000.0
SWA_HEADS = 16
SWA_KV_HEADS = 2
SWA_GROUP = SWA_HEADS // SWA_KV_HEADS
SWA_DIM = 64
WINDOW = 128
NUM_BUCKETS = 32
MAX_DISTANCE = 128
BLK = 128
RMS_EPS = 1e-6
NEG = -1e30
LOG2E = math.log2(math.e)
LAZY_MARGIN = 32.0

LANES = 128
ONES_ROWS = 16
VMEM_LIMIT = 60 * 1024 * 1024


def _cparams(sem):
    return pltpu.CompilerParams(dimension_semantics=sem, vmem_limit_bytes=VMEM_LIMIT)


_NT = (((1,), (1,)), ((), ()))


def _resident(shape):
    nd = len(shape)
    return pl.BlockSpec(shape, lambda *_: (0,) * nd, pipeline_mode=pl.Buffered(1))


def _proj_kernel(x_ref, g_ref, wt32_ref, gcq_ref, gckv_ref,
                 cq_ref, ckv_ref, krt_ref, sqt_ref, sk_ref, svt_ref, wt_ref, *, swa_scale, sub):
    nq = SWA_HEADS * SWA_DIM
    nkv = SWA_KV_HEADS * SWA_DIM
    nl = Q_RANK + KV_RANK

    @pl.when(pl.program_id(0) == 0)
    def _():
        wt_ref[...] = wt32_ref[...].astype(BF16)

    def latent_norm(c, gain_ref):
        rc = lax.rsqrt(jnp.sum(c * c, axis=-1, keepdims=True) * (1.0 / c.shape[-1]) + RMS_EPS)
        return (c * rc * gain_ref[...]).astype(BF16)

    for r0 in range(0, x_ref.shape[0], sub):
        rows = slice(r0, r0 + sub)
        x = x_ref[rows, :]
        r = lax.rsqrt(jnp.sum(x * x, axis=-1, keepdims=True) * (1.0 / x.shape[-1]) + RMS_EPS)
        h = (x * g_ref[...]).astype(BF16)
        yt = lax.dot_general(wt_ref[...], h, _NT, preferred_element_type=F32)
        c = yt[:nl].T * r
        cq_ref[rows, :] = latent_norm(c[:, :Q_RANK], gcq_ref)
        ckv_ref[rows, :] = latent_norm(c[:, Q_RANK:], gckv_ref)
        r_lanes = jnp.broadcast_to(r, (sub, LANES)).T[0:1, :]
        tail = yt[nl:] * r_lanes
        o = MLA_ROPE
        krt_ref[:, rows] = tail[:o]
        sqt_ref[:, rows] = (tail[o:o + nq] * swa_scale).astype(BF16)
        sk_ref[rows, :] = tail[o + nq:o + nq + nkv].T.astype(BF16)
        svt_ref[:, rows] = tail[o + nq + nkv:o + nq + 2 * nkv].astype(BF16)


def _proj(x2, g, w_in_t, g_cq, g_ckv, *, tm, sub):
    s, d = x2.shape
    nq = SWA_HEADS * SWA_DIM
    nkv = SWA_KV_HEADS * SWA_DIM
    row = lambda w: pl.BlockSpec((tm, w), lambda i: (i, 0))
    col = lambda r: pl.BlockSpec((r, tm), lambda i: (0, i))
    return pl.pallas_call(
        functools.partial(_proj_kernel, swa_scale=LOG2E / math.sqrt(SWA_DIM), sub=sub),
        grid=(s // tm,),
        in_specs=[row(d), _resident((1, d)), _resident(w_in_t.shape),
                  _resident((1, Q_RANK)), _resident((1, KV_RANK))],
        out_specs=[row(Q_RANK), row(KV_RANK), col(MLA_ROPE), col(nq), row(nkv), col(nkv)],
        out_shape=[jax.ShapeDtypeStruct((s, Q_RANK), BF16),
                   jax.ShapeDtypeStruct((s, KV_RANK), BF16),
                   jax.ShapeDtypeStruct((MLA_ROPE, s), F32),
                   jax.ShapeDtypeStruct((nq, s), BF16),
                   jax.ShapeDtypeStruct((s, nkv), BF16),
                   jax.ShapeDtypeStruct((nkv, s), BF16)],
        scratch_shapes=[pltpu.VMEM(w_in_t.shape, BF16)],
        compiler_params=_cparams(("arbitrary",)),
        name="proj",
    )(x2, g, w_in_t, g_cq, g_ckv)


def _qkv_kernel(cq_ref, ckv_ref, krt_ref, pos_ref, invf_ref, wuq_ref, wukv_ref,
                qt_ref, k_ref, vt_ref, wuqt_ref, wuk_ref, wuvt_ref, *, q_scale):
    tm = cq_ref.shape[0]
    half = MLA_ROPE // 2

    @pl.when(pl.program_id(0) == 0)
    def _():
        wuqt_ref[...] = wuq_ref[...].T.astype(BF16)
        for h in range(MLA_HEADS):
            c0 = h * (MLA_NOPE + MLA_V)
            wuk_ref[:, h * MLA_NOPE:(h + 1) * MLA_NOPE] = wukv_ref[:, c0:c0 + MLA_NOPE].astype(BF16)
            wuvt_ref[h * MLA_V:(h + 1) * MLA_V, :] = (
                wukv_ref[:, c0 + MLA_NOPE:c0 + MLA_NOPE + MLA_V].T.astype(BF16))
    ang = invf_ref[...] * pos_ref[...].astype(F32)
    cos = jnp.cos(ang)
    sin = jnp.sin(ang)

    def rope_t(t):
        t1, t2 = t[:half], t[half:]
        return jnp.concatenate([t1 * cos - t2 * sin, t2 * cos + t1 * sin], axis=0)

    cq = cq_ref[...]
    ckv = ckv_ref[...]
    qt = lax.dot_general(wuqt_ref[...], cq, _NT, preferred_element_type=F32) * q_scale
    for h in range(MLA_HEADS):
        base = h * MLA_QK
        qt_ref[h, 0:MLA_NOPE, :] = qt[base:base + MLA_NOPE].astype(BF16)
        qt_ref[h, MLA_NOPE:MLA_QK, :] = rope_t(qt[base + MLA_NOPE:base + MLA_QK]).astype(BF16)

    krt = jnp.concatenate([rope_t(krt_ref[...]), jnp.zeros((LANES - MLA_ROPE, tm), F32)], axis=0)
    kr = krt.T[:, :MLA_ROPE].astype(BF16)
    kn = jnp.dot(ckv, wuk_ref[...], preferred_element_type=F32)
    vt = lax.dot_general(wuvt_ref[...], ckv, _NT, preferred_element_type=F32)
    for h in range(MLA_HEADS):
        k_ref[h, :, 0:MLA_NOPE] = kn[:, h * MLA_NOPE:(h + 1) * MLA_NOPE].astype(BF16)
        k_ref[h, :, MLA_NOPE:MLA_QK] = kr
        for c in range(tm // LANES):
            vt_ref[h, c] = vt[h * MLA_V:(h + 1) * MLA_V, c * LANES:(c + 1) * LANES].astype(BF16)


def _qkv(cq, ckv, krt, pos_row, invf, w_uq, w_ukv, *, tm):
    s = cq.shape[0]
    rank = w_uq.shape[0]
    row = lambda w: pl.BlockSpec((tm, w), lambda i: (i, 0))
    col = lambda r: pl.BlockSpec((r, tm), lambda i: (0, i))
    nb = tm // LANES
    return pl.pallas_call(
        functools.partial(_qkv_kernel, q_scale=LOG2E / math.sqrt(MLA_QK)),
        grid=(s // tm,),
        in_specs=[row(Q_RANK), row(KV_RANK), col(MLA_ROPE), col(1), _resident(invf.shape),
                  _resident(w_uq.shape), _resident(w_ukv.shape)],
        out_specs=[pl.BlockSpec((MLA_HEADS, MLA_QK, tm), lambda i: (0, 0, i)),
                   pl.BlockSpec((MLA_HEADS, tm, MLA_QK), lambda i: (0, i, 0)),
                   pl.BlockSpec((MLA_HEADS, nb, MLA_V, LANES), lambda i: (0, i, 0, 0))],
        out_shape=[jax.ShapeDtypeStruct((MLA_HEADS, MLA_QK, s), BF16),
                   jax.ShapeDtypeStruct((MLA_HEADS, s, MLA_QK), BF16),
                   jax.ShapeDtypeStruct((MLA_HEADS, s // LANES, MLA_V, LANES), BF16)],
        scratch_shapes=[pltpu.VMEM((MLA_HEADS * MLA_QK, rank), BF16),
                        pltpu.VMEM((rank, MLA_HEADS * MLA_NOPE), BF16),
                        pltpu.VMEM((MLA_HEADS * MLA_V, rank), BF16)],
        compiler_params=_cparams(("arbitrary",)),
        name="qkv",
    )(cq, ckv, krt, pos_row, invf, w_uq, w_ukv)


def _mla_kernel(*refs, tq, tk, qps, ncast):
    qt_ref, k_ref, vt_ref = refs[:3]
    w_hbm = refs[3:3 + ncast]
    o_ref = refs[3 + ncast]
    wout_hbm = refs[4 + ncast:4 + 2 * ncast]
    m_sc, acc_sc, s_sc, smax_sc, excess_sc = refs[4 + 2 * ncast:9 + 2 * ncast]
    stage_in = refs[9 + 2 * ncast:9 + 3 * ncast]
    stage_out = refs[9 + 3 * ncast:9 + 4 * ncast]
    in_sems, out_sems = refs[9 + 4 * ncast:]

    step = pl.program_id(0) * pl.num_programs(1) + pl.program_id(1)
    last_step = pl.num_programs(0) * pl.num_programs(1) - 1

    def slab_in(w):
        rows = stage_in[w].shape[0]
        src = w_hbm[w].at[pl.ds(pl.multiple_of(step * rows, rows), rows), :]
        return pltpu.make_async_copy(src, stage_in[w], in_sems.at[w])

    def slab_out(w, at_step):
        rows = stage_out[w].shape[0]
        dst = wout_hbm[w].at[pl.ds(pl.multiple_of(at_step * rows, rows), rows), :]
        return pltpu.make_async_copy(stage_out[w], dst, out_sems.at[w])

    for w in range(ncast):
        slab_in(w).start()

    group = tq // tk
    assert group * tk == tq and group % 2 == 0
    nsub = tk // LANES
    full = slice(0, tq)

    def one_query_block(qi, q0):
        def reset():
            m_sc[...] = jnp.full(m_sc.shape, NEG, F32)
            acc_sc[...] = jnp.zeros(acc_sc.shape, F32)

        def raw_scores(j, cols=full):
            start = pl.multiple_of(j * tk, tk)
            qcols = slice(q0 + cols.start, q0 + cols.stop)
            return jnp.dot(k_ref[0, pl.ds(start, tk), :], qt_ref[0, :, qcols],
                           preferred_element_type=F32)

        def values_t(j):
            vt = jnp.concatenate([vt_ref[0, j * nsub + c] for c in range(nsub)], axis=1)
            return jnp.concatenate([vt, jnp.ones((ONES_ROWS, tk), BF16)], axis=0)

        def scores(j, slot, cols=full):
            s = raw_scores(j, cols)
            s_sc[slot, :, cols] = s
            smax_sc[slot, :, cols] = jnp.max(s, axis=0, keepdims=True)

        def update(j, slot, cols=full, tri=False):
            s = s_sc[slot, :, cols]
            if tri:
                mask = (lax.broadcasted_iota(jnp.int32, (tk, tk), 0)
                        <= lax.broadcasted_iota(jnp.int32, (tk, tk), 1))
                parts = [jnp.where(mask, s[:, :tk], NEG)] + ([s[:, tk:]] if s.shape[1] > tk else [])
                s = jnp.concatenate(parts, axis=1)
            s_max = jnp.max(s, axis=0, keepdims=True) if tri else smax_sc[slot, :, cols]
            m_old = m_sc[:, cols]
            m_new = jnp.maximum(m_old, s_max)
            alpha = jnp.exp2(m_old - m_new)
            p = jnp.exp2(s - m_new)
            acc_sc[:, cols] = (alpha * acc_sc[:, cols]
                               + jnp.dot(values_t(j), p.astype(BF16), preferred_element_type=F32))
            m_sc[:, cols] = m_new

        def full_group(i, carry):
            for r in range(group):
                b = group * i + r
                scores(b + 1, (r + 1) % 2)
                update(b, r % 2)
            return carry

        def diagonal_group():
            for r in range(group):
                b = group * qi + r
                if r + 1 < group:
                    scores(b + 1, (r + 1) % 2, slice((r + 1) * tk, tq))
                update(b, r % 2, slice(r * tk, tq), tri=True)

        def finalize():
            out_t = acc_sc[:MLA_V, :] / acc_sc[MLA_V:MLA_V + 1, :]
            o_ref[q0:q0 + tq, :] = out_t.T.astype(o_ref.dtype)

        reset()
        excess_sc[...] = jnp.full(excess_sc.shape, NEG, F32)
        scores(group * qi, 0)
        diagonal_group()

        def lazy_pair(i, carry):
            m = m_sc[...]
            for r in range(2):
                j = 2 * i + r
                s = raw_scores(j)
                excess_sc[...] = jnp.maximum(excess_sc[...], jnp.max(s, axis=0, keepdims=True) - m)
                acc_sc[...] += jnp.dot(values_t(j), jnp.exp2(s - m).astype(BF16),
                                       preferred_element_type=F32)
            return carry

        lax.fori_loop(0, qi * (group // 2), lazy_pair, 0)
        finalize()

        @pl.when(jnp.max(excess_sc[...]) > LAZY_MARGIN)
        def _():
            reset()
            scores(0, 0)
            lax.fori_loop(0, qi, full_group, 0)
            diagonal_group()
            finalize()

    for t in range(qps):
        one_query_block(pl.program_id(1) * qps + t, t * tq)

    @pl.when(step > 0)
    def _():
        for w in range(ncast):
            slab_out(w, step - 1).wait()

    for w in range(ncast):
        slab_in(w).wait()
        stage_out[w][...] = stage_in[w][...].astype(BF16)
        slab_out(w, step).start()

    @pl.when(step == last_step)
    def _():
        for w in range(ncast):
            slab_out(w, step).wait()


def _mla(qt, k, vt, cast_weights, *, tq, tk, qps):
    _, s, _ = k.shape
    grid = (MLA_HEADS, s // (tq * qps))
    nsteps = grid[0] * grid[1]
    ncast = len(cast_weights)
    slabs = [(w.shape[0] // nsteps, w.shape[1]) for w in cast_weights]
    for w, (rows, _) in zip(cast_weights, slabs):
        assert rows * nsteps == w.shape[0] and rows % 16 == 0, (w.shape, nsteps)
    anywhere = pl.BlockSpec(memory_space=pl.ANY)
    outs = pl.pallas_call(
        functools.partial(_mla_kernel, tq=tq, tk=tk, qps=qps, ncast=ncast),
        grid=grid,
        in_specs=[pl.BlockSpec((1, MLA_QK, tq * qps), lambda h, i: (h, 0, i)),
                  pl.BlockSpec((1, s, MLA_QK), lambda h, i: (h, 0, 0)),
                  pl.BlockSpec((1, s // LANES, MLA_V, LANES), lambda h, i: (h, 0, 0, 0))]
                 + [anywhere] * ncast,
        out_specs=[pl.BlockSpec((tq * qps, MLA_V), lambda h, i: (i, h))] + [anywhere] * ncast,
        out_shape=[jax.ShapeDtypeStruct((s, MLA_HEADS * MLA_V), BF16)]
                  + [jax.ShapeDtypeStruct(w.shape, BF16) for w in cast_weights],
        scratch_shapes=[pltpu.VMEM((1, tq), F32),
                        pltpu.VMEM((MLA_V + ONES_ROWS, tq), F32),
                        pltpu.VMEM((2, tk, tq), F32), pltpu.VMEM((2, 1, tq), F32),
                        pltpu.VMEM((1, tq), F32)]
                       + [pltpu.VMEM(sl, F32) for sl in slabs]
                       + [pltpu.VMEM(sl, BF16) for sl in slabs]
                       + [pltpu.SemaphoreType.DMA((ncast,)), pltpu.SemaphoreType.DMA((ncast,))],
        compiler_params=_cparams(("arbitrary", "arbitrary")),
        name="mla",
    )(qt, k, vt, *cast_weights)
    return outs[0], outs[1:]


def _t5_bucket_table_t():
    i = np.arange(BLK)[None, :]
    j = np.arange(2 * BLK)[:, None]
    dist = i + BLK - j
    max_exact = NUM_BUCKETS // 2
    d = np.maximum(dist, 0)
    large = max_exact + (np.log(np.maximum(d, 1) / max_exact)
                         / np.log(MAX_DISTANCE / max_exact)
                         * (NUM_BUCKETS - max_exact)).astype(np.int32)
    large = np.minimum(large, NUM_BUCKETS - 1)
    bucket = np.where(d < max_exact, d, large).astype(np.int32)
    in_window = (dist >= 0) & (dist < WINDOW)
    return np.where(in_window, bucket, -1).astype(np.int32)


def _swa_kernel(relb_ref, sink_ref, bucket_ref, qt_ref, kc_ref, kp_ref, vtc_ref, vtp_ref,
                o_ref, bias_sc, sink_sc, *, nb):
    i = pl.program_id(0)
    gw = SWA_GROUP * BLK

    @pl.when(i == 0)
    def _():
        bucket = bucket_ref[...]
        for h in range(SWA_HEADS):
            c, g = divmod(h, SWA_GROUP)
            b = jnp.full(bucket.shape, NEG, F32)
            for t in range(NUM_BUCKETS):
                b = jnp.where(bucket == t, relb_ref[t, h] * LOG2E, b)
            bias_sc[1, c, :, g * BLK:(g + 1) * BLK] = b
            bias_sc[0, c, BLK:, g * BLK:(g + 1) * BLK] = b[BLK:]
            bias_sc[0, c, :BLK, g * BLK:(g + 1) * BLK] = jnp.full((BLK, BLK), NEG, F32)
            sink_sc[c, :, g * BLK:(g + 1) * BLK] = jnp.full((1, BLK), sink_ref[h] * LOG2E, F32)

    zeros = jnp.zeros((SWA_DIM, gw), BF16)
    for t in range(nb):
        tsl = slice(t * BLK, (t + 1) * BLK)
        k_prev = kp_ref[...] if t == 0 else kc_ref[(t - 1) * BLK:t * BLK, :]
        vt_prev = vtp_ref[...] if t == 0 else vtc_ref[:, (t - 1) * BLK:t * BLK]
        kband = jnp.concatenate([k_prev, kc_ref[tsl, :]], axis=0)
        general = 1 if t > 0 else jnp.where(i == 0, 0, 1)
        for c in range(SWA_KV_HEADS):
            qt = jnp.concatenate(
                [qt_ref[(c * SWA_GROUP + g) * SWA_DIM:(c * SWA_GROUP + g + 1) * SWA_DIM, tsl]
                 for g in range(SWA_GROUP)], axis=1)
            qt_ext = jnp.concatenate([qt, zeros] if c == 0 else [zeros, qt], axis=0)
            s = jnp.dot(kband, qt_ext, preferred_element_type=F32) + bias_sc[general, c]
            sink = sink_sc[c]
            m = jnp.maximum(jnp.max(s, axis=0, keepdims=True), sink)
            p = jnp.exp2(s - m)
            dsl = slice(c * SWA_DIM, (c + 1) * SWA_DIM)
            vt = jnp.concatenate([vt_prev[dsl, :], vtc_ref[dsl, tsl]], axis=1)
            vt = jnp.concatenate([vt, jnp.ones((ONES_ROWS, 2 * BLK), BF16)], axis=0)
            pv = jnp.dot(vt, p.astype(BF16), preferred_element_type=F32)
            denom = pv[SWA_DIM:SWA_DIM + 1] + jnp.exp2(sink - m)
            ot = pv[:SWA_DIM] / denom
            for g2 in range(SWA_GROUP // 2):
                two = jnp.concatenate([ot[:, (2 * g2) * BLK:(2 * g2 + 1) * BLK],
                                       ot[:, (2 * g2 + 1) * BLK:(2 * g2 + 2) * BLK]], axis=0)
                col0 = (c * SWA_GROUP + 2 * g2) * SWA_DIM
                o_ref[tsl, col0:col0 + 2 * SWA_DIM] = two.T.astype(o_ref.dtype)


def _swa(sqt, sk, svt, sinks, rel_bias, *, nb):
    nq, s = sqt.shape
    nkv = sk.shape[1]
    tb = nb * BLK
    bucket = jnp.asarray(_t5_bucket_table_t())
    smem = pl.BlockSpec(memory_space=pltpu.SMEM)
    prev_blk = lambda i: jnp.maximum(i * nb - 1, 0)
    return pl.pallas_call(
        functools.partial(_swa_kernel, nb=nb),
        grid=(s // tb,),
        in_specs=[smem, smem, _resident((2 * BLK, BLK)),
                  pl.BlockSpec((nq, tb), lambda i: (0, i)),
                  pl.BlockSpec((tb, nkv), lambda i: (i, 0)),
                  pl.BlockSpec((BLK, nkv), lambda i: (prev_blk(i), 0)),
                  pl.BlockSpec((nkv, tb), lambda i: (0, i)),
                  pl.BlockSpec((nkv, BLK), lambda i: (0, prev_blk(i)))],
        out_specs=pl.BlockSpec((tb, nq), lambda i: (i, 0)),
        out_shape=jax.ShapeDtypeStruct((s, nq), BF16),
        scratch_shapes=[pltpu.VMEM((2, SWA_KV_HEADS, 2 * BLK, SWA_GROUP * BLK), F32),
                        pltpu.VMEM((SWA_KV_HEADS, 1, SWA_GROUP * BLK), F32)],
        compiler_params=_cparams(("arbitrary",)),
        name="swa",
    )(rel_bias, sinks, bucket, sqt, sk, sk, svt, svt)


def _rms(y, gain):
    r = lax.rsqrt(jnp.sum(y * y, axis=-1, keepdims=True) * (1.0 / y.shape[-1]) + RMS_EPS)
    return y * r * gain


def _oproj_kernel(a_ref, b_ref, wo_ref, x_ref, gpost_ref, gpre_ref, x1_ref, h_ref, *, sub):
    na = a_ref.shape[1]
    for r0 in range(0, x_ref.shape[0], sub):
        rows = slice(r0, r0 + sub)
        mix = (jnp.dot(a_ref[rows, :], wo_ref[:na, :], preferred_element_type=F32)
               + jnp.dot(b_ref[rows, :], wo_ref[na:, :], preferred_element_type=F32))
        x1 = x_ref[rows, :] + _rms(mix, gpost_ref[...])
        x1_ref[rows, :] = x1
        h_ref[rows, :] = _rms(x1, gpre_ref[...]).astype(BF16)


def _oproj(out_a, out_b, w_o, x2, g_post, g_pre, *, tm, sub):
    s, d = x2.shape
    row = lambda w: pl.BlockSpec((tm, w), lambda i: (i, 0))
    return pl.pallas_call(
        functools.partial(_oproj_kernel, sub=sub),
        grid=(s // tm,),
        in_specs=[row(out_a.shape[1]), row(out_b.shape[1]), _resident(w_o.shape),
                  row(d), _resident((1, d)), _resident((1, d))],
        out_specs=[row(d), row(d)],
        out_shape=[jax.ShapeDtypeStruct((s, d), F32), jax.ShapeDtypeStruct((s, d), BF16)],
        compiler_params=_cparams(("arbitrary",)),
        name="oproj",
    )(out_a, out_b, w_o, x2, g_post, g_pre)


def _ffn_kernel(h_ref, wg_ref, wu_ref, wd_ref, x1_hbm, gpost_ref, o_ref, x1_buf, x1_sem):
    i = pl.program_id(0)
    j = pl.program_id(1)
    tm = o_ref.shape[0]

    def x1_copy():
        rows = pl.ds(pl.multiple_of(i * tm, tm), tm)
        return pltpu.make_async_copy(x1_hbm.at[rows, :], x1_buf, x1_sem)

    @pl.when(j == 0)
    def _():
        x1_copy().start()
        o_ref[...] = jnp.zeros(o_ref.shape, F32)

    h = h_ref[...]
    tf = wg_ref.shape[1]
    halves = [slice(0, tf // 2), slice(tf // 2, tf)]
    gu = [(jnp.dot(h, wg_ref[:, c], preferred_element_type=F32),
           jnp.dot(h, wu_ref[:, c], preferred_element_type=F32)) for c in halves]
    for c, (gate, up) in zip(halves, gu):
        act = (gate * jax.nn.sigmoid(gate) * up).astype(BF16)
        o_ref[...] += jnp.dot(act, wd_ref[c, :], preferred_element_type=F32)

    @pl.when(j == pl.num_programs(1) - 1)
    def _():
        x1_copy().wait()
        o_ref[...] = x1_buf[...] + _rms(o_ref[...], gpost_ref[...])


def _ffn(h, wg, wu, wd, x1, g_post, *, tm, tf):
    s, d = x1.shape
    dff = wg.shape[1]
    return pl.pallas_call(
        _ffn_kernel,
        grid=(s // tm, dff // tf),
        in_specs=[pl.BlockSpec((tm, d), lambda i, j: (i, 0)),
                  pl.BlockSpec((d, tf), lambda i, j: (0, j)),
                  pl.BlockSpec((d, tf), lambda i, j: (0, j)),
                  pl.BlockSpec((tf, d), lambda i, j: (j, 0)),
                  pl.BlockSpec(memory_space=pl.ANY),
                  _resident((1, d))],
        out_specs=pl.BlockSpec((tm, d), lambda i, j: (i, 0)),
        out_shape=jax.ShapeDtypeStruct((s, d), F32),
        scratch_shapes=[pltpu.VMEM((tm, d), F32), pltpu.SemaphoreType.DMA(())],
        compiler_params=_cparams(("arbitrary", "arbitrary")),
        name="ffn",
    )(h, wg, wu, wd, x1, g_post)


class _Tiles(NamedTuple):
    proj_rows: int = 512
    proj_sub: int = 256
    qkv_rows: int = 1024
    mla_tq: int = 1024
    mla_tk: int = 512
    mla_qps: int = 8
    swa_blocks: int = 8
    ffn_rows: int = 1024
    ffn_cols: int = 512


def _layer(x2, pos_row, p, t=_Tiles()):
    half = MLA_ROPE // 2
    inv_freq = ROPE_THETA ** (-jnp.arange(half, dtype=F32) / half)
    invf = jnp.broadcast_to(inv_freq[:, None], (half, t.qkv_rows))

    row = lambda a: a[None, :]
    cq, ckv, krt, sqt, sk, svt = _proj(x2, row(p["g_mix_pre"]), p["w_in"].T, row(p["g_cq"]),
                                       row(p["g_ckv"]), tm=t.proj_rows, sub=t.proj_sub)
    qt, k, vt = _qkv(cq, ckv, krt, pos_row, invf, p["w_uq"], p["w_ukv"], tm=t.qkv_rows)
    out_a, (w_o, w_gate, w_up, w_down) = _mla(
        qt, k, vt, [p["w_o"], p["w_gate"], p["w_up"], p["w_down"]],
        tq=t.mla_tq, tk=t.mla_tk, qps=t.mla_qps)
    out_b = _swa(sqt, sk, svt, p["sinks"], p["rel_bias"], nb=t.swa_blocks)
    x1, h = _oproj(out_a, out_b, w_o, x2, row(p["g_mix_post"]), row(p["g_ffn_pre"]),
                   tm=t.proj_rows, sub=t.proj_sub)
    return _ffn(h, w_gate, w_up, w_down, x1, row(p["g_ffn_post"]), tm=t.ffn_rows, tf=t.ffn_cols)


def kernel(x, positions, g_mix_pre, w_in, g_cq, g_ckv, w_uq, w_ukv, sinks, rel_bias,
           w_o, g_mix_post, g_ffn_pre, w_gate, w_up, w_down, g_ffn_post):
    b, s, d = x.shape
    assert b == 1, "the row-major (S, D) pipeline assumes a single sequence"
    n_in = Q_RANK + KV_RANK + MLA_ROPE + (SWA_HEADS + 2 * SWA_KV_HEADS) * SWA_DIM
    d_mix = MLA_HEADS * MLA_V + SWA_HEADS * SWA_DIM
    assert w_in.shape[1:] == (d, n_in) and w_o.shape[1:] == (d_mix, d), (w_in.shape, w_o.shape)
    assert w_uq.shape[1:] == (Q_RANK, MLA_HEADS * MLA_QK), w_uq.shape
    assert w_ukv.shape[1:] == (KV_RANK, MLA_HEADS * (MLA_NOPE + MLA_V)), w_ukv.shape
    assert rel_bias.shape == (NUM_BUCKETS, SWA_HEADS) and sinks.shape[1:] == (SWA_HEADS,)
    x2 = x.reshape(s, d)
    pos_row = positions.reshape(1, s)
    for layer in range(w_in.shape[0]):
        p = dict(g_mix_pre=g_mix_pre[layer], w_in=w_in[layer], g_cq=g_cq[layer],
                 g_ckv=g_ckv[layer], w_uq=w_uq[layer], w_ukv=w_ukv[layer],
                 sinks=sinks[layer], rel_bias=rel_bias, w_o=w_o[layer],
                 g_mix_post=g_mix_post[layer], g_ffn_pre=g_ffn_pre[layer],
                 w_gate=w_gate[layer], w_up=w_up[layer], w_down=w_down[layer],
                 g_ffn_post=g_ffn_post[layer])
        x2 = _layer(x2, pos_row, p)
    return x2.reshape(b, s, d)
```

```python
import functools
import math
from typing import NamedTuple

import jax
import jax.numpy as jnp
import numpy as np
from jax import lax
from jax.experimental import pallas as pl
from jax.experimental.pallas import tpu as pltpu

F32 = jnp.float32
BF16 = jnp.bfloat16

MLA_HEADS = 8
MLA_NOPE = 128
MLA_ROPE = 64
MLA_QK = MLA_NOPE + MLA_ROPE
MLA_V = 128
Q_RANK = 512
KV_RANK = 512
ROPE_THETA = 10000.0
SWA_HEADS = 16
SWA_KV_HEADS = 2
SWA_GROUP = SWA_HEADS // SWA_KV_HEADS
SWA_DIM = 64
WINDOW = 128
NUM_BUCKETS = 32
MAX_DISTANCE = 128
BLK = 128
RMS_EPS = 1e-6
NEG = -1e30
LOG2E = math.log2(math.e)
LAZY_MARGIN = 32.0

LANES = 128
ONES_ROWS = 16
VMEM_LIMIT = 60 * 1024 * 1024


def _cparams(sem):
    return pltpu.CompilerParams(dimension_semantics=sem, vmem_limit_bytes=VMEM_LIMIT)


_NT = (((1,), (1,)), ((), ()))


def _resident(shape):
    nd = len(shape)
    return pl.BlockSpec(shape, lambda *_: (0,) * nd, pipeline_mode=pl.Buffered(1))


def _proj_kernel(x_ref, g_ref, wt32_ref, gcq_ref, gckv_ref,
                 cq_ref, ckv_ref, krt_ref, sqt_ref, sk_ref, svt_ref, wt_ref, *, swa_scale, sub):
    nq = SWA_HEADS * SWA_DIM
    nkv = SWA_KV_HEADS * SWA_DIM
    nl = Q_RANK + KV_RANK

    @pl.when(pl.program_id(0) == 0)
    def _():
        wt_ref[...] = wt32_ref[...].astype(BF16)

    def latent_norm(c, gain_ref):
        rc = lax.rsqrt(jnp.sum(c * c, axis=-1, keepdims=True) * (1.0 / c.shape[-1]) + RMS_EPS)
        return (c * rc * gain_ref[...]).astype(BF16)

    for r0 in range(0, x_ref.shape[0], sub):
        rows = slice(r0, r0 + sub)
        x = x_ref[rows, :]
        r = lax.rsqrt(jnp.sum(x * x, axis=-1, keepdims=True) * (1.0 / x.shape[-1]) + RMS_EPS)
        h = (x * g_ref[...]).astype(BF16)
        yt = lax.dot_general(wt_ref[...], h, _NT, preferred_element_type=F32)
        c = yt[:nl].T * r
        cq_ref[rows, :] = latent_norm(c[:, :Q_RANK], gcq_ref)
        ckv_ref[rows, :] = latent_norm(c[:, Q_RANK:], gckv_ref)
        r_lanes = jnp.broadcast_to(r, (sub, LANES)).T[0:1, :]
        tail = yt[nl:] * r_lanes
        o = MLA_ROPE
        krt_ref[:, rows] = tail[:o]
        sqt_ref[:, rows] = (tail[o:o + nq] * swa_scale).astype(BF16)
        sk_ref[rows, :] = tail[o + nq:o + nq + nkv].T.astype(BF16)
        svt_ref[:, rows] = tail[o + nq + nkv:o + nq + 2 * nkv].astype(BF16)


def _proj(x2, g, w_in_t, g_cq, g_ckv, *, tm, sub):
    s, d = x2.shape
    nq = SWA_HEADS * SWA_DIM
    nkv = SWA_KV_HEADS * SWA_DIM
    row = lambda w: pl.BlockSpec((tm, w), lambda i: (i, 0))
    col = lambda r: pl.BlockSpec((r, tm), lambda i: (0, i))
    return pl.pallas_call(
        functools.partial(_proj_kernel, swa_scale=LOG2E / math.sqrt(SWA_DIM), sub=sub),
        grid=(s // tm,),
        in_specs=[row(d), _resident((1, d)), _resident(w_in_t.shape),
                  _resident((1, Q_RANK)), _resident((1, KV_RANK))],
        out_specs=[row(Q_RANK), row(KV_RANK), col(MLA_ROPE), col(nq), row(nkv), col(nkv)],
        out_shape=[jax.ShapeDtypeStruct((s, Q_RANK), BF16),
                   jax.ShapeDtypeStruct((s, KV_RANK), BF16),
                   jax.ShapeDtypeStruct((MLA_ROPE, s), F32),
                   jax.ShapeDtypeStruct((nq, s), BF16),
                   jax.ShapeDtypeStruct((s, nkv), BF16),
                   jax.ShapeDtypeStruct((nkv, s), BF16)],
        scratch_shapes=[pltpu.VMEM(w_in_t.shape, BF16)],
        compiler_params=_cparams(("arbitrary",)),
        name="proj",
    )(x2, g, w_in_t, g_cq, g_ckv)


def _qkv_kernel(cq_ref, ckv_ref, krt_ref, pos_ref, invf_ref, wuq_ref, wukv_ref,
                qt_ref, k_ref, vt_ref, wuqt_ref, wuk_ref, wuvt_ref, *, q_scale):
    tm = cq_ref.shape[0]
    half = MLA_ROPE // 2

    @pl.when(pl.program_id(0) == 0)
    def _():
        wuqt_ref[...] = wuq_ref[...].T.astype(BF16)
        for h in range(MLA_HEADS):
            c0 = h * (MLA_NOPE + MLA_V)
            wuk_ref[:, h * MLA_NOPE:(h + 1) * MLA_NOPE] = wukv_ref[:, c0:c0 + MLA_NOPE].astype(BF16)
            wuvt_ref[h * MLA_V:(h + 1) * MLA_V, :] = (
                wukv_ref[:, c0 + MLA_NOPE:c0 + MLA_NOPE + MLA_V].T.astype(BF16))
    ang = invf_ref[...] * pos_ref[...].astype(F32)
    cos = jnp.cos(ang)
    sin = jnp.sin(ang)

    def rope_t(t):
        t1, t2 = t[:half], t[half:]
        return jnp.concatenate([t1 * cos - t2 * sin, t2 * cos + t1 * sin], axis=0)

    cq = cq_ref[...]
    ckv = ckv_ref[...]
    qt = lax.dot_general(wuqt_ref[...], cq, _NT, preferred_element_type=F32) * q_scale
    for h in range(MLA_HEADS):
        base = h * MLA_QK
        qt_ref[h, 0:MLA_NOPE, :] = qt[base:base + MLA_NOPE].astype(BF16)
        qt_ref[h, MLA_NOPE:MLA_QK, :] = rope_t(qt[base + MLA_NOPE:base + MLA_QK]).astype(BF16)

    krt = jnp.concatenate([rope_t(krt_ref[...]), jnp.zeros((LANES - MLA_ROPE, tm), F32)], axis=0)
    kr = krt.T[:, :MLA_ROPE].astype(BF16)
    kn = jnp.dot(ckv, wuk_ref[...], preferred_element_type=F32)
    vt = lax.dot_general(wuvt_ref[...], ckv, _NT, preferred_element_type=F32)
    for h in range(MLA_HEADS):
        k_ref[h, :, 0:MLA_NOPE] = kn[:, h * MLA_NOPE:(h + 1) * MLA_NOPE].astype(BF16)
        k_ref[h, :, MLA_NOPE:MLA_QK] = kr
        for c in range(tm // LANES):
            vt_ref[h, c] = vt[h * MLA_V:(h + 1) * MLA_V, c * LANES:(c + 1) * LANES].astype(BF16)


def _qkv(cq, ckv, krt, pos_row, invf, w_uq, w_ukv, *, tm):
    s = cq.shape[0]
    rank = w_uq.shape[0]
    row = lambda w: pl.BlockSpec((tm, w), lambda i: (i, 0))
    col = lambda r: pl.BlockSpec((r, tm), lambda i: (0, i))
    nb = tm // LANES
    return pl.pallas_call(
        functools.partial(_qkv_kernel, q_scale=LOG2E / math.sqrt(MLA_QK)),
        grid=(s // tm,),
        in_specs=[row(Q_RANK), row(KV_RANK), col(MLA_ROPE), col(1), _resident(invf.shape),
                  _resident(w_uq.shape), _resident(w_ukv.shape)],
        out_specs=[pl.BlockSpec((MLA_HEADS, MLA_QK, tm), lambda i: (0, 0, i)),
                   pl.BlockSpec((MLA_HEADS, tm, MLA_QK), lambda i: (0, i, 0)),
                   pl.BlockSpec((MLA_HEADS, nb, MLA_V, LANES), lambda i: (0, i, 0, 0))],
        out_shape=[jax.ShapeDtypeStruct((MLA_HEADS, MLA_QK, s), BF16),
                   jax.ShapeDtypeStruct((MLA_HEADS, s, MLA_QK), BF16),
                   jax.ShapeDtypeStruct((MLA_HEADS, s // LANES, MLA_V, LANES), BF16)],
        scratch_shapes=[pltpu.VMEM((MLA_HEADS * MLA_QK, rank), BF16),
                        pltpu.VMEM((rank, MLA_HEADS * MLA_NOPE), BF16),
                        pltpu.VMEM((MLA_HEADS * MLA_V, rank), BF16)],
        compiler_params=_cparams(("arbitrary",)),
        name="qkv",
    )(cq, ckv, krt, pos_row, invf, w_uq, w_ukv)


def _mla_kernel(*refs, tq, tk, ncast):
    qt_ref, k_ref, vt_ref = refs[:3]
    w_hbm = refs[3:3 + ncast]
    o_ref = refs[3 + ncast]
    wout_hbm = refs[4 + ncast:4 + 2 * ncast]
    m_sc, acc_sc, s_sc, smax_sc, excess_sc = refs[4 + 2 * ncast:9 + 2 * ncast]
    stage_in = refs[9 + 2 * ncast:9 + 3 * ncast]
    stage_out = refs[9 + 3 * ncast:9 + 4 * ncast]
    in_sems, out_sems = refs[9 + 4 * ncast:]

    step = pl.program_id(0)
    last_step = pl.num_programs(0) - 1

    def slab_in(w):
        rows = stage_in[w].shape[0]
        src = w_hbm[w].at[pl.ds(pl.multiple_of(step * rows, rows), rows), :]
        return pltpu.make_async_copy(src, stage_in[w], in_sems.at[w])

    def slab_out(w, at_step):
        rows = stage_out[w].shape[0]
        dst = wout_hbm[w].at[pl.ds(pl.multiple_of(at_step * rows, rows), rows), :]
        return pltpu.make_async_copy(stage_out[w], dst, out_sems.at[w])

    for w in range(ncast):
        slab_in(w).start()

    group = tq // tk
    assert group * tk == tq and group % 2 == 0
    nsub = tk // LANES
    full = slice(0, tq)
    lazy_unroll = 4

    def one_query_block(qi):
        q0 = qi * tq
        def reset():
            m_sc[...] = jnp.full(m_sc.shape, NEG, F32)
            acc_sc[...] = jnp.zeros(acc_sc.shape, F32)

        def raw_scores(j, cols=full):
            start = j * tk if isinstance(j, int) else pl.multiple_of(j * tk, tk)
            qcols = slice(q0 + cols.start, q0 + cols.stop)
            return jnp.dot(k_ref[0, pl.ds(start, tk), :], qt_ref[0, :, qcols],
                           preferred_element_type=F32)

        def values_t(j):
            vt = jnp.concatenate([vt_ref[0, j * nsub + c] for c in range(nsub)], axis=1)
            return jnp.concatenate([vt, jnp.ones((ONES_ROWS, tk), BF16)], axis=0)

        def scores(j, slot, cols=full):
            s = raw_scores(j, cols)
            s_sc[slot, :, cols] = s
            smax_sc[slot, :, cols] = jnp.max(s, axis=0, keepdims=True)

        def update(j, slot, cols=full, tri=False):
            s = s_sc[slot, :, cols]
            if tri:
                mask = (lax.broadcasted_iota(jnp.int32, (tk, tk), 0)
                        <= lax.broadcasted_iota(jnp.int32, (tk, tk), 1))
                parts = [jnp.where(mask, s[:, :tk], NEG)] + ([s[:, tk:]] if s.shape[1] > tk else [])
                s = jnp.concatenate(parts, axis=1)
            s_max = jnp.max(s, axis=0, keepdims=True) if tri else smax_sc[slot, :, cols]
            m_old = m_sc[:, cols]
            m_new = jnp.maximum(m_old, s_max)
            alpha = jnp.exp2(m_old - m_new)
            p = jnp.exp2(s - m_new)
            acc_sc[:, cols] = (alpha * acc_sc[:, cols]
                               + jnp.dot(values_t(j), p.astype(BF16), preferred_element_type=F32))
            m_sc[:, cols] = m_new

        def full_group(i, carry):
            for r in range(group):
                b = group * i + r
                scores(b + 1, (r + 1) % 2)
                update(b, r % 2)
            return carry

        def diagonal_group():
            for r in range(group):
                b = group * qi + r
                if r + 1 < group:
                    scores(b + 1, (r + 1) % 2, slice((r + 1) * tk, tq))
                update(b, r % 2, slice(r * tk, tq), tri=True)

        def finalize():
            out_t = acc_sc[:MLA_V, :] / acc_sc[MLA_V:MLA_V + 1, :]
            o_ref[q0:q0 + tq, :] = out_t.T.astype(o_ref.dtype)

        reset()
        excess_sc[...] = jnp.full(excess_sc.shape, NEG, F32)
        scores(group * qi, 0)
        diagonal_group()

        def lazy_blocks(first, count):
            m = m_sc[...]
            for r in range(count):
                j = first + r
                s = raw_scores(j)
                excess_sc[...] = jnp.maximum(excess_sc[...], jnp.max(s, axis=0, keepdims=True) - m)
                acc_sc[...] += jnp.dot(values_t(j), jnp.exp2(s - m).astype(BF16),
                                       preferred_element_type=F32)

        def lazy_trip(i, carry):
            lazy_blocks(lazy_unroll * i, lazy_unroll)
            return carry

        trips, rest = divmod(group * qi, lazy_unroll)
        if trips:
            lax.fori_loop(0, trips, lazy_trip, 0)
        if rest:
            lazy_blocks(trips * lazy_unroll, rest)
        finalize()

        @pl.when(jnp.max(excess_sc[...]) > LAZY_MARGIN)
        def _():
            reset()
            scores(0, 0)
            if qi:
                lax.fori_loop(0, qi, full_group, 0)
            diagonal_group()
            finalize()

    for qi in range(qt_ref.shape[2] // tq):
        one_query_block(qi)

    @pl.when(step > 0)
    def _():
        for w in range(ncast):
            slab_out(w, step - 1).wait()

    for w in range(ncast):
        slab_in(w).wait()
        stage_out[w][...] = stage_in[w][...].astype(BF16)
        slab_out(w, step).start()

    @pl.when(step == last_step)
    def _():
        for w in range(ncast):
            slab_out(w, step).wait()


def _mla(qt, k, vt, cast_weights, *, tq, tk):
    _, s, _ = k.shape
    assert s % tq == 0, (s, tq)
    nsteps = MLA_HEADS
    ncast = len(cast_weights)
    slabs = [(w.shape[0] // nsteps, w.shape[1]) for w in cast_weights]
    for w, (rows, _) in zip(cast_weights, slabs):
        assert rows * nsteps == w.shape[0] and rows % 16 == 0, (w.shape, nsteps)
    anywhere = pl.BlockSpec(memory_space=pl.ANY)
    outs = pl.pallas_call(
        functools.partial(_mla_kernel, tq=tq, tk=tk, ncast=ncast),
        grid=(MLA_HEADS,),
        in_specs=[pl.BlockSpec((1, MLA_QK, s), lambda h: (h, 0, 0)),
                  pl.BlockSpec((1, s, MLA_QK), lambda h: (h, 0, 0)),
                  pl.BlockSpec((1, s // LANES, MLA_V, LANES), lambda h: (h, 0, 0, 0))]
                 + [anywhere] * ncast,
        out_specs=[pl.BlockSpec((s, MLA_V), lambda h: (0, h))] + [anywhere] * ncast,
        out_shape=[jax.ShapeDtypeStruct((s, MLA_HEADS * MLA_V), BF16)]
                  + [jax.ShapeDtypeStruct(w.shape, BF16) for w in cast_weights],
        scratch_shapes=[pltpu.VMEM((1, tq), F32),
                        pltpu.VMEM((MLA_V + ONES_ROWS, tq), F32),
                        pltpu.VMEM((2, tk, tq), F32), pltpu.VMEM((2, 1, tq), F32),
                        pltpu.VMEM((1, tq), F32)]
                       + [pltpu.VMEM(sl, F32) for sl in slabs]
                       + [pltpu.VMEM(sl, BF16) for sl in slabs]
                       + [pltpu.SemaphoreType.DMA((ncast,)), pltpu.SemaphoreType.DMA((ncast,))],
        compiler_params=_cparams(("arbitrary",)),
        name="mla",
    )(qt, k, vt, *cast_weights)
    return outs[0], outs[1:]


def _t5_bucket_table_t():
    i = np.arange(BLK)[None, :]
    j = np.arange(2 * BLK)[:, None]
    dist = i + BLK - j
    max_exact = NUM_BUCKETS // 2
    d = np.maximum(dist, 0)
    large = max_exact + (np.log(np.maximum(d, 1) / max_exact)
                         / np.log(MAX_DISTANCE / max_exact)
                         * (NUM_BUCKETS - max_exact)).astype(np.int32)
    large = np.minimum(large, NUM_BUCKETS - 1)
    bucket = np.where(d < max_exact, d, large).astype(np.int32)
    in_window = (dist >= 0) & (dist < WINDOW)
    return np.where(in_window, bucket, -1).astype(np.int32)


def _swa_kernel(relb_ref, sink_ref, bucket_ref, qt_ref, kc_ref, kp_ref, vtc_ref, vtp_ref,
                o_ref, bias_sc, sink_sc, *, nb):
    i = pl.program_id(0)
    gw = SWA_GROUP * BLK

    @pl.when(i == 0)
    def _():
        bucket = bucket_ref[...]
        for h in range(SWA_HEADS):
            c, g = divmod(h, SWA_GROUP)
            b = jnp.full(bucket.shape, NEG, F32)
            for t in range(NUM_BUCKETS):
                b = jnp.where(bucket == t, relb_ref[t, h] * LOG2E, b)
            bias_sc[1, c, :, g * BLK:(g + 1) * BLK] = b
            bias_sc[0, c, BLK:, g * BLK:(g + 1) * BLK] = b[BLK:]
            bias_sc[0, c, :BLK, g * BLK:(g + 1) * BLK] = jnp.full((BLK, BLK), NEG, F32)
            sink_sc[c, :, g * BLK:(g + 1) * BLK] = jnp.full((1, BLK), sink_ref[h] * LOG2E, F32)

    zeros = jnp.zeros((SWA_DIM, gw), BF16)
    for t in range(nb):
        tsl = slice(t * BLK, (t + 1) * BLK)
        k_prev = kp_ref[...] if t == 0 else kc_ref[(t - 1) * BLK:t * BLK, :]
        vt_prev = vtp_ref[...] if t == 0 else vtc_ref[:, (t - 1) * BLK:t * BLK]
        kband = jnp.concatenate([k_prev, kc_ref[tsl, :]], axis=0)
        general = 1 if t > 0 else jnp.where(i == 0, 0, 1)
        for c in range(SWA_KV_HEADS):
            qt = jnp.concatenate(
                [qt_ref[(c * SWA_GROUP + g) * SWA_DIM:(c * SWA_GROUP + g + 1) * SWA_DIM, tsl]
                 for g in range(SWA_GROUP)], axis=1)
            qt_ext = jnp.concatenate([qt, zeros] if c == 0 else [zeros, qt], axis=0)
            s = jnp.dot(kband, qt_ext, preferred_element_type=F32) + bias_sc[general, c]
            sink = sink_sc[c]
            m = jnp.maximum(jnp.max(s, axis=0, keepdims=True), sink)
            p = jnp.exp2(s - m)
            dsl = slice(c * SWA_DIM, (c + 1) * SWA_DIM)
            vt = jnp.concatenate([vt_prev[dsl, :], vtc_ref[dsl, tsl]], axis=1)
            vt = jnp.concatenate([vt, jnp.ones((ONES_ROWS, 2 * BLK), BF16)], axis=0)
            pv = jnp.dot(vt, p.astype(BF16), preferred_element_type=F32)
            denom = pv[SWA_DIM:SWA_DIM + 1] + jnp.exp2(sink - m)
            ot = pv[:SWA_DIM] / denom
            for g2 in range(SWA_GROUP // 2):
                two = jnp.concatenate([ot[:, (2 * g2) * BLK:(2 * g2 + 1) * BLK],
                                       ot[:, (2 * g2 + 1) * BLK:(2 * g2 + 2) * BLK]], axis=0)
                col0 = (c * SWA_GROUP + 2 * g2) * SWA_DIM
                o_ref[tsl, col0:col0 + 2 * SWA_DIM] = two.T.astype(o_ref.dtype)


def _swa(sqt, sk, svt, sinks, rel_bias, *, nb):
    nq, s = sqt.shape
    nkv = sk.shape[1]
    tb = nb * BLK
    bucket = jnp.asarray(_t5_bucket_table_t())
    smem = pl.BlockSpec(memory_space=pltpu.SMEM)
    prev_blk = lambda i: jnp.maximum(i * nb - 1, 0)
    return pl.pallas_call(
        functools.partial(_swa_kernel, nb=nb),
        grid=(s // tb,),
        in_specs=[smem, smem, _resident((2 * BLK, BLK)),
                  pl.BlockSpec((nq, tb), lambda i: (0, i)),
                  pl.BlockSpec((tb, nkv), lambda i: (i, 0)),
                  pl.BlockSpec((BLK, nkv), lambda i: (prev_blk(i), 0)),
                  pl.BlockSpec((nkv, tb), lambda i: (0, i)),
                  pl.BlockSpec((nkv, BLK), lambda i: (0, prev_blk(i)))],
        out_specs=pl.BlockSpec((tb, nq), lambda i: (i, 0)),
        out_shape=jax.ShapeDtypeStruct((s, nq), BF16),
        scratch_shapes=[pltpu.VMEM((2, SWA_KV_HEADS, 2 * BLK, SWA_GROUP * BLK), F32),
                        pltpu.VMEM((SWA_KV_HEADS, 1, SWA_GROUP * BLK), F32)],
        compiler_params=_cparams(("arbitrary",)),
        name="swa",
    )(rel_bias, sinks, bucket, sqt, sk, sk, svt, svt)


def _rms(y, gain):
    r = lax.rsqrt(jnp.sum(y * y, axis=-1, keepdims=True) * (1.0 / y.shape[-1]) + RMS_EPS)
    return y * r * gain


def _oproj_kernel(a_ref, b_ref, wo_ref, x_ref, gpost_ref, gpre_ref, x1_ref, h_ref, *, sub):
    na = a_ref.shape[1]
    for r0 in range(0, x_ref.shape[0], sub):
        rows = slice(r0, r0 + sub)
        mix = (jnp.dot(a_ref[rows, :], wo_ref[:na, :], preferred_element_type=F32)
               + jnp.dot(b_ref[rows, :], wo_ref[na:, :], preferred_element_type=F32))
        x1 = x_ref[rows, :] + _rms(mix, gpost_ref[...])
        x1_ref[rows, :] = x1
        h_ref[rows, :] = _rms(x1, gpre_ref[...]).astype(BF16)


def _oproj(out_a, out_b, w_o, x2, g_post, g_pre, *, tm, sub):
    s, d = x2.shape
    row = lambda w: pl.BlockSpec((tm, w), lambda i: (i, 0))
    return pl.pallas_call(
        functools.partial(_oproj_kernel, sub=sub),
        grid=(s // tm,),
        in_specs=[row(out_a.shape[1]), row(out_b.shape[1]), _resident(w_o.shape),
                  row(d), _resident((1, d)), _resident((1, d))],
        out_specs=[row(d), row(d)],
        out_shape=[jax.ShapeDtypeStruct((s, d), F32), jax.ShapeDtypeStruct((s, d), BF16)],
        compiler_params=_cparams(("arbitrary",)),
        name="oproj",
    )(out_a, out_b, w_o, x2, g_post, g_pre)


def _ffn_kernel(h_ref, wg_ref, wu_ref, wd_ref, x1_hbm, gpost_ref, o_ref, x1_buf, x1_sem):
    i = pl.program_id(0)
    j = pl.program_id(1)
    tm = o_ref.shape[0]

    def x1_copy():
        rows = pl.ds(pl.multiple_of(i * tm, tm), tm)
        return pltpu.make_async_copy(x1_hbm.at[rows, :], x1_buf, x1_sem)

    @pl.when(j == 0)
    def _():
        x1_copy().start()
        o_ref[...] = jnp.zeros(o_ref.shape, F32)

    h = h_ref[...]
    tf = wg_ref.shape[1]
    halves = [slice(0, tf // 2), slice(tf // 2, tf)]
    gu = [(jnp.dot(h, wg_ref[:, c], preferred_element_type=F32),
           jnp.dot(h, wu_ref[:, c], preferred_element_type=F32)) for c in halves]
    for c, (gate, up) in zip(halves, gu):
        act = (gate * jax.nn.sigmoid(gate) * up).astype(BF16)
        o_ref[...] += jnp.dot(act, wd_ref[c, :], preferred_element_type=F32)

    @pl.when(j == pl.num_programs(1) - 1)
    def _():
        x1_copy().wait()
        o_ref[...] = x1_buf[...] + _rms(o_ref[...], gpost_ref[...])


def _ffn(h, wg, wu, wd, x1, g_post, *, tm, tf):
    s, d = x1.shape
    dff = wg.shape[1]
    return pl.pallas_call(
        _ffn_kernel,
        grid=(s // tm, dff // tf),
        in_specs=[pl.BlockSpec((tm, d), lambda i, j: (i, 0)),
                  pl.BlockSpec((d, tf), lambda i, j: (0, j)),
                  pl.BlockSpec((d, tf), lambda i, j: (0, j)),
                  pl.BlockSpec((tf, d), lambda i, j: (j, 0)),
                  pl.BlockSpec(memory_space=pl.ANY),
                  _resident((1, d))],
        out_specs=pl.BlockSpec((tm, d), lambda i, j: (i, 0)),
        out_shape=jax.ShapeDtypeStruct((s, d), F32),
        scratch_shapes=[pltpu.VMEM((tm, d), F32), pltpu.SemaphoreType.DMA(())],
        compiler_params=_cparams(("arbitrary", "arbitrary")),
        name="ffn",
    )(h, wg, wu, wd, x1, g_post)


class _Tiles(NamedTuple):
    proj_rows: int = 512
    proj_sub: int = 256
    qkv_rows: int = 1024
    mla_tq: int = 1024
    mla_tk: int = 512
    swa_blocks: int = 8
    ffn_rows: int = 1024
    ffn_cols: int = 512


def _layer(x2, pos_row, p, t=_Tiles()):
    half = MLA_ROPE // 2
    inv_freq = ROPE_THETA ** (-jnp.arange(half, dtype=F32) / half)
    invf = jnp.broadcast_to(inv_freq[:, None], (half, t.qkv_rows))

    row = lambda a: a[None, :]
    cq, ckv, krt, sqt, sk, svt = _proj(x2, row(p["g_mix_pre"]), p["w_in"].T, row(p["g_cq"]),
                                       row(p["g_ckv"]), tm=t.proj_rows, sub=t.proj_sub)
    qt, k, vt = _qkv(cq, ckv, krt, pos_row, invf, p["w_uq"], p["w_ukv"], tm=t.qkv_rows)
    out_a, (w_o, w_gate, w_up, w_down) = _mla(
        qt, k, vt, [p["w_o"], p["w_gate"], p["w_up"], p["w_down"]],
        tq=t.mla_tq, tk=t.mla_tk)
    out_b = _swa(sqt, sk, svt, p["sinks"], p["rel_bias"], nb=t.swa_blocks)
    x1, h = _oproj(out_a, out_b, w_o, x2, row(p["g_mix_post"]), row(p["g_ffn_pre"]),
                   tm=t.proj_rows, sub=t.proj_sub)
    return _ffn(h, w_gate, w_up, w_down, x1, row(p["g_ffn_post"]), tm=t.ffn_rows, tf=t.ffn_cols)


def kernel(x, positions, g_mix_pre, w_in, g_cq, g_ckv, w_uq, w_ukv, sinks, rel_bias,
           w_o, g_mix_post, g_ffn_pre, w_gate, w_up, w_down, g_ffn_post):
    b, s, d = x.shape
    assert b == 1, "the row-major (S, D) pipeline assumes a single sequence"
    n_in = Q_RANK + KV_RANK + MLA_ROPE + (SWA_HEADS + 2 * SWA_KV_HEADS) * SWA_DIM
    d_mix = MLA_HEADS * MLA_V + SWA_HEADS * SWA_DIM
    assert w_in.shape[1:] == (d, n_in) and w_o.shape[1:] == (d_mix, d), (w_in.shape, w_o.shape)
    assert w_uq.shape[1:] == (Q_RANK, MLA_HEADS * MLA_QK), w_uq.shape
    assert w_ukv.shape[1:] == (KV_RANK, MLA_HEADS * (MLA_NOPE + MLA_V)), w_ukv.shape
    assert rel_bias.shape == (NUM_BUCKETS, SWA_HEADS) and sinks.shape[1:] == (SWA_HEADS,)
    x2 = x.reshape(s, d)
    pos_row = positions.reshape(1, s)
    for layer in range(w_in.shape[0]):
        p = dict(g_mix_pre=g_mix_pre[layer], w_in=w_in[layer], g_cq=g_cq[layer],
                 g_ckv=g_ckv[layer], w_uq=w_uq[layer], w_ukv=w_ukv[layer],
                 sinks=sinks[layer], rel_bias=rel_bias, w_o=w_o[layer],
                 g_mix_post=g_mix_post[layer], g_ffn_pre=g_ffn_pre[layer],
                 w_gate=w_gate[layer], w_up=w_up[layer], w_down=w_down[layer],
                 g_ffn_post=g_ffn_post[layer])
        x2 = _layer(x2, pos_row, p)
    return x2.reshape(b, s, d)
```

```python
import functools
import math
from typing import NamedTuple

import jax
import jax.numpy as jnp
import numpy as np
from jax import lax
from jax.experimental import pallas as pl
from jax.experimental.pallas import tpu as pltpu

F32 = jnp.float32
BF16 = jnp.bfloat16

MLA_HEADS = 8
MLA_NOPE = 128
MLA_ROPE = 64
MLA_QK = MLA_NOPE + MLA_ROPE
MLA_V = 128
Q_RANK = 512
KV_RANK = 512
ROPE_THETA = 10000.0
SWA_HEADS = 16
SWA_KV_HEADS = 2
SWA_GROUP = SWA_HEADS // SWA_KV_HEADS
SWA_DIM = 64
WINDOW = 128
NUM_BUCKETS = 32
MAX_DISTANCE = 128
BLK = 128
RMS_EPS = 1e-6
NEG = -1e30
LOG2E = math.log2(math.e)
LAZY_MARGIN = 32.0

LANES = 128
ONES_ROWS = 16
VMEM_LIMIT = 60 * 1024 * 1024


def _cparams(sem):
    return pltpu.CompilerParams(dimension_semantics=sem, vmem_limit_bytes=VMEM_LIMIT)


_NT = (((1,), (1,)), ((), ()))


def _resident(shape):
    nd = len(shape)
    return pl.BlockSpec(shape, lambda *_: (0,) * nd, pipeline_mode=pl.Buffered(1))


def _proj_kernel(x_ref, g_ref, wt32_ref, gcq_ref, gckv_ref,
                 cq_ref, ckv_ref, krt_ref, sqt_ref, sk_ref, svt_ref, wt_ref, *, swa_scale, sub):
    nq = SWA_HEADS * SWA_DIM
    nkv = SWA_KV_HEADS * SWA_DIM
    nl = Q_RANK + KV_RANK

    @pl.when(pl.program_id(0) == 0)
    def _():
        wt_ref[...] = wt32_ref[...].astype(BF16)

    def latent_norm(c, gain_ref):
        rc = lax.rsqrt(jnp.sum(c * c, axis=-1, keepdims=True) * (1.0 / c.shape[-1]) + RMS_EPS)
        return (c * rc * gain_ref[...]).astype(BF16)

    for r0 in range(0, x_ref.shape[0], sub):
        rows = slice(r0, r0 + sub)
        x = x_ref[rows, :]
        r = lax.rsqrt(jnp.sum(x * x, axis=-1, keepdims=True) * (1.0 / x.shape[-1]) + RMS_EPS)
        h = (x * g_ref[...]).astype(BF16)
        yt = lax.dot_general(wt_ref[...], h, _NT, preferred_element_type=F32)
        c = yt[:nl].T * r
        cq_ref[rows, :] = latent_norm(c[:, :Q_RANK], gcq_ref)
        ckv_ref[rows, :] = latent_norm(c[:, Q_RANK:], gckv_ref)
        r_lanes = jnp.broadcast_to(r, (sub, LANES)).T[0:1, :]
        tail = yt[nl:] * r_lanes
        o = MLA_ROPE
        krt_ref[:, rows] = tail[:o]
        sqt_ref[:, rows] = (tail[o:o + nq] * swa_scale).astype(BF16)
        sk_ref[rows, :] = tail[o + nq:o + nq + nkv].T.astype(BF16)
        svt_ref[:, rows] = tail[o + nq + nkv:o + nq + 2 * nkv].astype(BF16)


def _proj(x2, g, w_in_t, g_cq, g_ckv, *, tm, sub):
    s, d = x2.shape
    nq = SWA_HEADS * SWA_DIM
    nkv = SWA_KV_HEADS * SWA_DIM
    row = lambda w: pl.BlockSpec((tm, w), lambda i: (i, 0))
    col = lambda r: pl.BlockSpec((r, tm), lambda i: (0, i))
    return pl.pallas_call(
        functools.partial(_proj_kernel, swa_scale=LOG2E / math.sqrt(SWA_DIM), sub=sub),
        grid=(s // tm,),
        in_specs=[row(d), _resident((1, d)), _resident(w_in_t.shape),
                  _resident((1, Q_RANK)), _resident((1, KV_RANK))],
        out_specs=[row(Q_RANK), row(KV_RANK), col(MLA_ROPE), col(nq), row(nkv), col(nkv)],
        out_shape=[jax.ShapeDtypeStruct((s, Q_RANK), BF16),
                   jax.ShapeDtypeStruct((s, KV_RANK), BF16),
                   jax.ShapeDtypeStruct((MLA_ROPE, s), F32),
                   jax.ShapeDtypeStruct((nq, s), BF16),
                   jax.ShapeDtypeStruct((s, nkv), BF16),
                   jax.ShapeDtypeStruct((nkv, s), BF16)],
        scratch_shapes=[pltpu.VMEM(w_in_t.shape, BF16)],
        compiler_params=_cparams(("arbitrary",)),
        name="proj",
    )(x2, g, w_in_t, g_cq, g_ckv)


def _qkv_kernel(cq_ref, ckv_ref, krt_ref, pos_ref, invf_ref, wuq_ref, wukv_ref,
                qt_ref, k_ref, vt_ref, wuqt_ref, wuk_ref, wuvt_ref, *, q_scale):
    tm = cq_ref.shape[0]
    half = MLA_ROPE // 2

    @pl.when(pl.program_id(0) == 0)
    def _():
        wuqt_ref[...] = wuq_ref[...].T.astype(BF16)
        for h in range(MLA_HEADS):
            c0 = h * (MLA_NOPE + MLA_V)
            wuk_ref[:, h * MLA_NOPE:(h + 1) * MLA_NOPE] = wukv_ref[:, c0:c0 + MLA_NOPE].astype(BF16)
            wuvt_ref[h * MLA_V:(h + 1) * MLA_V, :] = (
                wukv_ref[:, c0 + MLA_NOPE:c0 + MLA_NOPE + MLA_V].T.astype(BF16))
    ang = invf_ref[...] * pos_ref[...].astype(F32)
    cos = jnp.cos(ang)
    sin = jnp.sin(ang)

    def rope_t(t):
        t1, t2 = t[:half], t[half:]
        return jnp.concatenate([t1 * cos - t2 * sin, t2 * cos + t1 * sin], axis=0)

    cq = cq_ref[...]
    ckv = ckv_ref[...]
    qt = lax.dot_general(wuqt_ref[...], cq, _NT, preferred_element_type=F32) * q_scale
    for h in range(MLA_HEADS):
        base = h * MLA_QK
        qt_ref[h, 0:MLA_NOPE, :] = qt[base:base + MLA_NOPE].astype(BF16)
        qt_ref[h, MLA_NOPE:MLA_QK, :] = rope_t(qt[base + MLA_NOPE:base + MLA_QK]).astype(BF16)

    krt = jnp.concatenate([rope_t(krt_ref[...]), jnp.zeros((LANES - MLA_ROPE, tm), F32)], axis=0)
    kr = krt.T[:, :MLA_ROPE].astype(BF16)
    kn = jnp.dot(ckv, wuk_ref[...], preferred_element_type=F32)
    vt = lax.dot_general(wuvt_ref[...], ckv, _NT, preferred_element_type=F32)
    for h in range(MLA_HEADS):
        k_ref[h, :, 0:MLA_NOPE] = kn[:, h * MLA_NOPE:(h + 1) * MLA_NOPE].astype(BF16)
        k_ref[h, :, MLA_NOPE:MLA_QK] = kr
        for c in range(tm // LANES):
            vt_ref[h, c] = vt[h * MLA_V:(h + 1) * MLA_V, c * LANES:(c + 1) * LANES].astype(BF16)


def _qkv(cq, ckv, krt, pos_row, invf, w_uq, w_ukv, *, tm):
    s = cq.shape[0]
    rank = w_uq.shape[0]
    row = lambda w: pl.BlockSpec((tm, w), lambda i: (i, 0))
    col = lambda r: pl.BlockSpec((r, tm), lambda i: (0, i))
    nb = tm // LANES
    return pl.pallas_call(
        functools.partial(_qkv_kernel, q_scale=LOG2E / math.sqrt(MLA_QK)),
        grid=(s // tm,),
        in_specs=[row(Q_RANK), row(KV_RANK), col(MLA_ROPE), col(1), _resident(invf.shape),
                  _resident(w_uq.shape), _resident(w_ukv.shape)],
        out_specs=[pl.BlockSpec((MLA_HEADS, MLA_QK, tm), lambda i: (0, 0, i)),
                   pl.BlockSpec((MLA_HEADS, tm, MLA_QK), lambda i: (0, i, 0)),
                   pl.BlockSpec((MLA_HEADS, nb, MLA_V, LANES), lambda i: (0, i, 0, 0))],
        out_shape=[jax.ShapeDtypeStruct((MLA_HEADS, MLA_QK, s), BF16),
                   jax.ShapeDtypeStruct((MLA_HEADS, s, MLA_QK), BF16),
                   jax.ShapeDtypeStruct((MLA_HEADS, s // LANES, MLA_V, LANES), BF16)],
        scratch_shapes=[pltpu.VMEM((MLA_HEADS * MLA_QK, rank), BF16),
                        pltpu.VMEM((rank, MLA_HEADS * MLA_NOPE), BF16),
                        pltpu.VMEM((MLA_HEADS * MLA_V, rank), BF16)],
        compiler_params=_cparams(("arbitrary",)),
        name="qkv",
    )(cq, ckv, krt, pos_row, invf, w_uq, w_ukv)


def _mla_kernel(*refs, tq, tk, ncast):
    qt_ref, k_ref, vt_ref = refs[:3]
    w_hbm = refs[3:3 + ncast]
    o_ref = refs[3 + ncast]
    wout_hbm = refs[4 + ncast:4 + 2 * ncast]
    m_sc, acc_sc, s_sc, smax_sc, excess_sc = refs[4 + 2 * ncast:9 + 2 * ncast]
    stage_in = refs[9 + 2 * ncast:9 + 3 * ncast]
    stage_out = refs[9 + 3 * ncast:9 + 4 * ncast]
    in_sems, out_sems = refs[9 + 4 * ncast:]

    step = pl.program_id(0)
    last_step = pl.num_programs(0) - 1

    def slab_in(w):
        rows = stage_in[w].shape[0]
        src = w_hbm[w].at[pl.ds(pl.multiple_of(step * rows, rows), rows), :]
        return pltpu.make_async_copy(src, stage_in[w], in_sems.at[w])

    def slab_out(w, at_step):
        rows = stage_out[w].shape[0]
        dst = wout_hbm[w].at[pl.ds(pl.multiple_of(at_step * rows, rows), rows), :]
        return pltpu.make_async_copy(stage_out[w], dst, out_sems.at[w])

    for w in range(ncast):
        slab_in(w).start()

    group = tq // tk
    assert group * tk == tq and group % 2 == 0
    nsub = tk // LANES
    full = slice(0, tq)
    lazy_unroll = 4

    def one_query_block(qi, safe):
        def at(i, size):
            return pl.ds(i * size if isinstance(i, int) else pl.multiple_of(i * size, size), size)

        def queries(cols=full):
            if isinstance(qi, int):
                return slice(qi * tq + cols.start, qi * tq + cols.stop)
            return pl.ds(pl.multiple_of(qi * tq + cols.start, tk), cols.stop - cols.start)

        def reset():
            m_sc[...] = jnp.full(m_sc.shape, NEG, F32)
            acc_sc[...] = jnp.zeros(acc_sc.shape, F32)

        def raw_scores(j, cols=full):
            return jnp.dot(k_ref[0, at(j, tk), :], qt_ref[0, :, queries(cols)],
                           preferred_element_type=F32)

        def values_t(j):
            vt = jnp.concatenate([vt_ref[0, j * nsub + c] for c in range(nsub)], axis=1)
            return jnp.concatenate([vt, jnp.ones((ONES_ROWS, tk), BF16)], axis=0)

        def scores(j, slot, cols=full):
            s = raw_scores(j, cols)
            s_sc[slot, :, cols] = s
            smax_sc[slot, :, cols] = jnp.max(s, axis=0, keepdims=True)

        def update(j, slot, cols=full, tri=False):
            s = s_sc[slot, :, cols]
            if tri:
                mask = (lax.broadcasted_iota(jnp.int32, (tk, tk), 0)
                        <= lax.broadcasted_iota(jnp.int32, (tk, tk), 1))
                parts = [jnp.where(mask, s[:, :tk], NEG)] + ([s[:, tk:]] if s.shape[1] > tk else [])
                s = jnp.concatenate(parts, axis=1)
            s_max = jnp.max(s, axis=0, keepdims=True) if tri else smax_sc[slot, :, cols]
            m_old = m_sc[:, cols]
            m_new = jnp.maximum(m_old, s_max)
            alpha = jnp.exp2(m_old - m_new)
            p = jnp.exp2(s - m_new)
            acc_sc[:, cols] = (alpha * acc_sc[:, cols]
                               + jnp.dot(values_t(j), p.astype(BF16), preferred_element_type=F32))
            m_sc[:, cols] = m_new

        def full_group(i, carry):
            for r in range(group):
                b = group * i + r
                scores(b + 1, (r + 1) % 2)
                update(b, r % 2)
            return carry

        def diagonal_group():
            for r in range(group):
                b = group * qi + r
                if r + 1 < group:
                    scores(b + 1, (r + 1) % 2, slice((r + 1) * tk, tq))
                update(b, r % 2, slice(r * tk, tq), tri=True)

        def finalize():
            out_t = acc_sc[:MLA_V, :] / acc_sc[MLA_V:MLA_V + 1, :]
            o_ref[at(qi, tq), :] = out_t.T.astype(o_ref.dtype)

        if safe:
            reset()
            scores(0, 0)
            lax.fori_loop(0, qi, full_group, 0)
            diagonal_group()
            finalize()
            return

        reset()
        excess_sc[:, queries()] = jnp.full((1, tq), NEG, F32)
        scores(group * qi, 0)
        diagonal_group()

        def lazy_blocks(first, count):
            m = m_sc[...]
            for r in range(count):
                j = first + r
                s = raw_scores(j)
                excess_sc[:, queries()] = jnp.maximum(excess_sc[:, queries()],
                                                      jnp.max(s, axis=0, keepdims=True) - m)
                acc_sc[...] += jnp.dot(values_t(j), jnp.exp2(s - m).astype(BF16),
                                       preferred_element_type=F32)

        def lazy_trip(i, carry):
            lazy_blocks(lazy_unroll * i, lazy_unroll)
            return carry

        trips, rest = divmod(group * qi, lazy_unroll)
        if trips:
            lax.fori_loop(0, trips, lazy_trip, 0)
        if rest:
            lazy_blocks(trips * lazy_unroll, rest)
        finalize()

    nq = qt_ref.shape[2] // tq
    for qi in range(nq):
        one_query_block(qi, safe=False)

    def redo_if_needed(qi, carry):
        @pl.when(jnp.max(excess_sc[:, pl.ds(pl.multiple_of(qi * tq, tq), tq)]) > LAZY_MARGIN)
        def _():
            one_query_block(qi, safe=True)
        return carry

    lax.fori_loop(0, nq, redo_if_needed, 0)

    @pl.when(step > 0)
    def _():
        for w in range(ncast):
            slab_out(w, step - 1).wait()

    for w in range(ncast):
        slab_in(w).wait()
        stage_out[w][...] = stage_in[w][...].astype(BF16)
        slab_out(w, step).start()

    @pl.when(step == last_step)
    def _():
        for w in range(ncast):
            slab_out(w, step).wait()


def _mla(qt, k, vt, cast_weights, *, tq, tk):
    _, s, _ = k.shape
    assert s % tq == 0, (s, tq)
    nsteps = MLA_HEADS
    ncast = len(cast_weights)
    slabs = [(w.shape[0] // nsteps, w.shape[1]) for w in cast_weights]
    for w, (rows, _) in zip(cast_weights, slabs):
        assert rows * nsteps == w.shape[0] and rows % 16 == 0, (w.shape, nsteps)
    anywhere = pl.BlockSpec(memory_space=pl.ANY)
    outs = pl.pallas_call(
        functools.partial(_mla_kernel, tq=tq, tk=tk, ncast=ncast),
        grid=(MLA_HEADS,),
        in_specs=[pl.BlockSpec((1, MLA_QK, s), lambda h: (h, 0, 0)),
                  pl.BlockSpec((1, s, MLA_QK), lambda h: (h, 0, 0)),
                  pl.BlockSpec((1, s // LANES, MLA_V, LANES), lambda h: (h, 0, 0, 0))]
                 + [anywhere] * ncast,
        out_specs=[pl.BlockSpec((s, MLA_V), lambda h: (0, h))] + [anywhere] * ncast,
        out_shape=[jax.ShapeDtypeStruct((s, MLA_HEADS * MLA_V), BF16)]
                  + [jax.ShapeDtypeStruct(w.shape, BF16) for w in cast_weights],
        scratch_shapes=[pltpu.VMEM((1, tq), F32),
                        pltpu.VMEM((MLA_V + ONES_ROWS, tq), F32),
                        pltpu.VMEM((2, tk, tq), F32), pltpu.VMEM((2, 1, tq), F32),
                        pltpu.VMEM((1, s), F32)]
                       + [pltpu.VMEM(sl, F32) for sl in slabs]
                       + [pltpu.VMEM(sl, BF16) for sl in slabs]
                       + [pltpu.SemaphoreType.DMA((ncast,)), pltpu.SemaphoreType.DMA((ncast,))],
        compiler_params=_cparams(("arbitrary",)),
        name="mla",
    )(qt, k, vt, *cast_weights)
    return outs[0], outs[1:]


def _t5_bucket_table_t():
    i = np.arange(BLK)[None, :]
    j = np.arange(2 * BLK)[:, None]
    dist = i + BLK - j
    max_exact = NUM_BUCKETS // 2
    d = np.maximum(dist, 0)
    large = max_exact + (np.log(np.maximum(d, 1) / max_exact)
                         / np.log(MAX_DISTANCE / max_exact)
                         * (NUM_BUCKETS - max_exact)).astype(np.int32)
    large = np.minimum(large, NUM_BUCKETS - 1)
    bucket = np.where(d < max_exact, d, large).astype(np.int32)
    in_window = (dist >= 0) & (dist < WINDOW)
    return np.where(in_window, bucket, -1).astype(np.int32)


def _swa_kernel(relb_ref, sink_ref, bucket_ref, qt_ref, kc_ref, kp_ref, vtc_ref, vtp_ref,
                o_ref, bias_sc, sink_sc, *, nb):
    i = pl.program_id(0)
    gw = SWA_GROUP * BLK

    @pl.when(i == 0)
    def _():
        bucket = bucket_ref[...]
        for h in range(SWA_HEADS):
            c, g = divmod(h, SWA_GROUP)
            b = jnp.full(bucket.shape, NEG, F32)
            for t in range(NUM_BUCKETS):
                b = jnp.where(bucket == t, relb_ref[t, h] * LOG2E, b)
            bias_sc[1, c, :, g * BLK:(g + 1) * BLK] = b
            bias_sc[0, c, BLK:, g * BLK:(g + 1) * BLK] = b[BLK:]
            bias_sc[0, c, :BLK, g * BLK:(g + 1) * BLK] = jnp.full((BLK, BLK), NEG, F32)
            sink_sc[c, :, g * BLK:(g + 1) * BLK] = jnp.full((1, BLK), sink_ref[h] * LOG2E, F32)

    zeros = jnp.zeros((SWA_DIM, gw), BF16)
    for t in range(nb):
        tsl = slice(t * BLK, (t + 1) * BLK)
        k_prev = kp_ref[...] if t == 0 else kc_ref[(t - 1) * BLK:t * BLK, :]
        vt_prev = vtp_ref[...] if t == 0 else vtc_ref[:, (t - 1) * BLK:t * BLK]
        kband = jnp.concatenate([k_prev, kc_ref[tsl, :]], axis=0)
        general = 1 if t > 0 else jnp.where(i == 0, 0, 1)
        for c in range(SWA_KV_HEADS):
            qt = jnp.concatenate(
                [qt_ref[(c * SWA_GROUP + g) * SWA_DIM:(c * SWA_GROUP + g + 1) * SWA_DIM, tsl]
                 for g in range(SWA_GROUP)], axis=1)
            qt_ext = jnp.concatenate([qt, zeros] if c == 0 else [zeros, qt], axis=0)
            s = jnp.dot(kband, qt_ext, preferred_element_type=F32) + bias_sc[general, c]
            sink = sink_sc[c]
            m = jnp.maximum(jnp.max(s, axis=0, keepdims=True), sink)
            p = jnp.exp2(s - m)
            dsl = slice(c * SWA_DIM, (c + 1) * SWA_DIM)
            vt = jnp.concatenate([vt_prev[dsl, :], vtc_ref[dsl, tsl]], axis=1)
            vt = jnp.concatenate([vt, jnp.ones((ONES_ROWS, 2 * BLK), BF16)], axis=0)
            pv = jnp.dot(vt, p.astype(BF16), preferred_element_type=F32)
            denom = pv[SWA_DIM:SWA_DIM + 1] + jnp.exp2(sink - m)
            ot = pv[:SWA_DIM] / denom
            for g2 in range(SWA_GROUP // 2):
                two = jnp.concatenate([ot[:, (2 * g2) * BLK:(2 * g2 + 1) * BLK],
                                       ot[:, (2 * g2 + 1) * BLK:(2 * g2 + 2) * BLK]], axis=0)
                col0 = (c * SWA_GROUP + 2 * g2) * SWA_DIM
                o_ref[tsl, col0:col0 + 2 * SWA_DIM] = two.T.astype(o_ref.dtype)


def _swa(sqt, sk, svt, sinks, rel_bias, *, nb):
    nq, s = sqt.shape
    nkv = sk.shape[1]
    tb = nb * BLK
    bucket = jnp.asarray(_t5_bucket_table_t())
    smem = pl.BlockSpec(memory_space=pltpu.SMEM)
    prev_blk = lambda i: jnp.maximum(i * nb - 1, 0)
    return pl.pallas_call(
        functools.partial(_swa_kernel, nb=nb),
        grid=(s // tb,),
        in_specs=[smem, smem, _resident((2 * BLK, BLK)),
                  pl.BlockSpec((nq, tb), lambda i: (0, i)),
                  pl.BlockSpec((tb, nkv), lambda i: (i, 0)),
                  pl.BlockSpec((BLK, nkv), lambda i: (prev_blk(i), 0)),
                  pl.BlockSpec((nkv, tb), lambda i: (0, i)),
                  pl.BlockSpec((nkv, BLK), lambda i: (0, prev_blk(i)))],
        out_specs=pl.BlockSpec((tb, nq), lambda i: (i, 0)),
        out_shape=jax.ShapeDtypeStruct((s, nq), BF16),
        scratch_shapes=[pltpu.VMEM((2, SWA_KV_HEADS, 2 * BLK, SWA_GROUP * BLK), F32),
                        pltpu.VMEM((SWA_KV_HEADS, 1, SWA_GROUP * BLK), F32)],
        compiler_params=_cparams(("arbitrary",)),
        name="swa",
    )(rel_bias, sinks, bucket, sqt, sk, sk, svt, svt)


def _rms(y, gain):
    r = lax.rsqrt(jnp.sum(y * y, axis=-1, keepdims=True) * (1.0 / y.shape[-1]) + RMS_EPS)
    return y * r * gain


def _oproj_kernel(a_ref, b_ref, wo_ref, x_ref, gpost_ref, gpre_ref, x1_ref, h_ref, *, sub):
    na = a_ref.shape[1]
    for r0 in range(0, x_ref.shape[0], sub):
        rows = slice(r0, r0 + sub)
        mix = (jnp.dot(a_ref[rows, :], wo_ref[:na, :], preferred_element_type=F32)
               + jnp.dot(b_ref[rows, :], wo_ref[na:, :], preferred_element_type=F32))
        x1 = x_ref[rows, :] + _rms(mix, gpost_ref[...])
        x1_ref[rows, :] = x1
        h_ref[rows, :] = _rms(x1, gpre_ref[...]).astype(BF16)


def _oproj(out_a, out_b, w_o, x2, g_post, g_pre, *, tm, sub):
    s, d = x2.shape
    row = lambda w: pl.BlockSpec((tm, w), lambda i: (i, 0))
    return pl.pallas_call(
        functools.partial(_oproj_kernel, sub=sub),
        grid=(s // tm,),
        in_specs=[row(out_a.shape[1]), row(out_b.shape[1]), _resident(w_o.shape),
                  row(d), _resident((1, d)), _resident((1, d))],
        out_specs=[row(d), row(d)],
        out_shape=[jax.ShapeDtypeStruct((s, d), F32), jax.ShapeDtypeStruct((s, d), BF16)],
        compiler_params=_cparams(("arbitrary",)),
        name="oproj",
    )(out_a, out_b, w_o, x2, g_post, g_pre)


def _ffn_kernel(h_ref, wg_ref, wu_ref, wd_ref, x1_hbm, gpost_ref, o_ref, x1_buf, x1_sem):
    i = pl.program_id(0)
    j = pl.program_id(1)
    tm = o_ref.shape[0]

    def x1_copy():
        rows = pl.ds(pl.multiple_of(i * tm, tm), tm)
        return pltpu.make_async_copy(x1_hbm.at[rows, :], x1_buf, x1_sem)

    @pl.when(j == 0)
    def _():
        x1_copy().start()
        o_ref[...] = jnp.zeros(o_ref.shape, F32)

    h = h_ref[...]
    tf = wg_ref.shape[1]
    halves = [slice(0, tf // 2), slice(tf // 2, tf)]
    gu = [(jnp.dot(h, wg_ref[:, c], preferred_element_type=F32),
           jnp.dot(h, wu_ref[:, c], preferred_element_type=F32)) for c in halves]
    for c, (gate, up) in zip(halves, gu):
        act = (gate * jax.nn.sigmoid(gate) * up).astype(BF16)
        o_ref[...] += jnp.dot(act, wd_ref[c, :], preferred_element_type=F32)

    @pl.when(j == pl.num_programs(1) - 1)
    def _():
        x1_copy().wait()
        o_ref[...] = x1_buf[...] + _rms(o_ref[...], gpost_ref[...])


def _ffn(h, wg, wu, wd, x1, g_post, *, tm, tf):
    s, d = x1.shape
    dff = wg.shape[1]
    return pl.pallas_call(
        _ffn_kernel,
        grid=(s // tm, dff // tf),
        in_specs=[pl.BlockSpec((tm, d), lambda i, j: (i, 0)),
                  pl.BlockSpec((d, tf), lambda i, j: (0, j)),
                  pl.BlockSpec((d, tf), lambda i, j: (0, j)),
                  pl.BlockSpec((tf, d), lambda i, j: (j, 0)),
                  pl.BlockSpec(memory_space=pl.ANY),
                  _resident((1, d))],
        out_specs=pl.BlockSpec((tm, d), lambda i, j: (i, 0)),
        out_shape=jax.ShapeDtypeStruct((s, d), F32),
        scratch_shapes=[pltpu.VMEM((tm, d), F32), pltpu.SemaphoreType.DMA(())],
        compiler_params=_cparams(("arbitrary", "arbitrary")),
        name="ffn",
    )(h, wg, wu, wd, x1, g_post)


class _Tiles(NamedTuple):
    proj_rows: int = 512
    proj_sub: int = 256
    qkv_rows: int = 1024
    mla_tq: int = 1024
    mla_tk: int = 512
    swa_blocks: int = 8
    ffn_rows: int = 1024
    ffn_cols: int = 512


def _layer(x2, pos_row, p, t=_Tiles()):
    half = MLA_ROPE // 2
    inv_freq = ROPE_THETA ** (-jnp.arange(half, dtype=F32) / half)
    invf = jnp.broadcast_to(inv_freq[:, None], (half, t.qkv_rows))

    row = lambda a: a[None, :]
    cq, ckv, krt, sqt, sk, svt = _proj(x2, row(p["g_mix_pre"]), p["w_in"].T, row(p["g_cq"]),
                                       row(p["g_ckv"]), tm=t.proj_rows, sub=t.proj_sub)
    qt, k, vt = _qkv(cq, ckv, krt, pos_row, invf, p["w_uq"], p["w_ukv"], tm=t.qkv_rows)
    out_a, (w_o, w_gate, w_up, w_down) = _mla(
        qt, k, vt, [p["w_o"], p["w_gate"], p["w_up"], p["w_down"]],
        tq=t.mla_tq, tk=t.mla_tk)
    out_b = _swa(sqt, sk, svt, p["sinks"], p["rel_bias"], nb=t.swa_blocks)
    x1, h = _oproj(out_a, out_b, w_o, x2, row(p["g_mix_post"]), row(p["g_ffn_pre"]),
                   tm=t.proj_rows, sub=t.proj_sub)
    return _ffn(h, w_gate, w_up, w_down, x1, row(p["g_ffn_post"]), tm=t.ffn_rows, tf=t.ffn_cols)


def kernel(x, positions, g_mix_pre, w_in, g_cq, g_ckv, w_uq, w_ukv, sinks, rel_bias,
           w_o, g_mix_post, g_ffn_pre, w_gate, w_up, w_down, g_ffn_post):
    b, s, d = x.shape
    assert b == 1, "the row-major (S, D) pipeline assumes a single sequence"
    n_in = Q_RANK + KV_RANK + MLA_ROPE + (SWA_HEADS + 2 * SWA_KV_HEADS) * SWA_DIM
    d_mix = MLA_HEADS * MLA_V + SWA_HEADS * SWA_DIM
    assert w_in.shape[1:] == (d, n_in) and w_o.shape[1:] == (d_mix, d), (w_in.shape, w_o.shape)
    assert w_uq.shape[1:] == (Q_RANK, MLA_HEADS * MLA_QK), w_uq.shape
    assert w_ukv.shape[1:] == (KV_RANK, MLA_HEADS * (MLA_NOPE + MLA_V)), w_ukv.shape
    assert rel_bias.shape == (NUM_BUCKETS, SWA_HEADS) and sinks.shape[1:] == (SWA_HEADS,)
    x2 = x.reshape(s, d)
    pos_row = positions.reshape(1, s)
    for layer in range(w_in.shape[0]):
        p = dict(g_mix_pre=g_mix_pre[layer], w_in=w_in[layer], g_cq=g_cq[layer],
                 g_ckv=g_ckv[layer], w_uq=w_uq[layer], w_ukv=w_ukv[layer],
                 sinks=sinks[layer], rel_bias=rel_bias, w_o=w_o[layer],
                 g_mix_post=g_mix_post[layer], g_ffn_pre=g_ffn_pre[layer],
                 w_gate=w_gate[layer], w_up=w_up[layer], w_down=w_down[layer],
                 g_ffn_post=g_ffn_post[layer])
        x2 = _layer(x2, pos_row, p)
    return x2.reshape(b, s, d)
```

```python
import functools
import math
from typing import NamedTuple

import jax
import jax.numpy as jnp
import numpy as np
from jax import lax
from jax.experimental import pallas as pl
from jax.experimental.pallas import tpu as pltpu

F32 = jnp.float32
BF16 = jnp.bfloat16

MLA_HEADS = 8
MLA_NOPE = 128
MLA_ROPE = 64
MLA_QK = MLA_NOPE + MLA_ROPE
MLA_V = 128
Q_RANK = 512
KV_RANK = 512
ROPE_THETA = 10000.0
SWA_HEADS = 16
SWA_KV_HEADS = 2
SWA_GROUP = SWA_HEADS // SWA_KV_HEADS
SWA_DIM = 64
WINDOW = 128
NUM_BUCKETS = 32
MAX_DISTANCE = 128
BLK = 128
RMS_EPS = 1e-6
NEG = -1e30
LOG2E = math.log2(math.e)
LAZY_MARGIN = 32.0

LANES = 128
ONES_ROWS = 16
VMEM_LIMIT = 60 * 1024 * 1024


def _cparams(sem):
    return pltpu.CompilerParams(dimension_semantics=sem, vmem_limit_bytes=VMEM_LIMIT)


_NT = (((1,), (1,)), ((), ()))


def _resident(shape):
    nd = len(shape)
    return pl.BlockSpec(shape, lambda *_: (0,) * nd, pipeline_mode=pl.Buffered(1))


def _proj_kernel(x_ref, g_ref, wt32_ref, gcq_ref, gckv_ref,
                 cq_ref, ckv_ref, krt_ref, sqt_ref, sk_ref, svt_ref, wt_ref, *, swa_scale, sub):
    nq = SWA_HEADS * SWA_DIM
    nkv = SWA_KV_HEADS * SWA_DIM
    nl = Q_RANK + KV_RANK

    @pl.when(pl.program_id(0) == 0)
    def _():
        wt_ref[...] = wt32_ref[...].astype(BF16)

    def latent_norm(c, gain_ref):
        rc = lax.rsqrt(jnp.sum(c * c, axis=-1, keepdims=True) * (1.0 / c.shape[-1]) + RMS_EPS)
        return (c * rc * gain_ref[...]).astype(BF16)

    for r0 in range(0, x_ref.shape[0], sub):
        rows = slice(r0, r0 + sub)
        x = x_ref[rows, :]
        r = lax.rsqrt(jnp.sum(x * x, axis=-1, keepdims=True) * (1.0 / x.shape[-1]) + RMS_EPS)
        h = (x * g_ref[...]).astype(BF16)
        yt = lax.dot_general(wt_ref[...], h, _NT, preferred_element_type=F32)
        c = yt[:nl].T * r
        cq_ref[rows, :] = latent_norm(c[:, :Q_RANK], gcq_ref)
        ckv_ref[rows, :] = latent_norm(c[:, Q_RANK:], gckv_ref)
        r_lanes = jnp.broadcast_to(r, (sub, LANES)).T[0:1, :]
        tail = yt[nl:] * r_lanes
        o = MLA_ROPE
        krt_ref[:, rows] = tail[:o]
        sqt_ref[:, rows] = (tail[o:o + nq] * swa_scale).astype(BF16)
        sk_ref[rows, :] = tail[o + nq:o + nq + nkv].T.astype(BF16)
        svt_ref[:, rows] = tail[o + nq + nkv:o + nq + 2 * nkv].astype(BF16)


def _proj(x2, g, w_in_t, g_cq, g_ckv, *, tm, sub):
    s, d = x2.shape
    nq = SWA_HEADS * SWA_DIM
    nkv = SWA_KV_HEADS * SWA_DIM
    row = lambda w: pl.BlockSpec((tm, w), lambda i: (i, 0))
    col = lambda r: pl.BlockSpec((r, tm), lambda i: (0, i))
    return pl.pallas_call(
        functools.partial(_proj_kernel, swa_scale=LOG2E / math.sqrt(SWA_DIM), sub=sub),
        grid=(s // tm,),
        in_specs=[row(d), _resident((1, d)), _resident(w_in_t.shape),
                  _resident((1, Q_RANK)), _resident((1, KV_RANK))],
        out_specs=[row(Q_RANK), row(KV_RANK), col(MLA_ROPE), col(nq), row(nkv), col(nkv)],
        out_shape=[jax.ShapeDtypeStruct((s, Q_RANK), BF16),
                   jax.ShapeDtypeStruct((s, KV_RANK), BF16),
                   jax.ShapeDtypeStruct((MLA_ROPE, s), F32),
                   jax.ShapeDtypeStruct((nq, s), BF16),
                   jax.ShapeDtypeStruct((s, nkv), BF16),
                   jax.ShapeDtypeStruct((nkv, s), BF16)],
        scratch_shapes=[pltpu.VMEM(w_in_t.shape, BF16)],
        compiler_params=_cparams(("arbitrary",)),
        name="proj",
    )(x2, g, w_in_t, g_cq, g_ckv)


def _qkv_kernel(cq_ref, ckv_ref, krt_ref, pos_ref, invf_ref, wuq_ref, wukv_ref,
                qt_ref, k_ref, vt_ref, wuqt_ref, wuk_ref, wuvt_ref, *, q_scale):
    tm = cq_ref.shape[0]
    half = MLA_ROPE // 2

    @pl.when(pl.program_id(0) == 0)
    def _():
        wuqt_ref[...] = wuq_ref[...].T.astype(BF16)
        for h in range(MLA_HEADS):
            c0 = h * (MLA_NOPE + MLA_V)
            wuk_ref[:, h * MLA_NOPE:(h + 1) * MLA_NOPE] = wukv_ref[:, c0:c0 + MLA_NOPE].astype(BF16)
            wuvt_ref[h * MLA_V:(h + 1) * MLA_V, :] = (
                wukv_ref[:, c0 + MLA_NOPE:c0 + MLA_NOPE + MLA_V].T.astype(BF16))
    ang = invf_ref[...] * pos_ref[...].astype(F32)
    cos = jnp.cos(ang)
    sin = jnp.sin(ang)

    def rope_t(t):
        t1, t2 = t[:half], t[half:]
        return jnp.concatenate([t1 * cos - t2 * sin, t2 * cos + t1 * sin], axis=0)

    cq = cq_ref[...]
    ckv = ckv_ref[...]
    qt = lax.dot_general(wuqt_ref[...], cq, _NT, preferred_element_type=F32) * q_scale
    for h in range(MLA_HEADS):
        base = h * MLA_QK
        qt_ref[h, 0:MLA_NOPE, :] = qt[base:base + MLA_NOPE].astype(BF16)
        qt_ref[h, MLA_NOPE:MLA_QK, :] = rope_t(qt[base + MLA_NOPE:base + MLA_QK]).astype(BF16)

    krt = jnp.concatenate([rope_t(krt_ref[...]), jnp.zeros((LANES - MLA_ROPE, tm), F32)], axis=0)
    kr = krt.T[:, :MLA_ROPE].astype(BF16)
    kn = jnp.dot(ckv, wuk_ref[...], preferred_element_type=F32)
    vt = lax.dot_general(wuvt_ref[...], ckv, _NT, preferred_element_type=F32)
    for h in range(MLA_HEADS):
        k_ref[h, :, 0:MLA_NOPE] = kn[:, h * MLA_NOPE:(h + 1) * MLA_NOPE].astype(BF16)
        k_ref[h, :, MLA_NOPE:MLA_QK] = kr
        for c in range(tm // LANES):
            vt_ref[h, c] = vt[h * MLA_V:(h + 1) * MLA_V, c * LANES:(c + 1) * LANES].astype(BF16)


def _qkv(cq, ckv, krt, pos_row, invf, w_uq, w_ukv, *, tm):
    s = cq.shape[0]
    rank = w_uq.shape[0]
    row = lambda w: pl.BlockSpec((tm, w), lambda i: (i, 0))
    col = lambda r: pl.BlockSpec((r, tm), lambda i: (0, i))
    nb = tm // LANES
    return pl.pallas_call(
        functools.partial(_qkv_kernel, q_scale=LOG2E / math.sqrt(MLA_QK)),
        grid=(s // tm,),
        in_specs=[row(Q_RANK), row(KV_RANK), col(MLA_ROPE), col(1), _resident(invf.shape),
                  _resident(w_uq.shape), _resident(w_ukv.shape)],
        out_specs=[pl.BlockSpec((MLA_HEADS, MLA_QK, tm), lambda i: (0, 0, i)),
                   pl.BlockSpec((MLA_HEADS, tm, MLA_QK), lambda i: (0, i, 0)),
                   pl.BlockSpec((MLA_HEADS, nb, MLA_V, LANES), lambda i: (0, i, 0, 0))],
        out_shape=[jax.ShapeDtypeStruct((MLA_HEADS, MLA_QK, s), BF16),
                   jax.ShapeDtypeStruct((MLA_HEADS, s, MLA_QK), BF16),
                   jax.ShapeDtypeStruct((MLA_HEADS, s // LANES, MLA_V, LANES), BF16)],
        scratch_shapes=[pltpu.VMEM((MLA_HEADS * MLA_QK, rank), BF16),
                        pltpu.VMEM((rank, MLA_HEADS * MLA_NOPE), BF16),
                        pltpu.VMEM((MLA_HEADS * MLA_V, rank), BF16)],
        compiler_params=_cparams(("arbitrary",)),
        name="qkv",
    )(cq, ckv, krt, pos_row, invf, w_uq, w_ukv)


def _mla_kernel(*refs, tq, tk, ncast):
    qt_ref, k_ref, vt_ref = refs[:3]
    w_hbm = refs[3:3 + ncast]
    o_ref = refs[3 + ncast]
    wout_hbm = refs[4 + ncast:4 + 2 * ncast]
    m_sc, acc_sc, s_sc, smax_sc, excess_sc = refs[4 + 2 * ncast:9 + 2 * ncast]
    stage_in = refs[9 + 2 * ncast:9 + 3 * ncast]
    stage_out = refs[9 + 3 * ncast:9 + 4 * ncast]
    in_sems, out_sems = refs[9 + 4 * ncast:]

    step = pl.program_id(0)
    last_step = pl.num_programs(0) - 1

    def slab_in(w):
        rows = stage_in[w].shape[0]
        src = w_hbm[w].at[pl.ds(pl.multiple_of(step * rows, rows), rows), :]
        return pltpu.make_async_copy(src, stage_in[w], in_sems.at[w])

    def slab_out(w, at_step):
        rows = stage_out[w].shape[0]
        dst = wout_hbm[w].at[pl.ds(pl.multiple_of(at_step * rows, rows), rows), :]
        return pltpu.make_async_copy(stage_out[w], dst, out_sems.at[w])

    for w in range(ncast):
        slab_in(w).start()

    group = tq // tk
    assert group * tk == tq and group % 2 == 0
    nsub = tk // LANES
    full = slice(0, tq)
    lazy_unroll = 4

    def one_query_block(qi, safe):
        def at(i, size):
            return pl.ds(i * size if isinstance(i, int) else pl.multiple_of(i * size, size), size)

        def queries(cols=full):
            if isinstance(qi, int):
                return slice(qi * tq + cols.start, qi * tq + cols.stop)
            return pl.ds(pl.multiple_of(qi * tq + cols.start, tk), cols.stop - cols.start)

        def reset():
            m_sc[...] = jnp.full(m_sc.shape, NEG, F32)
            acc_sc[...] = jnp.zeros(acc_sc.shape, F32)

        def raw_scores(j, cols=full):
            return jnp.dot(k_ref[0, at(j, tk), :], qt_ref[0, :, queries(cols)],
                           preferred_element_type=F32)

        def values_t(j):
            vt = jnp.concatenate([vt_ref[0, j * nsub + c] for c in range(nsub)], axis=1)
            return jnp.concatenate([vt, jnp.ones((ONES_ROWS, tk), BF16)], axis=0)

        def scores(j, slot, cols=full):
            s = raw_scores(j, cols)
            s_sc[slot, :, cols] = s
            smax_sc[slot, :, cols] = jnp.max(s, axis=0, keepdims=True)

        def update(j, slot, cols=full, tri=False):
            s = s_sc[slot, :, cols]
            if tri:
                mask = (lax.broadcasted_iota(jnp.int32, (tk, tk), 0)
                        <= lax.broadcasted_iota(jnp.int32, (tk, tk), 1))
                parts = [jnp.where(mask, s[:, :tk], NEG)] + ([s[:, tk:]] if s.shape[1] > tk else [])
                s = jnp.concatenate(parts, axis=1)
            s_max = jnp.max(s, axis=0, keepdims=True) if tri else smax_sc[slot, :, cols]
            m_old = m_sc[:, cols]
            m_new = jnp.maximum(m_old, s_max)
            alpha = jnp.exp2(m_old - m_new)
            p = jnp.exp2(s - m_new)
            acc_sc[:, cols] = (alpha * acc_sc[:, cols]
                               + jnp.dot(values_t(j), p.astype(BF16), preferred_element_type=F32))
            m_sc[:, cols] = m_new

        def full_group(i, carry):
            for r in range(group):
                b = group * i + r
                scores(b + 1, (r + 1) % 2)
                update(b, r % 2)
            return carry

        def diagonal_group():
            for r in range(group):
                b = group * qi + r
                if r + 1 < group:
                    scores(b + 1, (r + 1) % 2, slice((r + 1) * tk, tq))
                update(b, r % 2, slice(r * tk, tq), tri=True)

        def finalize():
            out_t = acc_sc[:MLA_V, :] / acc_sc[MLA_V:MLA_V + 1, :]
            o_ref[at(qi, tq), :] = out_t.T.astype(o_ref.dtype)

        if safe:
            reset()
            scores(0, 0)
            lax.fori_loop(0, qi, full_group, 0)
            diagonal_group()
            finalize()
            return

        reset()
        excess_sc[:, queries()] = jnp.full((1, tq), NEG, F32)
        scores(group * qi, 0)
        diagonal_group()

        def lazy_blocks(first, count):
            m = m_sc[...]
            for r in range(count):
                j = first + r
                s = raw_scores(j)
                excess_sc[:, queries()] = jnp.maximum(excess_sc[:, queries()],
                                                      jnp.max(s, axis=0, keepdims=True) - m)
                acc_sc[...] += jnp.dot(values_t(j), jnp.exp2(s - m).astype(BF16),
                                       preferred_element_type=F32)

        def lazy_trip(i, carry):
            lazy_blocks(lazy_unroll * i, lazy_unroll)
            return carry

        trips, rest = divmod(group * qi, lazy_unroll)
        if trips:
            lax.fori_loop(0, trips, lazy_trip, 0)
        if rest:
            lazy_blocks(trips * lazy_unroll, rest)
        finalize()

    nq = qt_ref.shape[2] // tq
    for qi in range(nq):
        one_query_block(qi, safe=False)

    def redo_if_needed(qi, carry):
        @pl.when(jnp.max(excess_sc[:, pl.ds(pl.multiple_of(qi * tq, tq), tq)]) > LAZY_MARGIN)
        def _():
            one_query_block(qi, safe=True)
        return carry

    lax.fori_loop(0, nq, redo_if_needed, 0)

    @pl.when(step > 0)
    def _():
        for w in range(ncast):
            slab_out(w, step - 1).wait()

    for w in range(ncast):
        slab_in(w).wait()
        stage_out[w][...] = stage_in[w][...].astype(BF16)
        slab_out(w, step).start()

    @pl.when(step == last_step)
    def _():
        for w in range(ncast):
            slab_out(w, step).wait()


def _mla(qt, k, vt, cast_weights, *, tq, tk):
    _, s, _ = k.shape
    assert s % tq == 0, (s, tq)
    nsteps = MLA_HEADS
    ncast = len(cast_weights)
    slabs = [(w.shape[0] // nsteps, w.shape[1]) for w in cast_weights]
    for w, (rows, _) in zip(cast_weights, slabs):
        assert rows * nsteps == w.shape[0] and rows % 16 == 0, (w.shape, nsteps)
    anywhere = pl.BlockSpec(memory_space=pl.ANY)
    outs = pl.pallas_call(
        functools.partial(_mla_kernel, tq=tq, tk=tk, ncast=ncast),
        grid=(MLA_HEADS,),
        in_specs=[pl.BlockSpec((1, MLA_QK, s), lambda h: (h, 0, 0)),
                  pl.BlockSpec((1, s, MLA_QK), lambda h: (h, 0, 0)),
                  pl.BlockSpec((1, s // LANES, MLA_V, LANES), lambda h: (h, 0, 0, 0))]
                 + [anywhere] * ncast,
        out_specs=[pl.BlockSpec((s, MLA_V), lambda h: (0, h))] + [anywhere] * ncast,
        out_shape=[jax.ShapeDtypeStruct((s, MLA_HEADS * MLA_V), BF16)]
                  + [jax.ShapeDtypeStruct(w.shape, BF16) for w in cast_weights],
        scratch_shapes=[pltpu.VMEM((1, tq), F32),
                        pltpu.VMEM((MLA_V + ONES_ROWS, tq), F32),
                        pltpu.VMEM((2, tk, tq), F32), pltpu.VMEM((2, 1, tq), F32),
                        pltpu.VMEM((1, s), F32)]
                       + [pltpu.VMEM(sl, F32) for sl in slabs]
                       + [pltpu.VMEM(sl, BF16) for sl in slabs]
                       + [pltpu.SemaphoreType.DMA((ncast,)), pltpu.SemaphoreType.DMA((ncast,))],
        compiler_params=_cparams(("arbitrary",)),
        name="mla",
    )(qt, k, vt, *cast_weights)
    return outs[0], outs[1:]


def _t5_bucket_table_t():
    i = np.arange(BLK)[None, :]
    j = np.arange(2 * BLK)[:, None]
    dist = i + BLK - j
    max_exact = NUM_BUCKETS // 2
    d = np.maximum(dist, 0)
    large = max_exact + (np.log(np.maximum(d, 1) / max_exact)
                         / np.log(MAX_DISTANCE / max_exact)
                         * (NUM_BUCKETS - max_exact)).astype(np.int32)
    large = np.minimum(large, NUM_BUCKETS - 1)
    bucket = np.where(d < max_exact, d, large).astype(np.int32)
    in_window = (dist >= 0) & (dist < WINDOW)
    return np.where(in_window, bucket, -1).astype(np.int32)


def _swa_kernel(relb_ref, sink_ref, bucket_ref, qt_ref, kc_ref, kp_ref, vtc_ref, vtp_ref,
                o_ref, bias_sc, sink_sc, *, nb):
    i = pl.program_id(0)
    gw = SWA_GROUP * BLK

    @pl.when(i == 0)
    def _():
        bucket = bucket_ref[...]
        for h in range(SWA_HEADS):
            c, g = divmod(h, SWA_GROUP)
            b = jnp.full(bucket.shape, NEG, F32)
            for t in range(NUM_BUCKETS):
                b = jnp.where(bucket == t, relb_ref[t, h] * LOG2E, b)
            bias_sc[1, c, :, g * BLK:(g + 1) * BLK] = b
            bias_sc[0, c, BLK:, g * BLK:(g + 1) * BLK] = b[BLK:]
            bias_sc[0, c, :BLK, g * BLK:(g + 1) * BLK] = jnp.full((BLK, BLK), NEG, F32)
            sink_sc[c, :, g * BLK:(g + 1) * BLK] = jnp.full((1, BLK), sink_ref[h] * LOG2E, F32)

    zeros = jnp.zeros((SWA_DIM, gw), BF16)
    for t in range(nb):
        tsl = slice(t * BLK, (t + 1) * BLK)
        k_prev = kp_ref[...] if t == 0 else kc_ref[(t - 1) * BLK:t * BLK, :]
        vt_prev = vtp_ref[...] if t == 0 else vtc_ref[:, (t - 1) * BLK:t * BLK]
        kband = jnp.concatenate([k_prev, kc_ref[tsl, :]], axis=0)
        general = 1 if t > 0 else jnp.where(i == 0, 0, 1)
        for c in range(SWA_KV_HEADS):
            qt = jnp.concatenate(
                [qt_ref[(c * SWA_GROUP + g) * SWA_DIM:(c * SWA_GROUP + g + 1) * SWA_DIM, tsl]
                 for g in range(SWA_GROUP)], axis=1)
            qt_ext = jnp.concatenate([qt, zeros] if c == 0 else [zeros, qt], axis=0)
            s = jnp.dot(kband, qt_ext, preferred_element_type=F32) + bias_sc[general, c]
            sink = sink_sc[c]
            m = jnp.maximum(jnp.max(s, axis=0, keepdims=True), sink)
            p = jnp.exp2(s - m)
            dsl = slice(c * SWA_DIM, (c + 1) * SWA_DIM)
            vt = jnp.concatenate([vt_prev[dsl, :], vtc_ref[dsl, tsl]], axis=1)
            vt = jnp.concatenate([vt, jnp.ones((ONES_ROWS, 2 * BLK), BF16)], axis=0)
            pv = jnp.dot(vt, p.astype(BF16), preferred_element_type=F32)
            denom = pv[SWA_DIM:SWA_DIM + 1] + jnp.exp2(sink - m)
            ot = pv[:SWA_DIM] / denom
            for g2 in range(SWA_GROUP // 2):
                two = jnp.concatenate([ot[:, (2 * g2) * BLK:(2 * g2 + 1) * BLK],
                                       ot[:, (2 * g2 + 1) * BLK:(2 * g2 + 2) * BLK]], axis=0)
                col0 = (c * SWA_GROUP + 2 * g2) * SWA_DIM
                o_ref[tsl, col0:col0 + 2 * SWA_DIM] = two.T.astype(o_ref.dtype)


def _swa(sqt, sk, svt, sinks, rel_bias, *, nb):
    nq, s = sqt.shape
    nkv = sk.shape[1]
    tb = nb * BLK
    bucket = jnp.asarray(_t5_bucket_table_t())
    smem = pl.BlockSpec(memory_space=pltpu.SMEM)
    prev_blk = lambda i: jnp.maximum(i * nb - 1, 0)
    return pl.pallas_call(
        functools.partial(_swa_kernel, nb=nb),
        grid=(s // tb,),
        in_specs=[smem, smem, _resident((2 * BLK, BLK)),
                  pl.BlockSpec((nq, tb), lambda i: (0, i)),
                  pl.BlockSpec((tb, nkv), lambda i: (i, 0)),
                  pl.BlockSpec((BLK, nkv), lambda i: (prev_blk(i), 0)),
                  pl.BlockSpec((nkv, tb), lambda i: (0, i)),
                  pl.BlockSpec((nkv, BLK), lambda i: (0, prev_blk(i)))],
        out_specs=pl.BlockSpec((tb, nq), lambda i: (i, 0)),
        out_shape=jax.ShapeDtypeStruct((s, nq), BF16),
        scratch_shapes=[pltpu.VMEM((2, SWA_KV_HEADS, 2 * BLK, SWA_GROUP * BLK), F32),
                        pltpu.VMEM((SWA_KV_HEADS, 1, SWA_GROUP * BLK), F32)],
        compiler_params=_cparams(("arbitrary",)),
        name="swa",
    )(rel_bias, sinks, bucket, sqt, sk, sk, svt, svt)


def _rms(y, gain):
    r = lax.rsqrt(jnp.sum(y * y, axis=-1, keepdims=True) * (1.0 / y.shape[-1]) + RMS_EPS)
    return y * r * gain


def _oproj_kernel(a_ref, b_ref, wo_ref, x_ref, gpost_ref, gpre_ref, x1_ref, h_ref, *, sub):
    na = a_ref.shape[1]
    for r0 in range(0, x_ref.shape[0], sub):
        rows = slice(r0, r0 + sub)
        mix = (jnp.dot(a_ref[rows, :], wo_ref[:na, :], preferred_element_type=F32)
               + jnp.dot(b_ref[rows, :], wo_ref[na:, :], preferred_element_type=F32))
        x1 = x_ref[rows, :] + _rms(mix, gpost_ref[...])
        x1_ref[rows, :] = x1
        h_ref[rows, :] = _rms(x1, gpre_ref[...]).astype(BF16)


def _oproj(out_a, out_b, w_o, x2, g_post, g_pre, *, tm, sub):
    s, d = x2.shape
    row = lambda w: pl.BlockSpec((tm, w), lambda i: (i, 0))
    return pl.pallas_call(
        functools.partial(_oproj_kernel, sub=sub),
        grid=(s // tm,),
        in_specs=[row(out_a.shape[1]), row(out_b.shape[1]), _resident(w_o.shape),
                  row(d), _resident((1, d)), _resident((1, d))],
        out_specs=[row(d), row(d)],
        out_shape=[jax.ShapeDtypeStruct((s, d), F32), jax.ShapeDtypeStruct((s, d), BF16)],
        compiler_params=_cparams(("arbitrary",)),
        name="oproj",
    )(out_a, out_b, w_o, x2, g_post, g_pre)


def _ffn_kernel(h_ref, wg_ref, wu_ref, wd_ref, x1_hbm, gpost_ref, o_ref, x1_buf, x1_sem, *, sub):
    i = pl.program_id(0)
    j = pl.program_id(1)
    tm = o_ref.shape[0]

    def x1_copy():
        rows = pl.ds(pl.multiple_of(i * tm, tm), tm)
        return pltpu.make_async_copy(x1_hbm.at[rows, :], x1_buf, x1_sem)

    @pl.when(j == 0)
    def _():
        x1_copy().start()
        o_ref[...] = jnp.zeros(o_ref.shape, F32)

    tf = wg_ref.shape[1]
    halves = [slice(0, tf // 2), slice(tf // 2, tf)]

    def down_products(rows):
        h = h_ref[rows, :]
        gu = [(jnp.dot(h, wg_ref[:, c], preferred_element_type=F32),
               jnp.dot(h, wu_ref[:, c], preferred_element_type=F32)) for c in halves]
        for c, (gate, up) in zip(halves, gu):
            act = (gate * jax.nn.sigmoid(gate) * up).astype(BF16)
            yield jnp.dot(act, wd_ref[c, :], preferred_element_type=F32)

    last = pl.num_programs(1) - 1

    @pl.when(j < last)
    def _():
        for term in down_products(slice(0, tm)):
            o_ref[...] += term

    @pl.when(j == last)
    def _():
        x1_copy().wait()
        for r in range(tm // sub):
            rows = slice(r * sub, (r + 1) * sub)
            total = o_ref[rows, :]
            for term in down_products(rows):
                total = total + term
            o_ref[rows, :] = x1_buf[rows, :] + _rms(total, gpost_ref[...])


def _ffn(h, wg, wu, wd, x1, g_post, *, tm, tf, sub):
    s, d = x1.shape
    dff = wg.shape[1]
    assert tm % sub == 0, (tm, sub)
    return pl.pallas_call(
        functools.partial(_ffn_kernel, sub=sub),
        grid=(s // tm, dff // tf),
        in_specs=[pl.BlockSpec((tm, d), lambda i, j: (i, 0)),
                  pl.BlockSpec((d, tf), lambda i, j: (0, j)),
                  pl.BlockSpec((d, tf), lambda i, j: (0, j)),
                  pl.BlockSpec((tf, d), lambda i, j: (j, 0)),
                  pl.BlockSpec(memory_space=pl.ANY),
                  _resident((1, d))],
        out_specs=pl.BlockSpec((tm, d), lambda i, j: (i, 0)),
        out_shape=jax.ShapeDtypeStruct((s, d), F32),
        scratch_shapes=[pltpu.VMEM((tm, d), F32), pltpu.SemaphoreType.DMA(())],
        compiler_params=_cparams(("arbitrary", "arbitrary")),
        name="ffn",
    )(h, wg, wu, wd, x1, g_post)


class _Tiles(NamedTuple):
    proj_rows: int = 512
    proj_sub: int = 256
    qkv_rows: int = 1024
    mla_tq: int = 1024
    mla_tk: int = 512
    swa_blocks: int = 8
    ffn_rows: int = 1024
    ffn_cols: int = 512
    ffn_sub: int = 256


def _layer(x2, pos_row, p, t=_Tiles()):
    half = MLA_ROPE // 2
    inv_freq = ROPE_THETA ** (-jnp.arange(half, dtype=F32) / half)
    invf = jnp.broadcast_to(inv_freq[:, None], (half, t.qkv_rows))

    row = lambda a: a[None, :]
    cq, ckv, krt, sqt, sk, svt = _proj(x2, row(p["g_mix_pre"]), p["w_in"].T, row(p["g_cq"]),
                                       row(p["g_ckv"]), tm=t.proj_rows, sub=t.proj_sub)
    qt, k, vt = _qkv(cq, ckv, krt, pos_row, invf, p["w_uq"], p["w_ukv"], tm=t.qkv_rows)
    out_a, (w_o, w_gate, w_up, w_down) = _mla(
        qt, k, vt, [p["w_o"], p["w_gate"], p["w_up"], p["w_down"]],
        tq=t.mla_tq, tk=t.mla_tk)
    out_b = _swa(sqt, sk, svt, p["sinks"], p["rel_bias"], nb=t.swa_blocks)
    x1, h = _oproj(out_a, out_b, w_o, x2, row(p["g_mix_post"]), row(p["g_ffn_pre"]),
                   tm=t.proj_rows, sub=t.proj_sub)
    return _ffn(h, w_gate, w_up, w_down, x1, row(p["g_ffn_post"]), tm=t.ffn_rows, tf=t.ffn_cols,
                sub=t.ffn_sub)


def kernel(x, positions, g_mix_pre, w_in, g_cq, g_ckv, w_uq, w_ukv, sinks, rel_bias,
           w_o, g_mix_post, g_ffn_pre, w_gate, w_up, w_down, g_ffn_post):
    b, s, d = x.shape
    assert b == 1, "the row-major (S, D) pipeline assumes a single sequence"
    n_in = Q_RANK + KV_RANK + MLA_ROPE + (SWA_HEADS + 2 * SWA_KV_HEADS) * SWA_DIM
    d_mix = MLA_HEADS * MLA_V + SWA_HEADS * SWA_DIM
    assert w_in.shape[1:] == (d, n_in) and w_o.shape[1:] == (d_mix, d), (w_in.shape, w_o.shape)
    assert w_uq.shape[1:] == (Q_RANK, MLA_HEADS * MLA_QK), w_uq.shape
    assert w_ukv.shape[1:] == (KV_RANK, MLA_HEADS * (MLA_NOPE + MLA_V)), w_ukv.shape
    assert rel_bias.shape == (NUM_BUCKETS, SWA_HEADS) and sinks.shape[1:] == (SWA_HEADS,)
    x2 = x.reshape(s, d)
    pos_row = positions.reshape(1, s)
    for layer in range(w_in.shape[0]):
        p = dict(g_mix_pre=g_mix_pre[layer], w_in=w_in[layer], g_cq=g_cq[layer],
                 g_ckv=g_ckv[layer], w_uq=w_uq[layer], w_ukv=w_ukv[layer],
                 sinks=sinks[layer], rel_bias=rel_bias, w_o=w_o[layer],
                 g_mix_post=g_mix_post[layer], g_ffn_pre=g_ffn_pre[layer],
                 w_gate=w_gate[layer], w_up=w_up[layer], w_down=w_down[layer],
                 g_ffn_post=g_ffn_post[layer])
        x2 = _layer(x2, pos_row, p)
    return x2.reshape(b, s, d)
```

```python
import functools
import math
from typing import NamedTuple

import jax
import jax.numpy as jnp
import numpy as np
from jax import lax
from jax.experimental import pallas as pl
from jax.experimental.pallas import tpu as pltpu

F32 = jnp.float32
BF16 = jnp.bfloat16

MLA_HEADS = 8
MLA_NOPE = 128
MLA_ROPE = 64
MLA_QK = MLA_NOPE + MLA_ROPE
MLA_V = 128
Q_RANK = 512
KV_RANK = 512
ROPE_THETA = 10000.0
SWA_HEADS = 16
SWA_KV_HEADS = 2
SWA_GROUP = SWA_HEADS // SWA_KV_HEADS
SWA_DIM = 64
WINDOW = 128
NUM_BUCKETS = 32
MAX_DISTANCE = 128
BLK = 128
RMS_EPS = 1e-6
NEG = -1e30
LOG2E = math.log2(math.e)
LAZY_MARGIN = 32.0

LANES = 128
ONES_ROWS = 16
VMEM_LIMIT = 60 * 1024 * 1024


def _cparams(sem):
    return pltpu.CompilerParams(dimension_semantics=sem, vmem_limit_bytes=VMEM_LIMIT)


_NT = (((1,), (1,)), ((), ()))


def _resident(shape):
    nd = len(shape)
    return pl.BlockSpec(shape, lambda *_: (0,) * nd, pipeline_mode=pl.Buffered(1))


def _proj_kernel(x_ref, g_ref, wt32_ref, gcq_ref, gckv_ref,
                 cq_ref, ckv_ref, krt_ref, sqt_ref, sk_ref, svt_ref, wt_ref, *, swa_scale, sub):
    nq = SWA_HEADS * SWA_DIM
    nkv = SWA_KV_HEADS * SWA_DIM
    nl = Q_RANK + KV_RANK

    @pl.when(pl.program_id(0) == 0)
    def _():
        wt_ref[...] = wt32_ref[...].astype(BF16)

    def latent_norm(c, gain_ref):
        rc = lax.rsqrt(jnp.sum(c * c, axis=-1, keepdims=True) * (1.0 / c.shape[-1]) + RMS_EPS)
        return (c * rc * gain_ref[...]).astype(BF16)

    for r0 in range(0, x_ref.shape[0], sub):
        rows = slice(r0, r0 + sub)
        x = x_ref[rows, :]
        r = lax.rsqrt(jnp.sum(x * x, axis=-1, keepdims=True) * (1.0 / x.shape[-1]) + RMS_EPS)
        h = (x * g_ref[...]).astype(BF16)
        yt = lax.dot_general(wt_ref[...], h, _NT, preferred_element_type=F32)
        c = yt[:nl].T * r
        cq_ref[rows, :] = latent_norm(c[:, :Q_RANK], gcq_ref)
        ckv_ref[rows, :] = latent_norm(c[:, Q_RANK:], gckv_ref)
        r_lanes = jnp.broadcast_to(r, (sub, LANES)).T[0:1, :]
        tail = yt[nl:] * r_lanes
        o = MLA_ROPE
        krt_ref[:, rows] = tail[:o]
        sqt_ref[:, rows] = (tail[o:o + nq] * swa_scale).astype(BF16)
        sk_ref[rows, :] = tail[o + nq:o + nq + nkv].T.astype(BF16)
        svt_ref[:, rows] = tail[o + nq + nkv:o + nq + 2 * nkv].astype(BF16)


def _proj(x2, g, w_in_t, g_cq, g_ckv, *, tm, sub):
    s, d = x2.shape
    nq = SWA_HEADS * SWA_DIM
    nkv = SWA_KV_HEADS * SWA_DIM
    row = lambda w: pl.BlockSpec((tm, w), lambda i: (i, 0))
    col = lambda r: pl.BlockSpec((r, tm), lambda i: (0, i))
    return pl.pallas_call(
        functools.partial(_proj_kernel, swa_scale=LOG2E / math.sqrt(SWA_DIM), sub=sub),
        grid=(s // tm,),
        in_specs=[row(d), _resident((1, d)), _resident(w_in_t.shape),
                  _resident((1, Q_RANK)), _resident((1, KV_RANK))],
        out_specs=[row(Q_RANK), row(KV_RANK), col(MLA_ROPE), col(nq), row(nkv), col(nkv)],
        out_shape=[jax.ShapeDtypeStruct((s, Q_RANK), BF16),
                   jax.ShapeDtypeStruct((s, KV_RANK), BF16),
                   jax.ShapeDtypeStruct((MLA_ROPE, s), F32),
                   jax.ShapeDtypeStruct((nq, s), BF16),
                   jax.ShapeDtypeStruct((s, nkv), BF16),
                   jax.ShapeDtypeStruct((nkv, s), BF16)],
        scratch_shapes=[pltpu.VMEM(w_in_t.shape, BF16)],
        compiler_params=_cparams(("arbitrary",)),
        name="proj",
    )(x2, g, w_in_t, g_cq, g_ckv)


def _qkv_kernel(cq_ref, ckv_ref, krt_ref, pos_ref, invf_ref, wuq_ref, wukv_ref,
                qt_ref, k_ref, vt_ref, wuqt_ref, wuk_ref, wuvt_ref, *, q_scale):
    tm = cq_ref.shape[0]
    half = MLA_ROPE // 2

    @pl.when(pl.program_id(0) == 0)
    def _():
        wuqt_ref[...] = wuq_ref[...].T.astype(BF16)
        for h in range(MLA_HEADS):
            c0 = h * (MLA_NOPE + MLA_V)
            wuk_ref[:, h * MLA_NOPE:(h + 1) * MLA_NOPE] = wukv_ref[:, c0:c0 + MLA_NOPE].astype(BF16)
            wuvt_ref[h * MLA_V:(h + 1) * MLA_V, :] = (
                wukv_ref[:, c0 + MLA_NOPE:c0 + MLA_NOPE + MLA_V].T.astype(BF16))
    ang = invf_ref[...] * pos_ref[...].astype(F32)
    cos = jnp.cos(ang)
    sin = jnp.sin(ang)

    def rope_t(t):
        t1, t2 = t[:half], t[half:]
        return jnp.concatenate([t1 * cos - t2 * sin, t2 * cos + t1 * sin], axis=0)

    cq = cq_ref[...]
    ckv = ckv_ref[...]
    qt = lax.dot_general(wuqt_ref[...], cq, _NT, preferred_element_type=F32) * q_scale
    for h in range(MLA_HEADS):
        base = h * MLA_QK
        qt_ref[h, 0:MLA_NOPE, :] = qt[base:base + MLA_NOPE].astype(BF16)
        qt_ref[h, MLA_NOPE:MLA_QK, :] = rope_t(qt[base + MLA_NOPE:base + MLA_QK]).astype(BF16)

    krt = jnp.concatenate([rope_t(krt_ref[...]), jnp.zeros((LANES - MLA_ROPE, tm), F32)], axis=0)
    kr = krt.T[:, :MLA_ROPE].astype(BF16)
    kn = jnp.dot(ckv, wuk_ref[...], preferred_element_type=F32)
    vt = lax.dot_general(wuvt_ref[...], ckv, _NT, preferred_element_type=F32)
    for h in range(MLA_HEADS):
        k_ref[h, :, 0:MLA_NOPE] = kn[:, h * MLA_NOPE:(h + 1) * MLA_NOPE].astype(BF16)
        k_ref[h, :, MLA_NOPE:MLA_QK] = kr
        for c in range(tm // LANES):
            vt_ref[h, c] = vt[h * MLA_V:(h + 1) * MLA_V, c * LANES:(c + 1) * LANES].astype(BF16)


def _qkv(cq, ckv, krt, pos_row, invf, w_uq, w_ukv, *, tm):
    s = cq.shape[0]
    rank = w_uq.shape[0]
    row = lambda w: pl.BlockSpec((tm, w), lambda i: (i, 0))
    col = lambda r: pl.BlockSpec((r, tm), lambda i: (0, i))
    nb = tm // LANES
    return pl.pallas_call(
        functools.partial(_qkv_kernel, q_scale=LOG2E / math.sqrt(MLA_QK)),
        grid=(s // tm,),
        in_specs=[row(Q_RANK), row(KV_RANK), col(MLA_ROPE), col(1), _resident(invf.shape),
                  _resident(w_uq.shape), _resident(w_ukv.shape)],
        out_specs=[pl.BlockSpec((MLA_HEADS, MLA_QK, tm), lambda i: (0, 0, i)),
                   pl.BlockSpec((MLA_HEADS, tm, MLA_QK), lambda i: (0, i, 0)),
                   pl.BlockSpec((MLA_HEADS, nb, MLA_V, LANES), lambda i: (0, i, 0, 0))],
        out_shape=[jax.ShapeDtypeStruct((MLA_HEADS, MLA_QK, s), BF16),
                   jax.ShapeDtypeStruct((MLA_HEADS, s, MLA_QK), BF16),
                   jax.ShapeDtypeStruct((MLA_HEADS, s // LANES, MLA_V, LANES), BF16)],
        scratch_shapes=[pltpu.VMEM((MLA_HEADS * MLA_QK, rank), BF16),
                        pltpu.VMEM((rank, MLA_HEADS * MLA_NOPE), BF16),
                        pltpu.VMEM((MLA_HEADS * MLA_V, rank), BF16)],
        compiler_params=_cparams(("arbitrary",)),
        name="qkv",
    )(cq, ckv, krt, pos_row, invf, w_uq, w_ukv)


def _mla_kernel(*refs, tq, tk, ncast):
    qt_ref, k_ref, vt_ref = refs[:3]
    w_hbm = refs[3:3 + ncast]
    o_ref = refs[3 + ncast]
    wout_hbm = refs[4 + ncast:4 + 2 * ncast]
    m_sc, acc_sc, s_sc, smax_sc, excess_sc = refs[4 + 2 * ncast:9 + 2 * ncast]
    stage_in = refs[9 + 2 * ncast:9 + 3 * ncast]
    stage_out = refs[9 + 3 * ncast:9 + 4 * ncast]
    in_sems, out_sems = refs[9 + 4 * ncast:]

    step = pl.program_id(0)
    last_step = pl.num_programs(0) - 1

    def slab_in(w):
        rows = stage_in[w].shape[0]
        src = w_hbm[w].at[pl.ds(pl.multiple_of(step * rows, rows), rows), :]
        return pltpu.make_async_copy(src, stage_in[w], in_sems.at[w])

    def slab_out(w, at_step):
        rows = stage_out[w].shape[0]
        dst = wout_hbm[w].at[pl.ds(pl.multiple_of(at_step * rows, rows), rows), :]
        return pltpu.make_async_copy(stage_out[w], dst, out_sems.at[w])

    for w in range(ncast):
        slab_in(w).start()

    group = tq // tk
    assert group * tk == tq and group % 2 == 0
    nsub = tk // LANES
    full = slice(0, tq)
    lazy_unroll = 4

    def one_query_block(qi, safe):
        def at(i, size):
            return pl.ds(i * size if isinstance(i, int) else pl.multiple_of(i * size, size), size)

        def queries(cols=full):
            if isinstance(qi, int):
                return slice(qi * tq + cols.start, qi * tq + cols.stop)
            return pl.ds(pl.multiple_of(qi * tq + cols.start, tk), cols.stop - cols.start)

        def reset():
            m_sc[...] = jnp.full(m_sc.shape, NEG, F32)
            acc_sc[...] = jnp.zeros(acc_sc.shape, F32)

        def raw_scores(j, cols=full):
            return jnp.dot(k_ref[0, at(j, tk), :], qt_ref[0, :, queries(cols)],
                           preferred_element_type=F32)

        def values_t(j):
            vt = jnp.concatenate([vt_ref[0, j * nsub + c] for c in range(nsub)], axis=1)
            return jnp.concatenate([vt, jnp.ones((ONES_ROWS, tk), BF16)], axis=0)

        def scores(j, slot, cols=full):
            s = raw_scores(j, cols)
            s_sc[slot, :, cols] = s
            smax_sc[slot, :, cols] = jnp.max(s, axis=0, keepdims=True)

        def update(j, slot, cols=full, tri=False):
            s = s_sc[slot, :, cols]
            if tri:
                mask = (lax.broadcasted_iota(jnp.int32, (tk, tk), 0)
                        <= lax.broadcasted_iota(jnp.int32, (tk, tk), 1))
                parts = [jnp.where(mask, s[:, :tk], NEG)] + ([s[:, tk:]] if s.shape[1] > tk else [])
                s = jnp.concatenate(parts, axis=1)
            s_max = jnp.max(s, axis=0, keepdims=True) if tri else smax_sc[slot, :, cols]
            m_old = m_sc[:, cols]
            m_new = jnp.maximum(m_old, s_max)
            alpha = jnp.exp2(m_old - m_new)
            p = jnp.exp2(s - m_new)
            acc_sc[:, cols] = (alpha * acc_sc[:, cols]
                               + jnp.dot(values_t(j), p.astype(BF16), preferred_element_type=F32))
            m_sc[:, cols] = m_new

        def full_group(i, carry):
            for r in range(group):
                b = group * i + r
                scores(b + 1, (r + 1) % 2)
                update(b, r % 2)
            return carry

        def diagonal_group():
            for r in range(group):
                b = group * qi + r
                if r + 1 < group:
                    scores(b + 1, (r + 1) % 2, slice((r + 1) * tk, tq))
                update(b, r % 2, slice(r * tk, tq), tri=True)

        def finalize():
            out_t = acc_sc[:MLA_V, :] / acc_sc[MLA_V:MLA_V + 1, :]
            o_ref[at(qi, tq), :] = out_t.T.astype(o_ref.dtype)

        if safe:
            reset()
            scores(0, 0)
            lax.fori_loop(0, qi, full_group, 0)
            diagonal_group()
            finalize()
            return

        reset()
        excess_sc[:, queries()] = jnp.full((1, tq), NEG, F32)
        scores(group * qi, 0)
        diagonal_group()

        def lazy_blocks(first, count):
            m = m_sc[...]
            for r in range(count):
                j = first + r
                s = raw_scores(j)
                excess_sc[:, queries()] = jnp.maximum(excess_sc[:, queries()],
                                                      jnp.max(s, axis=0, keepdims=True) - m)
                acc_sc[...] += jnp.dot(values_t(j), jnp.exp2(s - m).astype(BF16),
                                       preferred_element_type=F32)

        def lazy_trip(i, carry):
            lazy_blocks(lazy_unroll * i, lazy_unroll)
            return carry

        trips, rest = divmod(group * qi, lazy_unroll)
        if trips:
            lax.fori_loop(0, trips, lazy_trip, 0)
        if rest:
            lazy_blocks(trips * lazy_unroll, rest)
        finalize()

    nq = qt_ref.shape[2] // tq
    for qi in range(nq):
        one_query_block(qi, safe=False)

    def redo_if_needed(qi, carry):
        @pl.when(jnp.max(excess_sc[:, pl.ds(pl.multiple_of(qi * tq, tq), tq)]) > LAZY_MARGIN)
        def _():
            one_query_block(qi, safe=True)
        return carry

    lax.fori_loop(0, nq, redo_if_needed, 0)

    @pl.when(step > 0)
    def _():
        for w in range(ncast):
            slab_out(w, step - 1).wait()

    for w in range(ncast):
        slab_in(w).wait()
        stage_out[w][...] = stage_in[w][...].astype(BF16)
        slab_out(w, step).start()

    @pl.when(step == last_step)
    def _():
        for w in range(ncast):
            slab_out(w, step).wait()


def _mla(qt, k, vt, cast_weights, *, tq, tk):
    _, s, _ = k.shape
    assert s % tq == 0, (s, tq)
    nsteps = MLA_HEADS
    ncast = len(cast_weights)
    slabs = [(w.shape[0] // nsteps, w.shape[1]) for w in cast_weights]
    for w, (rows, _) in zip(cast_weights, slabs):
        assert rows * nsteps == w.shape[0] and rows % 16 == 0, (w.shape, nsteps)
    anywhere = pl.BlockSpec(memory_space=pl.ANY)
    outs = pl.pallas_call(
        functools.partial(_mla_kernel, tq=tq, tk=tk, ncast=ncast),
        grid=(MLA_HEADS,),
        in_specs=[pl.BlockSpec((1, MLA_QK, s), lambda h: (h, 0, 0)),
                  pl.BlockSpec((1, s, MLA_QK), lambda h: (h, 0, 0)),
                  pl.BlockSpec((1, s // LANES, MLA_V, LANES), lambda h: (h, 0, 0, 0))]
                 + [anywhere] * ncast,
        out_specs=[pl.BlockSpec((s, MLA_V), lambda h: (0, h))] + [anywhere] * ncast,
        out_shape=[jax.ShapeDtypeStruct((s, MLA_HEADS * MLA_V), BF16)]
                  + [jax.ShapeDtypeStruct(w.shape, BF16) for w in cast_weights],
        scratch_shapes=[pltpu.VMEM((1, tq), F32),
                        pltpu.VMEM((MLA_V + ONES_ROWS, tq), F32),
                        pltpu.VMEM((2, tk, tq), F32), pltpu.VMEM((2, 1, tq), F32),
                        pltpu.VMEM((1, s), F32)]
                       + [pltpu.VMEM(sl, F32) for sl in slabs]
                       + [pltpu.VMEM(sl, BF16) for sl in slabs]
                       + [pltpu.SemaphoreType.DMA((ncast,)), pltpu.SemaphoreType.DMA((ncast,))],
        compiler_params=_cparams(("arbitrary",)),
        name="mla",
    )(qt, k, vt, *cast_weights)
    return outs[0], outs[1:]


def _t5_bucket_table_t():
    i = np.arange(BLK)[None, :]
    j = np.arange(2 * BLK)[:, None]
    dist = i + BLK - j
    max_exact = NUM_BUCKETS // 2
    d = np.maximum(dist, 0)
    large = max_exact + (np.log(np.maximum(d, 1) / max_exact)
                         / np.log(MAX_DISTANCE / max_exact)
                         * (NUM_BUCKETS - max_exact)).astype(np.int32)
    large = np.minimum(large, NUM_BUCKETS - 1)
    bucket = np.where(d < max_exact, d, large).astype(np.int32)
    in_window = (dist >= 0) & (dist < WINDOW)
    return np.where(in_window, bucket, -1).astype(np.int32)


def _swa_kernel(relb_ref, sink_ref, bucket_ref, qt_ref, kc_ref, kp_ref, vtc_ref, vtp_ref,
                o_ref, bias_sc, biasrel_sc, sink_sc, *, nb):
    i = pl.program_id(0)
    gw = SWA_GROUP * BLK

    @pl.when(i == 0)
    def _():
        bucket = bucket_ref[...]
        for h in range(SWA_HEADS):
            c, g = divmod(h, SWA_GROUP)
            lanes = slice(g * BLK, (g + 1) * BLK)
            sink = sink_ref[h] * LOG2E
            b = jnp.full(bucket.shape, NEG, F32)
            for t in range(NUM_BUCKETS):
                b = jnp.where(bucket == t, relb_ref[t, h] * LOG2E, b)
            masked = jnp.full((BLK, BLK), NEG, F32)
            for table, tb in ((bias_sc, b), (biasrel_sc, jnp.where(bucket < 0, NEG, b - sink))):
                table[1, c, :, lanes] = tb
                table[0, c, BLK:, lanes] = tb[BLK:]
                table[0, c, :BLK, lanes] = masked
            sink_sc[c, :, lanes] = jnp.full((1, BLK), sink, F32)

    zeros = jnp.zeros((SWA_DIM, gw), BF16)

    def all_blocks(safe):
        excess = jnp.full((8, gw), NEG, F32)
        for t in range(nb):
            tsl = slice(t * BLK, (t + 1) * BLK)
            k_prev = kp_ref[...] if t == 0 else kc_ref[(t - 1) * BLK:t * BLK, :]
            vt_prev = vtp_ref[...] if t == 0 else vtc_ref[:, (t - 1) * BLK:t * BLK]
            kband = jnp.concatenate([k_prev, kc_ref[tsl, :]], axis=0)
            general = 1 if t > 0 else jnp.where(i == 0, 0, 1)
            for c in range(SWA_KV_HEADS):
                qt = jnp.concatenate(
                    [qt_ref[(c * SWA_GROUP + g) * SWA_DIM:(c * SWA_GROUP + g + 1) * SWA_DIM, tsl]
                     for g in range(SWA_GROUP)], axis=1)
                qt_ext = jnp.concatenate([qt, zeros] if c == 0 else [zeros, qt], axis=0)
                raw = jnp.dot(kband, qt_ext, preferred_element_type=F32)
                if safe:
                    s = raw + bias_sc[general, c]
                    sink = sink_sc[c]
                    m = jnp.maximum(jnp.max(s, axis=0, keepdims=True), sink)
                    p = jnp.exp2(s - m)
                    sink_term = jnp.exp2(sink - m)
                else:
                    s = raw + biasrel_sc[general, c]
                    excess = functools.reduce(
                        jnp.maximum, [s[8 * r:8 * r + 8] for r in range(2 * BLK // 8)], excess)
                    p = jnp.exp2(s)
                    sink_term = 1.0
                dsl = slice(c * SWA_DIM, (c + 1) * SWA_DIM)
                vt = jnp.concatenate([vt_prev[dsl, :], vtc_ref[dsl, tsl]], axis=1)
                vt = jnp.concatenate([vt, jnp.ones((ONES_ROWS, 2 * BLK), BF16)], axis=0)
                pv = jnp.dot(vt, p.astype(BF16), preferred_element_type=F32)
                ot = pv[:SWA_DIM] / (pv[SWA_DIM:SWA_DIM + 1] + sink_term)
                for g2 in range(SWA_GROUP // 2):
                    two = jnp.concatenate([ot[:, (2 * g2) * BLK:(2 * g2 + 1) * BLK],
                                           ot[:, (2 * g2 + 1) * BLK:(2 * g2 + 2) * BLK]], axis=0)
                    col0 = (c * SWA_GROUP + 2 * g2) * SWA_DIM
                    o_ref[tsl, col0:col0 + 2 * SWA_DIM] = two.T.astype(o_ref.dtype)
        return excess

    excess = all_blocks(safe=False)

    @pl.when(jnp.max(excess) > LAZY_MARGIN)
    def _():
        all_blocks(safe=True)


def _swa(sqt, sk, svt, sinks, rel_bias, *, nb):
    nq, s = sqt.shape
    nkv = sk.shape[1]
    tb = nb * BLK
    bucket = jnp.asarray(_t5_bucket_table_t())
    smem = pl.BlockSpec(memory_space=pltpu.SMEM)
    prev_blk = lambda i: jnp.maximum(i * nb - 1, 0)
    return pl.pallas_call(
        functools.partial(_swa_kernel, nb=nb),
        grid=(s // tb,),
        in_specs=[smem, smem, _resident((2 * BLK, BLK)),
                  pl.BlockSpec((nq, tb), lambda i: (0, i)),
                  pl.BlockSpec((tb, nkv), lambda i: (i, 0)),
                  pl.BlockSpec((BLK, nkv), lambda i: (prev_blk(i), 0)),
                  pl.BlockSpec((nkv, tb), lambda i: (0, i)),
                  pl.BlockSpec((nkv, BLK), lambda i: (0, prev_blk(i)))],
        out_specs=pl.BlockSpec((tb, nq), lambda i: (i, 0)),
        out_shape=jax.ShapeDtypeStruct((s, nq), BF16),
        scratch_shapes=[pltpu.VMEM((2, SWA_KV_HEADS, 2 * BLK, SWA_GROUP * BLK), F32),
                        pltpu.VMEM((2, SWA_KV_HEADS, 2 * BLK, SWA_GROUP * BLK), F32),
                        pltpu.VMEM((SWA_KV_HEADS, 1, SWA_GROUP * BLK), F32)],
        compiler_params=_cparams(("arbitrary",)),
        name="swa",
    )(rel_bias, sinks, bucket, sqt, sk, sk, svt, svt)


def _rms(y, gain):
    r = lax.rsqrt(jnp.sum(y * y, axis=-1, keepdims=True) * (1.0 / y.shape[-1]) + RMS_EPS)
    return y * r * gain


def _oproj_kernel(a_ref, b_ref, wo_ref, x_ref, gpost_ref, gpre_ref, x1_ref, h_ref, *, sub):
    na = a_ref.shape[1]
    for r0 in range(0, x_ref.shape[0], sub):
        rows = slice(r0, r0 + sub)
        mix = (jnp.dot(a_ref[rows, :], wo_ref[:na, :], preferred_element_type=F32)
               + jnp.dot(b_ref[rows, :], wo_ref[na:, :], preferred_element_type=F32))
        x1 = x_ref[rows, :] + _rms(mix, gpost_ref[...])
        x1_ref[rows, :] = x1
        h_ref[rows, :] = _rms(x1, gpre_ref[...]).astype(BF16)


def _oproj(out_a, out_b, w_o, x2, g_post, g_pre, *, tm, sub):
    s, d = x2.shape
    row = lambda w: pl.BlockSpec((tm, w), lambda i: (i, 0))
    return pl.pallas_call(
        functools.partial(_oproj_kernel, sub=sub),
        grid=(s // tm,),
        in_specs=[row(out_a.shape[1]), row(out_b.shape[1]), _resident(w_o.shape),
                  row(d), _resident((1, d)), _resident((1, d))],
        out_specs=[row(d), row(d)],
        out_shape=[jax.ShapeDtypeStruct((s, d), F32), jax.ShapeDtypeStruct((s, d), BF16)],
        compiler_params=_cparams(("arbitrary",)),
        name="oproj",
    )(out_a, out_b, w_o, x2, g_post, g_pre)


def _ffn_kernel(h_ref, wg_ref, wu_ref, wd_ref, x1_hbm, gpost_ref, o_ref, x1_buf, x1_sem, *, sub):
    i = pl.program_id(0)
    j = pl.program_id(1)
    tm = o_ref.shape[0]

    def x1_copy():
        rows = pl.ds(pl.multiple_of(i * tm, tm), tm)
        return pltpu.make_async_copy(x1_hbm.at[rows, :], x1_buf, x1_sem)

    @pl.when(j == 0)
    def _():
        x1_copy().start()
        o_ref[...] = jnp.zeros(o_ref.shape, F32)

    tf = wg_ref.shape[1]
    halves = [slice(0, tf // 2), slice(tf // 2, tf)]

    def down_products(rows):
        h = h_ref[rows, :]
        gu = [(jnp.dot(h, wg_ref[:, c], preferred_element_type=F32),
               jnp.dot(h, wu_ref[:, c], preferred_element_type=F32)) for c in halves]
        for c, (gate, up) in zip(halves, gu):
            act = (gate * jax.nn.sigmoid(gate) * up).astype(BF16)
            yield jnp.dot(act, wd_ref[c, :], preferred_element_type=F32)

    last = pl.num_programs(1) - 1

    @pl.when(j < last)
    def _():
        for term in down_products(slice(0, tm)):
            o_ref[...] += term

    @pl.when(j == last)
    def _():
        x1_copy().wait()
        for r in range(tm // sub):
            rows = slice(r * sub, (r + 1) * sub)
            total = o_ref[rows, :]
            for term in down_products(rows):
                total = total + term
            o_ref[rows, :] = x1_buf[rows, :] + _rms(total, gpost_ref[...])


def _ffn(h, wg, wu, wd, x1, g_post, *, tm, tf, sub):
    s, d = x1.shape
    dff = wg.shape[1]
    assert tm % sub == 0, (tm, sub)
    return pl.pallas_call(
        functools.partial(_ffn_kernel, sub=sub),
        grid=(s // tm, dff // tf),
        in_specs=[pl.BlockSpec((tm, d), lambda i, j: (i, 0)),
                  pl.BlockSpec((d, tf), lambda i, j: (0, j)),
                  pl.BlockSpec((d, tf), lambda i, j: (0, j)),
                  pl.BlockSpec((tf, d), lambda i, j: (j, 0)),
                  pl.BlockSpec(memory_space=pl.ANY),
                  _resident((1, d))],
        out_specs=pl.BlockSpec((tm, d), lambda i, j: (i, 0)),
        out_shape=jax.ShapeDtypeStruct((s, d), F32),
        scratch_shapes=[pltpu.VMEM((tm, d), F32), pltpu.SemaphoreType.DMA(())],
        compiler_params=_cparams(("arbitrary", "arbitrary")),
        name="ffn",
    )(h, wg, wu, wd, x1, g_post)


class _Tiles(NamedTuple):
    proj_rows: int = 512
    proj_sub: int = 256
    qkv_rows: int = 1024
    mla_tq: int = 1024
    mla_tk: int = 512
    swa_blocks: int = 8
    ffn_rows: int = 1024
    ffn_cols: int = 512
    ffn_sub: int = 256


def _layer(x2, pos_row, p, t=_Tiles()):
    half = MLA_ROPE // 2
    inv_freq = ROPE_THETA ** (-jnp.arange(half, dtype=F32) / half)
    invf = jnp.broadcast_to(inv_freq[:, None], (half, t.qkv_rows))

    row = lambda a: a[None, :]
    cq, ckv, krt, sqt, sk, svt = _proj(x2, row(p["g_mix_pre"]), p["w_in"].T, row(p["g_cq"]),
                                       row(p["g_ckv"]), tm=t.proj_rows, sub=t.proj_sub)
    qt, k, vt = _qkv(cq, ckv, krt, pos_row, invf, p["w_uq"], p["w_ukv"], tm=t.qkv_rows)
    out_a, (w_o, w_gate, w_up, w_down) = _mla(
        qt, k, vt, [p["w_o"], p["w_gate"], p["w_up"], p["w_down"]],
        tq=t.mla_tq, tk=t.mla_tk)
    out_b = _swa(sqt, sk, svt, p["sinks"], p["rel_bias"], nb=t.swa_blocks)
    x1, h = _oproj(out_a, out_b, w_o, x2, row(p["g_mix_post"]), row(p["g_ffn_pre"]),
                   tm=t.proj_rows, sub=t.proj_sub)
    return _ffn(h, w_gate, w_up, w_down, x1, row(p["g_ffn_post"]), tm=t.ffn_rows, tf=t.ffn_cols,
                sub=t.ffn_sub)


def kernel(x, positions, g_mix_pre, w_in, g_cq, g_ckv, w_uq, w_ukv, sinks, rel_bias,
           w_o, g_mix_post, g_ffn_pre, w_gate, w_up, w_down, g_ffn_post):
    b, s, d = x.shape
    assert b == 1, "the row-major (S, D) pipeline assumes a single sequence"
    n_in = Q_RANK + KV_RANK + MLA_ROPE + (SWA_HEADS + 2 * SWA_KV_HEADS) * SWA_DIM
    d_mix = MLA_HEADS * MLA_V + SWA_HEADS * SWA_DIM
    assert w_in.shape[1:] == (d, n_in) and w_o.shape[1:] == (d_mix, d), (w_in.shape, w_o.shape)
    assert w_uq.shape[1:] == (Q_RANK, MLA_HEADS * MLA_QK), w_uq.shape
    assert w_ukv.shape[1:] == (KV_RANK, MLA_HEADS * (MLA_NOPE + MLA_V)), w_ukv.shape
    assert rel_bias.shape == (NUM_BUCKETS, SWA_HEADS) and sinks.shape[1:] == (SWA_HEADS,)
    x2 = x.reshape(s, d)
    pos_row = positions.reshape(1, s)
    for layer in range(w_in.shape[0]):
        p = dict(g_mix_pre=g_mix_pre[layer], w_in=w_in[layer], g_cq=g_cq[layer],
                 g_ckv=g_ckv[layer], w_uq=w_uq[layer], w_ukv=w_ukv[layer],
                 sinks=sinks[layer], rel_bias=rel_bias, w_o=w_o[layer],
                 g_mix_post=g_mix_post[layer], g_ffn_pre=g_ffn_pre[layer],
                 w_gate=w_gate[layer], w_up=w_up[layer], w_down=w_down[layer],
                 g_ffn_post=g_ffn_post[layer])
        x2 = _layer(x2, pos_row, p)
    return x2.reshape(b, s, d)
```

```python
import functools
import math
from typing import NamedTuple

import jax
import jax.numpy as jnp
import numpy as np
from jax import lax
from jax.experimental import pallas as pl
from jax.experimental.pallas import tpu as pltpu

F32 = jnp.float32
BF16 = jnp.bfloat16

MLA_HEADS = 8
MLA_NOPE = 128
MLA_ROPE = 64
MLA_QK = MLA_NOPE + MLA_ROPE
MLA_V = 128
Q_RANK = 512
KV_RANK = 512
ROPE_THETA = 10000.0
SWA_HEADS = 16
SWA_KV_HEADS = 2
SWA_GROUP = SWA_HEADS // SWA_KV_HEADS
SWA_DIM = 64
WINDOW = 128
NUM_BUCKETS = 32
MAX_DISTANCE = 128
BLK = 128
RMS_EPS = 1e-6
NEG = -1e30
LOG2E = math.log2(math.e)
LAZY_MARGIN = 32.0

LANES = 128
ONES_ROWS = 16
VMEM_LIMIT = 60 * 1024 * 1024


def _cparams(sem):
    return pltpu.CompilerParams(dimension_semantics=sem, vmem_limit_bytes=VMEM_LIMIT)


_NT = (((1,), (1,)), ((), ()))


def _resident(shape):
    nd = len(shape)
    return pl.BlockSpec(shape, lambda *_: (0,) * nd, pipeline_mode=pl.Buffered(1))


def _proj_kernel(x_ref, g_ref, wt32_ref, gcq_ref, gckv_ref,
                 cq_ref, ckv_ref, krt_ref, sqt_ref, sk_ref, svt_ref, wt_ref, *, swa_scale, sub):
    nq = SWA_HEADS * SWA_DIM
    nkv = SWA_KV_HEADS * SWA_DIM
    nl = Q_RANK + KV_RANK

    @pl.when(pl.program_id(0) == 0)
    def _():
        wt_ref[...] = wt32_ref[...].astype(BF16)

    def latent_norm(c, gain_ref):
        rc = lax.rsqrt(jnp.sum(c * c, axis=-1, keepdims=True) * (1.0 / c.shape[-1]) + RMS_EPS)
        return (c * rc * gain_ref[...]).astype(BF16)

    for r0 in range(0, x_ref.shape[0], sub):
        rows = slice(r0, r0 + sub)
        x = x_ref[rows, :]
        r = lax.rsqrt(jnp.sum(x * x, axis=-1, keepdims=True) * (1.0 / x.shape[-1]) + RMS_EPS)
        h = (x * g_ref[...]).astype(BF16)
        yt = lax.dot_general(wt_ref[...], h, _NT, preferred_element_type=F32)
        c = yt[:nl].T * r
        cq_ref[rows, :] = latent_norm(c[:, :Q_RANK], gcq_ref)
        ckv_ref[rows, :] = latent_norm(c[:, Q_RANK:], gckv_ref)
        r_lanes = jnp.broadcast_to(r, (sub, LANES)).T[0:1, :]
        tail = yt[nl:] * r_lanes
        o = MLA_ROPE
        krt_ref[:, rows] = tail[:o]
        sqt_ref[:, rows] = (tail[o:o + nq] * swa_scale).astype(BF16)
        sk_ref[rows, :] = tail[o + nq:o + nq + nkv].T.astype(BF16)
        svt_ref[:, rows] = tail[o + nq + nkv:o + nq + 2 * nkv].astype(BF16)


def _proj(x2, g, w_in_t, g_cq, g_ckv, *, tm, sub):
    s, d = x2.shape
    nq = SWA_HEADS * SWA_DIM
    nkv = SWA_KV_HEADS * SWA_DIM
    row = lambda w: pl.BlockSpec((tm, w), lambda i: (i, 0))
    col = lambda r: pl.BlockSpec((r, tm), lambda i: (0, i))
    return pl.pallas_call(
        functools.partial(_proj_kernel, swa_scale=LOG2E / math.sqrt(SWA_DIM), sub=sub),
        grid=(s // tm,),
        in_specs=[row(d), _resident((1, d)), _resident(w_in_t.shape),
                  _resident((1, Q_RANK)), _resident((1, KV_RANK))],
        out_specs=[row(Q_RANK), row(KV_RANK), col(MLA_ROPE), col(nq), row(nkv), col(nkv)],
        out_shape=[jax.ShapeDtypeStruct((s, Q_RANK), BF16),
                   jax.ShapeDtypeStruct((s, KV_RANK), BF16),
                   jax.ShapeDtypeStruct((MLA_ROPE, s), F32),
                   jax.ShapeDtypeStruct((nq, s), BF16),
                   jax.ShapeDtypeStruct((s, nkv), BF16),
                   jax.ShapeDtypeStruct((nkv, s), BF16)],
        scratch_shapes=[pltpu.VMEM(w_in_t.shape, BF16)],
        compiler_params=_cparams(("arbitrary",)),
        name="proj",
    )(x2, g, w_in_t, g_cq, g_ckv)


def _qkv_kernel(cq_ref, ckv_ref, krt_ref, pos_ref, invf_ref, wuq_ref, wukv_ref,
                qt_ref, k_ref, vt_ref, wuqt_ref, wuk_ref, wuvt_ref, *, q_scale):
    tm = cq_ref.shape[0]
    half = MLA_ROPE // 2

    @pl.when(pl.program_id(0) == 0)
    def _():
        wuqt_ref[...] = wuq_ref[...].T.astype(BF16)
        for h in range(MLA_HEADS):
            c0 = h * (MLA_NOPE + MLA_V)
            wuk_ref[:, h * MLA_NOPE:(h + 1) * MLA_NOPE] = wukv_ref[:, c0:c0 + MLA_NOPE].astype(BF16)
            wuvt_ref[h * MLA_V:(h + 1) * MLA_V, :] = (
                wukv_ref[:, c0 + MLA_NOPE:c0 + MLA_NOPE + MLA_V].T.astype(BF16))
    ang = invf_ref[...] * pos_ref[...].astype(F32)
    cos = jnp.cos(ang)
    sin = jnp.sin(ang)

    def rope_t(t):
        t1, t2 = t[:half], t[half:]
        return jnp.concatenate([t1 * cos - t2 * sin, t2 * cos + t1 * sin], axis=0)

    cq = cq_ref[...]
    ckv = ckv_ref[...]
    qt = lax.dot_general(wuqt_ref[...], cq, _NT, preferred_element_type=F32) * q_scale
    for h in range(MLA_HEADS):
        base = h * MLA_QK
        qt_ref[h, 0:MLA_NOPE, :] = qt[base:base + MLA_NOPE].astype(BF16)
        qt_ref[h, MLA_NOPE:MLA_QK, :] = rope_t(qt[base + MLA_NOPE:base + MLA_QK]).astype(BF16)

    krt = jnp.concatenate([rope_t(krt_ref[...]), jnp.zeros((LANES - MLA_ROPE, tm), F32)], axis=0)
    kr = krt.T[:, :MLA_ROPE].astype(BF16)
    kn = jnp.dot(ckv, wuk_ref[...], preferred_element_type=F32)
    vt = lax.dot_general(wuvt_ref[...], ckv, _NT, preferred_element_type=F32)
    for h in range(MLA_HEADS):
        k_ref[h, :, 0:MLA_NOPE] = kn[:, h * MLA_NOPE:(h + 1) * MLA_NOPE].astype(BF16)
        k_ref[h, :, MLA_NOPE:MLA_QK] = kr
        for c in range(tm // LANES):
            vt_ref[h, c] = vt[h * MLA_V:(h + 1) * MLA_V, c * LANES:(c + 1) * LANES].astype(BF16)


def _qkv(cq, ckv, krt, pos_row, invf, w_uq, w_ukv, *, tm):
    s = cq.shape[0]
    rank = w_uq.shape[0]
    row = lambda w: pl.BlockSpec((tm, w), lambda i: (i, 0))
    col = lambda r: pl.BlockSpec((r, tm), lambda i: (0, i))
    nb = tm // LANES
    return pl.pallas_call(
        functools.partial(_qkv_kernel, q_scale=LOG2E / math.sqrt(MLA_QK)),
        grid=(s // tm,),
        in_specs=[row(Q_RANK), row(KV_RANK), col(MLA_ROPE), col(1), _resident(invf.shape),
                  _resident(w_uq.shape), _resident(w_ukv.shape)],
        out_specs=[pl.BlockSpec((MLA_HEADS, MLA_QK, tm), lambda i: (0, 0, i)),
                   pl.BlockSpec((MLA_HEADS, tm, MLA_QK), lambda i: (0, i, 0)),
                   pl.BlockSpec((MLA_HEADS, nb, MLA_V, LANES), lambda i: (0, i, 0, 0))],
        out_shape=[jax.ShapeDtypeStruct((MLA_HEADS, MLA_QK, s), BF16),
                   jax.ShapeDtypeStruct((MLA_HEADS, s, MLA_QK), BF16),
                   jax.ShapeDtypeStruct((MLA_HEADS, s // LANES, MLA_V, LANES), BF16)],
        scratch_shapes=[pltpu.VMEM((MLA_HEADS * MLA_QK, rank), BF16),
                        pltpu.VMEM((rank, MLA_HEADS * MLA_NOPE), BF16),
                        pltpu.VMEM((MLA_HEADS * MLA_V, rank), BF16)],
        compiler_params=_cparams(("arbitrary",)),
        name="qkv",
    )(cq, ckv, krt, pos_row, invf, w_uq, w_ukv)


def _mla_kernel(*refs, tq, tk, ncast):
    qt_ref, k_ref, vt_ref = refs[:3]
    w_hbm = refs[3:3 + ncast]
    o_ref = refs[3 + ncast]
    wout_hbm = refs[4 + ncast:4 + 2 * ncast]
    m_sc, acc_sc, s_sc, smax_sc, excess_sc = refs[4 + 2 * ncast:9 + 2 * ncast]
    stage_in = refs[9 + 2 * ncast:9 + 3 * ncast]
    stage_out = refs[9 + 3 * ncast:9 + 4 * ncast]
    in_sems, out_sems = refs[9 + 4 * ncast:]

    step = pl.program_id(0)
    last_step = pl.num_programs(0) - 1

    def slab_in(w):
        rows = stage_in[w].shape[0]
        src = w_hbm[w].at[pl.ds(pl.multiple_of(step * rows, rows), rows), :]
        return pltpu.make_async_copy(src, stage_in[w], in_sems.at[w])

    def slab_out(w, at_step):
        rows = stage_out[w].shape[0]
        dst = wout_hbm[w].at[pl.ds(pl.multiple_of(at_step * rows, rows), rows), :]
        return pltpu.make_async_copy(stage_out[w], dst, out_sems.at[w])

    for w in range(ncast):
        slab_in(w).start()

    group = tq // tk
    assert group * tk == tq and group % 2 == 0
    nsub = tk // LANES
    full = slice(0, tq)
    lazy_unroll = 4

    def one_query_block(qi, safe):
        def at(i, size):
            return pl.ds(i * size if isinstance(i, int) else pl.multiple_of(i * size, size), size)

        def queries(cols=full):
            if isinstance(qi, int):
                return slice(qi * tq + cols.start, qi * tq + cols.stop)
            return pl.ds(pl.multiple_of(qi * tq + cols.start, tk), cols.stop - cols.start)

        def reset():
            m_sc[...] = jnp.full(m_sc.shape, NEG, F32)
            acc_sc[...] = jnp.zeros(acc_sc.shape, F32)

        def raw_scores(j, cols=full):
            return jnp.dot(k_ref[0, at(j, tk), :], qt_ref[0, :, queries(cols)],
                           preferred_element_type=F32)

        def values_t(j):
            vt = jnp.concatenate([vt_ref[0, j * nsub + c] for c in range(nsub)], axis=1)
            return jnp.concatenate([vt, jnp.ones((ONES_ROWS, tk), BF16)], axis=0)

        def scores(j, slot, cols=full):
            s = raw_scores(j, cols)
            s_sc[slot, :, cols] = s
            smax_sc[slot, :, cols] = jnp.max(s, axis=0, keepdims=True)

        def update(j, slot, cols=full, tri=False):
            s = s_sc[slot, :, cols]
            if tri:
                mask = (lax.broadcasted_iota(jnp.int32, (tk, tk), 0)
                        <= lax.broadcasted_iota(jnp.int32, (tk, tk), 1))
                parts = [jnp.where(mask, s[:, :tk], NEG)] + ([s[:, tk:]] if s.shape[1] > tk else [])
                s = jnp.concatenate(parts, axis=1)
            s_max = jnp.max(s, axis=0, keepdims=True) if tri else smax_sc[slot, :, cols]
            m_old = m_sc[:, cols]
            m_new = jnp.maximum(m_old, s_max)
            alpha = jnp.exp2(m_old - m_new)
            p = jnp.exp2(s - m_new)
            acc_sc[:, cols] = (alpha * acc_sc[:, cols]
                               + jnp.dot(values_t(j), p.astype(BF16), preferred_element_type=F32))
            m_sc[:, cols] = m_new

        def full_group(i, carry):
            for r in range(group):
                b = group * i + r
                scores(b + 1, (r + 1) % 2)
                update(b, r % 2)
            return carry

        def diagonal_group():
            for r in range(group):
                b = group * qi + r
                if r + 1 < group:
                    scores(b + 1, (r + 1) % 2, slice((r + 1) * tk, tq))
                update(b, r % 2, slice(r * tk, tq), tri=True)

        def finalize():
            out_t = acc_sc[:MLA_V, :] / acc_sc[MLA_V:MLA_V + 1, :]
            o_ref[at(qi, tq), :] = out_t.T.astype(o_ref.dtype)

        if safe:
            reset()
            scores(0, 0)
            lax.fori_loop(0, qi, full_group, 0)
            diagonal_group()
            finalize()
            return

        reset()
        excess_sc[:, queries()] = jnp.full((1, tq), NEG, F32)
        scores(group * qi, 0)
        diagonal_group()

        def lazy_blocks(first, count):
            m = m_sc[...]
            for r in range(count):
                j = first + r
                s = raw_scores(j)
                excess_sc[:, queries()] = jnp.maximum(excess_sc[:, queries()],
                                                      jnp.max(s, axis=0, keepdims=True) - m)
                acc_sc[...] += jnp.dot(values_t(j), jnp.exp2(s - m).astype(BF16),
                                       preferred_element_type=F32)

        def lazy_trip(i, carry):
            lazy_blocks(lazy_unroll * i, lazy_unroll)
            return carry

        trips, rest = divmod(group * qi, lazy_unroll)
        if trips:
            lax.fori_loop(0, trips, lazy_trip, 0)
        if rest:
            lazy_blocks(trips * lazy_unroll, rest)
        finalize()

    nq = qt_ref.shape[2] // tq
    for qi in range(nq):
        one_query_block(qi, safe=False)

    def redo_if_needed(qi, carry):
        @pl.when(jnp.max(excess_sc[:, pl.ds(pl.multiple_of(qi * tq, tq), tq)]) > LAZY_MARGIN)
        def _():
            one_query_block(qi, safe=True)
        return carry

    lax.fori_loop(0, nq, redo_if_needed, 0)

    @pl.when(step > 0)
    def _():
        for w in range(ncast):
            slab_out(w, step - 1).wait()

    for w in range(ncast):
        slab_in(w).wait()
        stage_out[w][...] = stage_in[w][...].astype(BF16)
        slab_out(w, step).start()

    @pl.when(step == last_step)
    def _():
        for w in range(ncast):
            slab_out(w, step).wait()


def _mla(qt, k, vt, cast_weights, *, tq, tk):
    _, s, _ = k.shape
    assert s % tq == 0, (s, tq)
    nsteps = MLA_HEADS
    ncast = len(cast_weights)
    slabs = [(w.shape[0] // nsteps, w.shape[1]) for w in cast_weights]
    for w, (rows, _) in zip(cast_weights, slabs):
        assert rows * nsteps == w.shape[0] and rows % 16 == 0, (w.shape, nsteps)
    anywhere = pl.BlockSpec(memory_space=pl.ANY)
    outs = pl.pallas_call(
        functools.partial(_mla_kernel, tq=tq, tk=tk, ncast=ncast),
        grid=(MLA_HEADS,),
        in_specs=[pl.BlockSpec((1, MLA_QK, s), lambda h: (h, 0, 0)),
                  pl.BlockSpec((1, s, MLA_QK), lambda h: (h, 0, 0)),
                  pl.BlockSpec((1, s // LANES, MLA_V, LANES), lambda h: (h, 0, 0, 0))]
                 + [anywhere] * ncast,
        out_specs=[pl.BlockSpec((s, MLA_V), lambda h: (0, h))] + [anywhere] * ncast,
        out_shape=[jax.ShapeDtypeStruct((s, MLA_HEADS * MLA_V), BF16)]
                  + [jax.ShapeDtypeStruct(w.shape, BF16) for w in cast_weights],
        scratch_shapes=[pltpu.VMEM((1, tq), F32),
                        pltpu.VMEM((MLA_V + ONES_ROWS, tq), F32),
                        pltpu.VMEM((2, tk, tq), F32), pltpu.VMEM((2, 1, tq), F32),
                        pltpu.VMEM((1, s), F32)]
                       + [pltpu.VMEM(sl, F32) for sl in slabs]
                       + [pltpu.VMEM(sl, BF16) for sl in slabs]
                       + [pltpu.SemaphoreType.DMA((ncast,)), pltpu.SemaphoreType.DMA((ncast,))],
        compiler_params=_cparams(("arbitrary",)),
        name="mla",
    )(qt, k, vt, *cast_weights)
    return outs[0], outs[1:]


def _t5_bucket_table_t():
    i = np.arange(BLK)[None, :]
    j = np.arange(2 * BLK)[:, None]
    dist = i + BLK - j
    max_exact = NUM_BUCKETS // 2
    d = np.maximum(dist, 0)
    large = max_exact + (np.log(np.maximum(d, 1) / max_exact)
                         / np.log(MAX_DISTANCE / max_exact)
                         * (NUM_BUCKETS - max_exact)).astype(np.int32)
    large = np.minimum(large, NUM_BUCKETS - 1)
    bucket = np.where(d < max_exact, d, large).astype(np.int32)
    in_window = (dist >= 0) & (dist < WINDOW)
    return np.where(in_window, bucket, -1).astype(np.int32)


def _swa_kernel(relb_ref, sink_ref, bucket_ref, qt_ref, kc_ref, kp_ref, vtc_ref, vtp_ref,
                o_ref, bias_sc, biasrel_sc, sink_sc, *, nb):
    i = pl.program_id(0)
    gw = SWA_GROUP * BLK

    @pl.when(i == 0)
    def _():
        bucket = bucket_ref[...]
        for h in range(SWA_HEADS):
            c, g = divmod(h, SWA_GROUP)
            lanes = slice(g * BLK, (g + 1) * BLK)
            sink = sink_ref[h] * LOG2E
            b = jnp.full(bucket.shape, NEG, F32)
            for t in range(NUM_BUCKETS):
                b = jnp.where(bucket == t, relb_ref[t, h] * LOG2E, b)
            masked = jnp.full((BLK, BLK), NEG, F32)
            for table, tb in ((bias_sc, b), (biasrel_sc, jnp.where(bucket < 0, NEG, b - sink))):
                table[1, c, :, lanes] = tb
                table[0, c, BLK:, lanes] = tb[BLK:]
                table[0, c, :BLK, lanes] = masked
            sink_sc[c, :, lanes] = jnp.full((1, BLK), sink, F32)

    zeros = jnp.zeros((SWA_DIM, gw), BF16)

    def all_blocks(safe):
        excess = jnp.full((8, gw), NEG, F32)
        for t in range(nb):
            tsl = slice(t * BLK, (t + 1) * BLK)
            k_prev = kp_ref[...] if t == 0 else kc_ref[(t - 1) * BLK:t * BLK, :]
            vt_prev = vtp_ref[...] if t == 0 else vtc_ref[:, (t - 1) * BLK:t * BLK]
            kband = jnp.concatenate([k_prev, kc_ref[tsl, :]], axis=0)
            general = 1 if t > 0 else jnp.where(i == 0, 0, 1)
            for c in range(SWA_KV_HEADS):
                qt = jnp.concatenate(
                    [qt_ref[(c * SWA_GROUP + g) * SWA_DIM:(c * SWA_GROUP + g + 1) * SWA_DIM, tsl]
                     for g in range(SWA_GROUP)], axis=1)
                qt_ext = jnp.concatenate([qt, zeros] if c == 0 else [zeros, qt], axis=0)
                raw = jnp.dot(kband, qt_ext, preferred_element_type=F32)
                if safe:
                    s = raw + bias_sc[general, c]
                    sink = sink_sc[c]
                    m = jnp.maximum(jnp.max(s, axis=0, keepdims=True), sink)
                    p = jnp.exp2(s - m)
                    sink_term = jnp.exp2(sink - m)
                else:
                    s = raw + biasrel_sc[general, c]
                    excess = functools.reduce(
                        jnp.maximum, [s[8 * r:8 * r + 8] for r in range(2 * BLK // 8)], excess)
                    p = jnp.exp2(s)
                    sink_term = 1.0
                dsl = slice(c * SWA_DIM, (c + 1) * SWA_DIM)
                vt = jnp.concatenate([vt_prev[dsl, :], vtc_ref[dsl, tsl]], axis=1)
                vt = jnp.concatenate([vt, jnp.ones((ONES_ROWS, 2 * BLK), BF16)], axis=0)
                pv = jnp.dot(vt, p.astype(BF16), preferred_element_type=F32)
                ot = pv[:SWA_DIM] / (pv[SWA_DIM:SWA_DIM + 1] + sink_term)
                for g2 in range(SWA_GROUP // 2):
                    two = jnp.concatenate([ot[:, (2 * g2) * BLK:(2 * g2 + 1) * BLK],
                                           ot[:, (2 * g2 + 1) * BLK:(2 * g2 + 2) * BLK]], axis=0)
                    col0 = (c * SWA_GROUP + 2 * g2) * SWA_DIM
                    o_ref[tsl, col0:col0 + 2 * SWA_DIM] = two.T.astype(o_ref.dtype)
        return excess

    excess = all_blocks(safe=False)

    @pl.when(jnp.max(excess) > LAZY_MARGIN)
    def _():
        all_blocks(safe=True)


def _swa(sqt, sk, svt, sinks, rel_bias, *, nb):
    nq, s = sqt.shape
    nkv = sk.shape[1]
    tb = nb * BLK
    bucket = jnp.asarray(_t5_bucket_table_t())
    smem = pl.BlockSpec(memory_space=pltpu.SMEM)
    prev_blk = lambda i: jnp.maximum(i * nb - 1, 0)
    return pl.pallas_call(
        functools.partial(_swa_kernel, nb=nb),
        grid=(s // tb,),
        in_specs=[smem, smem, _resident((2 * BLK, BLK)),
                  pl.BlockSpec((nq, tb), lambda i: (0, i)),
                  pl.BlockSpec((tb, nkv), lambda i: (i, 0)),
                  pl.BlockSpec((BLK, nkv), lambda i: (prev_blk(i), 0)),
                  pl.BlockSpec((nkv, tb), lambda i: (0, i)),
                  pl.BlockSpec((nkv, BLK), lambda i: (0, prev_blk(i)))],
        out_specs=pl.BlockSpec((tb, nq), lambda i: (i, 0)),
        out_shape=jax.ShapeDtypeStruct((s, nq), BF16),
        scratch_shapes=[pltpu.VMEM((2, SWA_KV_HEADS, 2 * BLK, SWA_GROUP * BLK), F32),
                        pltpu.VMEM((2, SWA_KV_HEADS, 2 * BLK, SWA_GROUP * BLK), F32),
                        pltpu.VMEM((SWA_KV_HEADS, 1, SWA_GROUP * BLK), F32)],
        compiler_params=_cparams(("arbitrary",)),
        name="swa",
    )(rel_bias, sinks, bucket, sqt, sk, sk, svt, svt)


def _rms(y, gain):
    r = lax.rsqrt(jnp.sum(y * y, axis=-1, keepdims=True) * (1.0 / y.shape[-1]) + RMS_EPS)
    return y * r * gain


def _oproj_kernel(a_ref, b_ref, wo_ref, x_ref, gpost_ref, gpre_ref, x1_ref, h_ref, *, sub):
    na = a_ref.shape[1]
    for r0 in range(0, x_ref.shape[0], sub):
        rows = slice(r0, r0 + sub)
        mix = (jnp.dot(a_ref[rows, :], wo_ref[:na, :], preferred_element_type=F32)
               + jnp.dot(b_ref[rows, :], wo_ref[na:, :], preferred_element_type=F32))
        x1 = x_ref[rows, :] + _rms(mix, gpost_ref[...])
        x1_ref[rows, :] = x1
        h_ref[rows, :] = _rms(x1, gpre_ref[...]).astype(BF16)


def _oproj(out_a, out_b, w_o, x2, g_post, g_pre, *, tm, sub):
    s, d = x2.shape
    row = lambda w: pl.BlockSpec((tm, w), lambda i: (i, 0))
    return pl.pallas_call(
        functools.partial(_oproj_kernel, sub=sub),
        grid=(s // tm,),
        in_specs=[row(out_a.shape[1]), row(out_b.shape[1]), _resident(w_o.shape),
                  row(d), _resident((1, d)), _resident((1, d))],
        out_specs=[row(d), row(d)],
        out_shape=[jax.ShapeDtypeStruct((s, d), F32), jax.ShapeDtypeStruct((s, d), BF16)],
        compiler_params=_cparams(("arbitrary",)),
        name="oproj",
    )(out_a, out_b, w_o, x2, g_post, g_pre)


def _ffn_kernel(h_ref, wg_ref, wu_ref, wd_ref, x1_hbm, gpost_ref, o_ref, x1_buf, x1_sem, *, sub):
    i = pl.program_id(0)
    j = pl.program_id(1)
    tm = o_ref.shape[0]

    def x1_copy():
        rows = pl.ds(pl.multiple_of(i * tm, tm), tm)
        return pltpu.make_async_copy(x1_hbm.at[rows, :], x1_buf, x1_sem)

    tf = wg_ref.shape[1]
    halves = [slice(0, tf // 2), slice(tf // 2, tf)]

    def down_products(rows):
        h = h_ref[rows, :]
        gu = [(jnp.dot(h, wg_ref[:, c], preferred_element_type=F32),
               jnp.dot(h, wu_ref[:, c], preferred_element_type=F32)) for c in halves]
        for c, (gate, up) in zip(halves, gu):
            act = (gate * jax.nn.sigmoid(gate) * up).astype(BF16)
            yield jnp.dot(act, wd_ref[c, :], preferred_element_type=F32)

    last = pl.num_programs(1) - 1

    @pl.when(j == 0)
    def _():
        x1_copy().start()
        terms = down_products(slice(0, tm))
        o_ref[...] = next(terms)
        for term in terms:
            o_ref[...] += term

    @pl.when((j > 0) & (j < last))
    def _():
        for term in down_products(slice(0, tm)):
            o_ref[...] += term

    @pl.when(j == last)
    def _():
        x1_copy().wait()
        for r in range(tm // sub):
            rows = slice(r * sub, (r + 1) * sub)
            total = o_ref[rows, :]
            for term in down_products(rows):
                total = total + term
            o_ref[rows, :] = x1_buf[rows, :] + _rms(total, gpost_ref[...])


def _ffn(h, wg, wu, wd, x1, g_post, *, tm, tf, sub):
    s, d = x1.shape
    dff = wg.shape[1]
    assert tm % sub == 0, (tm, sub)
    assert dff // tf >= 2, "the first d_ff tile assigns the accumulator and the last one finishes it"
    return pl.pallas_call(
        functools.partial(_ffn_kernel, sub=sub),
        grid=(s // tm, dff // tf),
        in_specs=[pl.BlockSpec((tm, d), lambda i, j: (i, 0)),
                  pl.BlockSpec((d, tf), lambda i, j: (0, j)),
                  pl.BlockSpec((d, tf), lambda i, j: (0, j)),
                  pl.BlockSpec((tf, d), lambda i, j: (j, 0)),
                  pl.BlockSpec(memory_space=pl.ANY),
                  _resident((1, d))],
        out_specs=pl.BlockSpec((tm, d), lambda i, j: (i, 0)),
        out_shape=jax.ShapeDtypeStruct((s, d), F32),
        scratch_shapes=[pltpu.VMEM((tm, d), F32), pltpu.SemaphoreType.DMA(())],
        compiler_params=_cparams(("arbitrary", "arbitrary")),
        name="ffn",
    )(h, wg, wu, wd, x1, g_post)


class _Tiles(NamedTuple):
    proj_rows: int = 512
    proj_sub: int = 256
    qkv_rows: int = 1024
    mla_tq: int = 1024
    mla_tk: int = 512
    swa_blocks: int = 8
    ffn_rows: int = 1024
    ffn_cols: int = 512
    ffn_sub: int = 256


def _layer(x2, pos_row, p, t=_Tiles()):
    half = MLA_ROPE // 2
    inv_freq = ROPE_THETA ** (-jnp.arange(half, dtype=F32) / half)
    invf = jnp.broadcast_to(inv_freq[:, None], (half, t.qkv_rows))

    row = lambda a: a[None, :]
    cq, ckv, krt, sqt, sk, svt = _proj(x2, row(p["g_mix_pre"]), p["w_in"].T, row(p["g_cq"]),
                                       row(p["g_ckv"]), tm=t.proj_rows, sub=t.proj_sub)
    qt, k, vt = _qkv(cq, ckv, krt, pos_row, invf, p["w_uq"], p["w_ukv"], tm=t.qkv_rows)
    out_a, (w_o, w_gate, w_up, w_down) = _mla(
        qt, k, vt, [p["w_o"], p["w_gate"], p["w_up"], p["w_down"]],
        tq=t.mla_tq, tk=t.mla_tk)
    out_b = _swa(sqt, sk, svt, p["sinks"], p["rel_bias"], nb=t.swa_blocks)
    x1, h = _oproj(out_a, out_b, w_o, x2, row(p["g_mix_post"]), row(p["g_ffn_pre"]),
                   tm=t.proj_rows, sub=t.proj_sub)
    return _ffn(h, w_gate, w_up, w_down, x1, row(p["g_ffn_post"]), tm=t.ffn_rows, tf=t.ffn_cols,
                sub=t.ffn_sub)


def kernel(x, positions, g_mix_pre, w_in, g_cq, g_ckv, w_uq, w_ukv, sinks, rel_bias,
           w_o, g_mix_post, g_ffn_pre, w_gate, w_up, w_down, g_ffn_post):
    b, s, d = x.shape
    assert b == 1, "the row-major (S, D) pipeline assumes a single sequence"
    n_in = Q_RANK + KV_RANK + MLA_ROPE + (SWA_HEADS + 2 * SWA_KV_HEADS) * SWA_DIM
    d_mix = MLA_HEADS * MLA_V + SWA_HEADS * SWA_DIM
    assert w_in.shape[1:] == (d, n_in) and w_o.shape[1:] == (d_mix, d), (w_in.shape, w_o.shape)
    assert w_uq.shape[1:] == (Q_RANK, MLA_HEADS * MLA_QK), w_uq.shape
    assert w_ukv.shape[1:] == (KV_RANK, MLA_HEADS * (MLA_NOPE + MLA_V)), w_ukv.shape
    assert rel_bias.shape == (NUM_BUCKETS, SWA_HEADS) and sinks.shape[1:] == (SWA_HEADS,)
    x2 = x.reshape(s, d)
    pos_row = positions.reshape(1, s)
    for layer in range(w_in.shape[0]):
        p = dict(g_mix_pre=g_mix_pre[layer], w_in=w_in[layer], g_cq=g_cq[layer],
                 g_ckv=g_ckv[layer], w_uq=w_uq[layer], w_ukv=w_ukv[layer],
                 sinks=sinks[layer], rel_bias=rel_bias, w_o=w_o[layer],
                 g_mix_post=g_mix_post[layer], g_ffn_pre=g_ffn_pre[layer],
                 w_gate=w_gate[layer], w_up=w_up[layer], w_down=w_down[layer],
                 g_ffn_post=g_ffn_post[layer])
        x2 = _layer(x2, pos_row, p)
    return x2.reshape(b, s, d)
```

```python
import functools
import math
from typing import NamedTuple

import jax
import jax.numpy as jnp
import numpy as np
from jax import lax
from jax.experimental import pallas as pl
from jax.experimental.pallas import tpu as pltpu

F32 = jnp.float32
BF16 = jnp.bfloat16

MLA_HEADS = 8
MLA_NOPE = 128
MLA_ROPE = 64
MLA_QK = MLA_NOPE + MLA_ROPE
MLA_V = 128
Q_RANK = 512
KV_RANK = 512
ROPE_THETA = 10000.0
SWA_HEADS = 16
SWA_KV_HEADS = 2
SWA_GROUP = SWA_HEADS // SWA_KV_HEADS
SWA_DIM = 64
WINDOW = 128
NUM_BUCKETS = 32
MAX_DISTANCE = 128
BLK = 128
RMS_EPS = 1e-6
NEG = -1e30
LOG2E = math.log2(math.e)
LAZY_MARGIN = 32.0

LANES = 128
ONES_ROWS = 16
VMEM_LIMIT = 60 * 1024 * 1024


def _cparams(sem):
    return pltpu.CompilerParams(dimension_semantics=sem, vmem_limit_bytes=VMEM_LIMIT)


_NT = (((1,), (1,)), ((), ()))


def _resident(shape):
    nd = len(shape)
    return pl.BlockSpec(shape, lambda *_: (0,) * nd, pipeline_mode=pl.Buffered(1))


def _proj_kernel(x_ref, g_ref, wt32_ref, gcq_ref, gckv_ref,
                 cq_ref, ckv_ref, krt_ref, sqt_ref, sk_ref, svt_ref, wt_ref, *, swa_scale, sub):
    nq = SWA_HEADS * SWA_DIM
    nkv = SWA_KV_HEADS * SWA_DIM
    nl = Q_RANK + KV_RANK

    @pl.when(pl.program_id(0) == 0)
    def _():
        wt_ref[...] = wt32_ref[...].astype(BF16)

    def latent_norm(c, gain_ref):
        rc = lax.rsqrt(jnp.sum(c * c, axis=-1, keepdims=True) * (1.0 / c.shape[-1]) + RMS_EPS)
        return (c * rc * gain_ref[...]).astype(BF16)

    for r0 in range(0, x_ref.shape[0], sub):
        rows = slice(r0, r0 + sub)
        x = x_ref[rows, :]
        r = lax.rsqrt(jnp.sum(x * x, axis=-1, keepdims=True) * (1.0 / x.shape[-1]) + RMS_EPS)
        h = (x * g_ref[...]).astype(BF16)
        yt = lax.dot_general(wt_ref[...], h, _NT, preferred_element_type=F32)
        c = yt[:nl].T * r
        cq_ref[rows, :] = latent_norm(c[:, :Q_RANK], gcq_ref)
        ckv_ref[rows, :] = latent_norm(c[:, Q_RANK:], gckv_ref)
        r_lanes = jnp.broadcast_to(r, (sub, LANES)).T[0:1, :]
        tail = yt[nl:] * r_lanes
        o = MLA_ROPE
        krt_ref[:, rows] = tail[:o]
        sqt_ref[:, rows] = (tail[o:o + nq] * swa_scale).astype(BF16)
        sk_ref[rows, :] = tail[o + nq:o + nq + nkv].T.astype(BF16)
        svt_ref[:, rows] = tail[o + nq + nkv:o + nq + 2 * nkv].astype(BF16)


def _proj(x2, g, w_in_t, g_cq, g_ckv, *, tm, sub):
    s, d = x2.shape
    nq = SWA_HEADS * SWA_DIM
    nkv = SWA_KV_HEADS * SWA_DIM
    row = lambda w: pl.BlockSpec((tm, w), lambda i: (i, 0))
    col = lambda r: pl.BlockSpec((r, tm), lambda i: (0, i))
    return pl.pallas_call(
        functools.partial(_proj_kernel, swa_scale=LOG2E / math.sqrt(SWA_DIM), sub=sub),
        grid=(s // tm,),
        in_specs=[row(d), _resident((1, d)), _resident(w_in_t.shape),
                  _resident((1, Q_RANK)), _resident((1, KV_RANK))],
        out_specs=[row(Q_RANK), row(KV_RANK), col(MLA_ROPE), col(nq), row(nkv), col(nkv)],
        out_shape=[jax.ShapeDtypeStruct((s, Q_RANK), BF16),
                   jax.ShapeDtypeStruct((s, KV_RANK), BF16),
                   jax.ShapeDtypeStruct((MLA_ROPE, s), F32),
                   jax.ShapeDtypeStruct((nq, s), BF16),
                   jax.ShapeDtypeStruct((s, nkv), BF16),
                   jax.ShapeDtypeStruct((nkv, s), BF16)],
        scratch_shapes=[pltpu.VMEM(w_in_t.shape, BF16)],
        compiler_params=_cparams(("arbitrary",)),
        name="proj",
    )(x2, g, w_in_t, g_cq, g_ckv)


def _qkv_kernel(cq_ref, ckv_ref, krt_ref, pos_ref, invf_ref, wuq_ref, wukv_ref,
                qt_ref, k_ref, vt_ref, wuqt_ref, wuk_ref, wuvt_ref, *, q_scale):
    tm = cq_ref.shape[0]
    half = MLA_ROPE // 2

    @pl.when(pl.program_id(0) == 0)
    def _():
        wuqt_ref[...] = wuq_ref[...].T.astype(BF16)
        for h in range(MLA_HEADS):
            c0 = h * (MLA_NOPE + MLA_V)
            wuk_ref[:, h * MLA_NOPE:(h + 1) * MLA_NOPE] = wukv_ref[:, c0:c0 + MLA_NOPE].astype(BF16)
            wuvt_ref[h * MLA_V:(h + 1) * MLA_V, :] = (
                wukv_ref[:, c0 + MLA_NOPE:c0 + MLA_NOPE + MLA_V].T.astype(BF16))
    ang = invf_ref[...] * pos_ref[...].astype(F32)
    cos = jnp.cos(ang)
    sin = jnp.sin(ang)

    def rope_t(t):
        t1, t2 = t[:half], t[half:]
        return jnp.concatenate([t1 * cos - t2 * sin, t2 * cos + t1 * sin], axis=0)

    cq = cq_ref[...]
    ckv = ckv_ref[...]
    qt = lax.dot_general(wuqt_ref[...], cq, _NT, preferred_element_type=F32) * q_scale
    for h in range(MLA_HEADS):
        base = h * MLA_QK
        qt_ref[h, 0:MLA_NOPE, :] = qt[base:base + MLA_NOPE].astype(BF16)
        qt_ref[h, MLA_NOPE:MLA_QK, :] = rope_t(qt[base + MLA_NOPE:base + MLA_QK]).astype(BF16)

    krt = jnp.concatenate([rope_t(krt_ref[...]), jnp.zeros((LANES - MLA_ROPE, tm), F32)], axis=0)
    kr = krt.T[:, :MLA_ROPE].astype(BF16)
    kn = jnp.dot(ckv, wuk_ref[...], preferred_element_type=F32)
    vt = lax.dot_general(wuvt_ref[...], ckv, _NT, preferred_element_type=F32)
    for h in range(MLA_HEADS):
        k_ref[h, :, 0:MLA_NOPE] = kn[:, h * MLA_NOPE:(h + 1) * MLA_NOPE].astype(BF16)
        k_ref[h, :, MLA_NOPE:MLA_QK] = kr
        for c in range(tm // LANES):
            vt_ref[h, c] = vt[h * MLA_V:(h + 1) * MLA_V, c * LANES:(c + 1) * LANES].astype(BF16)


def _qkv(cq, ckv, krt, pos_row, invf, w_uq, w_ukv, *, tm):
    s = cq.shape[0]
    rank = w_uq.shape[0]
    row = lambda w: pl.BlockSpec((tm, w), lambda i: (i, 0))
    col = lambda r: pl.BlockSpec((r, tm), lambda i: (0, i))
    nb = tm // LANES
    return pl.pallas_call(
        functools.partial(_qkv_kernel, q_scale=LOG2E / math.sqrt(MLA_QK)),
        grid=(s // tm,),
        in_specs=[row(Q_RANK), row(KV_RANK), col(MLA_ROPE), col(1), _resident(invf.shape),
                  _resident(w_uq.shape), _resident(w_ukv.shape)],
        out_specs=[pl.BlockSpec((MLA_HEADS, MLA_QK, tm), lambda i: (0, 0, i)),
                   pl.BlockSpec((MLA_HEADS, tm, MLA_QK), lambda i: (0, i, 0)),
                   pl.BlockSpec((MLA_HEADS, nb, MLA_V, LANES), lambda i: (0, i, 0, 0))],
        out_shape=[jax.ShapeDtypeStruct((MLA_HEADS, MLA_QK, s), BF16),
                   jax.ShapeDtypeStruct((MLA_HEADS, s, MLA_QK), BF16),
                   jax.ShapeDtypeStruct((MLA_HEADS, s // LANES, MLA_V, LANES), BF16)],
        scratch_shapes=[pltpu.VMEM((MLA_HEADS * MLA_QK, rank), BF16),
                        pltpu.VMEM((rank, MLA_HEADS * MLA_NOPE), BF16),
                        pltpu.VMEM((MLA_HEADS * MLA_V, rank), BF16)],
        compiler_params=_cparams(("arbitrary",)),
        name="qkv",
    )(cq, ckv, krt, pos_row, invf, w_uq, w_ukv)


def _mla_kernel(*refs, tq, tk, ncast):
    qt_ref, k_ref, vt_ref = refs[:3]
    w_hbm = refs[3:3 + ncast]
    o_ref = refs[3 + ncast]
    wout_hbm = refs[4 + ncast:4 + 2 * ncast]
    m_sc, acc_sc, s_sc, smax_sc, excess_sc = refs[4 + 2 * ncast:9 + 2 * ncast]
    stage_in = refs[9 + 2 * ncast:9 + 3 * ncast]
    stage_out = refs[9 + 3 * ncast:9 + 4 * ncast]
    in_sems, out_sems = refs[9 + 4 * ncast:]

    step = pl.program_id(0)
    last_step = pl.num_programs(0) - 1

    def slab_in(w):
        rows = stage_in[w].shape[0]
        src = w_hbm[w].at[pl.ds(pl.multiple_of(step * rows, rows), rows), :]
        return pltpu.make_async_copy(src, stage_in[w], in_sems.at[w])

    def slab_out(w, at_step):
        rows = stage_out[w].shape[0]
        dst = wout_hbm[w].at[pl.ds(pl.multiple_of(at_step * rows, rows), rows), :]
        return pltpu.make_async_copy(stage_out[w], dst, out_sems.at[w])

    for w in range(ncast):
        slab_in(w).start()

    group = tq // tk
    assert group * tk == tq and group % 2 == 0
    nsub = tk // LANES
    full = slice(0, tq)
    lazy_unroll = 4

    def one_query_block(qi, safe):
        def at(i, size):
            return pl.ds(i * size if isinstance(i, int) else pl.multiple_of(i * size, size), size)

        def queries(cols=full):
            if isinstance(qi, int):
                return slice(qi * tq + cols.start, qi * tq + cols.stop)
            return pl.ds(pl.multiple_of(qi * tq + cols.start, tk), cols.stop - cols.start)

        def reset():
            m_sc[...] = jnp.full(m_sc.shape, NEG, F32)
            acc_sc[...] = jnp.zeros(acc_sc.shape, F32)

        def raw_scores(j, cols=full):
            return jnp.dot(k_ref[0, at(j, tk), :], qt_ref[0, :, queries(cols)],
                           preferred_element_type=F32)

        def values_t(j):
            vt = jnp.concatenate([vt_ref[0, j * nsub + c] for c in range(nsub)], axis=1)
            return jnp.concatenate([vt, jnp.ones((ONES_ROWS, tk), BF16)], axis=0)

        def scores(j, slot, cols=full):
            s = raw_scores(j, cols)
            s_sc[slot, :, cols] = s
            smax_sc[slot, :, cols] = jnp.max(s, axis=0, keepdims=True)

        def update(j, slot, cols=full, tri=False):
            s = s_sc[slot, :, cols]
            if tri:
                mask = (lax.broadcasted_iota(jnp.int32, (tk, tk), 0)
                        <= lax.broadcasted_iota(jnp.int32, (tk, tk), 1))
                parts = [jnp.where(mask, s[:, :tk], NEG)] + ([s[:, tk:]] if s.shape[1] > tk else [])
                s = jnp.concatenate(parts, axis=1)
            s_max = jnp.max(s, axis=0, keepdims=True) if tri else smax_sc[slot, :, cols]
            m_old = m_sc[:, cols]
            m_new = jnp.maximum(m_old, s_max)
            alpha = jnp.exp2(m_old - m_new)
            p = jnp.exp2(s - m_new)
            acc_sc[:, cols] = (alpha * acc_sc[:, cols]
                               + jnp.dot(values_t(j), p.astype(BF16), preferred_element_type=F32))
            m_sc[:, cols] = m_new

        def full_group(i, carry):
            for r in range(group):
                b = group * i + r
                scores(b + 1, (r + 1) % 2)
                update(b, r % 2)
            return carry

        def diagonal_group():
            for r in range(group):
                b = group * qi + r
                if r + 1 < group:
                    scores(b + 1, (r + 1) % 2, slice((r + 1) * tk, tq))
                update(b, r % 2, slice(r * tk, tq), tri=True)

        def finalize():
            out_t = acc_sc[:MLA_V, :] / acc_sc[MLA_V:MLA_V + 1, :]
            o_ref[at(qi, tq), :] = out_t.T.astype(o_ref.dtype)

        if safe:
            reset()
            scores(0, 0)
            lax.fori_loop(0, qi, full_group, 0)
            diagonal_group()
            finalize()
            return

        reset()
        excess_sc[:, queries()] = jnp.full((1, tq), NEG, F32)
        scores(group * qi, 0)
        diagonal_group()

        def lazy_blocks(first, count):
            m = m_sc[...]
            for r in range(count):
                j = first + r
                s = raw_scores(j)
                excess_sc[:, queries()] = jnp.maximum(excess_sc[:, queries()],
                                                      jnp.max(s, axis=0, keepdims=True) - m)
                acc_sc[...] += jnp.dot(values_t(j), jnp.exp2(s - m).astype(BF16),
                                       preferred_element_type=F32)

        def lazy_trip(i, carry):
            lazy_blocks(lazy_unroll * i, lazy_unroll)
            return carry

        trips, rest = divmod(group * qi, lazy_unroll)
        if trips:
            lax.fori_loop(0, trips, lazy_trip, 0)
        if rest:
            lazy_blocks(trips * lazy_unroll, rest)
        finalize()

    nq = qt_ref.shape[2] // tq
    for qi in range(nq):
        one_query_block(qi, safe=False)

    def redo_if_needed(qi, carry):
        @pl.when(jnp.max(excess_sc[:, pl.ds(pl.multiple_of(qi * tq, tq), tq)]) > LAZY_MARGIN)
        def _():
            one_query_block(qi, safe=True)
        return carry

    lax.fori_loop(0, nq, redo_if_needed, 0)

    @pl.when(step > 0)
    def _():
        for w in range(ncast):
            slab_out(w, step - 1).wait()

    for w in range(ncast):
        slab_in(w).wait()
        stage_out[w][...] = stage_in[w][...].astype(BF16)
        slab_out(w, step).start()

    @pl.when(step == last_step)
    def _():
        for w in range(ncast):
            slab_out(w, step).wait()


def _mla(qt, k, vt, cast_weights, *, tq, tk):
    _, s, _ = k.shape
    assert s % tq == 0, (s, tq)
    nsteps = MLA_HEADS
    ncast = len(cast_weights)
    slabs = [(w.shape[0] // nsteps, w.shape[1]) for w in cast_weights]
    for w, (rows, _) in zip(cast_weights, slabs):
        assert rows * nsteps == w.shape[0] and rows % 16 == 0, (w.shape, nsteps)
    anywhere = pl.BlockSpec(memory_space=pl.ANY)
    outs = pl.pallas_call(
        functools.partial(_mla_kernel, tq=tq, tk=tk, ncast=ncast),
        grid=(MLA_HEADS,),
        in_specs=[pl.BlockSpec((1, MLA_QK, s), lambda h: (h, 0, 0)),
                  pl.BlockSpec((1, s, MLA_QK), lambda h: (h, 0, 0)),
                  pl.BlockSpec((1, s // LANES, MLA_V, LANES), lambda h: (h, 0, 0, 0))]
                 + [anywhere] * ncast,
        out_specs=[pl.BlockSpec((s, MLA_V), lambda h: (0, h))] + [anywhere] * ncast,
        out_shape=[jax.ShapeDtypeStruct((s, MLA_HEADS * MLA_V), BF16)]
                  + [jax.ShapeDtypeStruct(w.shape, BF16) for w in cast_weights],
        scratch_shapes=[pltpu.VMEM((1, tq), F32),
                        pltpu.VMEM((MLA_V + ONES_ROWS, tq), F32),
                        pltpu.VMEM((2, tk, tq), F32), pltpu.VMEM((2, 1, tq), F32),
                        pltpu.VMEM((1, s), F32)]
                       + [pltpu.VMEM(sl, F32) for sl in slabs]
                       + [pltpu.VMEM(sl, BF16) for sl in slabs]
                       + [pltpu.SemaphoreType.DMA((ncast,)), pltpu.SemaphoreType.DMA((ncast,))],
        compiler_params=_cparams(("arbitrary",)),
        name="mla",
    )(qt, k, vt, *cast_weights)
    return outs[0], outs[1:]


def _t5_bucket_table_t():
    i = np.arange(BLK)[None, :]
    j = np.arange(2 * BLK)[:, None]
    dist = i + BLK - j
    max_exact = NUM_BUCKETS // 2
    d = np.maximum(dist, 0)
    large = max_exact + (np.log(np.maximum(d, 1) / max_exact)
                         / np.log(MAX_DISTANCE / max_exact)
                         * (NUM_BUCKETS - max_exact)).astype(np.int32)
    large = np.minimum(large, NUM_BUCKETS - 1)
    bucket = np.where(d < max_exact, d, large).astype(np.int32)
    in_window = (dist >= 0) & (dist < WINDOW)
    return np.where(in_window, bucket, -1).astype(np.int32)


def _swa_kernel(relb_ref, sink_ref, bucket_ref, qt_ref, kc_ref, kp_ref, vtc_ref, vtp_ref,
                o_ref, bias_sc, biasrel_sc, sink_sc, *, nb):
    i = pl.program_id(0)
    gw = SWA_GROUP * BLK

    @pl.when(i == 0)
    def _():
        bucket = bucket_ref[...]
        for h in range(SWA_HEADS):
            c, g = divmod(h, SWA_GROUP)
            lanes = slice(g * BLK, (g + 1) * BLK)
            sink = sink_ref[h] * LOG2E
            b = jnp.full(bucket.shape, NEG, F32)
            for t in range(NUM_BUCKETS):
                b = jnp.where(bucket == t, relb_ref[t, h] * LOG2E, b)
            masked = jnp.full((BLK, BLK), NEG, F32)
            for table, tb in ((bias_sc, b), (biasrel_sc, jnp.where(bucket < 0, NEG, b - sink))):
                table[1, c, :, lanes] = tb
                table[0, c, BLK:, lanes] = tb[BLK:]
                table[0, c, :BLK, lanes] = masked
            sink_sc[c, :, lanes] = jnp.full((1, BLK), sink, F32)

    zeros = jnp.zeros((SWA_DIM, gw), BF16)

    def all_blocks(safe):
        excess = jnp.full((8, gw), NEG, F32)
        for t in range(nb):
            tsl = slice(t * BLK, (t + 1) * BLK)
            k_prev = kp_ref[...] if t == 0 else kc_ref[(t - 1) * BLK:t * BLK, :]
            vt_prev = vtp_ref[...] if t == 0 else vtc_ref[:, (t - 1) * BLK:t * BLK]
            kband = jnp.concatenate([k_prev, kc_ref[tsl, :]], axis=0)
            general = 1 if t > 0 else jnp.where(i == 0, 0, 1)
            for c in range(SWA_KV_HEADS):
                qt = jnp.concatenate(
                    [qt_ref[(c * SWA_GROUP + g) * SWA_DIM:(c * SWA_GROUP + g + 1) * SWA_DIM, tsl]
                     for g in range(SWA_GROUP)], axis=1)
                qt_ext = jnp.concatenate([qt, zeros] if c == 0 else [zeros, qt], axis=0)
                raw = jnp.dot(kband, qt_ext, preferred_element_type=F32)
                if safe:
                    s = raw + bias_sc[general, c]
                    sink = sink_sc[c]
                    m = jnp.maximum(jnp.max(s, axis=0, keepdims=True), sink)
                    p = jnp.exp2(s - m)
                    sink_term = jnp.exp2(sink - m)
                else:
                    s = raw + biasrel_sc[general, c]
                    excess = functools.reduce(
                        jnp.maximum, [s[8 * r:8 * r + 8] for r in range(2 * BLK // 8)], excess)
                    p = jnp.exp2(s)
                    sink_term = 1.0
                dsl = slice(c * SWA_DIM, (c + 1) * SWA_DIM)
                vt = jnp.concatenate([vt_prev[dsl, :], vtc_ref[dsl, tsl]], axis=1)
                vt = jnp.concatenate([vt, jnp.ones((ONES_ROWS, 2 * BLK), BF16)], axis=0)
                pv = jnp.dot(vt, p.astype(BF16), preferred_element_type=F32)
                ot = pv[:SWA_DIM] / (pv[SWA_DIM:SWA_DIM + 1] + sink_term)
                for g2 in range(SWA_GROUP // 2):
                    two = jnp.concatenate([ot[:, (2 * g2) * BLK:(2 * g2 + 1) * BLK],
                                           ot[:, (2 * g2 + 1) * BLK:(2 * g2 + 2) * BLK]], axis=0)
                    col0 = (c * SWA_GROUP + 2 * g2) * SWA_DIM
                    o_ref[tsl, col0:col0 + 2 * SWA_DIM] = two.T.astype(o_ref.dtype)
        return excess

    excess = all_blocks(safe=False)

    @pl.when(jnp.max(excess) > LAZY_MARGIN)
    def _():
        all_blocks(safe=True)


def _swa(sqt, sk, svt, sinks, rel_bias, *, nb):
    nq, s = sqt.shape
    nkv = sk.shape[1]
    tb = nb * BLK
    bucket = jnp.asarray(_t5_bucket_table_t())
    smem = pl.BlockSpec(memory_space=pltpu.SMEM)
    prev_blk = lambda i: jnp.maximum(i * nb - 1, 0)
    return pl.pallas_call(
        functools.partial(_swa_kernel, nb=nb),
        grid=(s // tb,),
        in_specs=[smem, smem, _resident((2 * BLK, BLK)),
                  pl.BlockSpec((nq, tb), lambda i: (0, i)),
                  pl.BlockSpec((tb, nkv), lambda i: (i, 0)),
                  pl.BlockSpec((BLK, nkv), lambda i: (prev_blk(i), 0)),
                  pl.BlockSpec((nkv, tb), lambda i: (0, i)),
                  pl.BlockSpec((nkv, BLK), lambda i: (0, prev_blk(i)))],
        out_specs=pl.BlockSpec((tb, nq), lambda i: (i, 0)),
        out_shape=jax.ShapeDtypeStruct((s, nq), BF16),
        scratch_shapes=[pltpu.VMEM((2, SWA_KV_HEADS, 2 * BLK, SWA_GROUP * BLK), F32),
                        pltpu.VMEM((2, SWA_KV_HEADS, 2 * BLK, SWA_GROUP * BLK), F32),
                        pltpu.VMEM((SWA_KV_HEADS, 1, SWA_GROUP * BLK), F32)],
        compiler_params=_cparams(("arbitrary",)),
        name="swa",
    )(rel_bias, sinks, bucket, sqt, sk, sk, svt, svt)


def _rms(y, gain):
    r = lax.rsqrt(jnp.sum(y * y, axis=-1, keepdims=True) * (1.0 / y.shape[-1]) + RMS_EPS)
    return y * r * gain


def _oproj_kernel(a_ref, b_ref, wo_ref, x_ref, gpost_ref, gpre_ref, x1_ref, h_ref, *, sub):
    na = a_ref.shape[1]
    for r0 in range(0, x_ref.shape[0], sub):
        rows = slice(r0, r0 + sub)
        mix = (jnp.dot(a_ref[rows, :], wo_ref[:na, :], preferred_element_type=F32)
               + jnp.dot(b_ref[rows, :], wo_ref[na:, :], preferred_element_type=F32))
        x1 = x_ref[rows, :] + _rms(mix, gpost_ref[...])
        x1_ref[rows, :] = x1
        h_ref[rows, :] = _rms(x1, gpre_ref[...]).astype(BF16)


def _oproj(out_a, out_b, w_o, x2, g_post, g_pre, *, tm, sub):
    s, d = x2.shape
    row = lambda w: pl.BlockSpec((tm, w), lambda i: (i, 0))
    return pl.pallas_call(
        functools.partial(_oproj_kernel, sub=sub),
        grid=(s // tm,),
        in_specs=[row(out_a.shape[1]), row(out_b.shape[1]), _resident(w_o.shape),
                  row(d), _resident((1, d)), _resident((1, d))],
        out_specs=[row(d), row(d)],
        out_shape=[jax.ShapeDtypeStruct((s, d), F32), jax.ShapeDtypeStruct((s, d), BF16)],
        compiler_params=_cparams(("arbitrary",)),
        name="oproj",
    )(out_a, out_b, w_o, x2, g_post, g_pre)


def _ffn_kernel(h_ref, wg_ref, wu_ref, wd_ref, x1_hbm, gpost_ref, o_ref, x1_buf, x1_sem, *, sub):
    i = pl.program_id(0)
    j = pl.program_id(1)
    tm = o_ref.shape[0]

    def x1_copy():
        rows = pl.ds(pl.multiple_of(i * tm, tm), tm)
        return pltpu.make_async_copy(x1_hbm.at[rows, :], x1_buf, x1_sem)

    @pl.when(j == 0)
    def _():
        x1_copy().start()
        o_ref[...] = jnp.zeros(o_ref.shape, F32)

    tf = wg_ref.shape[1]
    halves = [slice(0, tf // 2), slice(tf // 2, tf)]

    def down_products(rows):
        h = h_ref[rows, :]
        gu = [(jnp.dot(h, wg_ref[:, c], preferred_element_type=F32),
               jnp.dot(h, wu_ref[:, c], preferred_element_type=F32)) for c in halves]
        for c, (gate, up) in zip(halves, gu):
            act = (gate * jax.nn.sigmoid(gate) * up).astype(BF16)
            yield jnp.dot(act, wd_ref[c, :], preferred_element_type=F32)

    last = pl.num_programs(1) - 1

    @pl.when(j < last)
    def _():
        for term in down_products(slice(0, tm)):
            o_ref[...] += term

    @pl.when(j == last)
    def _():
        x1_copy().wait()
        for r in range(tm // sub):
            rows = slice(r * sub, (r + 1) * sub)
            total = o_ref[rows, :]
            for term in down_products(rows):
                total = total + term
            o_ref[rows, :] = x1_buf[rows, :] + _rms(total, gpost_ref[...])


def _ffn(h, wg, wu, wd, x1, g_post, *, tm, tf, sub):
    s, d = x1.shape
    dff = wg.shape[1]
    assert tm % sub == 0, (tm, sub)
    return pl.pallas_call(
        functools.partial(_ffn_kernel, sub=sub),
        grid=(s // tm, dff // tf),
        in_specs=[pl.BlockSpec((tm, d), lambda i, j: (i, 0)),
                  pl.BlockSpec((d, tf), lambda i, j: (0, j)),
                  pl.BlockSpec((d, tf), lambda i, j: (0, j)),
                  pl.BlockSpec((tf, d), lambda i, j: (j, 0)),
                  pl.BlockSpec(memory_space=pl.ANY),
                  _resident((1, d))],
        out_specs=pl.BlockSpec((tm, d), lambda i, j: (i, 0)),
        out_shape=jax.ShapeDtypeStruct((s, d), F32),
        scratch_shapes=[pltpu.VMEM((tm, d), F32), pltpu.SemaphoreType.DMA(())],
        compiler_params=_cparams(("arbitrary", "arbitrary")),
        name="ffn",
    )(h, wg, wu, wd, x1, g_post)


class _Tiles(NamedTuple):
    proj_rows: int = 512
    proj_sub: int = 256
    qkv_rows: int = 1024
    mla_tq: int = 1024
    mla_tk: int = 512
    swa_blocks: int = 4
    ffn_rows: int = 1024
    ffn_cols: int = 512
    ffn_sub: int = 256


def _layer(x2, pos_row, p, t=_Tiles()):
    half = MLA_ROPE // 2
    inv_freq = ROPE_THETA ** (-jnp.arange(half, dtype=F32) / half)
    invf = jnp.broadcast_to(inv_freq[:, None], (half, t.qkv_rows))

    row = lambda a: a[None, :]
    cq, ckv, krt, sqt, sk, svt = _proj(x2, row(p["g_mix_pre"]), p["w_in"].T, row(p["g_cq"]),
                                       row(p["g_ckv"]), tm=t.proj_rows, sub=t.proj_sub)
    qt, k, vt = _qkv(cq, ckv, krt, pos_row, invf, p["w_uq"], p["w_ukv"], tm=t.qkv_rows)
    out_a, (w_o, w_gate, w_up, w_down) = _mla(
        qt, k, vt, [p["w_o"], p["w_gate"], p["w_up"], p["w_down"]],
        tq=t.mla_tq, tk=t.mla_tk)
    out_b = _swa(sqt, sk, svt, p["sinks"], p["rel_bias"], nb=t.swa_blocks)
    x1, h = _oproj(out_a, out_b, w_o, x2, row(p["g_mix_post"]), row(p["g_ffn_pre"]),
                   tm=t.proj_rows, sub=t.proj_sub)
    return _ffn(h, w_gate, w_up, w_down, x1, row(p["g_ffn_post"]), tm=t.ffn_rows, tf=t.ffn_cols,
                sub=t.ffn_sub)


def kernel(x, positions, g_mix_pre, w_in, g_cq, g_ckv, w_uq, w_ukv, sinks, rel_bias,
           w_o, g_mix_post, g_ffn_pre, w_gate, w_up, w_down, g_ffn_post):
    b, s, d = x.shape
    assert b == 1, "the row-major (S, D) pipeline assumes a single sequence"
    n_in = Q_RANK + KV_RANK + MLA_ROPE + (SWA_HEADS + 2 * SWA_KV_HEADS) * SWA_DIM
    d_mix = MLA_HEADS * MLA_V + SWA_HEADS * SWA_DIM
    assert w_in.shape[1:] == (d, n_in) and w_o.shape[1:] == (d_mix, d), (w_in.shape, w_o.shape)
    assert w_uq.shape[1:] == (Q_RANK, MLA_HEADS * MLA_QK), w_uq.shape
    assert w_ukv.shape[1:] == (KV_RANK, MLA_HEADS * (MLA_NOPE + MLA_V)), w_ukv.shape
    assert rel_bias.shape == (NUM_BUCKETS, SWA_HEADS) and sinks.shape[1:] == (SWA_HEADS,)
    x2 = x.reshape(s, d)
    pos_row = positions.reshape(1, s)
    for layer in range(w_in.shape[0]):
        p = dict(g_mix_pre=g_mix_pre[layer], w_in=w_in[layer], g_cq=g_cq[layer],
                 g_ckv=g_ckv[layer], w_uq=w_uq[layer], w_ukv=w_ukv[layer],
                 sinks=sinks[layer], rel_bias=rel_bias, w_o=w_o[layer],
                 g_mix_post=g_mix_post[layer], g_ffn_pre=g_ffn_pre[layer],
                 w_gate=w_gate[layer], w_up=w_up[layer], w_down=w_down[layer],
                 g_ffn_post=g_ffn_post[layer])
        x2 = _layer(x2, pos_row, p)
    return x2.reshape(b, s, d)
```

```python
import functools
import math
from typing import NamedTuple

import jax
import jax.numpy as jnp
import numpy as np
from jax import lax
from jax.experimental import pallas as pl
from jax.experimental.pallas import tpu as pltpu

F32 = jnp.float32
BF16 = jnp.bfloat16

MLA_HEADS = 8
MLA_NOPE = 128
MLA_ROPE = 64
MLA_QK = MLA_NOPE + MLA_ROPE
MLA_V = 128
Q_RANK = 512
KV_RANK = 512
ROPE_THETA = 10000.0
SWA_HEADS = 16
SWA_KV_HEADS = 2
SWA_GROUP = SWA_HEADS // SWA_KV_HEADS
SWA_DIM = 64
WINDOW = 128
NUM_BUCKETS = 32
MAX_DISTANCE = 128
BLK = 128
RMS_EPS = 1e-6
NEG = -1e30
LOG2E = math.log2(math.e)
LAZY_MARGIN = 32.0

LANES = 128
ONES_ROWS = 16
VMEM_LIMIT = 60 * 1024 * 1024


def _cparams(sem):
    return pltpu.CompilerParams(dimension_semantics=sem, vmem_limit_bytes=VMEM_LIMIT)


_NT = (((1,), (1,)), ((), ()))


def _resident(shape):
    nd = len(shape)
    return pl.BlockSpec(shape, lambda *_: (0,) * nd, pipeline_mode=pl.Buffered(1))


def _proj_kernel(x_ref, g_ref, wt32_ref, gcq_ref, gckv_ref,
                 cq_ref, ckv_ref, krt_ref, sqt_ref, sk_ref, svt_ref, wt_ref, *, swa_scale, sub):
    nq = SWA_HEADS * SWA_DIM
    nkv = SWA_KV_HEADS * SWA_DIM
    nl = Q_RANK + KV_RANK

    @pl.when(pl.program_id(0) == 0)
    def _():
        wt_ref[...] = wt32_ref[...].astype(BF16)

    def latent_norm(c, gain_ref):
        rc = lax.rsqrt(jnp.sum(c * c, axis=-1, keepdims=True) * (1.0 / c.shape[-1]) + RMS_EPS)
        return (c * rc * gain_ref[...]).astype(BF16)

    for r0 in range(0, x_ref.shape[0], sub):
        rows = slice(r0, r0 + sub)
        x = x_ref[rows, :]
        r = lax.rsqrt(jnp.sum(x * x, axis=-1, keepdims=True) * (1.0 / x.shape[-1]) + RMS_EPS)
        h = (x * g_ref[...]).astype(BF16)
        yt = lax.dot_general(wt_ref[...], h, _NT, preferred_element_type=F32)
        c = yt[:nl].T * r
        cq_ref[rows, :] = latent_norm(c[:, :Q_RANK], gcq_ref)
        ckv_ref[rows, :] = latent_norm(c[:, Q_RANK:], gckv_ref)
        r_lanes = jnp.broadcast_to(r, (sub, LANES)).T[0:1, :]
        tail = yt[nl:] * r_lanes
        o = MLA_ROPE
        krt_ref[:, rows] = tail[:o]
        sqt_ref[:, rows] = (tail[o:o + nq] * swa_scale).astype(BF16)
        sk_ref[rows, :] = tail[o + nq:o + nq + nkv].T.astype(BF16)
        svt_ref[:, rows] = tail[o + nq + nkv:o + nq + 2 * nkv].astype(BF16)


def _proj(x2, g, w_in_t, g_cq, g_ckv, *, tm, sub):
    s, d = x2.shape
    nq = SWA_HEADS * SWA_DIM
    nkv = SWA_KV_HEADS * SWA_DIM
    row = lambda w: pl.BlockSpec((tm, w), lambda i: (i, 0))
    col = lambda r: pl.BlockSpec((r, tm), lambda i: (0, i))
    return pl.pallas_call(
        functools.partial(_proj_kernel, swa_scale=LOG2E / math.sqrt(SWA_DIM), sub=sub),
        grid=(s // tm,),
        in_specs=[row(d), _resident((1, d)), _resident(w_in_t.shape),
                  _resident((1, Q_RANK)), _resident((1, KV_RANK))],
        out_specs=[row(Q_RANK), row(KV_RANK), col(MLA_ROPE), col(nq), row(nkv), col(nkv)],
        out_shape=[jax.ShapeDtypeStruct((s, Q_RANK), BF16),
                   jax.ShapeDtypeStruct((s, KV_RANK), BF16),
                   jax.ShapeDtypeStruct((MLA_ROPE, s), F32),
                   jax.ShapeDtypeStruct((nq, s), BF16),
                   jax.ShapeDtypeStruct((s, nkv), BF16),
                   jax.ShapeDtypeStruct((nkv, s), BF16)],
        scratch_shapes=[pltpu.VMEM(w_in_t.shape, BF16)],
        compiler_params=_cparams(("arbitrary",)),
        name="proj",
    )(x2, g, w_in_t, g_cq, g_ckv)


def _qkv_kernel(cq_ref, ckv_ref, krt_ref, pos_ref, invf_ref, wuq_ref, wukv_ref,
                qt_ref, k_ref, vt_ref, wuqt_ref, wuk_ref, wuvt_ref, *, q_scale):
    tm = cq_ref.shape[0]
    half = MLA_ROPE // 2

    @pl.when(pl.program_id(0) == 0)
    def _():
        wuqt_ref[...] = wuq_ref[...].T.astype(BF16)
        for h in range(MLA_HEADS):
            c0 = h * (MLA_NOPE + MLA_V)
            wuk_ref[:, h * MLA_NOPE:(h + 1) * MLA_NOPE] = wukv_ref[:, c0:c0 + MLA_NOPE].astype(BF16)
            wuvt_ref[h * MLA_V:(h + 1) * MLA_V, :] = (
                wukv_ref[:, c0 + MLA_NOPE:c0 + MLA_NOPE + MLA_V].T.astype(BF16))
    ang = invf_ref[...] * pos_ref[...].astype(F32)
    cos = jnp.cos(ang)
    sin = jnp.sin(ang)

    def rope_t(t):
        t1, t2 = t[:half], t[half:]
        return jnp.concatenate([t1 * cos - t2 * sin, t2 * cos + t1 * sin], axis=0)

    cq = cq_ref[...]
    ckv = ckv_ref[...]
    qt = lax.dot_general(wuqt_ref[...], cq, _NT, preferred_element_type=F32) * q_scale
    for h in range(MLA_HEADS):
        base = h * MLA_QK
        qt_ref[h, 0:MLA_NOPE, :] = qt[base:base + MLA_NOPE].astype(BF16)
        qt_ref[h, MLA_NOPE:MLA_QK, :] = rope_t(qt[base + MLA_NOPE:base + MLA_QK]).astype(BF16)

    krt = jnp.concatenate([rope_t(krt_ref[...]), jnp.zeros((LANES - MLA_ROPE, tm), F32)], axis=0)
    kr = krt.T[:, :MLA_ROPE].astype(BF16)
    kn = jnp.dot(ckv, wuk_ref[...], preferred_element_type=F32)
    vt = lax.dot_general(wuvt_ref[...], ckv, _NT, preferred_element_type=F32)
    for h in range(MLA_HEADS):
        k_ref[h, :, 0:MLA_NOPE] = kn[:, h * MLA_NOPE:(h + 1) * MLA_NOPE].astype(BF16)
        k_ref[h, :, MLA_NOPE:MLA_QK] = kr
        for c in range(tm // LANES):
            vt_ref[h, c] = vt[h * MLA_V:(h + 1) * MLA_V, c * LANES:(c + 1) * LANES].astype(BF16)


def _qkv(cq, ckv, krt, pos_row, invf, w_uq, w_ukv, *, tm):
    s = cq.shape[0]
    rank = w_uq.shape[0]
    row = lambda w: pl.BlockSpec((tm, w), lambda i: (i, 0))
    col = lambda r: pl.BlockSpec((r, tm), lambda i: (0, i))
    nb = tm // LANES
    return pl.pallas_call(
        functools.partial(_qkv_kernel, q_scale=LOG2E / math.sqrt(MLA_QK)),
        grid=(s // tm,),
        in_specs=[row(Q_RANK), row(KV_RANK), col(MLA_ROPE), col(1), _resident(invf.shape),
                  _resident(w_uq.shape), _resident(w_ukv.shape)],
        out_specs=[pl.BlockSpec((MLA_HEADS, MLA_QK, tm), lambda i: (0, 0, i)),
                   pl.BlockSpec((MLA_HEADS, tm, MLA_QK), lambda i: (0, i, 0)),
                   pl.BlockSpec((MLA_HEADS, nb, MLA_V, LANES), lambda i: (0, i, 0, 0))],
        out_shape=[jax.ShapeDtypeStruct((MLA_HEADS, MLA_QK, s), BF16),
                   jax.ShapeDtypeStruct((MLA_HEADS, s, MLA_QK), BF16),
                   jax.ShapeDtypeStruct((MLA_HEADS, s // LANES, MLA_V, LANES), BF16)],
        scratch_shapes=[pltpu.VMEM((MLA_HEADS * MLA_QK, rank), BF16),
                        pltpu.VMEM((rank, MLA_HEADS * MLA_NOPE), BF16),
                        pltpu.VMEM((MLA_HEADS * MLA_V, rank), BF16)],
        compiler_params=_cparams(("arbitrary",)),
        name="qkv",
    )(cq, ckv, krt, pos_row, invf, w_uq, w_ukv)


def _mla_kernel(*refs, tq, tk, ncast):
    qt_ref, k_ref, vt_ref = refs[:3]
    w_hbm = refs[3:3 + ncast]
    o_ref = refs[3 + ncast]
    wout_hbm = refs[4 + ncast:4 + 2 * ncast]
    m_sc, acc_sc, s_sc, smax_sc, excess_sc = refs[4 + 2 * ncast:9 + 2 * ncast]
    stage_in = refs[9 + 2 * ncast:9 + 3 * ncast]
    stage_out = refs[9 + 3 * ncast:9 + 4 * ncast]
    in_sems, out_sems = refs[9 + 4 * ncast:]

    step = pl.program_id(0)
    last_step = pl.num_programs(0) - 1

    def slab_in(w):
        rows = stage_in[w].shape[0]
        src = w_hbm[w].at[pl.ds(pl.multiple_of(step * rows, rows), rows), :]
        return pltpu.make_async_copy(src, stage_in[w], in_sems.at[w])

    def slab_out(w, at_step):
        rows = stage_out[w].shape[0]
        dst = wout_hbm[w].at[pl.ds(pl.multiple_of(at_step * rows, rows), rows), :]
        return pltpu.make_async_copy(stage_out[w], dst, out_sems.at[w])

    for w in range(ncast):
        slab_in(w).start()

    group = tq // tk
    assert group * tk == tq and group % 2 == 0
    nsub = tk // LANES
    full = slice(0, tq)
    lazy_unroll = 4

    def one_query_block(qi, safe):
        def at(i, size):
            return pl.ds(i * size if isinstance(i, int) else pl.multiple_of(i * size, size), size)

        def queries(cols=full):
            if isinstance(qi, int):
                return slice(qi * tq + cols.start, qi * tq + cols.stop)
            return pl.ds(pl.multiple_of(qi * tq + cols.start, tk), cols.stop - cols.start)

        def reset():
            m_sc[...] = jnp.full(m_sc.shape, NEG, F32)
            acc_sc[...] = jnp.zeros(acc_sc.shape, F32)

        def raw_scores(j, cols=full):
            return jnp.dot(k_ref[0, at(j, tk), :], qt_ref[0, :, queries(cols)],
                           preferred_element_type=F32)

        def values_t(j):
            vt = jnp.concatenate([vt_ref[0, j * nsub + c] for c in range(nsub)], axis=1)
            return jnp.concatenate([vt, jnp.ones((ONES_ROWS, tk), BF16)], axis=0)

        def scores(j, slot, cols=full):
            s = raw_scores(j, cols)
            s_sc[slot, :, cols] = s
            smax_sc[slot, :, cols] = jnp.max(s, axis=0, keepdims=True)

        def update(j, slot, cols=full, tri=False):
            s = s_sc[slot, :, cols]
            if tri:
                mask = (lax.broadcasted_iota(jnp.int32, (tk, tk), 0)
                        <= lax.broadcasted_iota(jnp.int32, (tk, tk), 1))
                parts = [jnp.where(mask, s[:, :tk], NEG)] + ([s[:, tk:]] if s.shape[1] > tk else [])
                s = jnp.concatenate(parts, axis=1)
            s_max = jnp.max(s, axis=0, keepdims=True) if tri else smax_sc[slot, :, cols]
            m_old = m_sc[:, cols]
            m_new = jnp.maximum(m_old, s_max)
            alpha = jnp.exp2(m_old - m_new)
            p = jnp.exp2(s - m_new)
            acc_sc[:, cols] = (alpha * acc_sc[:, cols]
                               + jnp.dot(values_t(j), p.astype(BF16), preferred_element_type=F32))
            m_sc[:, cols] = m_new

        def full_group(i, carry):
            for r in range(group):
                b = group * i + r
                scores(b + 1, (r + 1) % 2)
                update(b, r % 2)
            return carry

        def diagonal_group():
            for r in range(group):
                b = group * qi + r
                if r + 1 < group:
                    scores(b + 1, (r + 1) % 2, slice((r + 1) * tk, tq))
                update(b, r % 2, slice(r * tk, tq), tri=True)

        def finalize():
            out_t = acc_sc[:MLA_V, :] / acc_sc[MLA_V:MLA_V + 1, :]
            o_ref[at(qi, tq), :] = out_t.T.astype(o_ref.dtype)

        if safe:
            reset()
            scores(0, 0)
            lax.fori_loop(0, qi, full_group, 0)
            diagonal_group()
            finalize()
            return

        reset()
        excess_sc[:, queries()] = jnp.full((1, tq), NEG, F32)
        scores(group * qi, 0)
        diagonal_group()

        def lazy_blocks(first, count):
            m = m_sc[...]
            for r in range(count):
                j = first + r
                s = raw_scores(j)
                excess_sc[:, queries()] = jnp.maximum(excess_sc[:, queries()],
                                                      jnp.max(s, axis=0, keepdims=True) - m)
                acc_sc[...] += jnp.dot(values_t(j), jnp.exp2(s - m).astype(BF16),
                                       preferred_element_type=F32)

        def lazy_trip(i, carry):
            lazy_blocks(lazy_unroll * i, lazy_unroll)
            return carry

        trips, rest = divmod(group * qi, lazy_unroll)
        if trips:
            lax.fori_loop(0, trips, lazy_trip, 0)
        if rest:
            lazy_blocks(trips * lazy_unroll, rest)
        finalize()

    nq = qt_ref.shape[2] // tq
    for qi in range(nq):
        one_query_block(qi, safe=False)

    def redo_if_needed(qi, carry):
        @pl.when(jnp.max(excess_sc[:, pl.ds(pl.multiple_of(qi * tq, tq), tq)]) > LAZY_MARGIN)
        def _():
            one_query_block(qi, safe=True)
        return carry

    lax.fori_loop(0, nq, redo_if_needed, 0)

    @pl.when(step > 0)
    def _():
        for w in range(ncast):
            slab_out(w, step - 1).wait()

    for w in range(ncast):
        slab_in(w).wait()
        stage_out[w][...] = stage_in[w][...].astype(BF16)
        slab_out(w, step).start()

    @pl.when(step == last_step)
    def _():
        for w in range(ncast):
            slab_out(w, step).wait()


def _mla(qt, k, vt, cast_weights, *, tq, tk):
    _, s, _ = k.shape
    assert s % tq == 0, (s, tq)
    nsteps = MLA_HEADS
    ncast = len(cast_weights)
    slabs = [(w.shape[0] // nsteps, w.shape[1]) for w in cast_weights]
    for w, (rows, _) in zip(cast_weights, slabs):
        assert rows * nsteps == w.shape[0] and rows % 16 == 0, (w.shape, nsteps)
    anywhere = pl.BlockSpec(memory_space=pl.ANY)
    outs = pl.pallas_call(
        functools.partial(_mla_kernel, tq=tq, tk=tk, ncast=ncast),
        grid=(MLA_HEADS,),
        in_specs=[pl.BlockSpec((1, MLA_QK, s), lambda h: (h, 0, 0)),
                  pl.BlockSpec((1, s, MLA_QK), lambda h: (h, 0, 0)),
                  pl.BlockSpec((1, s // LANES, MLA_V, LANES), lambda h: (h, 0, 0, 0))]
                 + [anywhere] * ncast,
        out_specs=[pl.BlockSpec((s, MLA_V), lambda h: (0, h))] + [anywhere] * ncast,
        out_shape=[jax.ShapeDtypeStruct((s, MLA_HEADS * MLA_V), BF16)]
                  + [jax.ShapeDtypeStruct(w.shape, BF16) for w in cast_weights],
        scratch_shapes=[pltpu.VMEM((1, tq), F32),
                        pltpu.VMEM((MLA_V + ONES_ROWS, tq), F32),
                        pltpu.VMEM((2, tk, tq), F32), pltpu.VMEM((2, 1, tq), F32),
                        pltpu.VMEM((1, s), F32)]
                       + [pltpu.VMEM(sl, F32) for sl in slabs]
                       + [pltpu.VMEM(sl, BF16) for sl in slabs]
                       + [pltpu.SemaphoreType.DMA((ncast,)), pltpu.SemaphoreType.DMA((ncast,))],
        compiler_params=_cparams(("arbitrary",)),
        name="mla",
    )(qt, k, vt, *cast_weights)
    return outs[0], outs[1:]


def _t5_bucket_table_t():
    i = np.arange(BLK)[None, :]
    j = np.arange(2 * BLK)[:, None]
    dist = i + BLK - j
    max_exact = NUM_BUCKETS // 2
    d = np.maximum(dist, 0)
    large = max_exact + (np.log(np.maximum(d, 1) / max_exact)
                         / np.log(MAX_DISTANCE / max_exact)
                         * (NUM_BUCKETS - max_exact)).astype(np.int32)
    large = np.minimum(large, NUM_BUCKETS - 1)
    bucket = np.where(d < max_exact, d, large).astype(np.int32)
    in_window = (dist >= 0) & (dist < WINDOW)
    return np.where(in_window, bucket, -1).astype(np.int32)


def _swa_kernel(relb_ref, sink_ref, bucket_ref, qt_ref, kc_ref, kp_ref, vtc_ref, vtp_ref,
                o_ref, bias_sc, biasrel_sc, sink_sc, *, nb):
    i = pl.program_id(0)
    gw = SWA_GROUP * BLK

    @pl.when(i == 0)
    def _():
        bucket = bucket_ref[...]
        for h in range(SWA_HEADS):
            c, g = divmod(h, SWA_GROUP)
            lanes = slice(g * BLK, (g + 1) * BLK)
            sink = sink_ref[h] * LOG2E
            b = jnp.full(bucket.shape, NEG, F32)
            for t in range(NUM_BUCKETS):
                b = jnp.where(bucket == t, relb_ref[t, h] * LOG2E, b)
            masked = jnp.full((BLK, BLK), NEG, F32)
            for table, tb in ((bias_sc, b), (biasrel_sc, jnp.where(bucket < 0, NEG, b - sink))):
                table[1, c, :, lanes] = tb
                table[0, c, BLK:, lanes] = tb[BLK:]
                table[0, c, :BLK, lanes] = masked
            sink_sc[c, :, lanes] = jnp.full((1, BLK), sink, F32)

    zeros = jnp.zeros((SWA_DIM, gw), BF16)

    def all_blocks(safe):
        excess = jnp.full((8, gw), NEG, F32)
        for t in range(nb):
            tsl = slice(t * BLK, (t + 1) * BLK)
            k_prev = kp_ref[...] if t == 0 else kc_ref[(t - 1) * BLK:t * BLK, :]
            vt_prev = vtp_ref[...] if t == 0 else vtc_ref[:, (t - 1) * BLK:t * BLK]
            kband = jnp.concatenate([k_prev, kc_ref[tsl, :]], axis=0)
            general = 1 if t > 0 else jnp.where(i == 0, 0, 1)
            for c in range(SWA_KV_HEADS):
                qt = jnp.concatenate(
                    [qt_ref[(c * SWA_GROUP + g) * SWA_DIM:(c * SWA_GROUP + g + 1) * SWA_DIM, tsl]
                     for g in range(SWA_GROUP)], axis=1)
                qt_ext = jnp.concatenate([qt, zeros] if c == 0 else [zeros, qt], axis=0)
                raw = jnp.dot(kband, qt_ext, preferred_element_type=F32)
                if safe:
                    s = raw + bias_sc[general, c]
                    sink = sink_sc[c]
                    m = jnp.maximum(jnp.max(s, axis=0, keepdims=True), sink)
                    p = jnp.exp2(s - m)
                    sink_term = jnp.exp2(sink - m)
                else:
                    s = raw + biasrel_sc[general, c]
                    excess = functools.reduce(
                        jnp.maximum, [s[8 * r:8 * r + 8] for r in range(2 * BLK // 8)], excess)
                    p = jnp.exp2(s)
                    sink_term = 1.0
                dsl = slice(c * SWA_DIM, (c + 1) * SWA_DIM)
                vt = jnp.concatenate([vt_prev[dsl, :], vtc_ref[dsl, tsl]], axis=1)
                vt = jnp.concatenate([vt, jnp.ones((ONES_ROWS, 2 * BLK), BF16)], axis=0)
                pv = jnp.dot(vt, p.astype(BF16), preferred_element_type=F32)
                ot = pv[:SWA_DIM] / (pv[SWA_DIM:SWA_DIM + 1] + sink_term)
                for g2 in range(SWA_GROUP // 2):
                    two = jnp.concatenate([ot[:, (2 * g2) * BLK:(2 * g2 + 1) * BLK],
                                           ot[:, (2 * g2 + 1) * BLK:(2 * g2 + 2) * BLK]], axis=0)
                    col0 = (c * SWA_GROUP + 2 * g2) * SWA_DIM
                    o_ref[tsl, col0:col0 + 2 * SWA_DIM] = two.T.astype(o_ref.dtype)
        return excess

    excess = all_blocks(safe=False)

    @pl.when(jnp.max(excess) > LAZY_MARGIN)
    def _():
        all_blocks(safe=True)


def _swa(sqt, sk, svt, sinks, rel_bias, *, nb):
    nq, s = sqt.shape
    nkv = sk.shape[1]
    tb = nb * BLK
    bucket = jnp.asarray(_t5_bucket_table_t())
    smem = pl.BlockSpec(memory_space=pltpu.SMEM)
    prev_blk = lambda i: jnp.maximum(i * nb - 1, 0)
    return pl.pallas_call(
        functools.partial(_swa_kernel, nb=nb),
        grid=(s // tb,),
        in_specs=[smem, smem, _resident((2 * BLK, BLK)),
                  pl.BlockSpec((nq, tb), lambda i: (0, i)),
                  pl.BlockSpec((tb, nkv), lambda i: (i, 0)),
                  pl.BlockSpec((BLK, nkv), lambda i: (prev_blk(i), 0)),
                  pl.BlockSpec((nkv, tb), lambda i: (0, i)),
                  pl.BlockSpec((nkv, BLK), lambda i: (0, prev_blk(i)))],
        out_specs=pl.BlockSpec((tb, nq), lambda i: (i, 0)),
        out_shape=jax.ShapeDtypeStruct((s, nq), BF16),
        scratch_shapes=[pltpu.VMEM((2, SWA_KV_HEADS, 2 * BLK, SWA_GROUP * BLK), F32),
                        pltpu.VMEM((2, SWA_KV_HEADS, 2 * BLK, SWA_GROUP * BLK), F32),
                        pltpu.VMEM((SWA_KV_HEADS, 1, SWA_GROUP * BLK), F32)],
        compiler_params=_cparams(("arbitrary",)),
        name="swa",
    )(rel_bias, sinks, bucket, sqt, sk, sk, svt, svt)


def _rms(y, gain):
    r = lax.rsqrt(jnp.sum(y * y, axis=-1, keepdims=True) * (1.0 / y.shape[-1]) + RMS_EPS)
    return y * r * gain


def _oproj_kernel(a_ref, b_ref, wo_ref, x_ref, gpost_ref, gpre_ref, x1_ref, h_ref, *, sub):
    na = a_ref.shape[1]
    for r0 in range(0, x_ref.shape[0], sub):
        rows = slice(r0, r0 + sub)
        mix = (jnp.dot(a_ref[rows, :], wo_ref[:na, :], preferred_element_type=F32)
               + jnp.dot(b_ref[rows, :], wo_ref[na:, :], preferred_element_type=F32))
        x1 = x_ref[rows, :] + _rms(mix, gpost_ref[...])
        x1_ref[rows, :] = x1
        h_ref[rows, :] = _rms(x1, gpre_ref[...]).astype(BF16)


def _oproj(out_a, out_b, w_o, x2, g_post, g_pre, *, tm, sub):
    s, d = x2.shape
    row = lambda w: pl.BlockSpec((tm, w), lambda i: (i, 0))
    return pl.pallas_call(
        functools.partial(_oproj_kernel, sub=sub),
        grid=(s // tm,),
        in_specs=[row(out_a.shape[1]), row(out_b.shape[1]), _resident(w_o.shape),
                  row(d), _resident((1, d)), _resident((1, d))],
        out_specs=[row(d), row(d)],
        out_shape=[jax.ShapeDtypeStruct((s, d), F32), jax.ShapeDtypeStruct((s, d), BF16)],
        compiler_params=_cparams(("arbitrary",)),
        name="oproj",
    )(out_a, out_b, w_o, x2, g_post, g_pre)


def _ffn_kernel(h_ref, wg_ref, wu_ref, wd_ref, x1_hbm, gpost_ref, o_ref, x1_buf, x1_sem, *, sub):
    i = pl.program_id(0)
    j = pl.program_id(1)
    tm = o_ref.shape[0]

    def x1_copy():
        rows = pl.ds(pl.multiple_of(i * tm, tm), tm)
        return pltpu.make_async_copy(x1_hbm.at[rows, :], x1_buf, x1_sem)

    tf = wg_ref.shape[1]
    halves = [slice(0, tf // 2), slice(tf // 2, tf)]

    def down_products(rows):
        h = h_ref[rows, :]
        gu = [(jnp.dot(h, wg_ref[:, c], preferred_element_type=F32),
               jnp.dot(h, wu_ref[:, c], preferred_element_type=F32)) for c in halves]
        for c, (gate, up) in zip(halves, gu):
            act = (gate * jax.nn.sigmoid(gate) * up).astype(BF16)
            yield jnp.dot(act, wd_ref[c, :], preferred_element_type=F32)

    last = pl.num_programs(1) - 1

    @pl.when(j == 0)
    def _():
        x1_copy().start()
        terms = down_products(slice(0, tm))
        o_ref[...] = next(terms)
        for term in terms:
            o_ref[...] += term

    @pl.when((j > 0) & (j < last))
    def _():
        for term in down_products(slice(0, tm)):
            o_ref[...] += term

    @pl.when(j == last)
    def _():
        x1_copy().wait()
        for r in range(tm // sub):
            rows = slice(r * sub, (r + 1) * sub)
            total = o_ref[rows, :]
            for term in down_products(rows):
                total = total + term
            o_ref[rows, :] = x1_buf[rows, :] + _rms(total, gpost_ref[...])


def _ffn(h, wg, wu, wd, x1, g_post, *, tm, tf, sub):
    s, d = x1.shape
    dff = wg.shape[1]
    assert tm % sub == 0, (tm, sub)
    assert dff // tf >= 2, "the first d_ff tile assigns the accumulator and the last one finishes it"
    return pl.pallas_call(
        functools.partial(_ffn_kernel, sub=sub),
        grid=(s // tm, dff // tf),
        in_specs=[pl.BlockSpec((tm, d), lambda i, j: (i, 0)),
                  pl.BlockSpec((d, tf), lambda i, j: (0, j)),
                  pl.BlockSpec((d, tf), lambda i, j: (0, j)),
                  pl.BlockSpec((tf, d), lambda i, j: (j, 0)),
                  pl.BlockSpec(memory_space=pl.ANY),
                  _resident((1, d))],
        out_specs=pl.BlockSpec((tm, d), lambda i, j: (i, 0)),
        out_shape=jax.ShapeDtypeStruct((s, d), F32),
        scratch_shapes=[pltpu.VMEM((tm, d), F32), pltpu.SemaphoreType.DMA(())],
        compiler_params=_cparams(("arbitrary", "arbitrary")),
        name="ffn",
    )(h, wg, wu, wd, x1, g_post)


class _Tiles(NamedTuple):
    proj_rows: int = 512
    proj_sub: int = 256
    qkv_rows: int = 1024
    mla_tq: int = 1024
    mla_tk: int = 512
    swa_blocks: int = 4
    ffn_rows: int = 1024
    ffn_cols: int = 512
    ffn_sub: int = 256


def _layer(x2, pos_row, p, t=_Tiles()):
    half = MLA_ROPE // 2
    inv_freq = ROPE_THETA ** (-jnp.arange(half, dtype=F32) / half)
    invf = jnp.broadcast_to(inv_freq[:, None], (half, t.qkv_rows))

    row = lambda a: a[None, :]
    cq, ckv, krt, sqt, sk, svt = _proj(x2, row(p["g_mix_pre"]), p["w_in"].T, row(p["g_cq"]),
                                       row(p["g_ckv"]), tm=t.proj_rows, sub=t.proj_sub)
    qt, k, vt = _qkv(cq, ckv, krt, pos_row, invf, p["w_uq"], p["w_ukv"], tm=t.qkv_rows)
    out_a, (w_o, w_gate, w_up, w_down) = _mla(
        qt, k, vt, [p["w_o"], p["w_gate"], p["w_up"], p["w_down"]],
        tq=t.mla_tq, tk=t.mla_tk)
    out_b = _swa(sqt, sk, svt, p["sinks"], p["rel_bias"], nb=t.swa_blocks)
    x1, h = _oproj(out_a, out_b, w_o, x2, row(p["g_mix_post"]), row(p["g_ffn_pre"]),
                   tm=t.proj_rows, sub=t.proj_sub)
    return _ffn(h, w_gate, w_up, w_down, x1, row(p["g_ffn_post"]), tm=t.ffn_rows, tf=t.ffn_cols,
                sub=t.ffn_sub)


def kernel(x, positions, g_mix_pre, w_in, g_cq, g_ckv, w_uq, w_ukv, sinks, rel_bias,
           w_o, g_mix_post, g_ffn_pre, w_gate, w_up, w_down, g_ffn_post):
    b, s, d = x.shape
    assert b == 1, "the row-major (S, D) pipeline assumes a single sequence"
    n_in = Q_RANK + KV_RANK + MLA_ROPE + (SWA_HEADS + 2 * SWA_KV_HEADS) * SWA_DIM
    d_mix = MLA_HEADS * MLA_V + SWA_HEADS * SWA_DIM
    assert w_in.shape[1:] == (d, n_in) and w_o.shape[1:] == (d_mix, d), (w_in.shape, w_o.shape)
    assert w_uq.shape[1:] == (Q_RANK, MLA_HEADS * MLA_QK), w_uq.shape
    assert w_ukv.shape[1:] == (KV_RANK, MLA_HEADS * (MLA_NOPE + MLA_V)), w_ukv.shape
    assert rel_bias.shape == (NUM_BUCKETS, SWA_HEADS) and sinks.shape[1:] == (SWA_HEADS,)
    x2 = x.reshape(s, d)
    pos_row = positions.reshape(1, s)
    for layer in range(w_in.shape[0]):
        p = dict(g_mix_pre=g_mix_pre[layer], w_in=w_in[layer], g_cq=g_cq[layer],
                 g_ckv=g_ckv[layer], w_uq=w_uq[layer], w_ukv=w_ukv[layer],
                 sinks=sinks[layer], rel_bias=rel_bias, w_o=w_o[layer],
                 g_mix_post=g_mix_post[layer], g_ffn_pre=g_ffn_pre[layer],
                 w_gate=w_gate[layer], w_up=w_up[layer], w_down=w_down[layer],
                 g_ffn_post=g_ffn_post[layer])
        x2 = _layer(x2, pos_row, p)
    return x2.reshape(b, s, d)
```
